```python
import jax, jax.numpy as jnp
from jax import lax
import numpy as np

D_MODEL = 2048
BATCH = 8
SEQ = 8192
DEPTH = 4

N_MIXERS = 3
BLOCK = 128
EPS = 1e-6

FOX_HEADS = 16
FOX_HEAD_DIM = D_MODEL // FOX_HEADS
FOX_FORGET_BIAS = 2.0

SGU_WIDTH = D_MODEL
SGU_GROUPS = 16
SGU_GROUP_DIM = SGU_WIDTH // SGU_GROUPS
SGU_CHUNK = 128

SWA_HEAD_DIM = 64
SWA_Q_HEADS = D_MODEL // SWA_HEAD_DIM
SWA_KV_HEADS = 8
SWA_WINDOW = 128
ROPE_DIM = SWA_HEAD_DIM // 4
ROPE_THETA = 500000.0

D_FF = ((8 * D_MODEL + 3 * 256 - 1) // (3 * 256)) * 256

N_FOX = (DEPTH + 2) // 3
N_SGU = (DEPTH + 1) // 3
N_SWA = DEPTH // 3

kernel_name = "hybrid_fox_gmlp_swa_sink_adaln"

F32 = jnp.float32


def rmsnorm(x, g):
    xf = x.astype(F32)
    y = xf * lax.rsqrt(jnp.mean(xf * xf, axis=-1, keepdims=True) + EPS)
    return (y * g.astype(F32)).astype(x.dtype)


def layernorm(x, g, b):
    xf = x.astype(F32)
    mu = jnp.mean(xf, axis=-1, keepdims=True)
    var = jnp.mean(jnp.square(xf - mu), axis=-1, keepdims=True)
    y = (xf - mu) * lax.rsqrt(var + EPS)
    return (y * g.astype(F32) + b.astype(F32)).astype(x.dtype)


def modulate(h, shift, scale):
    return h * (1 + scale[:, None, :]) + shift[:, None, :]


def rope_tables(positions):
    inv = ROPE_THETA ** (-jnp.arange(0, ROPE_DIM, 2, dtype=F32) / ROPE_DIM)
    ang = positions.astype(F32)[..., None] * inv
    return jnp.cos(ang), jnp.sin(ang)


def apply_partial_rope(x, cos, sin):
    half = ROPE_DIM // 2
    x1 = x[..., :half]
    x2 = x[..., half:ROPE_DIM]
    rest = x[..., ROPE_DIM:]
    c = cos[:, :, None, :].astype(x.dtype)
    s = sin[:, :, None, :].astype(x.dtype)
    return jnp.concatenate([x1 * c - x2 * s, x2 * c + x1 * s, rest], axis=-1)


def fox_attention(h, w_in, b_f, w_out):
    B, S, _ = h.shape
    H, Dh = FOX_HEADS, FOX_HEAD_DIM
    proj = h @ w_in
    q, k, v, fg = jnp.split(proj, [H * Dh, 2 * H * Dh, 3 * H * Dh], axis=-1)
    q = q.reshape(B, S, H, Dh)
    k = k.reshape(B, S, H, Dh)
    v = v.reshape(B, S, H, Dh)
    log_f = jax.nn.log_sigmoid((fg + b_f).astype(F32))
    cum = lax.cumsum(log_f, axis=1)
    cum_t = cum.transpose(0, 2, 1)
    nb = S // BLOCK
    qb = q.reshape(B, nb, BLOCK, H, Dh).transpose(1, 0, 2, 3, 4)
    fq = cum.reshape(B, nb, BLOCK, H).transpose(1, 0, 3, 2)
    kpos = jnp.arange(S)
    scale = Dh ** -0.5

    def block_fn(args):
        i, q_i, f_i = args
        s = jnp.einsum('bqhd,bkhd->bhqk', q_i, k, preferred_element_type=F32) * scale
        s = s + f_i[..., None] - cum_t[:, :, None, :]
        qpos = i * BLOCK + jnp.arange(BLOCK)
        mask = kpos[None, :] <= qpos[:, None]
        s = jnp.where(mask, s, -jnp.inf)
        p = jax.nn.softmax(s, axis=-1)
        return jnp.einsum('bhqk,bkhd->bqhd', p.astype(v.dtype), v)

    out = lax.map(block_fn, (jnp.arange(nb), qb, fq))
    out = out.transpose(1, 0, 2, 3, 4).reshape(B, S, H * Dh)
    return out @ w_out


def gmlp_sgu(h, w_in, ln_g, ln_b, w_s, b_s, w_out):
    B, S, _ = h.shape
    z = jax.nn.gelu(h @ w_in)
    u, v = jnp.split(z, 2, axis=-1)
    v = layernorm(v, ln_g, ln_b)
    nc = S // SGU_CHUNK
    vg = v.reshape(B, nc, SGU_CHUNK, SGU_GROUPS, SGU_GROUP_DIM)
    causal = jnp.tril(jnp.ones((SGU_CHUNK, SGU_CHUNK), dtype=bool))
    ws = jnp.where(causal[None], w_s, jnp.zeros_like(w_s))
    f = jnp.einsum('gts,bcsgd->bctgd', ws, vg)
    f = f + b_s.T[None, None, :, :, None]
    gated = u * f.reshape(B, S, SGU_WIDTH)
    return gated @ w_out


def swa_sink_attention(h, w_in, sinks, w_out, cos, sin):
    B, S, _ = h.shape
    Hq, Hk, Dh = SWA_Q_HEADS, SWA_KV_HEADS, SWA_HEAD_DIM
    G = Hq // Hk
    proj = h @ w_in
    q, k, v = jnp.split(proj, [Hq * Dh, (Hq + Hk) * Dh], axis=-1)
    q = apply_partial_rope(q.reshape(B, S, Hq, Dh), cos, sin)
    k = apply_partial_rope(k.reshape(B, S, Hk, Dh), cos, sin)
    v = v.reshape(B, S, Hk, Dh)
    nb = S // BLOCK
    qb = q.reshape(B, nb, BLOCK, Hk, G, Dh)
    kb = k.reshape(B, nb, BLOCK, Hk, Dh)
    vb = v.reshape(B, nb, BLOCK, Hk, Dh)
    pad = ((0, 0), (1, 0), (0, 0), (0, 0), (0, 0))
    kband = jnp.concatenate([jnp.pad(kb[:, :-1], pad), kb], axis=2)
    vband = jnp.concatenate([jnp.pad(vb[:, :-1], pad), vb], axis=2)
    s = jnp.einsum('bnqhgd,bnkhd->bnhgqk', qb, kband, preferred_element_type=F32) * (Dh ** -0.5)
    qi = jnp.arange(BLOCK)[:, None]
    ki = jnp.arange(2 * BLOCK)[None, :] - BLOCK
    rel = qi - ki
    valid = (rel >= 0) & (rel < SWA_WINDOW)
    in_seq = (jnp.arange(nb)[:, None, None] * BLOCK + ki[None]) >= 0
    mask = valid[None] & in_seq
    s = jnp.where(mask[None, :, None, None], s, -jnp.inf)
    sink = sinks.astype(F32).reshape(Hk, G)[None, None, :, :, None, None]
    m = jnp.maximum(jnp.max(s, axis=-1, keepdims=True), sink)
    p = jnp.exp(s - m)
    p = p / (jnp.sum(p, axis=-1, keepdims=True) + jnp.exp(sink - m))
    o = jnp.einsum('bnhgqk,bnkhd->bnqhgd', p.astype(v.dtype), vband)
    return o.reshape(B, S, Hq * Dh) @ w_out


def swiglu(h, w_gu, w_down):
    g, u = jnp.split(h @ w_gu, 2, axis=-1)
    return (jax.nn.silu(g) * u) @ w_down


def _fwd_setup_inputs(seed: int = 0) -> dict:
    key = jax.random.key(seed)
    ks = jax.random.split(key, 32)
    D = D_MODEL
    nrm = lambda k, shape, fan_in, mult=1.0: jax.random.normal(k, shape, F32) * (mult * fan_in ** -0.5)
    fox_in = 3 * FOX_HEADS * FOX_HEAD_DIM + FOX_HEADS
    swa_in = (SWA_Q_HEADS + 2 * SWA_KV_HEADS) * SWA_HEAD_DIM
    x = jax.random.normal(ks[0], (BATCH, SEQ, D), F32)
    c = jax.random.normal(ks[1], (BATCH, D), F32)
    offset = jax.random.randint(ks[2], (BATCH, 1), 0, 4096, dtype=jnp.int32)
    positions = offset + jnp.arange(SEQ, dtype=jnp.int32)[None, :]
    gain = lambda k, shape: 1.0 + 0.02 * jax.random.normal(k, shape, F32)
    return {
        "x": x,
        "c": c,
        "positions": positions,
        "ada_w": nrm(ks[3], (DEPTH, D, 6 * D), D, 0.5),
        "ada_b": 0.01 * jax.random.normal(ks[4], (DEPTH, 6 * D), F32),
        "mix_pre_g": gain(ks[5], (DEPTH, D)),
        "mix_post_g": gain(ks[6], (DEPTH, D)),
        "ffn_pre_g": gain(ks[7], (DEPTH, D)),
        "ffn_post_g": gain(ks[8], (DEPTH, D)),
        "ffn_w_gu": nrm(ks[9], (DEPTH, D, 2 * D_FF), D),
        "ffn_w_down": nrm(ks[10], (DEPTH, D_FF, D), D_FF),
        "fox_w_in": nrm(ks[11], (N_FOX, D, fox_in), D),
        "fox_b_f": FOX_FORGET_BIAS + 0.5 * jax.random.normal(ks[12], (N_FOX, FOX_HEADS), F32),
        "fox_w_out": nrm(ks[13], (N_FOX, FOX_HEADS * FOX_HEAD_DIM, D), FOX_HEADS * FOX_HEAD_DIM),
        "sgu_w_in": nrm(ks[14], (N_SGU, D, 2 * SGU_WIDTH), D),
        "sgu_ln_g": gain(ks[15], (N_SGU, SGU_WIDTH)),
        "sgu_ln_b": 0.01 * jax.random.normal(ks[16], (N_SGU, SGU_WIDTH), F32),
        "sgu_w_s": nrm(ks[17], (N_SGU, SGU_GROUPS, SGU_CHUNK, SGU_CHUNK), SGU_CHUNK),
        "sgu_b_s": 1.0 + 0.02 * jax.random.normal(ks[18], (N_SGU, SGU_GROUPS, SGU_CHUNK), F32),
        "sgu_w_out": nrm(ks[19], (N_SGU, SGU_WIDTH, D), SGU_WIDTH),
        "swa_w_in": nrm(ks[20], (N_SWA, D, swa_in), D),
        "swa_sinks": 0.5 * jax.random.normal(ks[21], (N_SWA, SWA_Q_HEADS), F32),
        "swa_w_out": nrm(ks[22], (N_SWA, SWA_Q_HEADS * SWA_HEAD_DIM, D), SWA_Q_HEADS * SWA_HEAD_DIM),
    }


def _fwd_reference(x, c, positions, ada_w, ada_b, mix_pre_g, mix_post_g, ffn_pre_g, ffn_post_g,
              ffn_w_gu, ffn_w_down, fox_w_in, fox_b_f, fox_w_out, sgu_w_in, sgu_ln_g,
              sgu_ln_b, sgu_w_s, sgu_b_s, sgu_w_out, swa_w_in, swa_sinks, swa_w_out):
    cos, sin = rope_tables(positions)
    c_act = jax.nn.silu(c)
    for i in range(DEPTH):
        mod = c_act @ ada_w[i] + ada_b[i]
        sh_m, sc_m, g_m, sh_f, sc_f, g_f = jnp.split(mod, 6, axis=-1)
        h = modulate(rmsnorm(x, mix_pre_g[i]), sh_m, sc_m)
        kind, j = i % N_MIXERS, i // N_MIXERS
        if kind == 0:
            y = fox_attention(h, fox_w_in[j], fox_b_f[j], fox_w_out[j])
        elif kind == 1:
            y = gmlp_sgu(h, sgu_w_in[j], sgu_ln_g[j], sgu_ln_b[j], sgu_w_s[j], sgu_b_s[j], sgu_w_out[j])
        else:
            y = swa_sink_attention(h, swa_w_in[j], swa_sinks[j], swa_w_out[j], cos, sin)
        x = x + g_m[:, None, :] * rmsnorm(y, mix_post_g[i])
        h = modulate(rmsnorm(x, ffn_pre_g[i]), sh_f, sc_f)
        y = swiglu(h, ffn_w_gu[i], ffn_w_down[i])
        x = x + g_f[:, None, :] * rmsnorm(y, ffn_post_g[i])
    return x


import jax as _jax
import jax.numpy as _jnp

TWIN_FORMAT = 'train_step'
FWD_PARAMS = ['x', 'c', 'positions', 'ada_w', 'ada_b', 'mix_pre_g', 'mix_post_g', 'ffn_pre_g', 'ffn_post_g', 'ffn_w_gu', 'ffn_w_down', 'fox_w_in', 'fox_b_f', 'fox_w_out', 'sgu_w_in', 'sgu_ln_g', 'sgu_ln_b', 'sgu_w_s', 'sgu_b_s', 'sgu_w_out', 'swa_w_in', 'swa_sinks', 'swa_w_out']
TWIN_WEIGHTS = ['ada_w', 'ada_b', 'mix_pre_g', 'mix_post_g', 'ffn_pre_g', 'ffn_post_g', 'ffn_w_gu', 'ffn_w_down', 'fox_w_in', 'fox_b_f', 'fox_w_out', 'sgu_w_in', 'sgu_ln_g', 'sgu_ln_b', 'sgu_w_s', 'sgu_b_s', 'sgu_w_out', 'swa_w_in', 'swa_sinks', 'swa_w_out']
TWIN_DIFF_INPUT = 'x'
TWIN_INPUTS = ['x', 'c', 'positions', 'ada_w', 'ada_b', 'mix_pre_g', 'mix_post_g', 'ffn_pre_g', 'ffn_post_g', 'ffn_w_gu', 'ffn_w_down', 'fox_w_in', 'fox_b_f', 'fox_w_out', 'sgu_w_in', 'sgu_ln_g', 'sgu_ln_b', 'sgu_w_s', 'sgu_b_s', 'sgu_w_out', 'swa_w_in', 'swa_sinks', 'swa_w_out', 'loss_target', 'm_ada_w', 'm_ada_b', 'm_mix_pre_g', 'm_mix_post_g', 'm_ffn_pre_g', 'm_ffn_post_g', 'm_ffn_w_gu', 'm_ffn_w_down', 'm_fox_w_in', 'm_fox_b_f', 'm_fox_w_out', 'm_sgu_w_in', 'm_sgu_ln_g', 'm_sgu_ln_b', 'm_sgu_w_s', 'm_sgu_b_s', 'm_sgu_w_out', 'm_swa_w_in', 'm_swa_sinks', 'm_swa_w_out', 'v_ada_w', 'v_ada_b', 'v_mix_pre_g', 'v_mix_post_g', 'v_ffn_pre_g', 'v_ffn_post_g', 'v_ffn_w_gu', 'v_ffn_w_down', 'v_fox_w_in', 'v_fox_b_f', 'v_fox_w_out', 'v_sgu_w_in', 'v_sgu_ln_g', 'v_sgu_ln_b', 'v_sgu_w_s', 'v_sgu_b_s', 'v_sgu_w_out', 'v_swa_w_in', 'v_swa_sinks', 'v_swa_w_out']
TWIN_OUTPUTS = ['loss', 'grad_x', 'grad_ada_w', 'grad_ada_b', 'grad_mix_pre_g', 'grad_mix_post_g', 'grad_ffn_pre_g', 'grad_ffn_post_g', 'grad_ffn_w_gu', 'grad_ffn_w_down', 'grad_fox_w_in', 'grad_fox_b_f', 'grad_fox_w_out', 'grad_sgu_w_in', 'grad_sgu_ln_g', 'grad_sgu_ln_b', 'grad_sgu_w_s', 'grad_sgu_b_s', 'grad_sgu_w_out', 'grad_swa_w_in', 'grad_swa_sinks', 'grad_swa_w_out', 'delta_ada_w', 'delta_ada_b', 'delta_mix_pre_g', 'delta_mix_post_g', 'delta_ffn_pre_g', 'delta_ffn_post_g', 'delta_ffn_w_gu', 'delta_ffn_w_down', 'delta_fox_w_in', 'delta_fox_b_f', 'delta_fox_w_out', 'delta_sgu_w_in', 'delta_sgu_ln_g', 'delta_sgu_ln_b', 'delta_sgu_w_s', 'delta_sgu_b_s', 'delta_sgu_w_out', 'delta_swa_w_in', 'delta_swa_sinks', 'delta_swa_w_out', 'new_m_ada_w', 'new_m_ada_b', 'new_m_mix_pre_g', 'new_m_mix_post_g', 'new_m_ffn_pre_g', 'new_m_ffn_post_g', 'new_m_ffn_w_gu', 'new_m_ffn_w_down', 'new_m_fox_w_in', 'new_m_fox_b_f', 'new_m_fox_w_out', 'new_m_sgu_w_in', 'new_m_sgu_ln_g', 'new_m_sgu_ln_b', 'new_m_sgu_w_s', 'new_m_sgu_b_s', 'new_m_sgu_w_out', 'new_m_swa_w_in', 'new_m_swa_sinks', 'new_m_swa_w_out', 'new_v_ada_w', 'new_v_ada_b', 'new_v_mix_pre_g', 'new_v_mix_post_g', 'new_v_ffn_pre_g', 'new_v_ffn_post_g', 'new_v_ffn_w_gu', 'new_v_ffn_w_down', 'new_v_fox_w_in', 'new_v_fox_b_f', 'new_v_fox_w_out', 'new_v_sgu_w_in', 'new_v_sgu_ln_g', 'new_v_sgu_ln_b', 'new_v_sgu_w_s', 'new_v_sgu_b_s', 'new_v_sgu_w_out', 'new_v_swa_w_in', 'new_v_swa_sinks', 'new_v_swa_w_out']
TWIN_LEAF_KINDS = {'loss': 'loss', 'grad_x': 'grad_x', 'grad_ada_w': 'grad_w', 'grad_ada_b': 'grad_w', 'grad_mix_pre_g': 'grad_w', 'grad_mix_post_g': 'grad_w', 'grad_ffn_pre_g': 'grad_w', 'grad_ffn_post_g': 'grad_w', 'grad_ffn_w_gu': 'grad_w', 'grad_ffn_w_down': 'grad_w', 'grad_fox_w_in': 'grad_w', 'grad_fox_b_f': 'grad_w', 'grad_fox_w_out': 'grad_w', 'grad_sgu_w_in': 'grad_w', 'grad_sgu_ln_g': 'grad_w', 'grad_sgu_ln_b': 'grad_w', 'grad_sgu_w_s': 'grad_w', 'grad_sgu_b_s': 'grad_w', 'grad_sgu_w_out': 'grad_w', 'grad_swa_w_in': 'grad_w', 'grad_swa_sinks': 'grad_w', 'grad_swa_w_out': 'grad_w', 'delta_ada_w': 'delta_w', 'delta_ada_b': 'delta_w', 'delta_mix_pre_g': 'delta_w', 'delta_mix_post_g': 'delta_w', 'delta_ffn_pre_g': 'delta_w', 'delta_ffn_post_g': 'delta_w', 'delta_ffn_w_gu': 'delta_w', 'delta_ffn_w_down': 'delta_w', 'delta_fox_w_in': 'delta_w', 'delta_fox_b_f': 'delta_w', 'delta_fox_w_out': 'delta_w', 'delta_sgu_w_in': 'delta_w', 'delta_sgu_ln_g': 'delta_w', 'delta_sgu_ln_b': 'delta_w', 'delta_sgu_w_s': 'delta_w', 'delta_sgu_b_s': 'delta_w', 'delta_sgu_w_out': 'delta_w', 'delta_swa_w_in': 'delta_w', 'delta_swa_sinks': 'delta_w', 'delta_swa_w_out': 'delta_w', 'new_m_ada_w': 'new_m', 'new_m_ada_b': 'new_m', 'new_m_mix_pre_g': 'new_m', 'new_m_mix_post_g': 'new_m', 'new_m_ffn_pre_g': 'new_m', 'new_m_ffn_post_g': 'new_m', 'new_m_ffn_w_gu': 'new_m', 'new_m_ffn_w_down': 'new_m', 'new_m_fox_w_in': 'new_m', 'new_m_fox_b_f': 'new_m', 'new_m_fox_w_out': 'new_m', 'new_m_sgu_w_in': 'new_m', 'new_m_sgu_ln_g': 'new_m', 'new_m_sgu_ln_b': 'new_m', 'new_m_sgu_w_s': 'new_m', 'new_m_sgu_b_s': 'new_m', 'new_m_sgu_w_out': 'new_m', 'new_m_swa_w_in': 'new_m', 'new_m_swa_sinks': 'new_m', 'new_m_swa_w_out': 'new_m', 'new_v_ada_w': 'new_v', 'new_v_ada_b': 'new_v', 'new_v_mix_pre_g': 'new_v', 'new_v_mix_post_g': 'new_v', 'new_v_ffn_pre_g': 'new_v', 'new_v_ffn_post_g': 'new_v', 'new_v_ffn_w_gu': 'new_v', 'new_v_ffn_w_down': 'new_v', 'new_v_fox_w_in': 'new_v', 'new_v_fox_b_f': 'new_v', 'new_v_fox_w_out': 'new_v', 'new_v_sgu_w_in': 'new_v', 'new_v_sgu_ln_g': 'new_v', 'new_v_sgu_ln_b': 'new_v', 'new_v_sgu_w_s': 'new_v', 'new_v_sgu_b_s': 'new_v', 'new_v_sgu_w_out': 'new_v', 'new_v_swa_w_in': 'new_v', 'new_v_swa_sinks': 'new_v', 'new_v_swa_w_out': 'new_v'}


def _forward(args):
    return _fwd_reference(*[args[k] for k in FWD_PARAMS])


def _output_shape():
    def fwd():
        inp = _fwd_setup_inputs(0)
        return _fwd_reference(*[inp[k] for k in FWD_PARAMS])
    out = _jax.eval_shape(fwd)
    return out.shape, out.dtype

N_MICROBATCH = 1
ADAM_LR = 0.001
ADAM_B1 = 0.9
ADAM_B2 = 0.999
ADAM_EPS = 1e-08
ADAM_WD = 0.01
ADAM_STEP = 10
PER_EXAMPLE_BATCH_AXIS = {'x': 0, 'c': 0, 'positions': 0, 'loss_target': 0}
SHARED_INPUTS = []
_WEIGHT_DTYPES = {'ada_w': _jnp.float32, 'ada_b': _jnp.float32, 'mix_pre_g': _jnp.float32, 'mix_post_g': _jnp.float32, 'ffn_pre_g': _jnp.float32, 'ffn_post_g': _jnp.float32, 'ffn_w_gu': _jnp.float32, 'ffn_w_down': _jnp.float32, 'fox_w_in': _jnp.float32, 'fox_b_f': _jnp.float32, 'fox_w_out': _jnp.float32, 'sgu_w_in': _jnp.float32, 'sgu_ln_g': _jnp.float32, 'sgu_ln_b': _jnp.float32, 'sgu_w_s': _jnp.float32, 'sgu_b_s': _jnp.float32, 'sgu_w_out': _jnp.float32, 'swa_w_in': _jnp.float32, 'swa_sinks': _jnp.float32, 'swa_w_out': _jnp.float32}
MOMENT_SCALE = {'ada_w': 1.772314e+00, 'ada_b': 3.385867e+00, 'mix_pre_g': 4.814718e-01, 'mix_post_g': 3.670090e+00, 'ffn_pre_g': 1.787061e-01, 'ffn_post_g': 3.238670e+00, 'ffn_w_gu': 1.120779e-01, 'ffn_w_down': 2.256036e-01, 'fox_w_in': 6.013812e-01, 'fox_b_f': 8.203883e-01, 'fox_w_out': 1.104169e+00, 'sgu_w_in': 2.213304e-01, 'sgu_ln_g': 4.654713e-02, 'sgu_ln_b': 5.766471e-02, 'sgu_w_s': 4.515342e-02, 'sgu_b_s': 6.852305e-02, 'sgu_w_out': 7.848993e-01, 'swa_w_in': 1.318647e+00, 'swa_sinks': 3.293019e-02, 'swa_w_out': 1.600851e+00}


def _to_microbatches(a, axis):
    t = _jnp.moveaxis(a, axis, 0)
    t = t.reshape((N_MICROBATCH, t.shape[0] // N_MICROBATCH) + t.shape[1:])
    return _jnp.moveaxis(t, 1, axis + 1)


def setup_inputs(seed: int = 0) -> dict:
    inp = _fwd_setup_inputs(seed)
    key = _jax.random.fold_in(_jax.random.key(seed), 7919)
    shape, _ = _output_shape()
    out = dict(inp)
    out["loss_target"] = _jax.random.normal(_jax.random.fold_in(key, 0), shape, _jnp.float32)
    for i, name in enumerate(TWIN_WEIGHTS):
        w = inp[name].astype(_jnp.float32)
        if MOMENT_SCALE is None:
            s = _jnp.sqrt(_jnp.mean(_jnp.square(w)) + 1e-30)
        else:
            s = MOMENT_SCALE[name]
        km, kv = _jax.random.split(_jax.random.fold_in(key, i + 1))
        out[name] = w
        out["m_" + name] = s * _jax.random.normal(km, w.shape, _jnp.float32)
        out["v_" + name] = (s * s) * _jax.random.uniform(kv, w.shape, _jnp.float32, 0.5, 1.5)
    if N_MICROBATCH > 1:
        for name, axis in PER_EXAMPLE_BATCH_AXIS.items():
            out[name] = _to_microbatches(out[name], axis)
    return {'x': out['x'], 'c': out['c'], 'positions': out['positions'], 'ada_w': out['ada_w'], 'ada_b': out['ada_b'], 'mix_pre_g': out['mix_pre_g'], 'mix_post_g': out['mix_post_g'], 'ffn_pre_g': out['ffn_pre_g'], 'ffn_post_g': out['ffn_post_g'], 'ffn_w_gu': out['ffn_w_gu'], 'ffn_w_down': out['ffn_w_down'], 'fox_w_in': out['fox_w_in'], 'fox_b_f': out['fox_b_f'], 'fox_w_out': out['fox_w_out'], 'sgu_w_in': out['sgu_w_in'], 'sgu_ln_g': out['sgu_ln_g'], 'sgu_ln_b': out['sgu_ln_b'], 'sgu_w_s': out['sgu_w_s'], 'sgu_b_s': out['sgu_b_s'], 'sgu_w_out': out['sgu_w_out'], 'swa_w_in': out['swa_w_in'], 'swa_sinks': out['swa_sinks'], 'swa_w_out': out['swa_w_out'], 'loss_target': out['loss_target'], 'm_ada_w': out['m_ada_w'], 'm_ada_b': out['m_ada_b'], 'm_mix_pre_g': out['m_mix_pre_g'], 'm_mix_post_g': out['m_mix_post_g'], 'm_ffn_pre_g': out['m_ffn_pre_g'], 'm_ffn_post_g': out['m_ffn_post_g'], 'm_ffn_w_gu': out['m_ffn_w_gu'], 'm_ffn_w_down': out['m_ffn_w_down'], 'm_fox_w_in': out['m_fox_w_in'], 'm_fox_b_f': out['m_fox_b_f'], 'm_fox_w_out': out['m_fox_w_out'], 'm_sgu_w_in': out['m_sgu_w_in'], 'm_sgu_ln_g': out['m_sgu_ln_g'], 'm_sgu_ln_b': out['m_sgu_ln_b'], 'm_sgu_w_s': out['m_sgu_w_s'], 'm_sgu_b_s': out['m_sgu_b_s'], 'm_sgu_w_out': out['m_sgu_w_out'], 'm_swa_w_in': out['m_swa_w_in'], 'm_swa_sinks': out['m_swa_sinks'], 'm_swa_w_out': out['m_swa_w_out'], 'v_ada_w': out['v_ada_w'], 'v_ada_b': out['v_ada_b'], 'v_mix_pre_g': out['v_mix_pre_g'], 'v_mix_post_g': out['v_mix_post_g'], 'v_ffn_pre_g': out['v_ffn_pre_g'], 'v_ffn_post_g': out['v_ffn_post_g'], 'v_ffn_w_gu': out['v_ffn_w_gu'], 'v_ffn_w_down': out['v_ffn_w_down'], 'v_fox_w_in': out['v_fox_w_in'], 'v_fox_b_f': out['v_fox_b_f'], 'v_fox_w_out': out['v_fox_w_out'], 'v_sgu_w_in': out['v_sgu_w_in'], 'v_sgu_ln_g': out['v_sgu_ln_g'], 'v_sgu_ln_b': out['v_sgu_ln_b'], 'v_sgu_w_s': out['v_sgu_w_s'], 'v_sgu_b_s': out['v_sgu_b_s'], 'v_sgu_w_out': out['v_sgu_w_out'], 'v_swa_w_in': out['v_swa_w_in'], 'v_swa_sinks': out['v_swa_sinks'], 'v_swa_w_out': out['v_swa_w_out']}


def _loss(weights, diff, rest, loss_target):
    with _jax.named_scope("forward"):
        args = {**rest, TWIN_DIFF_INPUT: diff, **{k: w.astype(_WEIGHT_DTYPES[k]) for k, w in weights.items()}}
        y = _forward(args)
    with _jax.named_scope("loss_head"):
        err = _jnp.square(y.astype(_jnp.float32) - loss_target)
        return 0.5 * _jnp.sum(_jnp.mean(err, axis=-1)) if err.ndim else 0.5 * err


def _adamw(w, g, m, v):
    m = ADAM_B1 * m + (1.0 - ADAM_B1) * g
    v = ADAM_B2 * v + (1.0 - ADAM_B2) * _jnp.square(g)
    m_hat = m / (1.0 - ADAM_B1 ** ADAM_STEP)
    v_hat = v / (1.0 - ADAM_B2 ** ADAM_STEP)
    delta = -ADAM_LR * (m_hat / (_jnp.sqrt(v_hat) + ADAM_EPS) + ADAM_WD * w)
    return delta, m, v


def reference(x, c, positions, ada_w, ada_b, mix_pre_g, mix_post_g, ffn_pre_g, ffn_post_g, ffn_w_gu, ffn_w_down, fox_w_in, fox_b_f, fox_w_out, sgu_w_in, sgu_ln_g, sgu_ln_b, sgu_w_s, sgu_b_s, sgu_w_out, swa_w_in, swa_sinks, swa_w_out, loss_target, m_ada_w, m_ada_b, m_mix_pre_g, m_mix_post_g, m_ffn_pre_g, m_ffn_post_g, m_ffn_w_gu, m_ffn_w_down, m_fox_w_in, m_fox_b_f, m_fox_w_out, m_sgu_w_in, m_sgu_ln_g, m_sgu_ln_b, m_sgu_w_s, m_sgu_b_s, m_sgu_w_out, m_swa_w_in, m_swa_sinks, m_swa_w_out, v_ada_w, v_ada_b, v_mix_pre_g, v_mix_post_g, v_ffn_pre_g, v_ffn_post_g, v_ffn_w_gu, v_ffn_w_down, v_fox_w_in, v_fox_b_f, v_fox_w_out, v_sgu_w_in, v_sgu_ln_g, v_sgu_ln_b, v_sgu_w_s, v_sgu_b_s, v_sgu_w_out, v_swa_w_in, v_swa_sinks, v_swa_w_out):
    given = dict(x=x, c=c, positions=positions, ada_w=ada_w, ada_b=ada_b, mix_pre_g=mix_pre_g, mix_post_g=mix_post_g, ffn_pre_g=ffn_pre_g, ffn_post_g=ffn_post_g, ffn_w_gu=ffn_w_gu, ffn_w_down=ffn_w_down, fox_w_in=fox_w_in, fox_b_f=fox_b_f, fox_w_out=fox_w_out, sgu_w_in=sgu_w_in, sgu_ln_g=sgu_ln_g, sgu_ln_b=sgu_ln_b, sgu_w_s=sgu_w_s, sgu_b_s=sgu_b_s, sgu_w_out=sgu_w_out, swa_w_in=swa_w_in, swa_sinks=swa_sinks, swa_w_out=swa_w_out, loss_target=loss_target, m_ada_w=m_ada_w, m_ada_b=m_ada_b, m_mix_pre_g=m_mix_pre_g, m_mix_post_g=m_mix_post_g, m_ffn_pre_g=m_ffn_pre_g, m_ffn_post_g=m_ffn_post_g, m_ffn_w_gu=m_ffn_w_gu, m_ffn_w_down=m_ffn_w_down, m_fox_w_in=m_fox_w_in, m_fox_b_f=m_fox_b_f, m_fox_w_out=m_fox_w_out, m_sgu_w_in=m_sgu_w_in, m_sgu_ln_g=m_sgu_ln_g, m_sgu_ln_b=m_sgu_ln_b, m_sgu_w_s=m_sgu_w_s, m_sgu_b_s=m_sgu_b_s, m_sgu_w_out=m_sgu_w_out, m_swa_w_in=m_swa_w_in, m_swa_sinks=m_swa_sinks, m_swa_w_out=m_swa_w_out, v_ada_w=v_ada_w, v_ada_b=v_ada_b, v_mix_pre_g=v_mix_pre_g, v_mix_post_g=v_mix_post_g, v_ffn_pre_g=v_ffn_pre_g, v_ffn_post_g=v_ffn_post_g, v_ffn_w_gu=v_ffn_w_gu, v_ffn_w_down=v_ffn_w_down, v_fox_w_in=v_fox_w_in, v_fox_b_f=v_fox_b_f, v_fox_w_out=v_fox_w_out, v_sgu_w_in=v_sgu_w_in, v_sgu_ln_g=v_sgu_ln_g, v_sgu_ln_b=v_sgu_ln_b, v_sgu_w_s=v_sgu_w_s, v_sgu_b_s=v_sgu_b_s, v_sgu_w_out=v_sgu_w_out, v_swa_w_in=v_swa_w_in, v_swa_sinks=v_swa_sinks, v_swa_w_out=v_swa_w_out)
    weights = {n: given[n] for n in TWIN_WEIGHTS}
    shared = {n: given[n] for n in SHARED_INPUTS}
    per_example = {n: given[n] for n in ['x', 'c', 'positions']}
    grad_fn = _jax.value_and_grad(_loss, argnums=(0, 1))

    def one_microbatch(ex, loss_target):
        ex = dict(ex)
        diff = ex.pop(TWIN_DIFF_INPUT)
        return grad_fn(weights, diff, {**shared, **ex}, loss_target)

    if N_MICROBATCH == 1:
        loss, (grad_w, grad_x) = one_microbatch(per_example, given["loss_target"])
    else:
        def body(carry, xs):
            loss_sum, grad_sum = carry
            l_k, (gw_k, gx_k) = one_microbatch(xs[0], xs[1])
            with _jax.named_scope("update"):
                return (loss_sum + l_k, _jax.tree.map(_jnp.add, grad_sum, gw_k)), gx_k

        init = (_jnp.zeros((), _jnp.float32), _jax.tree.map(_jnp.zeros_like, weights))
        (loss, grad_w), grad_x = _jax.lax.scan(body, init, (per_example, given["loss_target"]))
    with _jax.named_scope("update"):
        delta_w, new_m, new_v = {}, {}, {}
        for n in TWIN_WEIGHTS:
            delta_w[n], new_m[n], new_v[n] = _adamw(weights[n], grad_w[n], given["m_" + n], given["v_" + n])
    return (loss, grad_x, *[grad_w[n] for n in TWIN_WEIGHTS], *[delta_w[n] for n in TWIN_WEIGHTS],
            *[new_m[n] for n in TWIN_WEIGHTS], *[new_v[n] for n in TWIN_WEIGHTS])
```

```python
import numpy as np
import jax
import jax.numpy as jnp
from jax import lax
from jax.experimental import pallas as pl
from jax.experimental.pallas import tpu as pltpu

F32 = jnp.float32
BF16 = jnp.bfloat16
NDEV = 8
AXES = ("x", "y", "c")
LANES = 128
VMEM_LIMIT_BYTES = 48 * 1024 * 1024
NEG_INF = float("-inf")
ROW_TILE = 256

EPS = 1e-6
BLOCK = 128
FOX_DH = 128
SWA_DH = 64
ROPE_DIM = 16
ROPE_THETA = 500000.0
FOX_SCALE = FOX_DH ** -0.5
SWA_SCALE = SWA_DH ** -0.5
GELU_C0 = 0.7978845608028654
GELU_C1 = 0.044715

ADAM_LR = 0.001
ADAM_B1 = 0.9
ADAM_B2 = 0.999
ADAM_EPS = 1e-08
ADAM_WD = 0.01
ADAM_STEP = 10

NN = (((1,), (0,)), ((), ()))
NT = (((1,), (1,)), ((), ()))
TN = (((0,), (0,)), ((), ()))

WEIGHTS = ['ada_w', 'ada_b', 'mix_pre_g', 'mix_post_g', 'ffn_pre_g', 'ffn_post_g', 'ffn_w_gu', 'ffn_w_down',
           'fox_w_in', 'fox_b_f', 'fox_w_out', 'sgu_w_in', 'sgu_ln_g', 'sgu_ln_b', 'sgu_w_s', 'sgu_b_s',
           'sgu_w_out', 'swa_w_in', 'swa_sinks', 'swa_w_out']
SMALL = ['ada_b', 'mix_pre_g', 'mix_post_g', 'ffn_pre_g', 'ffn_post_g', 'fox_b_f', 'sgu_ln_g', 'sgu_ln_b',
         'sgu_w_s', 'sgu_b_s', 'swa_sinks']


def _dot(a, b, dims):
    return lax.dot_general(a, b, dims, preferred_element_type=F32)


def _tile(n, pref, mult=LANES):
    t = (min(pref, n) // mult) * mult
    while t >= mult:
        if n % t == 0:
            return t
        t -= mult
    return n


def _params():
    return pltpu.CompilerParams(vmem_limit_bytes=VMEM_LIMIT_BYTES)


def _rep(a, n):
    return a if n == 1 else jnp.concatenate([a] * n, axis=-1)


def _vec_spec(d):
    return pl.BlockSpec((1, d), lambda *_: (0, 0))


def _sigmoid(z):
    return 1.0 / (1.0 + jnp.exp(-z))


def _gelu(z):
    t = jnp.tanh(GELU_C0 * (z + GELU_C1 * z * z * z))
    return 0.5 * z * (1.0 + t)


def _gelu_grad(z):
    t = jnp.tanh(GELU_C0 * (z + GELU_C1 * z * z * z))
    return 0.5 * (1.0 + t) + 0.5 * z * (1.0 - t * t) * GELU_C0 * (1.0 + 3.0 * GELU_C1 * z * z)


def _comm(name, arrs, kind):
    n = len(arrs)
    gather = kind == 'gather'
    out_shape = [jax.ShapeDtypeStruct(((NDEV,) + a.shape) if gather else a.shape, a.dtype) for a in arrs]

    def body(*refs):
        ins, outs = refs[:n], refs[n:2 * n]
        send_sems, recv_sems, local_sems = refs[2 * n:]
        x, y, c = lax.axis_index("x"), lax.axis_index("y"), lax.axis_index("c")
        me = 4 * x + 2 * y + c
        local = []
        for a in range(n):
            cp = pltpu.make_async_copy(ins[a] if gather else ins[a].at[me], outs[a].at[me], local_sems.at[a])
            cp.start()
            local.append(cp)
        sends = []
        for a in range(n):
            for k in range(NDEV - 1):
                bits = k + 1
                px = (1 - x) if bits & 4 else x
                py = (1 - y) if bits & 2 else y
                pc = (1 - c) if bits & 1 else c
                peer = 4 * px + 2 * py + pc
                src = ins[a] if gather else ins[a].at[peer]
                cp = pltpu.make_async_remote_copy(
                    src_ref=src, dst_ref=outs[a].at[me],
                    send_sem=send_sems.at[a * (NDEV - 1) + k], recv_sem=recv_sems.at[a * (NDEV - 1) + k],
                    device_id=(px, py, pc), device_id_type=pl.DeviceIdType.MESH)
                cp.start()
                arrival = pltpu.make_async_remote_copy(
                    src_ref=src, dst_ref=outs[a].at[peer],
                    send_sem=send_sems.at[a * (NDEV - 1) + k], recv_sem=recv_sems.at[a * (NDEV - 1) + k],
                    device_id=(px, py, pc), device_id_type=pl.DeviceIdType.MESH)
                sends.append((cp, arrival))
        for cp, arrival in sends:
            arrival.wait_recv()
            cp.wait_send()
        for cp in local:
            cp.wait()

    any_spec = pl.BlockSpec(memory_space=pl.ANY)
    return pl.pallas_call(
        body, name=name, out_shape=out_shape,
        in_specs=[any_spec] * n, out_specs=[any_spec] * n,
        scratch_shapes=[pltpu.SemaphoreType.DMA((n * (NDEV - 1),)), pltpu.SemaphoreType.DMA((n * (NDEV - 1),)),
                        pltpu.SemaphoreType.DMA((n,))],
    )(*arrs)


def _matmul(name, pairs, mode, out_dtype, M, N, K, tm=None, tn=None, tk=None):
    tm = tm or _tile(M, 1024)
    tn = tn or _tile(N, 1536 if mode == 'tn' else 1024)
    tk = tk or _tile(K, 1024 if mode == 'tn' else 2048)
    nk = K // tk
    dims = {'nn': NN, 'nt': NT, 'tn': TN}[mode]
    in_specs, ops = [], []
    for a, b, ao, bo in pairs:
        if mode == 'tn':
            assert ao[0] % tk == 0 and ao[1] % tm == 0
            sa = pl.BlockSpec((tk, tm), lambda i, j, k, r=ao[0] // tk, c=ao[1] // tm: (k + r, i + c))
        else:
            assert ao[0] % tm == 0 and ao[1] % tk == 0
            sa = pl.BlockSpec((tm, tk), lambda i, j, k, r=ao[0] // tm, c=ao[1] // tk: (i + r, k + c))
        if mode == 'nt':
            assert bo[0] % tn == 0 and bo[1] % tk == 0
            sb = pl.BlockSpec((tn, tk), lambda i, j, k, r=bo[0] // tn, c=bo[1] // tk: (j + r, k + c))
        else:
            assert bo[0] % tk == 0 and bo[1] % tn == 0
            sb = pl.BlockSpec((tk, tn), lambda i, j, k, r=bo[0] // tk, c=bo[1] // tn: (k + r, j + c))
        in_specs += [sa, sb]
        ops += [a, b]
    npairs = len(pairs)

    def body(*refs):
        o_ref = refs[2 * npairs]
        part = _dot(refs[0][...], refs[1][...], dims)
        for p in range(1, npairs):
            part = part + _dot(refs[2 * p][...], refs[2 * p + 1][...], dims)
        if nk == 1:
            o_ref[...] = part.astype(out_dtype)
            return
        acc = refs[2 * npairs + 1]
        k = pl.program_id(2)

        @pl.when(k == 0)
        def _():
            acc[...] = part

        @pl.when(k > 0)
        def _():
            acc[...] += part

        @pl.when(k == nk - 1)
        def _():
            o_ref[...] = acc[...].astype(out_dtype)

    return pl.pallas_call(
        body, name=name, grid=(M // tm, N // tn, nk),
        in_specs=in_specs, out_specs=pl.BlockSpec((tm, tn), lambda i, j, k: (i, j)),
        out_shape=jax.ShapeDtypeStruct((M, N), out_dtype),
        scratch_shapes=[] if nk == 1 else [pltpu.VMEM((tm, tn), F32)],
        compiler_params=_params(),
    )(*ops)


def _mm(name, a, b, mode, out_dtype):
    if mode == 'nn':
        (M, K), N = a.shape, b.shape[1]
    elif mode == 'nt':
        (M, K), N = a.shape, b.shape[0]
    else:
        (K, M), N = a.shape, b.shape[1]
    return _matmul(name, [(a, b, (0, 0), (0, 0))], mode, out_dtype, M, N, K)


def _ffn_up(h, wgu):
    T, D = h.shape
    F = wgu.shape[1] // 2
    tm, tn = _tile(T, 1024), _tile(F, 512)

    def body(h_ref, wg_ref, wu_ref, g_ref, u_ref, a_ref):
        hv = h_ref[...]
        g = _dot(hv, wg_ref[...], NN)
        u = _dot(hv, wu_ref[...], NN)
        g_ref[...] = g
        u_ref[...] = u
        a_ref[...] = (g * _sigmoid(g) * u).astype(BF16)

    out = pl.BlockSpec((tm, tn), lambda i, j: (i, j))
    return pl.pallas_call(
        body, name="ffn_up", grid=(T // tm, F // tn),
        in_specs=[pl.BlockSpec((tm, D), lambda i, j: (i, 0)),
                  pl.BlockSpec((D, tn), lambda i, j: (0, j)),
                  pl.BlockSpec((D, tn), lambda i, j, o=F // tn: (0, j + o))],
        out_specs=[out, out, out],
        out_shape=[jax.ShapeDtypeStruct((T, F), F32), jax.ShapeDtypeStruct((T, F), F32),
                   jax.ShapeDtypeStruct((T, F), BF16)],
        compiler_params=_params(),
    )(h, wgu, wgu)


def _ffn_dact(dy, wd, g, u):
    T, D = dy.shape
    F = wd.shape[0]
    tm, tn = _tile(T, 1024), _tile(F, 512)

    def body(dy_ref, wd_ref, g_ref, u_ref, dg_ref, du_ref):
        da = _dot(dy_ref[...], wd_ref[...], NT)
        g = g_ref[...]
        sg = _sigmoid(g)
        dg_ref[...] = (da * u_ref[...] * (sg * (1.0 + g * (1.0 - sg)))).astype(BF16)
        du_ref[...] = (da * (g * sg)).astype(BF16)

    blk = pl.BlockSpec((tm, tn), lambda i, j: (i, j))
    return pl.pallas_call(
        body, name="ffn_dact", grid=(T // tm, F // tn),
        in_specs=[pl.BlockSpec((tm, D), lambda i, j: (i, 0)), pl.BlockSpec((tn, D), lambda i, j: (j, 0)), blk, blk],
        out_specs=[blk, blk],
        out_shape=[jax.ShapeDtypeStruct((T, F), BF16)] * 2,
        compiler_params=_params(),
    )(dy, wd, g, u)


def _rstd(v):
    return lax.rsqrt(jnp.mean(v * v, axis=-1, keepdims=True) + EPS)


def _pre_fwd(x, g, sc, sh):
    T, D = x.shape
    tr = _tile(T, ROW_TILE, 8)

    def body(x_ref, g_ref, sc_ref, sh_ref, h_ref):
        xv = x_ref[...]
        r = xv * _rstd(xv) * g_ref[...]
        h_ref[...] = (r * (1.0 + sc_ref[...]) + sh_ref[...]).astype(BF16)

    row = pl.BlockSpec((tr, D), lambda i: (i, 0))
    return pl.pallas_call(
        body, name="pre_fwd", grid=(T // tr,),
        in_specs=[row, _vec_spec(D), _vec_spec(D), _vec_spec(D)], out_specs=row,
        out_shape=jax.ShapeDtypeStruct((T, D), BF16), compiler_params=_params(),
    )(x, g, sc, sh)


def _post_fwd(x, y, g, gate):
    T, D = x.shape
    tr = _tile(T, ROW_TILE, 8)

    def body(x_ref, y_ref, g_ref, gate_ref, o_ref):
        yv = y_ref[...]
        o_ref[...] = x_ref[...] + gate_ref[...] * (yv * _rstd(yv) * g_ref[...])

    row = pl.BlockSpec((tr, D), lambda i: (i, 0))
    return pl.pallas_call(
        body, name="post_fwd", grid=(T // tr,),
        in_specs=[row, row, _vec_spec(D), _vec_spec(D)], out_specs=row,
        out_shape=jax.ShapeDtypeStruct((T, D), F32), compiler_params=_params(),
    )(x, y, g, gate)


def _post_bwd(dx, y, g, gate):
    T, D = dx.shape
    tr = _tile(T, ROW_TILE, 8)

    def body(dx_ref, y_ref, g_ref, gate_ref, dy_ref, dgate_ref, dg_ref):
        @pl.when(pl.program_id(0) == 0)
        def _():
            dgate_ref[...] = jnp.zeros_like(dgate_ref)
            dg_ref[...] = jnp.zeros_like(dg_ref)

        yv, dxv = y_ref[...], dx_ref[...]
        rstd = _rstd(yv)
        yh = yv * rstd
        dgate_ref[...] += jnp.sum(dxv * (yh * g_ref[...]), axis=0, keepdims=True)
        dn = dxv * gate_ref[...]
        dg_ref[...] += jnp.sum(dn * yh, axis=0, keepdims=True)
        dyh = dn * g_ref[...]
        dy_ref[...] = (rstd * (dyh - yh * jnp.mean(dyh * yh, axis=-1, keepdims=True))).astype(BF16)

    row = pl.BlockSpec((tr, D), lambda i: (i, 0))
    vec = jax.ShapeDtypeStruct((1, D), F32)
    return pl.pallas_call(
        body, name="post_bwd", grid=(T // tr,),
        in_specs=[row, row, _vec_spec(D), _vec_spec(D)], out_specs=[row, _vec_spec(D), _vec_spec(D)],
        out_shape=[jax.ShapeDtypeStruct((T, D), BF16), vec, vec], compiler_params=_params(),
    )(dx, y, g, gate)


def _pre_bwd(dh, x, g, sc, dx_res):
    T, D = x.shape
    tr = _tile(T, ROW_TILE, 8)

    def body(dh_ref, x_ref, g_ref, sc_ref, dxr_ref, dx_ref, dsh_ref, dsc_ref, dg_ref):
        @pl.when(pl.program_id(0) == 0)
        def _():
            dsh_ref[...] = jnp.zeros_like(dsh_ref)
            dsc_ref[...] = jnp.zeros_like(dsc_ref)
            dg_ref[...] = jnp.zeros_like(dg_ref)

        xv, dhv = x_ref[...], dh_ref[...]
        rstd = _rstd(xv)
        xh = xv * rstd
        dsh_ref[...] += jnp.sum(dhv, axis=0, keepdims=True)
        dsc_ref[...] += jnp.sum(dhv * (xh * g_ref[...]), axis=0, keepdims=True)
        dr = dhv * (1.0 + sc_ref[...])
        dg_ref[...] += jnp.sum(dr * xh, axis=0, keepdims=True)
        dxh = dr * g_ref[...]
        dx_ref[...] = dxr_ref[...] + rstd * (dxh - xh * jnp.mean(dxh * xh, axis=-1, keepdims=True))

    row = pl.BlockSpec((tr, D), lambda i: (i, 0))
    vec = jax.ShapeDtypeStruct((1, D), F32)
    return pl.pallas_call(
        body, name="pre_bwd", grid=(T // tr,),
        in_specs=[row, row, _vec_spec(D), _vec_spec(D), row],
        out_specs=[row, _vec_spec(D), _vec_spec(D), _vec_spec(D)],
        out_shape=[jax.ShapeDtypeStruct((T, D), F32), vec, vec, vec], compiler_params=_params(),
    )(dh, x, g, sc, dx_res)


def _loss(x, target):
    T, D = x.shape
    tr = _tile(T, ROW_TILE, 8)

    def body(x_ref, t_ref, dx_ref, l_ref):
        @pl.when(pl.program_id(0) == 0)
        def _():
            l_ref[...] = jnp.zeros_like(l_ref)

        e = x_ref[...] - t_ref[...]
        dx_ref[...] = e / D
        rows = jnp.sum(e * e, axis=-1, keepdims=True)
        l_ref[...] += jnp.broadcast_to(jnp.sum(rows, axis=0, keepdims=True), (1, LANES))

    row = pl.BlockSpec((tr, D), lambda i: (i, 0))
    return pl.pallas_call(
        body, name="loss", grid=(T // tr,),
        in_specs=[row, row], out_specs=[row, _vec_spec(LANES)],
        out_shape=[jax.ShapeDtypeStruct((T, D), F32), jax.ShapeDtypeStruct((1, LANES), F32)],
        compiler_params=_params(),
    )(x, target)


def _split3(x):
    hi = x.astype(BF16)
    r = x - hi.astype(F32)
    mid = r.astype(BF16)
    lo = (r - mid.astype(F32)).astype(BF16)
    return hi, mid, lo


def _tri_sum(tri, x):
    hi, mid, lo = _split3(x)
    return _dot(tri, hi, NN) + _dot(tri, mid, NN) + _dot(tri, lo, NN)


def _fox_gate_fwd(fg, bf, H):
    T = fg.shape[0]
    tb = _tile(T, 512)

    def body(fg_ref, bf_ref, cum_ref, rep_ref, carry):
        @pl.when(pl.program_id(0) == 0)
        def _():
            carry[...] = jnp.zeros_like(carry)

        z = fg_ref[...] + bf_ref[...]
        logf = jnp.minimum(z, 0.0) - jnp.log(1.0 + jnp.exp(-jnp.abs(z)))
        row = lax.broadcasted_iota(jnp.int32, (tb, tb), 0)
        col = lax.broadcasted_iota(jnp.int32, (tb, tb), 1)
        cum = _tri_sum((row >= col).astype(BF16), logf) + carry[...]
        cum_ref[...] = cum
        carry[...] = cum_ref[pl.ds(tb - 1, 1), :]
        lane = lax.broadcasted_iota(jnp.int32, (tb, LANES), 1)
        for h in range(H):
            colv = jnp.sum(jnp.where(lane == h, cum, 0.0), axis=-1, keepdims=True)
            rep_ref[h] = jnp.broadcast_to(colv, (tb, LANES))

    return pl.pallas_call(
        body, name="fox_gate_fwd", grid=(T // tb,),
        in_specs=[pl.BlockSpec((tb, LANES), lambda i: (i, 0)), _vec_spec(LANES)],
        out_specs=[pl.BlockSpec((tb, LANES), lambda i: (i, 0)), pl.BlockSpec((H, tb, LANES), lambda i: (0, i, 0))],
        out_shape=[jax.ShapeDtypeStruct((T, LANES), F32), jax.ShapeDtypeStruct((H, T, LANES), F32)],
        scratch_shapes=[pltpu.VMEM((1, LANES), F32)], compiler_params=_params(),
    )(fg, bf)


def _fox_gate_bwd(dcum, fg, bf):
    T = fg.shape[0]
    tb = _tile(T, 512)
    nb = T // tb

    def body(dc_ref, fg_ref, bf_ref, dfg_ref, dbf_ref, carry):
        @pl.when(pl.program_id(0) == 0)
        def _():
            carry[...] = jnp.zeros_like(carry)
            dbf_ref[...] = jnp.zeros_like(dbf_ref)

        row = lax.broadcasted_iota(jnp.int32, (tb, tb), 0)
        col = lax.broadcasted_iota(jnp.int32, (tb, tb), 1)
        dc = dc_ref[...]
        dlogf = _tri_sum((row <= col).astype(BF16), dc) + carry[...]
        z = fg_ref[...] + bf_ref[...]
        dfg = dlogf * _sigmoid(-z)
        dfg_ref[...] = dfg
        dbf_ref[...] += jnp.sum(dfg, axis=0, keepdims=True)
        carry[...] += jnp.sum(dc, axis=0, keepdims=True)

    rev = pl.BlockSpec((tb, LANES), lambda i: (nb - 1 - i, 0))
    return pl.pallas_call(
        body, name="fox_gate_bwd", grid=(nb,),
        in_specs=[rev, rev, _vec_spec(LANES)], out_specs=[rev, _vec_spec(LANES)],
        out_shape=[jax.ShapeDtypeStruct((T, LANES), F32), jax.ShapeDtypeStruct((1, LANES), F32)],
        scratch_shapes=[pltpu.VMEM((1, LANES), F32)], compiler_params=_params(),
    )(dcum, fg, bf)


def _fox_blocks(T):
    tb = 512 if T >= 2048 else BLOCK
    return tb, T // tb


def _fox_fwd(qkv, cq_rep, ck):
    T = qkv.shape[0]
    D = qkv.shape[1] // 3
    H = D // FOX_DH
    tb, nb = _fox_blocks(T)
    pairs = [(i, j) for i in range(nb) for j in range(i + 1)]
    qi = np.array([p[0] for p in pairs], np.int32)
    kj = np.array([p[1] for p in pairs], np.int32)
    nrep = tb // LANES

    def body(qi_ref, kj_ref, q_ref, k_ref, v_ref, cq_ref, ck_ref, o_ref, obf_ref, lse_ref, m_sc, l_sc, acc_sc):
        p = pl.program_id(1)
        i, j = qi_ref[p], kj_ref[p]

        @pl.when(j == 0)
        def _():
            m_sc[...] = jnp.full_like(m_sc, NEG_INF)
            l_sc[...] = jnp.zeros_like(l_sc)
            acc_sc[...] = jnp.zeros_like(acc_sc)

        s = _dot(q_ref[...], k_ref[...], NT) * FOX_SCALE
        s = s + _rep(cq_ref[...], nrep) - ck_ref[...]
        row = lax.broadcasted_iota(jnp.int32, (tb, tb), 0)
        col = lax.broadcasted_iota(jnp.int32, (tb, tb), 1)
        s = jnp.where(jnp.logical_or(j < i, row >= col), s, NEG_INF)
        m_prev = m_sc[...]
        m_new = jnp.maximum(m_prev, jnp.max(s, axis=-1, keepdims=True))
        alpha = jnp.exp(m_prev - m_new)
        pm = jnp.exp(s - _rep(m_new, nrep))
        l_sc[...] = alpha * l_sc[...] + jnp.sum(pm, axis=-1, keepdims=True)
        acc_sc[...] = alpha * acc_sc[...] + _dot(pm.astype(BF16), v_ref[...], NN)
        m_sc[...] = m_new

        @pl.when(j == i)
        def _():
            o = acc_sc[...] / l_sc[...]
            o_ref[...] = o
            obf_ref[...] = o.astype(BF16)
            lse_ref[...] = m_sc[...] + jnp.log(l_sc[...])

    grid_spec = pltpu.PrefetchScalarGridSpec(
        num_scalar_prefetch=2, grid=(H, len(pairs)),
        in_specs=[pl.BlockSpec((tb, FOX_DH), lambda h, p, qi, kj: (qi[p], h)),
                  pl.BlockSpec((tb, FOX_DH), lambda h, p, qi, kj: (kj[p], H + h)),
                  pl.BlockSpec((tb, FOX_DH), lambda h, p, qi, kj: (kj[p], 2 * H + h)),
                  pl.BlockSpec((None, tb, LANES), lambda h, p, qi, kj: (h, qi[p], 0)),
                  pl.BlockSpec((None, 1, tb), lambda h, p, qi, kj: (h, 0, kj[p]))],
        out_specs=[pl.BlockSpec((tb, FOX_DH), lambda h, p, qi, kj: (qi[p], h)),
                   pl.BlockSpec((tb, FOX_DH), lambda h, p, qi, kj: (qi[p], h)),
                   pl.BlockSpec((None, tb, LANES), lambda h, p, qi, kj: (h, qi[p], 0))],
        scratch_shapes=[pltpu.VMEM((tb, LANES), F32), pltpu.VMEM((tb, LANES), F32), pltpu.VMEM((tb, FOX_DH), F32)])
    return pl.pallas_call(
        body, name="fox_fwd", grid_spec=grid_spec,
        out_shape=[jax.ShapeDtypeStruct((T, D), F32), jax.ShapeDtypeStruct((T, D), BF16),
                   jax.ShapeDtypeStruct((H, T, LANES), F32)],
        compiler_params=_params(),
    )(qi, kj, qkv, qkv, qkv, cq_rep, ck)


def _fox_bwd_prep(do, o, lse_rep, cq_rep):
    T, D = do.shape
    H = D // FOX_DH
    tr = _tile(T, ROW_TILE, 8)

    def body(do_ref, o_ref, lse_ref, cq_ref, dob_ref, delta_ref, cql_ref):
        dov = do_ref[...]
        dob_ref[...] = dov.astype(BF16)
        prod = dov * o_ref[...]
        for h in range(H):
            d = jnp.sum(prod[:, h * FOX_DH:(h + 1) * FOX_DH], axis=-1, keepdims=True)
            delta_ref[h] = jnp.broadcast_to(d, (tr, LANES))
        cql_ref[...] = cq_ref[...] - lse_ref[...]

    row = pl.BlockSpec((tr, D), lambda i: (i, 0))
    rep = pl.BlockSpec((H, tr, LANES), lambda i: (0, i, 0))
    return pl.pallas_call(
        body, name="fox_bwd_prep", grid=(T // tr,),
        in_specs=[row, row, rep, rep], out_specs=[row, rep, rep],
        out_shape=[jax.ShapeDtypeStruct((T, D), BF16), jax.ShapeDtypeStruct((H, T, LANES), F32),
                   jax.ShapeDtypeStruct((H, T, LANES), F32)],
        compiler_params=_params(),
    )(do, o, lse_rep, cq_rep)


def _fox_bwd(qkv, dob, delta_rep, cql_rep, ck):
    T = qkv.shape[0]
    D = qkv.shape[1] // 3
    H = D // FOX_DH
    tb, nb = _fox_blocks(T)
    pairs = [(i, j) for j in range(nb) for i in range(j, nb)]
    qi = np.array([p[0] for p in pairs], np.int32)
    kj = np.array([p[1] for p in pairs], np.int32)
    npairs = len(pairs)
    nrep = tb // LANES

    def body(qi_ref, kj_ref, q_ref, k_ref, v_ref, do_ref, delta_ref, cql_ref, ck_ref,
             dq_ref, dk_ref, dv_ref, dck_ref, dcq_ref, dq_acc, dk_acc, dv_acc, dc_acc):
        p = pl.program_id(1)
        i, j = qi_ref[p], kj_ref[p]

        @pl.when(p == 0)
        def _():
            dq_acc[...] = jnp.zeros_like(dq_acc)
            dcq_ref[...] = jnp.zeros_like(dcq_ref)

        @pl.when(i == j)
        def _():
            dk_acc[...] = jnp.zeros_like(dk_acc)
            dv_acc[...] = jnp.zeros_like(dv_acc)
            dc_acc[...] = jnp.zeros_like(dc_acc)

        q, k, v, dov = q_ref[...], k_ref[...], v_ref[...], do_ref[...]
        s = _dot(q, k, NT) * FOX_SCALE + _rep(cql_ref[...], nrep) - ck_ref[...]
        row = lax.broadcasted_iota(jnp.int32, (tb, tb), 0)
        col = lax.broadcasted_iota(jnp.int32, (tb, tb), 1)
        s = jnp.where(jnp.logical_or(j < i, row >= col), s, NEG_INF)
        pm = jnp.exp(s)
        dv_acc[...] += _dot(pm.astype(BF16), dov, TN)
        dp = _dot(dov, v, NT)
        ds = pm * (dp - _rep(delta_ref[...], nrep))
        dsb = (ds * FOX_SCALE).astype(BF16)
        dk_acc[...] += _dot(dsb, q, TN)
        rows = pl.ds(pl.multiple_of(i * tb, tb), tb)
        dq_acc[rows, :] += _dot(dsb, k, NN)
        dc_acc[...] -= jnp.sum(ds, axis=0, keepdims=True)
        dcq_ref[rows, :] += jnp.broadcast_to(jnp.sum(ds, axis=-1, keepdims=True), (tb, LANES))

        @pl.when(i == nb - 1)
        def _():
            dk_ref[...] = dk_acc[...].astype(BF16)
            dv_ref[...] = dv_acc[...].astype(BF16)
            dck_ref[...] = dc_acc[...]

        @pl.when(p == npairs - 1)
        def _():
            dq_ref[...] = dq_acc[...].astype(BF16)

    qrep = pl.BlockSpec((None, tb, LANES), lambda h, p, qi, kj: (h, qi[p], 0))
    kblk = pl.BlockSpec((tb, FOX_DH), lambda h, p, qi, kj: (kj[p], h))
    grid_spec = pltpu.PrefetchScalarGridSpec(
        num_scalar_prefetch=2, grid=(H, npairs),
        in_specs=[pl.BlockSpec((tb, FOX_DH), lambda h, p, qi, kj: (qi[p], h)),
                  pl.BlockSpec((tb, FOX_DH), lambda h, p, qi, kj: (kj[p], H + h)),
                  pl.BlockSpec((tb, FOX_DH), lambda h, p, qi, kj: (kj[p], 2 * H + h)),
                  pl.BlockSpec((tb, FOX_DH), lambda h, p, qi, kj: (qi[p], h)),
                  qrep, qrep,
                  pl.BlockSpec((None, 1, tb), lambda h, p, qi, kj: (h, 0, kj[p]))],
        out_specs=[pl.BlockSpec((T, FOX_DH), lambda h, p, qi, kj: (0, h)), kblk, kblk,
                   pl.BlockSpec((None, 1, tb), lambda h, p, qi, kj: (h, 0, kj[p])),
                   pl.BlockSpec((None, T, LANES), lambda h, p, qi, kj: (h, 0, 0))],
        scratch_shapes=[pltpu.VMEM((T, FOX_DH), F32), pltpu.VMEM((tb, FOX_DH), F32),
                        pltpu.VMEM((tb, FOX_DH), F32), pltpu.VMEM((1, tb), F32)])
    act = jax.ShapeDtypeStruct((T, D), BF16)
    return pl.pallas_call(
        body, name="fox_bwd", grid_spec=grid_spec,
        out_shape=[act, act, act, jax.ShapeDtypeStruct((H, 1, T), F32), jax.ShapeDtypeStruct((H, T, LANES), F32)],
        compiler_params=_params(),
    )(qi, kj, qkv, qkv, qkv, dob, delta_rep, cql_rep, ck)


def _sgu_rows(T):
    return 2 * BLOCK if T % (2 * BLOCK) == 0 else BLOCK


def _sgu_norm(zv, g_ref, b_ref):
    vv = _gelu(zv)
    mu = jnp.mean(vv, axis=-1, keepdims=True)
    cen = vv - mu
    rstd = lax.rsqrt(jnp.mean(cen * cen, axis=-1, keepdims=True) + EPS)
    vh = cen * rstd
    return vh, rstd, vh * g_ref[...] + b_ref[...]


def _sgu_fwd(zpre, ln_g, ln_b, wsm, bs_rep):
    T = zpre.shape[0]
    W = zpre.shape[1] // 2
    G = W // BLOCK
    tr = _sgu_rows(T)

    def body(z_ref, g_ref, b_ref, ws_ref, bs_ref, o_ref):
        u = _gelu(z_ref[:, :W])
        _, _, vln = _sgu_norm(z_ref[:, W:], g_ref, b_ref)
        for c in range(tr // BLOCK):
            rows = slice(c * BLOCK, (c + 1) * BLOCK)
            for gi in range(G):
                cols = slice(gi * BLOCK, (gi + 1) * BLOCK)
                f = _dot(ws_ref[gi], vln[rows, cols].astype(BF16), NN) + bs_ref[gi]
                o_ref[rows, cols] = (u[rows, cols] * f).astype(BF16)

    full3 = pl.BlockSpec((G, BLOCK, BLOCK), lambda i: (0, 0, 0))
    return pl.pallas_call(
        body, name="sgu_fwd", grid=(T // tr,),
        in_specs=[pl.BlockSpec((tr, 2 * W), lambda i: (i, 0)), _vec_spec(W), _vec_spec(W), full3, full3],
        out_specs=pl.BlockSpec((tr, W), lambda i: (i, 0)),
        out_shape=jax.ShapeDtypeStruct((T, W), BF16), compiler_params=_params(),
    )(zpre, ln_g, ln_b, wsm, bs_rep)


def _sgu_bwd(dgt, zpre, ln_g, ln_b, wsm, wsmT, bs_rep):
    T = zpre.shape[0]
    W = zpre.shape[1] // 2
    G = W // BLOCK
    tr = BLOCK

    def body(dgt_ref, z_ref, g_ref, b_ref, ws_ref, wst_ref, bs_ref,
             dz_ref, dws_ref, dbs_ref, dlg_ref, dlb_ref, du_sc, dvln_sc):
        @pl.when(pl.program_id(0) == 0)
        def _():
            dws_ref[...] = jnp.zeros_like(dws_ref)
            dbs_ref[...] = jnp.zeros_like(dbs_ref)
            dlg_ref[...] = jnp.zeros_like(dlg_ref)
            dlb_ref[...] = jnp.zeros_like(dlb_ref)

        zu = z_ref[:, :W]
        zv = z_ref[:, W:]
        u = _gelu(zu)
        vh, rstd, vln = _sgu_norm(zv, g_ref, b_ref)
        dgtv = dgt_ref[...]
        trow = lax.broadcasted_iota(jnp.int32, (BLOCK, BLOCK), 0)
        tcol = lax.broadcasted_iota(jnp.int32, (BLOCK, BLOCK), 1)
        causal = trow >= tcol
        for c in range(tr // BLOCK):
            rows = slice(c * BLOCK, (c + 1) * BLOCK)
            for gi in range(G):
                cols = slice(gi * BLOCK, (gi + 1) * BLOCK)
                vb = vln[rows, cols].astype(BF16)
                f = _dot(ws_ref[gi], vb, NN) + bs_ref[gi]
                d = dgtv[rows, cols]
                du_sc[rows, cols] = d * f
                df = d * u[rows, cols]
                dfb = df.astype(BF16)
                dvln_sc[rows, cols] = _dot(wst_ref[gi], dfb, NN)
                dws_ref[gi] += jnp.where(causal, _dot(dfb, vb, NT), 0.0)
                dbs_ref[gi] += jnp.broadcast_to(jnp.sum(df, axis=-1, keepdims=True), (BLOCK, BLOCK))
        dvln = dvln_sc[...]
        dlg_ref[...] += jnp.sum(dvln * vh, axis=0, keepdims=True)
        dlb_ref[...] += jnp.sum(dvln, axis=0, keepdims=True)
        dvh = dvln * g_ref[...]
        dvv = rstd * (dvh - jnp.mean(dvh, axis=-1, keepdims=True)
                      - vh * jnp.mean(dvh * vh, axis=-1, keepdims=True))
        dz_ref[:, :W] = (du_sc[...] * _gelu_grad(zu)).astype(BF16)
        dz_ref[:, W:] = (dvv * _gelu_grad(zv)).astype(BF16)

    full3 = pl.BlockSpec((G, BLOCK, BLOCK), lambda i: (0, 0, 0))
    vec = jax.ShapeDtypeStruct((1, W), F32)
    acc3 = jax.ShapeDtypeStruct((G, BLOCK, BLOCK), F32)
    return pl.pallas_call(
        body, name="sgu_bwd", grid=(T // tr,),
        in_specs=[pl.BlockSpec((tr, W), lambda i: (i, 0)), pl.BlockSpec((tr, 2 * W), lambda i: (i, 0)),
                  _vec_spec(W), _vec_spec(W), full3, full3, full3],
        out_specs=[pl.BlockSpec((tr, 2 * W), lambda i: (i, 0)), full3, full3, _vec_spec(W), _vec_spec(W)],
        out_shape=[jax.ShapeDtypeStruct((T, 2 * W), BF16), acc3, acc3, vec, vec],
        scratch_shapes=[pltpu.VMEM((tr, W), F32), pltpu.VMEM((tr, W), F32)],
        compiler_params=_params(),
    )(dgt, zpre, ln_g, ln_b, wsm, wsmT, bs_rep)


def _rope(x, cos_t, sin_t):
    T, N = x.shape
    tr = _tile(T, ROW_TILE, 8)
    nrep = N // LANES
    half = ROPE_DIM // 2

    def body(x_ref, c_ref, s_ref, o_ref):
        xv = x_ref[...]
        lane = jnp.bitwise_and(lax.broadcasted_iota(jnp.int32, (tr, N), 1), SWA_DH - 1)
        partner = jnp.where(lane < half, -pltpu.roll(xv, N - half, 1), pltpu.roll(xv, half, 1))
        o_ref[...] = (xv * _rep(c_ref[...], nrep) + partner * _rep(s_ref[...], nrep)).astype(BF16)

    tab = pl.BlockSpec((tr, LANES), lambda i: (i, 0))
    row = pl.BlockSpec((tr, N), lambda i: (i, 0))
    return pl.pallas_call(
        body, name="rope", grid=(T // tr,), in_specs=[row, tab, tab], out_specs=row,
        out_shape=jax.ShapeDtypeStruct((T, N), BF16), compiler_params=_params(),
    )(x, cos_t, sin_t)


def _swa_tiles(T):
    sb = 4 if T >= 2048 else 2
    return sb, BLOCK * sb, T // (BLOCK * sb)


def _band_mask():
    row = lax.broadcasted_iota(jnp.int32, (BLOCK, 2 * BLOCK), 0)
    col = lax.broadcasted_iota(jnp.int32, (BLOCK, 2 * BLOCK), 1)
    return jnp.logical_and(col > row, col <= row + BLOCK), col


def _swa_specs(T, G):
    sb, tq, nq = _swa_tiles(T)
    q = pl.BlockSpec((None, tq, LANES), lambda h, i: (h, i, 0))
    kc = pl.BlockSpec((None, tq, LANES), lambda h, i: (h // G, i, 0))
    kp = pl.BlockSpec((None, BLOCK, LANES), lambda h, i: (h // G, jnp.maximum(i * sb - 1, 0), 0))
    return q, kc, kp


def _swa_band(b, i, kc_ref, kp_ref, vc_ref, vp_ref):
    rows = slice(b * BLOCK, (b + 1) * BLOCK)
    prev = slice((b - 1) * BLOCK, b * BLOCK)
    kprev = kp_ref[...] if b == 0 else kc_ref[prev, :]
    vprev = vp_ref[...] if b == 0 else vc_ref[prev, :]
    K = jnp.concatenate([kprev, kc_ref[rows, :]], axis=0)
    V = jnp.concatenate([vprev, vc_ref[rows, :]], axis=0)
    band, col = _band_mask()
    if b == 0:
        band = jnp.logical_and(band, jnp.logical_or(col >= BLOCK, i > 0))
    return rows, K, V, band


def _swa_fwd(qp, kp, vp, sinks):
    Hq, T, _ = qp.shape
    G = Hq // kp.shape[0]
    sb, tq, nq = _swa_tiles(T)

    def body(sink_ref, q_ref, kc_ref, kp_ref, vc_ref, vp_ref, o_ref, lse_ref):
        h, i = pl.program_id(0), pl.program_id(1)
        sink = sink_ref[h]
        for b in range(sb):
            rows, K, V, band = _swa_band(b, i, kc_ref, kp_ref, vc_ref, vp_ref)
            s = jnp.where(band, _dot(q_ref[rows, :], K, NT) * SWA_SCALE, NEG_INF)
            m = jnp.maximum(jnp.max(s, axis=-1, keepdims=True), sink)
            pm = jnp.exp(s - m)
            den = jnp.sum(pm, axis=-1, keepdims=True) + jnp.exp(sink - m)
            o_ref[rows, :] = _dot((pm / den).astype(BF16), V, NN)
            lse_ref[rows, :] = jnp.broadcast_to(m + jnp.log(den), (BLOCK, LANES))

    q, kc, kpv = _swa_specs(T, G)
    out = jax.ShapeDtypeStruct((Hq, T, LANES), F32)
    return pl.pallas_call(
        body, name="swa_fwd", grid=(Hq, nq),
        in_specs=[pl.BlockSpec(memory_space=pltpu.SMEM), q, kc, kpv, kc, kpv], out_specs=[q, q],
        out_shape=[out, out], compiler_params=_params(),
    )(sinks, qp, kp, kp, vp, vp)


def _swa_bwd_dq(qp, kp, vp, dop, op, lse_rep, sinks):
    Hq, T, _ = qp.shape
    G = Hq // kp.shape[0]
    sb, tq, nq = _swa_tiles(T)

    def body(sink_ref, q_ref, kc_ref, kp_ref, vc_ref, vp_ref, do_ref, o_ref, lse_ref, dq_ref, dsink_ref):
        h, i = pl.program_id(0), pl.program_id(1)
        sink = sink_ref[h]

        @pl.when(i == 0)
        def _():
            dsink_ref[...] = jnp.zeros_like(dsink_ref)

        for b in range(sb):
            rows, K, V, band = _swa_band(b, i, kc_ref, kp_ref, vc_ref, vp_ref)
            dov = do_ref[rows, :]
            delta = jnp.sum(dov * o_ref[rows, :], axis=-1, keepdims=True)
            lse = lse_ref[rows, :]
            s = jnp.where(band, _dot(q_ref[rows, :], K, NT) * SWA_SCALE, NEG_INF)
            pm = jnp.exp(s - _rep(lse, 2))
            dp = _dot(dov.astype(BF16), V, NT)
            ds = pm * (dp - delta)
            dq_ref[rows, :] = _dot((ds * SWA_SCALE).astype(BF16), K, NN)
            part = jnp.sum(jnp.exp(sink - lse) * delta, axis=0, keepdims=True)
            dsink_ref[...] -= jnp.broadcast_to(part, (8, LANES))

    q, kc, kpv = _swa_specs(T, G)
    return pl.pallas_call(
        body, name="swa_bwd_dq", grid=(Hq, nq),
        in_specs=[pl.BlockSpec(memory_space=pltpu.SMEM), q, kc, kpv, kc, kpv, q, q, q],
        out_specs=[q, pl.BlockSpec((None, 8, LANES), lambda h, i: (h, 0, 0))],
        out_shape=[jax.ShapeDtypeStruct((Hq, T, LANES), F32), jax.ShapeDtypeStruct((Hq, 8, LANES), F32)],
        compiler_params=_params(),
    )(sinks, qp, kp, kp, vp, vp, dop, op, lse_rep)


def _swa_bwd_dkv(qp, kp, vp, dop, op, lse_rep):
    Hq, T, _ = qp.shape
    Hk = kp.shape[0]
    G = Hq // Hk
    sb, tq, nq = _swa_tiles(T)
    nblk = T // BLOCK

    def body(k_ref, v_ref, q_ref, qn_ref, do_ref, don_ref, o_ref, on_ref, lse_ref, lsen_ref, dk_ref, dv_ref):
        i = pl.program_id(1)
        trow = lax.broadcasted_iota(jnp.int32, (2 * BLOCK, BLOCK), 0)
        scol = lax.broadcasted_iota(jnp.int32, (2 * BLOCK, BLOCK), 1)
        band0 = jnp.logical_and(trow >= scol, trow < scol + BLOCK)
        for b in range(sb):
            rows = slice(b * BLOCK, (b + 1) * BLOCK)
            nxt = slice((b + 1) * BLOCK, (b + 2) * BLOCK)
            last = b == sb - 1
            band = band0
            if last:
                band = jnp.logical_and(band0, jnp.logical_or(trow < BLOCK, i < nq - 1))
            kb, vb = k_ref[rows, :], v_ref[rows, :]
            dk = jnp.zeros((BLOCK, LANES), F32)
            dv = jnp.zeros((BLOCK, LANES), F32)
            for g in range(G):
                def two(cur, nx):
                    return jnp.concatenate([cur[g, rows, :], nx[g] if last else cur[g, nxt, :]], axis=0)
                Q, dov, ov, lse = two(q_ref, qn_ref), two(do_ref, don_ref), two(o_ref, on_ref), two(lse_ref, lsen_ref)
                delta = jnp.sum(dov * ov, axis=-1, keepdims=True)
                s = jnp.where(band, _dot(Q, kb, NT) * SWA_SCALE, NEG_INF)
                pm = jnp.exp(s - lse)
                dob = dov.astype(BF16)
                dv = dv + _dot(pm.astype(BF16), dob, TN)
                ds = pm * (_dot(dob, vb, NT) - delta)
                dk = dk + _dot((ds * SWA_SCALE).astype(BF16), Q, TN)
            dk_ref[rows, :] = dk
            dv_ref[rows, :] = dv

    kspec = pl.BlockSpec((None, tq, LANES), lambda h, i: (h, i, 0))
    cur = pl.BlockSpec((G, tq, LANES), lambda h, i: (h, i, 0))
    nxt = pl.BlockSpec((G, BLOCK, LANES), lambda h, i: (h, jnp.minimum((i + 1) * sb, nblk - 1), 0))
    out = jax.ShapeDtypeStruct((Hk, T, LANES), F32)
    return pl.pallas_call(
        body, name="swa_bwd_dkv", grid=(Hk, nq),
        in_specs=[kspec, kspec, cur, nxt, cur, nxt, cur, nxt, cur, nxt], out_specs=[kspec, kspec],
        out_shape=[out, out], compiler_params=_params(),
    )(kp, vp, qp, qp, dop, dop, op, op, lse_rep, lse_rep)


def _to_heads(a, nh):
    T = a.shape[0]
    a = a.reshape(T, nh, SWA_DH).transpose(1, 0, 2)
    return jnp.pad(a, ((0, 0), (0, 0), (0, LANES - SWA_DH)))


def _from_heads(a):
    nh, T, _ = a.shape
    return a[:, :, :SWA_DH].transpose(1, 0, 2).reshape(T, nh * SWA_DH)


def _adam(g, w, m, v):
    m2 = ADAM_B1 * m + (1.0 - ADAM_B1) * g
    v2 = ADAM_B2 * v + (1.0 - ADAM_B2) * (g * g)
    m_hat = m2 / (1.0 - ADAM_B1 ** ADAM_STEP)
    v_hat = v2 / (1.0 - ADAM_B2 ** ADAM_STEP)
    delta = -ADAM_LR * (m_hat / (jnp.sqrt(v_hat) + ADAM_EPS) + ADAM_WD * w)
    return delta, m2, v2


def _ada_fwd(c_all, w, b):
    L, D, n = w.shape
    tn = _tile(n, 768)

    def body(c_ref, w_ref, b_ref, o_ref):
        cv = c_ref[...]
        ca = (cv * _sigmoid(cv)).astype(BF16)
        o_ref[...] = _dot(ca, w_ref[...].astype(BF16), NN) + b_ref[...]

    return pl.pallas_call(
        body, name="ada_fwd", grid=(L, n // tn),
        in_specs=[pl.BlockSpec((NDEV, D), lambda l, j: (0, 0)), pl.BlockSpec((None, D, tn), lambda l, j: (l, 0, j)),
                  pl.BlockSpec((None, 1, tn), lambda l, j: (l, 0, j))],
        out_specs=pl.BlockSpec((None, NDEV, tn), lambda l, j: (l, 0, j)),
        out_shape=jax.ShapeDtypeStruct((L, NDEV, n), F32), compiler_params=_params(),
    )(c_all, w, b)


def _ada_update(c_rep, dm, w, m, v):
    L, D, n = w.shape
    tr = _tile(D, 256, 8)
    nrep = n // LANES

    def body(c_ref, dm_ref, w_ref, m_ref, v_ref, g_ref, d_ref, m2_ref, v2_ref):
        g = jnp.zeros((tr, n), F32)
        for b in range(NDEV):
            cv = c_ref[b]
            g = g + _rep(cv * _sigmoid(cv), nrep) * dm_ref[pl.ds(b, 1), :]
        g_ref[...] = g
        d_ref[...], m2_ref[...], v2_ref[...] = _adam(g, w_ref[...], m_ref[...], v_ref[...])

    blk = pl.BlockSpec((None, tr, n), lambda l, i: (l, i, 0))
    out = jax.ShapeDtypeStruct((L, D, n), F32)
    return pl.pallas_call(
        body, name="ada_update", grid=(L, D // tr),
        in_specs=[pl.BlockSpec((NDEV, tr, LANES), lambda l, i: (0, i, 0)),
                  pl.BlockSpec((None, NDEV, n), lambda l, i: (l, 0, 0)), blk, blk, blk],
        out_specs=[blk, blk, blk, blk], out_shape=[out, out, out, out], compiler_params=_params(),
    )(c_rep, dm, w, m, v)


def _adamw(name, parts, w, m, v, layer):
    P, R, C = parts.shape
    cpad = -(-C // LANES) * LANES
    per_row = cpad * (P * parts.dtype.itemsize + 7 * 4) * 2
    tr = _tile(R, max(8, (24 * 1024 * 1024 // per_row) // 8 * 8), 8)

    def body(p_ref, w_ref, m_ref, v_ref, g_ref, d_ref, m2_ref, v2_ref):
        g = p_ref[0].astype(F32)
        for s in range(1, P):
            g = g + p_ref[s].astype(F32)
        g_ref[...] = g
        d_ref[...], m2_ref[...], v2_ref[...] = _adam(g, w_ref[...], m_ref[...], v_ref[...])

    stk = pl.BlockSpec((None, tr, C), lambda i: (layer, i, 0))
    blk = pl.BlockSpec((tr, C), lambda i: (i, 0))
    out = jax.ShapeDtypeStruct((R, C), F32)
    return pl.pallas_call(
        body, name=name, grid=(R // tr,),
        in_specs=[pl.BlockSpec((P, tr, C), lambda i: (0, i, 0)), stk, stk, stk],
        out_specs=[blk, blk, blk, blk], out_shape=[out, out, out, out], compiler_params=_params(),
    )(parts, w, m, v)


def _colcat(a):
    s, k, n = a.shape
    return a.transpose(1, 0, 2).reshape(k, s * n)


def _colsplit(a):
    k, n8 = a.shape
    return a.reshape(k, NDEV, n8 // NDEV).transpose(1, 0, 2)


def _rowsplit(a):
    r, c = a.shape
    return a.reshape(NDEV, r // NDEV, c)


def kernel(x, c, positions, ada_w, ada_b, mix_pre_g, mix_post_g, ffn_pre_g, ffn_post_g, ffn_w_gu, ffn_w_down, fox_w_in, fox_b_f, fox_w_out, sgu_w_in, sgu_ln_g, sgu_ln_b, sgu_w_s, sgu_b_s, sgu_w_out, swa_w_in, swa_sinks, swa_w_out, loss_target, m_ada_w, m_ada_b, m_mix_pre_g, m_mix_post_g, m_ffn_pre_g, m_ffn_post_g, m_ffn_w_gu, m_ffn_w_down, m_fox_w_in, m_fox_b_f, m_fox_w_out, m_sgu_w_in, m_sgu_ln_g, m_sgu_ln_b, m_sgu_w_s, m_sgu_b_s, m_sgu_w_out, m_swa_w_in, m_swa_sinks, m_swa_w_out, v_ada_w, v_ada_b, v_mix_pre_g, v_mix_post_g, v_ffn_pre_g, v_ffn_post_g, v_ffn_w_gu, v_ffn_w_down, v_fox_w_in, v_fox_b_f, v_fox_w_out, v_sgu_w_in, v_sgu_ln_g, v_sgu_ln_b, v_sgu_w_s, v_sgu_b_s, v_sgu_w_out, v_swa_w_in, v_swa_sinks, v_swa_w_out):
    env = locals()
    W = {n: env[n] for n in WEIGHTS}
    M = {n: env["m_" + n] for n in WEIGHTS}
    V = {n: env["v_" + n] for n in WEIGHTS}

    me = 4 * lax.axis_index("x") + 2 * lax.axis_index("y") + lax.axis_index("c")
    _, T, D = x.shape
    L = ada_w.shape[0]
    n_ada = ada_w.shape[2]
    F = ffn_w_gu.shape[2] * NDEV // 2
    H = D // FOX_DH
    Hq = D // SWA_DH
    x0 = x[0]
    mixer = {0: 'fox', 1: 'sgu', 2: 'swa'}

    c_all = _comm("gather_c", [c], 'gather')[0].reshape(NDEV, D)
    ada_b_mine = lax.dynamic_slice_in_dim(ada_b, me * n_ada, n_ada, axis=1).reshape(L, 1, n_ada)
    mod_cols = _ada_fwd(c_all, ada_w, ada_b_mine)
    mod = _comm("a2a_mod", [mod_cols.transpose(1, 0, 2)], 'a2a')[0]
    mod = mod.transpose(1, 0, 2).reshape(L, 6, 1, D)

    inv = ROPE_THETA ** (-jnp.arange(0, ROPE_DIM, 2, dtype=F32) / ROPE_DIM)
    ang = positions[0].astype(F32)[:, None] * inv
    pad1 = jnp.ones((T, SWA_DH - ROPE_DIM), F32)
    cos64 = jnp.concatenate([jnp.cos(ang), jnp.cos(ang), pad1], axis=1)
    sin64 = jnp.concatenate([jnp.sin(ang), jnp.sin(ang), 0.0 * pad1], axis=1)
    cos_t = jnp.concatenate([cos64, cos64], axis=1)
    sin_t = jnp.concatenate([sin64, sin64], axis=1)

    def gather_weights(i):
        kind, j = mixer[i % 3], i // 3
        got = _comm("gather_w%d" % i, [ffn_w_gu[i].astype(BF16), ffn_w_down[i].astype(BF16),
                                        W[kind + '_w_in'][j].astype(BF16), W[kind + '_w_out'][j].astype(BF16)], 'gather')
        return dict(wgu=_colcat(got[0]), wd=got[1].reshape(F, D), win=_colcat(got[2]), wout=got[3].reshape(D, D))

    saved = []
    xc = x0
    for i in range(L):
        kind, j = mixer[i % 3], i // 3
        wt = gather_weights(i)
        s = dict(wt=wt, x_in=xc)
        sh_m, sc_m, g_m, sh_f, sc_f, g_f = [mod[i, t] for t in range(6)]
        h = _pre_fwd(xc, mix_pre_g[i:i + 1], sc_m, sh_m)
        s['h'] = h
        if kind == 'fox':
            wqkv = wt['win'][:, :3 * D]
            wf = jnp.pad(wt['win'][:, 3 * D:], ((0, 0), (0, LANES - H)))
            s['win_pad'] = jnp.concatenate([wqkv, wf], axis=1)
            bf = jnp.pad(fox_b_f[j:j + 1], ((0, 0), (0, LANES - H)))
            qkv = _mm("fox_qkv", h, wqkv, 'nn', BF16)
            fg = _mm("fox_fg", h, wf, 'nn', F32)
            cum, cq_rep = _fox_gate_fwd(fg, bf, H)
            ck = cum[:, :H].T.reshape(H, 1, T)
            o, obf, lse_rep = _fox_fwd(qkv, cq_rep, ck)
            s.update(qkv=qkv, fg=fg, bf=bf, cq_rep=cq_rep, ck=ck, o=o, lse_rep=lse_rep, mix_out=obf)
        elif kind == 'sgu':
            G = D // BLOCK
            causal = jnp.tril(jnp.ones((BLOCK, BLOCK), bool))
            wsm = jnp.where(causal[None], sgu_w_s[j], 0.0).astype(BF16)
            bs_rep = jnp.broadcast_to(sgu_b_s[j][:, :, None], (G, BLOCK, BLOCK))
            zpre = _mm("sgu_in", h, wt['win'], 'nn', F32)
            gated = _sgu_fwd(zpre, sgu_ln_g[j:j + 1], sgu_ln_b[j:j + 1], wsm, bs_rep)
            s.update(zpre=zpre, wsm=wsm, bs_rep=bs_rep, mix_out=gated)
        else:
            Hk = (wt['win'].shape[1] // SWA_DH - Hq) // 2
            proj = _mm("swa_in", h, wt['win'], 'nn', F32)
            qr = _rope(proj[:, :Hq * SWA_DH], cos_t, sin_t)
            kr = _rope(proj[:, Hq * SWA_DH:(Hq + Hk) * SWA_DH], cos_t, sin_t)
            qp, kp = _to_heads(qr, Hq), _to_heads(kr, Hk)
            vp = _to_heads(proj[:, (Hq + Hk) * SWA_DH:].astype(BF16), Hk)
            op, lse_rep = _swa_fwd(qp, kp, vp, swa_sinks[j])
            s.update(qp=qp, kp=kp, vp=vp, op=op, lse_rep=lse_rep, Hk=Hk, mix_out=_from_heads(op).astype(BF16))
        y = _mm("mix_out", s['mix_out'], wt['wout'], 'nn', F32)
        x_mid = _post_fwd(xc, y, mix_post_g[i:i + 1], g_m)
        s.update(y_mix=y, x_mid=x_mid)
        h2 = _pre_fwd(x_mid, ffn_pre_g[i:i + 1], sc_f, sh_f)
        g, u, a = _ffn_up(h2, wt['wgu'])
        y2 = _mm("ffn_down", a, wt['wd'], 'nn', F32)
        xc = _post_fwd(x_mid, y2, ffn_post_g[i:i + 1], g_f)
        s.update(h2=h2, g=g, u=u, a=a, y_ffn=y2)
        saved.append(s)

    dx, lsum = _loss(xc, loss_target[0])
    loss = lax.psum(0.5 * lsum[0, 0] / D, AXES)

    small = {n: [None] * W[n].shape[0] for n in SMALL}
    dmod = [None] * L
    out = {n: dict(g=[None] * W[n].shape[0], d=[None] * W[n].shape[0], m=[None] * W[n].shape[0],
                   v=[None] * W[n].shape[0]) for n in WEIGHTS}

    for i in reversed(range(L)):
        kind, j = mixer[i % 3], i // 3
        s = saved[i]
        wt = s['wt']
        sh_m, sc_m, g_m, sh_f, sc_f, g_f = [mod[i, t] for t in range(6)]
        dy2, dg_f, dpost_f = _post_bwd(dx, s['y_ffn'], ffn_post_g[i:i + 1], g_f)
        dwd = _mm("ffn_dwd", s['a'], dy2, 'tn', BF16)
        dg, du = _ffn_dact(dy2, wt['wd'], s['g'], s['u'])
        dwgu = jnp.concatenate([_mm("ffn_dwg", s['h2'], dg, 'tn', BF16), _mm("ffn_dwu", s['h2'], du, 'tn', BF16)], axis=1)
        dh2 = _matmul("ffn_dh", [(dg, wt['wgu'], (0, 0), (0, 0)), (du, wt['wgu'], (0, 0), (0, F))], 'nt', F32, T, D, F,
                      tk=_tile(F, 1024))
        dx, dsh_f, dsc_f, dpre_f = _pre_bwd(dh2, s['x_mid'], ffn_pre_g[i:i + 1], sc_f, dx)
        dy, dg_m, dpost_m = _post_bwd(dx, s['y_mix'], mix_post_g[i:i + 1], g_m)
        dwout = _mm("mix_dwout", s['mix_out'], dy, 'tn', BF16)
        dmix = _mm("mix_dout", dy, wt['wout'], 'nt', F32)
        if kind == 'fox':
            dob, delta_rep, cql_rep = _fox_bwd_prep(dmix, s['o'], s['lse_rep'], s['cq_rep'])
            dq, dk, dv, dck, dcq = _fox_bwd(s['qkv'], dob, delta_rep, cql_rep, s['ck'])
            dcum = jnp.pad((dck.reshape(H, T) + dcq[:, :, 0]).T, ((0, 0), (0, LANES - H)))
            dfg, dbf = _fox_gate_bwd(dcum, s['fg'], s['bf'])
            small['fox_b_f'][j] = dbf[0, :H]
            dproj = jnp.concatenate([dq, dk, dv, dfg.astype(BF16)], axis=1)
            dwin = _mm("fox_dwin", s['h'], dproj, 'tn', BF16)[:, :3 * D + H]
            dh = _mm("fox_dh", dproj, s['win_pad'], 'nt', F32)
        elif kind == 'sgu':
            wsmT = s['wsm'].transpose(0, 2, 1)
            dz, dws, dbs, dlg, dlb = _sgu_bwd(dmix, s['zpre'], sgu_ln_g[j:j + 1], sgu_ln_b[j:j + 1], s['wsm'], wsmT, s['bs_rep'])
            small['sgu_w_s'][j], small['sgu_b_s'][j] = dws, dbs[:, :, 0]
            small['sgu_ln_g'][j], small['sgu_ln_b'][j] = dlg[0], dlb[0]
            dwin = _mm("sgu_dwin", s['h'], dz, 'tn', BF16)
            dh = _mm("sgu_dh", dz, wt['win'], 'nt', F32)
        else:
            Hk = s['Hk']
            dop = _to_heads(dmix, Hq)
            dqp, dsink = _swa_bwd_dq(s['qp'], s['kp'], s['vp'], dop, s['op'], s['lse_rep'], swa_sinks[j])
            dkp, dvp = _swa_bwd_dkv(s['qp'], s['kp'], s['vp'], dop, s['op'], s['lse_rep'])
            small['swa_sinks'][j] = dsink[:, 0, 0]
            dproj = jnp.concatenate([_rope(_from_heads(dqp), cos_t, -sin_t), _rope(_from_heads(dkp), cos_t, -sin_t),
                                     _from_heads(dvp).astype(BF16)], axis=1)
            dwin = _mm("swa_dwin", s['h'], dproj, 'tn', BF16)
            dh = _mm("swa_dh", dproj, wt['win'], 'nt', F32)
        dx, dsh_m, dsc_m, dpre_m = _pre_bwd(dh, s['x_in'], mix_pre_g[i:i + 1], sc_m, dx)
        small['mix_pre_g'][i], small['mix_post_g'][i] = dpre_m[0], dpost_m[0]
        small['ffn_pre_g'][i], small['ffn_post_g'][i] = dpre_f[0], dpost_f[0]
        dmod[i] = jnp.concatenate([dsh_m, dsc_m, dg_m, dsh_f, dsc_f, dg_f], axis=1)[0]

        recv = _comm("a2a_grad%d" % i, [_colsplit(dwgu), _rowsplit(dwd), _colsplit(dwin), _rowsplit(dwout)], 'a2a')
        for name, parts, idx in (('ffn_w_gu', recv[0], i), ('ffn_w_down', recv[1], i),
                                 (kind + '_w_in', recv[2], j), (kind + '_w_out', recv[3], j)):
            res = _adamw("adamw_" + name, parts, W[name], M[name], V[name], idx)
            for key, val in zip("gdmv", res):
                out[name][key][idx] = val

    grad_x = dx[None]

    small['ada_b'] = dmod
    flat = jnp.concatenate([jnp.stack(small[n]).reshape(-1) for n in SMALL])
    width = 8 * LANES
    npad = -flat.shape[0] % (8 * width)
    packed = jnp.pad(flat, (0, npad)).reshape(-1, width)
    parts = _comm("gather_small", [packed], 'gather')[0]

    def pack(d):
        f = jnp.concatenate([d[n].reshape(-1) for n in SMALL])
        return jnp.pad(f, (0, npad)).reshape(1, -1, width)

    res = _adamw("adamw_small", parts, pack(W), pack(M), pack(V), 0)
    off = 0
    for n in SMALL:
        size = W[n].size
        for key, val in zip("gdmv", res):
            out[n][key] = val.reshape(-1)[off:off + size].reshape(W[n].shape)
        off += size

    dmod_all = parts.reshape(NDEV, -1)[:, :L * 6 * D].reshape(NDEV, L, 6 * D)
    dm = lax.dynamic_slice_in_dim(dmod_all, me * n_ada, n_ada, axis=2).transpose(1, 0, 2)
    c_rep = jnp.broadcast_to(c_all[:, :, None], (NDEV, D, LANES))
    for key, val in zip("gdmv", _ada_update(c_rep, dm, ada_w, m_ada_w, v_ada_w)):
        out['ada_w'][key] = val

    def leaf(n, key):
        val = out[n][key]
        return jnp.stack(val) if isinstance(val, list) else val

    return (loss, grad_x, *[leaf(n, 'g') for n in WEIGHTS], *[leaf(n, 'd') for n in WEIGHTS],
            *[leaf(n, 'm') for n in WEIGHTS], *[leaf(n, 'v') for n in WEIGHTS])
```

```python
import numpy as np
import jax
import jax.numpy as jnp
from jax import lax
from jax.experimental import pallas as pl
from jax.experimental.pallas import tpu as pltpu

F32 = jnp.float32
BF16 = jnp.bfloat16
NDEV = 8
AXES = ("x", "y", "c")
LANES = 128
VMEM_LIMIT_BYTES = 48 * 1024 * 1024
NEG_INF = float("-inf")
ROW_TILE = 256

EPS = 1e-6
BLOCK = 128
FOX_DH = 128
FOX_HPS = 2
SWA_DH = 64
ROPE_DIM = 16
ROPE_THETA = 500000.0
FOX_SCALE = FOX_DH ** -0.5
SWA_SCALE = SWA_DH ** -0.5
GELU_C0 = 0.7978845608028654
GELU_C1 = 0.044715

ADAM_LR = 0.001
ADAM_B1 = 0.9
ADAM_B2 = 0.999
ADAM_EPS = 1e-08
ADAM_WD = 0.01
ADAM_STEP = 10

NN = (((1,), (0,)), ((), ()))
NT = (((1,), (1,)), ((), ()))
TN = (((0,), (0,)), ((), ()))

WEIGHTS = ['ada_w', 'ada_b', 'mix_pre_g', 'mix_post_g', 'ffn_pre_g', 'ffn_post_g', 'ffn_w_gu', 'ffn_w_down',
           'fox_w_in', 'fox_b_f', 'fox_w_out', 'sgu_w_in', 'sgu_ln_g', 'sgu_ln_b', 'sgu_w_s', 'sgu_b_s',
           'sgu_w_out', 'swa_w_in', 'swa_sinks', 'swa_w_out']
SMALL = ['ada_b', 'mix_pre_g', 'mix_post_g', 'ffn_pre_g', 'ffn_post_g', 'fox_b_f', 'sgu_ln_g', 'sgu_ln_b',
         'sgu_w_s', 'sgu_b_s', 'swa_sinks']


def _dot(a, b, dims):
    return lax.dot_general(a, b, dims, preferred_element_type=F32)


def _tile(n, pref, mult=LANES):
    t = (min(pref, n) // mult) * mult
    while t >= mult:
        if n % t == 0:
            return t
        t -= mult
    return n


def _params():
    return pltpu.CompilerParams(vmem_limit_bytes=VMEM_LIMIT_BYTES)


def _rep(a, n):
    return a if n == 1 else jnp.concatenate([a] * n, axis=-1)


def _vec_spec(d):
    return pl.BlockSpec((1, d), lambda *_: (0, 0))


def _sigmoid(z):
    return 1.0 / (1.0 + jnp.exp(-z))


def _gelu(z):
    t = jnp.tanh(GELU_C0 * (z + GELU_C1 * z * z * z))
    return 0.5 * z * (1.0 + t)


def _gelu_grad(z):
    t = jnp.tanh(GELU_C0 * (z + GELU_C1 * z * z * z))
    return 0.5 * (1.0 + t) + 0.5 * z * (1.0 - t * t) * GELU_C0 * (1.0 + 3.0 * GELU_C1 * z * z)


def _comm_out_shapes(arrs, gather):
    return [jax.ShapeDtypeStruct(((NDEV,) + a.shape) if gather else a.shape, a.dtype) for a in arrs]


def _comm_sems(n):
    return [pltpu.SemaphoreType.DMA((n,)), pltpu.SemaphoreType.DMA((n,)), pltpu.SemaphoreType.DMA((n,))]


def _me():
    x, y, c = lax.axis_index("x"), lax.axis_index("y"), lax.axis_index("c")
    return x, y, c, 4 * x + 2 * y + c


def _comm_start(ins, outs, gather, send_sems, recv_sems, local_sems):
    x, y, c, me = _me()
    for a in range(len(ins)):
        pltpu.make_async_copy(ins[a] if gather else ins[a].at[me], outs[a].at[me], local_sems.at[a]).start()
        for bits in range(1, NDEV):
            px = (1 - x) if bits & 4 else x
            py = (1 - y) if bits & 2 else y
            pc = (1 - c) if bits & 1 else c
            pltpu.make_async_remote_copy(
                src_ref=ins[a] if gather else ins[a].at[4 * px + 2 * py + pc], dst_ref=outs[a].at[me],
                send_sem=send_sems.at[a], recv_sem=recv_sems.at[a],
                device_id=(px, py, pc), device_id_type=pl.DeviceIdType.MESH).start()


def _comm_wait(ins, outs, gather, send_sems, recv_sems, local_sems):
    x, y, c, me = _me()
    for a in range(len(ins)):
        seven = outs[a].at[pl.ds(0, NDEV - 1)]
        pltpu.make_async_remote_copy(src_ref=seven, dst_ref=seven, send_sem=send_sems.at[a], recv_sem=recv_sems.at[a],
                                     device_id=(x, y, c), device_id_type=pl.DeviceIdType.MESH).wait()
        pltpu.make_async_copy(ins[a] if gather else ins[a].at[me], outs[a].at[me], local_sems.at[a]).wait()


def _comm(name, arrs, kind):
    n = len(arrs)
    gather = kind == 'gather'

    def body(*refs):
        ins, outs, sems = refs[:n], refs[n:2 * n], refs[2 * n:]
        _comm_start(ins, outs, gather, *sems)
        _comm_wait(ins, outs, gather, *sems)

    any_spec = pl.BlockSpec(memory_space=pl.ANY)
    return pl.pallas_call(
        body, name=name, out_shape=_comm_out_shapes(arrs, gather),
        in_specs=[any_spec] * n, out_specs=[any_spec] * n, scratch_shapes=_comm_sems(n),
    )(*arrs)


def _matmul(name, pairs, mode, out_dtype, M, N, K, tm=None, tn=None, tk=None):
    tm = tm or _tile(M, 1024)
    tn = tn or _tile(N, 1536 if mode == 'tn' else 1024)
    tk = tk or _tile(K, 1024 if mode == 'tn' else 2048)
    nk = K // tk
    dims = {'nn': NN, 'nt': NT, 'tn': TN}[mode]
    in_specs, ops = [], []
    for a, b, ao, bo in pairs:
        if mode == 'tn':
            assert ao[0] % tk == 0 and ao[1] % tm == 0
            sa = pl.BlockSpec((tk, tm), lambda i, j, k, r=ao[0] // tk, c=ao[1] // tm: (k + r, i + c))
        else:
            assert ao[0] % tm == 0 and ao[1] % tk == 0
            sa = pl.BlockSpec((tm, tk), lambda i, j, k, r=ao[0] // tm, c=ao[1] // tk: (i + r, k + c))
        if mode == 'nt':
            assert bo[0] % tn == 0 and bo[1] % tk == 0
            sb = pl.BlockSpec((tn, tk), lambda i, j, k, r=bo[0] // tn, c=bo[1] // tk: (j + r, k + c))
        else:
            assert bo[0] % tk == 0 and bo[1] % tn == 0
            sb = pl.BlockSpec((tk, tn), lambda i, j, k, r=bo[0] // tk, c=bo[1] // tn: (k + r, j + c))
        in_specs += [sa, sb]
        ops += [a, b]
    npairs = len(pairs)

    def body(*refs):
        o_ref = refs[2 * npairs]
        part = _dot(refs[0][...], refs[1][...], dims)
        for p in range(1, npairs):
            part = part + _dot(refs[2 * p][...], refs[2 * p + 1][...], dims)
        if nk == 1:
            o_ref[...] = part.astype(out_dtype)
            return
        acc = refs[2 * npairs + 1]
        k = pl.program_id(2)

        @pl.when(k == 0)
        def _():
            acc[...] = part

        @pl.when(k > 0)
        def _():
            acc[...] += part

        @pl.when(k == nk - 1)
        def _():
            o_ref[...] = acc[...].astype(out_dtype)

    return pl.pallas_call(
        body, name=name, grid=(M // tm, N // tn, nk),
        in_specs=in_specs, out_specs=pl.BlockSpec((tm, tn), lambda i, j, k: (i, j)),
        out_shape=jax.ShapeDtypeStruct((M, N), out_dtype),
        scratch_shapes=[] if nk == 1 else [pltpu.VMEM((tm, tn), F32)],
        compiler_params=_params(),
    )(*ops)


def _mm(name, a, b, mode, out_dtype):
    if mode == 'nn':
        (M, K), N = a.shape, b.shape[1]
    elif mode == 'nt':
        (M, K), N = a.shape, b.shape[0]
    else:
        (K, M), N = a.shape, b.shape[1]
    return _matmul(name, [(a, b, (0, 0), (0, 0))], mode, out_dtype, M, N, K)


def _ffn_up(h, wgu):
    T, D = h.shape
    F = wgu.shape[1] // 2
    tm, tn = _tile(T, 1024), _tile(F, 512)

    def body(h_ref, wg_ref, wu_ref, g_ref, u_ref, a_ref):
        hv = h_ref[...]
        g = _dot(hv, wg_ref[...], NN)
        u = _dot(hv, wu_ref[...], NN)
        g_ref[...] = g
        u_ref[...] = u
        a_ref[...] = (g * _sigmoid(g) * u).astype(BF16)

    out = pl.BlockSpec((tm, tn), lambda i, j: (i, j))
    return pl.pallas_call(
        body, name="ffn_up", grid=(T // tm, F // tn),
        in_specs=[pl.BlockSpec((tm, D), lambda i, j: (i, 0)),
                  pl.BlockSpec((D, tn), lambda i, j: (0, j)),
                  pl.BlockSpec((D, tn), lambda i, j, o=F // tn: (0, j + o))],
        out_specs=[out, out, out],
        out_shape=[jax.ShapeDtypeStruct((T, F), F32), jax.ShapeDtypeStruct((T, F), F32),
                   jax.ShapeDtypeStruct((T, F), BF16)],
        compiler_params=_params(),
    )(h, wgu, wgu)


def _ffn_dact(dy, wd, g, u):
    T, D = dy.shape
    F = wd.shape[0]
    tm, tn = _tile(T, 1024), _tile(F, 512)

    def body(dy_ref, wd_ref, g_ref, u_ref, dg_ref, du_ref):
        da = _dot(dy_ref[...], wd_ref[...], NT)
        g = g_ref[...]
        sg = _sigmoid(g)
        dg_ref[...] = (da * u_ref[...] * (sg * (1.0 + g * (1.0 - sg)))).astype(BF16)
        du_ref[...] = (da * (g * sg)).astype(BF16)

    blk = pl.BlockSpec((tm, tn), lambda i, j: (i, j))
    return pl.pallas_call(
        body, name="ffn_dact", grid=(T // tm, F // tn),
        in_specs=[pl.BlockSpec((tm, D), lambda i, j: (i, 0)), pl.BlockSpec((tn, D), lambda i, j: (j, 0)), blk, blk],
        out_specs=[blk, blk],
        out_shape=[jax.ShapeDtypeStruct((T, F), BF16)] * 2,
        compiler_params=_params(),
    )(dy, wd, g, u)


def _rstd(v):
    return lax.rsqrt(jnp.mean(v * v, axis=-1, keepdims=True) + EPS)


def _pre_fwd(x, g, sc, sh):
    T, D = x.shape
    tr = _tile(T, ROW_TILE, 8)

    def body(x_ref, g_ref, sc_ref, sh_ref, h_ref):
        xv = x_ref[...]
        r = xv * _rstd(xv) * g_ref[...]
        h_ref[...] = (r * (1.0 + sc_ref[...]) + sh_ref[...]).astype(BF16)

    row = pl.BlockSpec((tr, D), lambda i: (i, 0))
    return pl.pallas_call(
        body, name="pre_fwd", grid=(T // tr,),
        in_specs=[row, _vec_spec(D), _vec_spec(D), _vec_spec(D)], out_specs=row,
        out_shape=jax.ShapeDtypeStruct((T, D), BF16), compiler_params=_params(),
    )(x, g, sc, sh)


def _post_fwd(x, y, g, gate):
    T, D = x.shape
    tr = _tile(T, ROW_TILE, 8)

    def body(x_ref, y_ref, g_ref, gate_ref, o_ref):
        yv = y_ref[...]
        o_ref[...] = x_ref[...] + gate_ref[...] * (yv * _rstd(yv) * g_ref[...])

    row = pl.BlockSpec((tr, D), lambda i: (i, 0))
    return pl.pallas_call(
        body, name="post_fwd", grid=(T // tr,),
        in_specs=[row, row, _vec_spec(D), _vec_spec(D)], out_specs=row,
        out_shape=jax.ShapeDtypeStruct((T, D), F32), compiler_params=_params(),
    )(x, y, g, gate)


def _post_bwd(dx, y, g, gate):
    T, D = dx.shape
    tr = _tile(T, ROW_TILE, 8)

    def body(dx_ref, y_ref, g_ref, gate_ref, dy_ref, dgate_ref, dg_ref):
        @pl.when(pl.program_id(0) == 0)
        def _():
            dgate_ref[...] = jnp.zeros_like(dgate_ref)
            dg_ref[...] = jnp.zeros_like(dg_ref)

        yv, dxv = y_ref[...], dx_ref[...]
        rstd = _rstd(yv)
        yh = yv * rstd
        dgate_ref[...] += jnp.sum(dxv * (yh * g_ref[...]), axis=0, keepdims=True)
        dn = dxv * gate_ref[...]
        dg_ref[...] += jnp.sum(dn * yh, axis=0, keepdims=True)
        dyh = dn * g_ref[...]
        dy_ref[...] = (rstd * (dyh - yh * jnp.mean(dyh * yh, axis=-1, keepdims=True))).astype(BF16)

    row = pl.BlockSpec((tr, D), lambda i: (i, 0))
    vec = jax.ShapeDtypeStruct((1, D), F32)
    return pl.pallas_call(
        body, name="post_bwd", grid=(T // tr,),
        in_specs=[row, row, _vec_spec(D), _vec_spec(D)], out_specs=[row, _vec_spec(D), _vec_spec(D)],
        out_shape=[jax.ShapeDtypeStruct((T, D), BF16), vec, vec], compiler_params=_params(),
    )(dx, y, g, gate)


def _pre_bwd(dh, x, g, sc, dx_res):
    T, D = x.shape
    tr = _tile(T, ROW_TILE, 8)

    def body(dh_ref, x_ref, g_ref, sc_ref, dxr_ref, dx_ref, dsh_ref, dsc_ref, dg_ref):
        @pl.when(pl.program_id(0) == 0)
        def _():
            dsh_ref[...] = jnp.zeros_like(dsh_ref)
            dsc_ref[...] = jnp.zeros_like(dsc_ref)
            dg_ref[...] = jnp.zeros_like(dg_ref)

        xv, dhv = x_ref[...], dh_ref[...]
        rstd = _rstd(xv)
        xh = xv * rstd
        dsh_ref[...] += jnp.sum(dhv, axis=0, keepdims=True)
        dsc_ref[...] += jnp.sum(dhv * (xh * g_ref[...]), axis=0, keepdims=True)
        dr = dhv * (1.0 + sc_ref[...])
        dg_ref[...] += jnp.sum(dr * xh, axis=0, keepdims=True)
        dxh = dr * g_ref[...]
        dx_ref[...] = dxr_ref[...] + rstd * (dxh - xh * jnp.mean(dxh * xh, axis=-1, keepdims=True))

    row = pl.BlockSpec((tr, D), lambda i: (i, 0))
    vec = jax.ShapeDtypeStruct((1, D), F32)
    return pl.pallas_call(
        body, name="pre_bwd", grid=(T // tr,),
        in_specs=[row, row, _vec_spec(D), _vec_spec(D), row],
        out_specs=[row, _vec_spec(D), _vec_spec(D), _vec_spec(D)],
        out_shape=[jax.ShapeDtypeStruct((T, D), F32), vec, vec, vec], compiler_params=_params(),
    )(dh, x, g, sc, dx_res)


def _loss(x, target):
    T, D = x.shape
    tr = _tile(T, ROW_TILE, 8)

    def body(x_ref, t_ref, dx_ref, l_ref):
        @pl.when(pl.program_id(0) == 0)
        def _():
            l_ref[...] = jnp.zeros_like(l_ref)

        e = x_ref[...] - t_ref[...]
        dx_ref[...] = e / D
        rows = jnp.sum(e * e, axis=-1, keepdims=True)
        l_ref[...] += jnp.broadcast_to(jnp.sum(rows, axis=0, keepdims=True), (1, LANES))

    row = pl.BlockSpec((tr, D), lambda i: (i, 0))
    return pl.pallas_call(
        body, name="loss", grid=(T // tr,),
        in_specs=[row, row], out_specs=[row, _vec_spec(LANES)],
        out_shape=[jax.ShapeDtypeStruct((T, D), F32), jax.ShapeDtypeStruct((1, LANES), F32)],
        compiler_params=_params(),
    )(x, target)


def _split3(x):
    hi = x.astype(BF16)
    r = x - hi.astype(F32)
    mid = r.astype(BF16)
    lo = (r - mid.astype(F32)).astype(BF16)
    return hi, mid, lo


def _tri_sum(tri, x):
    hi, mid, lo = _split3(x)
    return _dot(tri, hi, NN) + _dot(tri, mid, NN) + _dot(tri, lo, NN)


def _fox_gate_fwd(fg, bf, H):
    T = fg.shape[0]
    tb = _tile(T, 512)

    def body(fg_ref, bf_ref, cum_ref, rep_ref, carry):
        @pl.when(pl.program_id(0) == 0)
        def _():
            carry[...] = jnp.zeros_like(carry)

        z = fg_ref[...] + bf_ref[...]
        logf = jnp.minimum(z, 0.0) - jnp.log(1.0 + jnp.exp(-jnp.abs(z)))
        row = lax.broadcasted_iota(jnp.int32, (tb, tb), 0)
        col = lax.broadcasted_iota(jnp.int32, (tb, tb), 1)
        cum = _tri_sum((row >= col).astype(BF16), logf) + carry[...]
        cum_ref[...] = cum
        carry[...] = cum_ref[pl.ds(tb - 1, 1), :]
        lane = lax.broadcasted_iota(jnp.int32, (tb, LANES), 1)
        for h in range(H):
            colv = jnp.sum(jnp.where(lane == h, cum, 0.0), axis=-1, keepdims=True)
            rep_ref[h] = jnp.broadcast_to(colv, (tb, LANES))

    return pl.pallas_call(
        body, name="fox_gate_fwd", grid=(T // tb,),
        in_specs=[pl.BlockSpec((tb, LANES), lambda i: (i, 0)), _vec_spec(LANES)],
        out_specs=[pl.BlockSpec((tb, LANES), lambda i: (i, 0)), pl.BlockSpec((H, tb, LANES), lambda i: (0, i, 0))],
        out_shape=[jax.ShapeDtypeStruct((T, LANES), F32), jax.ShapeDtypeStruct((H, T, LANES), F32)],
        scratch_shapes=[pltpu.VMEM((1, LANES), F32)], compiler_params=_params(),
    )(fg, bf)


def _fox_gate_bwd(dcum, fg, bf):
    T = fg.shape[0]
    tb = _tile(T, 512)
    nb = T // tb

    def body(dc_ref, fg_ref, bf_ref, dfg_ref, dbf_ref, carry):
        @pl.when(pl.program_id(0) == 0)
        def _():
            carry[...] = jnp.zeros_like(carry)
            dbf_ref[...] = jnp.zeros_like(dbf_ref)

        row = lax.broadcasted_iota(jnp.int32, (tb, tb), 0)
        col = lax.broadcasted_iota(jnp.int32, (tb, tb), 1)
        dc = dc_ref[...]
        dlogf = _tri_sum((row <= col).astype(BF16), dc) + carry[...]
        z = fg_ref[...] + bf_ref[...]
        dfg = dlogf * _sigmoid(-z)
        dfg_ref[...] = dfg
        dbf_ref[...] += jnp.sum(dfg, axis=0, keepdims=True)
        carry[...] += jnp.sum(dc, axis=0, keepdims=True)

    rev = pl.BlockSpec((tb, LANES), lambda i: (nb - 1 - i, 0))
    return pl.pallas_call(
        body, name="fox_gate_bwd", grid=(nb,),
        in_specs=[rev, rev, _vec_spec(LANES)], out_specs=[rev, _vec_spec(LANES)],
        out_shape=[jax.ShapeDtypeStruct((T, LANES), F32), jax.ShapeDtypeStruct((1, LANES), F32)],
        scratch_shapes=[pltpu.VMEM((1, LANES), F32)], compiler_params=_params(),
    )(dcum, fg, bf)


def _fox_blocks(T):
    tb = 512 if T >= 2048 else BLOCK
    return tb, T // tb


def _fox_fwd(name, qkv, cq_rep, ck, gather=()):
    T = qkv.shape[0]
    D = qkv.shape[1] // 3
    H = D // FOX_DH
    hps = FOX_HPS
    ng, wl = H // hps, hps * FOX_DH
    tb, nb = _fox_blocks(T)
    pairs = [(i, j) for i in range(nb) for j in range(i + 1)]
    qi = np.array([p[0] for p in pairs], np.int32)
    kj = np.array([p[1] for p in pairs], np.int32)
    npairs = len(pairs)
    nrep = tb // LANES
    nc = len(gather)

    def body(qi_ref, kj_ref, q_ref, k_ref, v_ref, cq_ref, ck_ref, *rest):
        cin, (o_ref, obf_ref, lse_ref), cout = rest[:nc], rest[nc:nc + 3], rest[nc + 3:2 * nc + 3]
        m_sc, l_sc, acc_sc = rest[2 * nc + 3:2 * nc + 6]
        sems = rest[2 * nc + 6:]
        g, p = pl.program_id(0), pl.program_id(1)
        i, j = qi_ref[p], kj_ref[p]

        if nc:
            @pl.when(jnp.logical_and(g == 0, p == 0))
            def _():
                _comm_start(cin, cout, True, *sems)

        @pl.when(j == 0)
        def _():
            m_sc[...] = jnp.full_like(m_sc, NEG_INF)
            l_sc[...] = jnp.zeros_like(l_sc)
            acc_sc[...] = jnp.zeros_like(acc_sc)

        row = lax.broadcasted_iota(jnp.int32, (tb, tb), 0)
        col = lax.broadcasted_iota(jnp.int32, (tb, tb), 1)
        visible = jnp.logical_or(j < i, row >= col)
        for hh in range(hps):
            cols = slice(hh * FOX_DH, (hh + 1) * FOX_DH)
            s = _dot(q_ref[:, cols], k_ref[:, cols], NT) * FOX_SCALE
            s = jnp.where(visible, s + _rep(cq_ref[hh], nrep) - ck_ref[hh], NEG_INF)
            m_prev = m_sc[hh]
            m_new = jnp.maximum(m_prev, jnp.max(s, axis=-1, keepdims=True))
            alpha = jnp.exp(m_prev - m_new)
            pm = jnp.exp(s - _rep(m_new, nrep))
            l_sc[hh] = alpha * l_sc[hh] + jnp.sum(pm, axis=-1, keepdims=True)
            acc_sc[:, cols] = alpha * acc_sc[:, cols] + _dot(pm.astype(BF16), v_ref[:, cols], NN)
            m_sc[hh] = m_new

        @pl.when(j == i)
        def _():
            for hh in range(hps):
                cols = slice(hh * FOX_DH, (hh + 1) * FOX_DH)
                o = acc_sc[:, cols] / l_sc[hh]
                o_ref[:, cols] = o
                obf_ref[:, cols] = o.astype(BF16)
                lse_ref[hh] = m_sc[hh] + jnp.log(l_sc[hh])

        if nc:
            @pl.when(jnp.logical_and(g == ng - 1, p == npairs - 1))
            def _():
                _comm_wait(cin, cout, True, *sems)

    any_spec = pl.BlockSpec(memory_space=pl.ANY)
    qblk = pl.BlockSpec((tb, wl), lambda g, p, qi, kj: (qi[p], g))
    qrep = pl.BlockSpec((hps, tb, LANES), lambda g, p, qi, kj: (g, qi[p], 0))
    grid_spec = pltpu.PrefetchScalarGridSpec(
        num_scalar_prefetch=2, grid=(ng, npairs),
        in_specs=[qblk,
                  pl.BlockSpec((tb, wl), lambda g, p, qi, kj: (kj[p], ng + g)),
                  pl.BlockSpec((tb, wl), lambda g, p, qi, kj: (kj[p], 2 * ng + g)),
                  qrep,
                  pl.BlockSpec((hps, 1, tb), lambda g, p, qi, kj: (g, 0, kj[p]))] + [any_spec] * nc,
        out_specs=[qblk, qblk, qrep] + [any_spec] * nc,
        scratch_shapes=[pltpu.VMEM((hps, tb, LANES), F32), pltpu.VMEM((hps, tb, LANES), F32),
                        pltpu.VMEM((tb, wl), F32)] + (_comm_sems(nc) if nc else []))
    res = pl.pallas_call(
        body, name=name, grid_spec=grid_spec,
        out_shape=[jax.ShapeDtypeStruct((T, D), F32), jax.ShapeDtypeStruct((T, D), BF16),
                   jax.ShapeDtypeStruct((H, T, LANES), F32)] + _comm_out_shapes(gather, True),
        compiler_params=_params(),
    )(qi, kj, qkv, qkv, qkv, cq_rep, ck, *gather)
    return res[:3], res[3:]


def _fox_bwd_prep(do, o, lse_rep, cq_rep):
    T, D = do.shape
    H = D // FOX_DH
    tr = _tile(T, ROW_TILE, 8)

    def body(do_ref, o_ref, lse_ref, cq_ref, dob_ref, delta_ref, cql_ref):
        dov = do_ref[...]
        dob_ref[...] = dov.astype(BF16)
        prod = dov * o_ref[...]
        for h in range(H):
            d = jnp.sum(prod[:, h * FOX_DH:(h + 1) * FOX_DH], axis=-1, keepdims=True)
            delta_ref[h] = jnp.broadcast_to(d, (tr, LANES))
        cql_ref[...] = cq_ref[...] - lse_ref[...]

    row = pl.BlockSpec((tr, D), lambda i: (i, 0))
    rep = pl.BlockSpec((H, tr, LANES), lambda i: (0, i, 0))
    return pl.pallas_call(
        body, name="fox_bwd_prep", grid=(T // tr,),
        in_specs=[row, row, rep, rep], out_specs=[row, rep, rep],
        out_shape=[jax.ShapeDtypeStruct((T, D), BF16), jax.ShapeDtypeStruct((H, T, LANES), F32),
                   jax.ShapeDtypeStruct((H, T, LANES), F32)],
        compiler_params=_params(),
    )(do, o, lse_rep, cq_rep)


def _fox_bwd(name, qkv, dob, delta_rep, cql_rep, ck, a2a=()):
    T = qkv.shape[0]
    D = qkv.shape[1] // 3
    H = D // FOX_DH
    hps = FOX_HPS
    ng, wl = H // hps, hps * FOX_DH
    tb, nb = _fox_blocks(T)
    pairs = [(i, j) for j in range(nb) for i in range(j, nb)]
    qi = np.array([p[0] for p in pairs], np.int32)
    kj = np.array([p[1] for p in pairs], np.int32)
    npairs = len(pairs)
    nrep = tb // LANES
    nc = len(a2a)

    def body(qi_ref, kj_ref, q_ref, k_ref, v_ref, do_ref, delta_ref, cql_ref, ck_ref, *rest):
        cin, (dq_ref, dk_ref, dv_ref, dck_ref, dcq_ref), cout = rest[:nc], rest[nc:nc + 5], rest[nc + 5:2 * nc + 5]
        dq_acc, dk_acc, dv_acc, dc_acc = rest[2 * nc + 5:2 * nc + 9]
        sems = rest[2 * nc + 9:]
        g, p = pl.program_id(0), pl.program_id(1)
        i, j = qi_ref[p], kj_ref[p]

        @pl.when(jnp.logical_and(g == 0, p == 0))
        def _():
            dcq_ref[...] = jnp.zeros_like(dcq_ref)
            if nc:
                _comm_start(cin, cout, False, *sems)

        @pl.when(p == 0)
        def _():
            dq_acc[...] = jnp.zeros_like(dq_acc)

        @pl.when(i == j)
        def _():
            dk_acc[...] = jnp.zeros_like(dk_acc)
            dv_acc[...] = jnp.zeros_like(dv_acc)
            dc_acc[...] = jnp.zeros_like(dc_acc)

        row = lax.broadcasted_iota(jnp.int32, (tb, tb), 0)
        col = lax.broadcasted_iota(jnp.int32, (tb, tb), 1)
        visible = jnp.logical_or(j < i, row >= col)
        lane = lax.broadcasted_iota(jnp.int32, (tb, LANES), 1)
        rows = pl.ds(pl.multiple_of(i * tb, tb), tb)
        dcq = jnp.zeros((tb, LANES), F32)
        for hh in range(hps):
            cols = slice(hh * FOX_DH, (hh + 1) * FOX_DH)
            q, k, v, dov = q_ref[:, cols], k_ref[:, cols], v_ref[:, cols], do_ref[:, cols]
            s = _dot(q, k, NT) * FOX_SCALE + _rep(cql_ref[hh], nrep) - ck_ref[hh]
            pm = jnp.exp(jnp.where(visible, s, NEG_INF))
            dv_acc[:, cols] += _dot(pm.astype(BF16), dov, TN)
            ds = pm * (_dot(dov, v, NT) - _rep(delta_ref[hh], nrep))
            dsb = (ds * FOX_SCALE).astype(BF16)
            dk_acc[:, cols] += _dot(dsb, q, TN)
            dq_acc[rows, cols] += _dot(dsb, k, NN)
            dc_acc[hh] -= jnp.sum(ds, axis=0, keepdims=True)
            dcq = dcq + jnp.where(lane == g * hps + hh, jnp.sum(ds, axis=-1, keepdims=True), 0.0)
        dcq_ref[rows, :] += dcq

        @pl.when(i == nb - 1)
        def _():
            dk_ref[...] = dk_acc[...].astype(BF16)
            dv_ref[...] = dv_acc[...].astype(BF16)
            dck_ref[...] = dc_acc[...]

        @pl.when(p == npairs - 1)
        def _():
            dq_ref[...] = dq_acc[...].astype(BF16)

        if nc:
            @pl.when(jnp.logical_and(g == ng - 1, p == npairs - 1))
            def _():
                _comm_wait(cin, cout, False, *sems)

    any_spec = pl.BlockSpec(memory_space=pl.ANY)
    qblk = pl.BlockSpec((tb, wl), lambda g, p, qi, kj: (qi[p], g))
    qrep = pl.BlockSpec((hps, tb, LANES), lambda g, p, qi, kj: (g, qi[p], 0))
    kblk = pl.BlockSpec((tb, wl), lambda g, p, qi, kj: (kj[p], g))
    krow = pl.BlockSpec((hps, 1, tb), lambda g, p, qi, kj: (g, 0, kj[p]))
    grid_spec = pltpu.PrefetchScalarGridSpec(
        num_scalar_prefetch=2, grid=(ng, npairs),
        in_specs=[qblk,
                  pl.BlockSpec((tb, wl), lambda g, p, qi, kj: (kj[p], ng + g)),
                  pl.BlockSpec((tb, wl), lambda g, p, qi, kj: (kj[p], 2 * ng + g)),
                  qblk, qrep, qrep, krow] + [any_spec] * nc,
        out_specs=[pl.BlockSpec((T, wl), lambda g, p, qi, kj: (0, g)), kblk, kblk, krow,
                   pl.BlockSpec((T, LANES), lambda g, p, qi, kj: (0, 0))] + [any_spec] * nc,
        scratch_shapes=[pltpu.VMEM((T, wl), F32), pltpu.VMEM((tb, wl), F32), pltpu.VMEM((tb, wl), F32),
                        pltpu.VMEM((hps, 1, tb), F32)] + (_comm_sems(nc) if nc else []))
    act = jax.ShapeDtypeStruct((T, D), BF16)
    res = pl.pallas_call(
        body, name=name, grid_spec=grid_spec,
        out_shape=[act, act, act, jax.ShapeDtypeStruct((H, 1, T), F32),
                   jax.ShapeDtypeStruct((T, LANES), F32)] + _comm_out_shapes(a2a, False),
        compiler_params=_params(),
    )(qi, kj, qkv, qkv, qkv, dob, delta_rep, cql_rep, ck, *a2a)
    return res[:5], res[5:]


def _sgu_rows(T):
    return 2 * BLOCK if T % (2 * BLOCK) == 0 else BLOCK


def _sgu_norm(zv, g_ref, b_ref):
    vv = _gelu(zv)
    mu = jnp.mean(vv, axis=-1, keepdims=True)
    cen = vv - mu
    rstd = lax.rsqrt(jnp.mean(cen * cen, axis=-1, keepdims=True) + EPS)
    vh = cen * rstd
    return vh, rstd, vh * g_ref[...] + b_ref[...]


def _sgu_fwd(zpre, ln_g, ln_b, wsm, bs_rep):
    T = zpre.shape[0]
    W = zpre.shape[1] // 2
    G = W // BLOCK
    tr = _sgu_rows(T)

    def body(z_ref, g_ref, b_ref, ws_ref, bs_ref, o_ref):
        u = _gelu(z_ref[:, :W])
        _, _, vln = _sgu_norm(z_ref[:, W:], g_ref, b_ref)
        for c in range(tr // BLOCK):
            rows = slice(c * BLOCK, (c + 1) * BLOCK)
            for gi in range(G):
                cols = slice(gi * BLOCK, (gi + 1) * BLOCK)
                f = _dot(ws_ref[gi], vln[rows, cols].astype(BF16), NN) + bs_ref[gi]
                o_ref[rows, cols] = (u[rows, cols] * f).astype(BF16)

    full3 = pl.BlockSpec((G, BLOCK, BLOCK), lambda i: (0, 0, 0))
    return pl.pallas_call(
        body, name="sgu_fwd", grid=(T // tr,),
        in_specs=[pl.BlockSpec((tr, 2 * W), lambda i: (i, 0)), _vec_spec(W), _vec_spec(W), full3, full3],
        out_specs=pl.BlockSpec((tr, W), lambda i: (i, 0)),
        out_shape=jax.ShapeDtypeStruct((T, W), BF16), compiler_params=_params(),
    )(zpre, ln_g, ln_b, wsm, bs_rep)


def _sgu_bwd(dgt, zpre, ln_g, ln_b, wsm, wsmT, bs_rep):
    T = zpre.shape[0]
    W = zpre.shape[1] // 2
    G = W // BLOCK
    tr = BLOCK

    def body(dgt_ref, z_ref, g_ref, b_ref, ws_ref, wst_ref, bs_ref,
             dz_ref, dws_ref, dbs_ref, dlg_ref, dlb_ref, du_sc, dvln_sc):
        @pl.when(pl.program_id(0) == 0)
        def _():
            dws_ref[...] = jnp.zeros_like(dws_ref)
            dbs_ref[...] = jnp.zeros_like(dbs_ref)
            dlg_ref[...] = jnp.zeros_like(dlg_ref)
            dlb_ref[...] = jnp.zeros_like(dlb_ref)

        zu = z_ref[:, :W]
        zv = z_ref[:, W:]
        u = _gelu(zu)
        vh, rstd, vln = _sgu_norm(zv, g_ref, b_ref)
        dgtv = dgt_ref[...]
        trow = lax.broadcasted_iota(jnp.int32, (BLOCK, BLOCK), 0)
        tcol = lax.broadcasted_iota(jnp.int32, (BLOCK, BLOCK), 1)
        causal = trow >= tcol
        for c in range(tr // BLOCK):
            rows = slice(c * BLOCK, (c + 1) * BLOCK)
            for gi in range(G):
                cols = slice(gi * BLOCK, (gi + 1) * BLOCK)
                vb = vln[rows, cols].astype(BF16)
                f = _dot(ws_ref[gi], vb, NN) + bs_ref[gi]
                d = dgtv[rows, cols]
                du_sc[rows, cols] = d * f
                df = d * u[rows, cols]
                dfb = df.astype(BF16)
                dvln_sc[rows, cols] = _dot(wst_ref[gi], dfb, NN)
                dws_ref[gi] += jnp.where(causal, _dot(dfb, vb, NT), 0.0)
                dbs_ref[gi] += jnp.broadcast_to(jnp.sum(df, axis=-1, keepdims=True), (BLOCK, BLOCK))
        dvln = dvln_sc[...]
        dlg_ref[...] += jnp.sum(dvln * vh, axis=0, keepdims=True)
        dlb_ref[...] += jnp.sum(dvln, axis=0, keepdims=True)
        dvh = dvln * g_ref[...]
        dvv = rstd * (dvh - jnp.mean(dvh, axis=-1, keepdims=True)
                      - vh * jnp.mean(dvh * vh, axis=-1, keepdims=True))
        dz_ref[:, :W] = (du_sc[...] * _gelu_grad(zu)).astype(BF16)
        dz_ref[:, W:] = (dvv * _gelu_grad(zv)).astype(BF16)

    full3 = pl.BlockSpec((G, BLOCK, BLOCK), lambda i: (0, 0, 0))
    vec = jax.ShapeDtypeStruct((1, W), F32)
    acc3 = jax.ShapeDtypeStruct((G, BLOCK, BLOCK), F32)
    return pl.pallas_call(
        body, name="sgu_bwd", grid=(T // tr,),
        in_specs=[pl.BlockSpec((tr, W), lambda i: (i, 0)), pl.BlockSpec((tr, 2 * W), lambda i: (i, 0)),
                  _vec_spec(W), _vec_spec(W), full3, full3, full3],
        out_specs=[pl.BlockSpec((tr, 2 * W), lambda i: (i, 0)), full3, full3, _vec_spec(W), _vec_spec(W)],
        out_shape=[jax.ShapeDtypeStruct((T, 2 * W), BF16), acc3, acc3, vec, vec],
        scratch_shapes=[pltpu.VMEM((tr, W), F32), pltpu.VMEM((tr, W), F32)],
        compiler_params=_params(),
    )(dgt, zpre, ln_g, ln_b, wsm, wsmT, bs_rep)


def _rope(x, cos_t, sin_t):
    T, N = x.shape
    tr = _tile(T, ROW_TILE, 8)
    nrep = N // LANES
    half = ROPE_DIM // 2

    def body(x_ref, c_ref, s_ref, o_ref):
        xv = x_ref[...]
        lane = jnp.bitwise_and(lax.broadcasted_iota(jnp.int32, (tr, N), 1), SWA_DH - 1)
        partner = jnp.where(lane < half, -pltpu.roll(xv, N - half, 1), pltpu.roll(xv, half, 1))
        o_ref[...] = (xv * _rep(c_ref[...], nrep) + partner * _rep(s_ref[...], nrep)).astype(BF16)

    tab = pl.BlockSpec((tr, LANES), lambda i: (i, 0))
    row = pl.BlockSpec((tr, N), lambda i: (i, 0))
    return pl.pallas_call(
        body, name="rope", grid=(T // tr,), in_specs=[row, tab, tab], out_specs=row,
        out_shape=jax.ShapeDtypeStruct((T, N), BF16), compiler_params=_params(),
    )(x, cos_t, sin_t)


def _swa_tiles(T):
    sb = 4 if T >= 2048 else 2
    return sb, BLOCK * sb, T // (BLOCK * sb)


def _band_mask():
    row = lax.broadcasted_iota(jnp.int32, (BLOCK, 2 * BLOCK), 0)
    col = lax.broadcasted_iota(jnp.int32, (BLOCK, 2 * BLOCK), 1)
    return jnp.logical_and(col > row, col <= row + BLOCK), col


def _swa_specs(T, G):
    sb, tq, nq = _swa_tiles(T)
    q = pl.BlockSpec((None, tq, LANES), lambda h, i: (h, i, 0))
    kc = pl.BlockSpec((None, tq, LANES), lambda h, i: (h // G, i, 0))
    kp = pl.BlockSpec((None, BLOCK, LANES), lambda h, i: (h // G, jnp.maximum(i * sb - 1, 0), 0))
    return q, kc, kp


def _swa_band(b, i, kc_ref, kp_ref, vc_ref, vp_ref):
    rows = slice(b * BLOCK, (b + 1) * BLOCK)
    prev = slice((b - 1) * BLOCK, b * BLOCK)
    kprev = kp_ref[...] if b == 0 else kc_ref[prev, :]
    vprev = vp_ref[...] if b == 0 else vc_ref[prev, :]
    K = jnp.concatenate([kprev, kc_ref[rows, :]], axis=0)
    V = jnp.concatenate([vprev, vc_ref[rows, :]], axis=0)
    band, col = _band_mask()
    if b == 0:
        band = jnp.logical_and(band, jnp.logical_or(col >= BLOCK, i > 0))
    return rows, K, V, band


def _swa_fwd(qp, kp, vp, sinks):
    Hq, T, _ = qp.shape
    G = Hq // kp.shape[0]
    sb, tq, nq = _swa_tiles(T)

    def body(sink_ref, q_ref, kc_ref, kp_ref, vc_ref, vp_ref, o_ref, lse_ref):
        h, i = pl.program_id(0), pl.program_id(1)
        sink = sink_ref[h]
        for b in range(sb):
            rows, K, V, band = _swa_band(b, i, kc_ref, kp_ref, vc_ref, vp_ref)
            s = jnp.where(band, _dot(q_ref[rows, :], K, NT) * SWA_SCALE, NEG_INF)
            m = jnp.maximum(jnp.max(s, axis=-1, keepdims=True), sink)
            pm = jnp.exp(s - m)
            den = jnp.sum(pm, axis=-1, keepdims=True) + jnp.exp(sink - m)
            o_ref[rows, :] = _dot((pm / den).astype(BF16), V, NN)
            lse_ref[rows, :] = jnp.broadcast_to(m + jnp.log(den), (BLOCK, LANES))

    q, kc, kpv = _swa_specs(T, G)
    out = jax.ShapeDtypeStruct((Hq, T, LANES), F32)
    return pl.pallas_call(
        body, name="swa_fwd", grid=(Hq, nq),
        in_specs=[pl.BlockSpec(memory_space=pltpu.SMEM), q, kc, kpv, kc, kpv], out_specs=[q, q],
        out_shape=[out, out], compiler_params=_params(),
    )(sinks, qp, kp, kp, vp, vp)


def _swa_bwd_dq(qp, kp, vp, dop, op, lse_rep, sinks):
    Hq, T, _ = qp.shape
    G = Hq // kp.shape[0]
    sb, tq, nq = _swa_tiles(T)

    def body(sink_ref, q_ref, kc_ref, kp_ref, vc_ref, vp_ref, do_ref, o_ref, lse_ref, dq_ref, dsink_ref):
        h, i = pl.program_id(0), pl.program_id(1)
        sink = sink_ref[h]

        @pl.when(i == 0)
        def _():
            dsink_ref[...] = jnp.zeros_like(dsink_ref)

        for b in range(sb):
            rows, K, V, band = _swa_band(b, i, kc_ref, kp_ref, vc_ref, vp_ref)
            dov = do_ref[rows, :]
            delta = jnp.sum(dov * o_ref[rows, :], axis=-1, keepdims=True)
            lse = lse_ref[rows, :]
            s = jnp.where(band, _dot(q_ref[rows, :], K, NT) * SWA_SCALE, NEG_INF)
            pm = jnp.exp(s - _rep(lse, 2))
            dp = _dot(dov.astype(BF16), V, NT)
            ds = pm * (dp - delta)
            dq_ref[rows, :] = _dot((ds * SWA_SCALE).astype(BF16), K, NN)
            part = jnp.sum(jnp.exp(sink - lse) * delta, axis=0, keepdims=True)
            dsink_ref[...] -= jnp.broadcast_to(part, (8, LANES))

    q, kc, kpv = _swa_specs(T, G)
    return pl.pallas_call(
        body, name="swa_bwd_dq", grid=(Hq, nq),
        in_specs=[pl.BlockSpec(memory_space=pltpu.SMEM), q, kc, kpv, kc, kpv, q, q, q],
        out_specs=[q, pl.BlockSpec((None, 8, LANES), lambda h, i: (h, 0, 0))],
        out_shape=[jax.ShapeDtypeStruct((Hq, T, LANES), F32), jax.ShapeDtypeStruct((Hq, 8, LANES), F32)],
        compiler_params=_params(),
    )(sinks, qp, kp, kp, vp, vp, dop, op, lse_rep)


def _swa_bwd_dkv(qp, kp, vp, dop, op, lse_rep):
    Hq, T, _ = qp.shape
    Hk = kp.shape[0]
    G = Hq // Hk
    sb, tq, nq = _swa_tiles(T)
    nblk = T // BLOCK

    def body(k_ref, v_ref, q_ref, qn_ref, do_ref, don_ref, o_ref, on_ref, lse_ref, lsen_ref, dk_ref, dv_ref):
        i = pl.program_id(1)
        trow = lax.broadcasted_iota(jnp.int32, (2 * BLOCK, BLOCK), 0)
        scol = lax.broadcasted_iota(jnp.int32, (2 * BLOCK, BLOCK), 1)
        band0 = jnp.logical_and(trow >= scol, trow < scol + BLOCK)
        for b in range(sb):
            rows = slice(b * BLOCK, (b + 1) * BLOCK)
            nxt = slice((b + 1) * BLOCK, (b + 2) * BLOCK)
            last = b == sb - 1
            band = band0
            if last:
                band = jnp.logical_and(band0, jnp.logical_or(trow < BLOCK, i < nq - 1))
            kb, vb = k_ref[rows, :], v_ref[rows, :]
            dk = jnp.zeros((BLOCK, LANES), F32)
            dv = jnp.zeros((BLOCK, LANES), F32)
            for g in range(G):
                def two(cur, nx):
                    return jnp.concatenate([cur[g, rows, :], nx[g] if last else cur[g, nxt, :]], axis=0)
                Q, dov, ov, lse = two(q_ref, qn_ref), two(do_ref, don_ref), two(o_ref, on_ref), two(lse_ref, lsen_ref)
                delta = jnp.sum(dov * ov, axis=-1, keepdims=True)
                s = jnp.where(band, _dot(Q, kb, NT) * SWA_SCALE, NEG_INF)
                pm = jnp.exp(s - lse)
                dob = dov.astype(BF16)
                dv = dv + _dot(pm.astype(BF16), dob, TN)
                ds = pm * (_dot(dob, vb, NT) - delta)
                dk = dk + _dot((ds * SWA_SCALE).astype(BF16), Q, TN)
            dk_ref[rows, :] = dk
            dv_ref[rows, :] = dv

    kspec = pl.BlockSpec((None, tq, LANES), lambda h, i: (h, i, 0))
    cur = pl.BlockSpec((G, tq, LANES), lambda h, i: (h, i, 0))
    nxt = pl.BlockSpec((G, BLOCK, LANES), lambda h, i: (h, jnp.minimum((i + 1) * sb, nblk - 1), 0))
    out = jax.ShapeDtypeStruct((Hk, T, LANES), F32)
    return pl.pallas_call(
        body, name="swa_bwd_dkv", grid=(Hk, nq),
        in_specs=[kspec, kspec, cur, nxt, cur, nxt, cur, nxt, cur, nxt], out_specs=[kspec, kspec],
        out_shape=[out, out], compiler_params=_params(),
    )(kp, vp, qp, qp, dop, dop, op, op, lse_rep, lse_rep)


def _to_heads(a, nh):
    T = a.shape[0]
    a = a.reshape(T, nh, SWA_DH).transpose(1, 0, 2)
    return jnp.pad(a, ((0, 0), (0, 0), (0, LANES - SWA_DH)))


def _from_heads(a):
    nh, T, _ = a.shape
    return a[:, :, :SWA_DH].transpose(1, 0, 2).reshape(T, nh * SWA_DH)


def _adam(g, w, m, v):
    m2 = ADAM_B1 * m + (1.0 - ADAM_B1) * g
    v2 = ADAM_B2 * v + (1.0 - ADAM_B2) * (g * g)
    m_hat = m2 / (1.0 - ADAM_B1 ** ADAM_STEP)
    v_hat = v2 / (1.0 - ADAM_B2 ** ADAM_STEP)
    delta = -ADAM_LR * (m_hat / (jnp.sqrt(v_hat) + ADAM_EPS) + ADAM_WD * w)
    return delta, m2, v2


def _ada_fwd(c_all, w, b):
    L, D, n = w.shape
    tn = _tile(n, 768)

    def body(c_ref, w_ref, b_ref, o_ref):
        cv = c_ref[...]
        ca = (cv * _sigmoid(cv)).astype(BF16)
        o_ref[...] = _dot(ca, w_ref[...].astype(BF16), NN) + b_ref[...]

    return pl.pallas_call(
        body, name="ada_fwd", grid=(L, n // tn),
        in_specs=[pl.BlockSpec((NDEV, D), lambda l, j: (0, 0)), pl.BlockSpec((None, D, tn), lambda l, j: (l, 0, j)),
                  pl.BlockSpec((None, 1, tn), lambda l, j: (l, 0, j))],
        out_specs=pl.BlockSpec((None, NDEV, tn), lambda l, j: (l, 0, j)),
        out_shape=jax.ShapeDtypeStruct((L, NDEV, n), F32), compiler_params=_params(),
    )(c_all, w, b)


def _ada_update(c_rep, dm, w, m, v):
    L, D, n = w.shape
    tr = _tile(D, 256, 8)
    nrep = n // LANES

    def body(c_ref, dm_ref, w_ref, m_ref, v_ref, g_ref, d_ref, m2_ref, v2_ref):
        g = jnp.zeros((tr, n), F32)
        for b in range(NDEV):
            cv = c_ref[b]
            g = g + _rep(cv * _sigmoid(cv), nrep) * dm_ref[pl.ds(b, 1), :]
        g_ref[...] = g
        d_ref[...], m2_ref[...], v2_ref[...] = _adam(g, w_ref[...], m_ref[...], v_ref[...])

    blk = pl.BlockSpec((None, tr, n), lambda l, i: (l, i, 0))
    out = jax.ShapeDtypeStruct((L, D, n), F32)
    return pl.pallas_call(
        body, name="ada_update", grid=(L, D // tr),
        in_specs=[pl.BlockSpec((NDEV, tr, LANES), lambda l, i: (0, i, 0)),
                  pl.BlockSpec((None, NDEV, n), lambda l, i: (l, 0, 0)), blk, blk, blk],
        out_specs=[blk, blk, blk, blk], out_shape=[out, out, out, out], compiler_params=_params(),
    )(c_rep, dm, w, m, v)


def _adamw(name, parts, w, m, v, layer):
    P, R, C = parts.shape
    cpad = -(-C // LANES) * LANES
    per_row = cpad * (P * parts.dtype.itemsize + 7 * 4) * 2
    tr = _tile(R, max(8, (24 * 1024 * 1024 // per_row) // 8 * 8), 8)

    def body(p_ref, w_ref, m_ref, v_ref, g_ref, d_ref, m2_ref, v2_ref):
        g = p_ref[0].astype(F32)
        for s in range(1, P):
            g = g + p_ref[s].astype(F32)
        g_ref[...] = g
        d_ref[...], m2_ref[...], v2_ref[...] = _adam(g, w_ref[...], m_ref[...], v_ref[...])

    stk = pl.BlockSpec((None, tr, C), lambda i: (layer, i, 0))
    blk = pl.BlockSpec((tr, C), lambda i: (i, 0))
    out = jax.ShapeDtypeStruct((R, C), F32)
    return pl.pallas_call(
        body, name=name, grid=(R // tr,),
        in_specs=[pl.BlockSpec((P, tr, C), lambda i: (0, i, 0)), stk, stk, stk],
        out_specs=[blk, blk, blk, blk], out_shape=[out, out, out, out], compiler_params=_params(),
    )(parts, w, m, v)


def _colcat(a):
    s, k, n = a.shape
    return a.transpose(1, 0, 2).reshape(k, s * n)


def _colsplit(a):
    k, n8 = a.shape
    return a.reshape(k, NDEV, n8 // NDEV).transpose(1, 0, 2)


def _rowsplit(a):
    r, c = a.shape
    return a.reshape(NDEV, r // NDEV, c)


def kernel(x, c, positions, ada_w, ada_b, mix_pre_g, mix_post_g, ffn_pre_g, ffn_post_g, ffn_w_gu, ffn_w_down, fox_w_in, fox_b_f, fox_w_out, sgu_w_in, sgu_ln_g, sgu_ln_b, sgu_w_s, sgu_b_s, sgu_w_out, swa_w_in, swa_sinks, swa_w_out, loss_target, m_ada_w, m_ada_b, m_mix_pre_g, m_mix_post_g, m_ffn_pre_g, m_ffn_post_g, m_ffn_w_gu, m_ffn_w_down, m_fox_w_in, m_fox_b_f, m_fox_w_out, m_sgu_w_in, m_sgu_ln_g, m_sgu_ln_b, m_sgu_w_s, m_sgu_b_s, m_sgu_w_out, m_swa_w_in, m_swa_sinks, m_swa_w_out, v_ada_w, v_ada_b, v_mix_pre_g, v_mix_post_g, v_ffn_pre_g, v_ffn_post_g, v_ffn_w_gu, v_ffn_w_down, v_fox_w_in, v_fox_b_f, v_fox_w_out, v_sgu_w_in, v_sgu_ln_g, v_sgu_ln_b, v_sgu_w_s, v_sgu_b_s, v_sgu_w_out, v_swa_w_in, v_swa_sinks, v_swa_w_out):
    env = locals()
    W = {n: env[n] for n in WEIGHTS}
    M = {n: env["m_" + n] for n in WEIGHTS}
    V = {n: env["v_" + n] for n in WEIGHTS}

    me = 4 * lax.axis_index("x") + 2 * lax.axis_index("y") + lax.axis_index("c")
    _, T, D = x.shape
    L = ada_w.shape[0]
    n_ada = ada_w.shape[2]
    F = ffn_w_gu.shape[2] * NDEV // 2
    H = D // FOX_DH
    Hq = D // SWA_DH
    x0 = x[0]
    mixer = {0: 'fox', 1: 'sgu', 2: 'swa'}

    c_all = _comm("gather_c", [c], 'gather')[0].reshape(NDEV, D)
    ada_b_mine = lax.dynamic_slice_in_dim(ada_b, me * n_ada, n_ada, axis=1).reshape(L, 1, n_ada)
    mod_cols = _ada_fwd(c_all, ada_w, ada_b_mine)
    mod = _comm("a2a_mod", [mod_cols.transpose(1, 0, 2)], 'a2a')[0]
    mod = mod.transpose(1, 0, 2).reshape(L, 6, 1, D)

    inv = ROPE_THETA ** (-jnp.arange(0, ROPE_DIM, 2, dtype=F32) / ROPE_DIM)
    ang = positions[0].astype(F32)[:, None] * inv
    pad1 = jnp.ones((T, SWA_DH - ROPE_DIM), F32)
    cos64 = jnp.concatenate([jnp.cos(ang), jnp.cos(ang), pad1], axis=1)
    sin64 = jnp.concatenate([jnp.sin(ang), jnp.sin(ang), 0.0 * pad1], axis=1)
    cos_t = jnp.concatenate([cos64, cos64], axis=1)
    sin_t = jnp.concatenate([sin64, sin64], axis=1)

    fox_layers = [i for i in range(L) if mixer[i % 3] == 'fox']
    assert fox_layers and fox_layers[0] == 0

    def slice_of(i, role):
        kind, j = mixer[i % 3], i // 3
        src = {'wgu': (ffn_w_gu, i), 'wd': (ffn_w_down, i), 'win': (W[kind + '_w_in'], j), 'wout': (W[kind + '_w_out'], j)}[role]
        return src[0][src[1]].astype(BF16)

    def hosted_keys(f):
        later = [i for i in fox_layers if i > f]
        stop = later[0] if later else L
        keys = [(f, 'wgu'), (f, 'wd')]
        for i in range(f + 1, stop):
            keys += [(i, 'wgu'), (i, 'wd'), (i, 'win'), (i, 'wout')]
        if later:
            keys += [(stop, 'win'), (stop, 'wout')]
        return keys

    raw, full = {}, {}
    first_keys = [(0, 'win'), (0, 'wout')]
    raw.update(zip(first_keys, _comm("gather_first", [slice_of(*k) for k in first_keys], 'gather')))

    def wget(i, role):
        if (i, role) not in full:
            got = raw[(i, role)]
            full[(i, role)] = _colcat(got) if role in ('wgu', 'win') else got.reshape(-1, D)
        return full[(i, role)]

    saved = []
    xc = x0
    for i in range(L):
        kind, j = mixer[i % 3], i // 3
        s = dict(x_in=xc)
        sh_m, sc_m, g_m, sh_f, sc_f, g_f = [mod[i, t] for t in range(6)]
        h = _pre_fwd(xc, mix_pre_g[i:i + 1], sc_m, sh_m)
        s['h'] = h
        if kind == 'fox':
            wqkv = wget(i, 'win')[:, :3 * D]
            wf = jnp.pad(wget(i, 'win')[:, 3 * D:], ((0, 0), (0, LANES - H)))
            s['win_pad'] = jnp.concatenate([wqkv, wf], axis=1)
            bf = jnp.pad(fox_b_f[j:j + 1], ((0, 0), (0, LANES - H)))
            qkv = _mm("fox_qkv", h, wqkv, 'nn', BF16)
            fg = _mm("fox_fg", h, wf, 'nn', F32)
            cum, cq_rep = _fox_gate_fwd(fg, bf, H)
            ck = cum[:, :H].T.reshape(H, 1, T)
            keys = hosted_keys(i)
            (o, obf, lse_rep), got = _fox_fwd("fox_fwd%d" % j, qkv, cq_rep, ck, gather=[slice_of(*k) for k in keys])
            raw.update(zip(keys, got))
            s.update(qkv=qkv, fg=fg, bf=bf, cq_rep=cq_rep, ck=ck, o=o, lse_rep=lse_rep, mix_out=obf)
        elif kind == 'sgu':
            G = D // BLOCK
            causal = jnp.tril(jnp.ones((BLOCK, BLOCK), bool))
            wsm = jnp.where(causal[None], sgu_w_s[j], 0.0).astype(BF16)
            bs_rep = jnp.broadcast_to(sgu_b_s[j][:, :, None], (G, BLOCK, BLOCK))
            zpre = _mm("sgu_in", h, wget(i, 'win'), 'nn', F32)
            gated = _sgu_fwd(zpre, sgu_ln_g[j:j + 1], sgu_ln_b[j:j + 1], wsm, bs_rep)
            s.update(zpre=zpre, wsm=wsm, bs_rep=bs_rep, mix_out=gated)
        else:
            Hk = (wget(i, 'win').shape[1] // SWA_DH - Hq) // 2
            proj = _mm("swa_in", h, wget(i, 'win'), 'nn', F32)
            qr = _rope(proj[:, :Hq * SWA_DH], cos_t, sin_t)
            kr = _rope(proj[:, Hq * SWA_DH:(Hq + Hk) * SWA_DH], cos_t, sin_t)
            qp, kp = _to_heads(qr, Hq), _to_heads(kr, Hk)
            vp = _to_heads(proj[:, (Hq + Hk) * SWA_DH:].astype(BF16), Hk)
            op, lse_rep = _swa_fwd(qp, kp, vp, swa_sinks[j])
            s.update(qp=qp, kp=kp, vp=vp, op=op, lse_rep=lse_rep, Hk=Hk, mix_out=_from_heads(op).astype(BF16))
        y = _mm("mix_out", s['mix_out'], wget(i, 'wout'), 'nn', F32)
        x_mid = _post_fwd(xc, y, mix_post_g[i:i + 1], g_m)
        s.update(y_mix=y, x_mid=x_mid)
        h2 = _pre_fwd(x_mid, ffn_pre_g[i:i + 1], sc_f, sh_f)
        g, u, a = _ffn_up(h2, wget(i, 'wgu'))
        y2 = _mm("ffn_down", a, wget(i, 'wd'), 'nn', F32)
        xc = _post_fwd(x_mid, y2, ffn_post_g[i:i + 1], g_f)
        s.update(h2=h2, g=g, u=u, a=a, y_ffn=y2)
        saved.append(s)

    dx, lsum = _loss(xc, loss_target[0])
    loss = lax.psum(0.5 * lsum[0, 0] / D, AXES)

    small = {n: [None] * W[n].shape[0] for n in SMALL}
    dmod = [None] * L
    out = {n: dict(g=[None] * W[n].shape[0], d=[None] * W[n].shape[0], m=[None] * W[n].shape[0],
                   v=[None] * W[n].shape[0]) for n in WEIGHTS}

    pending = []

    def update(items, recv):
        for (name, idx, _), parts in zip(items, recv):
            res = _adamw("adamw_" + name, parts, W[name], M[name], V[name], idx)
            for key, val in zip("gdmv", res):
                out[name][key][idx] = val

    for i in reversed(range(L)):
        kind, j = mixer[i % 3], i // 3
        s = saved[i]
        sh_m, sc_m, g_m, sh_f, sc_f, g_f = [mod[i, t] for t in range(6)]
        dy2, dg_f, dpost_f = _post_bwd(dx, s['y_ffn'], ffn_post_g[i:i + 1], g_f)
        dwd = _mm("ffn_dwd", s['a'], dy2, 'tn', BF16)
        dg, du = _ffn_dact(dy2, wget(i, 'wd'), s['g'], s['u'])
        dwgu = jnp.concatenate([_mm("ffn_dwg", s['h2'], dg, 'tn', BF16), _mm("ffn_dwu", s['h2'], du, 'tn', BF16)], axis=1)
        dh2 = _matmul("ffn_dh", [(dg, wget(i, 'wgu'), (0, 0), (0, 0)), (du, wget(i, 'wgu'), (0, 0), (0, F))], 'nt', F32,
                      T, D, F, tk=_tile(F, 1024))
        pending += [('ffn_w_gu', i, _colsplit(dwgu)), ('ffn_w_down', i, _rowsplit(dwd))]
        dx, dsh_f, dsc_f, dpre_f = _pre_bwd(dh2, s['x_mid'], ffn_pre_g[i:i + 1], sc_f, dx)
        dy, dg_m, dpost_m = _post_bwd(dx, s['y_mix'], mix_post_g[i:i + 1], g_m)
        dwout = _mm("mix_dwout", s['mix_out'], dy, 'tn', BF16)
        dmix = _mm("mix_dout", dy, wget(i, 'wout'), 'nt', F32)
        if kind == 'fox':
            dob, delta_rep, cql_rep = _fox_bwd_prep(dmix, s['o'], s['lse_rep'], s['cq_rep'])
            (dq, dk, dv, dck, dcq), recv = _fox_bwd("fox_bwd%d" % j, s['qkv'], dob, delta_rep, cql_rep, s['ck'],
                                                    a2a=[item[2] for item in pending])
            update(pending, recv)
            pending = []
            dcum = jnp.pad(dck.reshape(H, T).T, ((0, 0), (0, LANES - H))) + dcq
            dfg, dbf = _fox_gate_bwd(dcum, s['fg'], s['bf'])
            small['fox_b_f'][j] = dbf[0, :H]
            dproj = jnp.concatenate([dq, dk, dv, dfg.astype(BF16)], axis=1)
            dwin = _mm("fox_dwin", s['h'], dproj, 'tn', BF16)[:, :3 * D + H]
            dh = _mm("fox_dh", dproj, s['win_pad'], 'nt', F32)
        elif kind == 'sgu':
            wsmT = s['wsm'].transpose(0, 2, 1)
            dz, dws, dbs, dlg, dlb = _sgu_bwd(dmix, s['zpre'], sgu_ln_g[j:j + 1], sgu_ln_b[j:j + 1], s['wsm'], wsmT, s['bs_rep'])
            small['sgu_w_s'][j], small['sgu_b_s'][j] = dws, dbs[:, :, 0]
            small['sgu_ln_g'][j], small['sgu_ln_b'][j] = dlg[0], dlb[0]
            dwin = _mm("sgu_dwin", s['h'], dz, 'tn', BF16)
            dh = _mm("sgu_dh", dz, wget(i, 'win'), 'nt', F32)
        else:
            Hk = s['Hk']
            dop = _to_heads(dmix, Hq)
            dqp, dsink = _swa_bwd_dq(s['qp'], s['kp'], s['vp'], dop, s['op'], s['lse_rep'], swa_sinks[j])
            dkp, dvp = _swa_bwd_dkv(s['qp'], s['kp'], s['vp'], dop, s['op'], s['lse_rep'])
            small['swa_sinks'][j] = dsink[:, 0, 0]
            dproj = jnp.concatenate([_rope(_from_heads(dqp), cos_t, -sin_t), _rope(_from_heads(dkp), cos_t, -sin_t),
                                     _from_heads(dvp).astype(BF16)], axis=1)
            dwin = _mm("swa_dwin", s['h'], dproj, 'tn', BF16)
            dh = _mm("swa_dh", dproj, wget(i, 'win'), 'nt', F32)
        dx, dsh_m, dsc_m, dpre_m = _pre_bwd(dh, s['x_in'], mix_pre_g[i:i + 1], sc_m, dx)
        small['mix_pre_g'][i], small['mix_post_g'][i] = dpre_m[0], dpost_m[0]
        small['ffn_pre_g'][i], small['ffn_post_g'][i] = dpre_f[0], dpost_f[0]
        dmod[i] = jnp.concatenate([dsh_m, dsc_m, dg_m, dsh_f, dsc_f, dg_f], axis=1)[0]

        pending += [(kind + '_w_in', j, _colsplit(dwin)), (kind + '_w_out', j, _rowsplit(dwout))]

    update(pending, _comm("a2a_last", [item[2] for item in pending], 'a2a'))
    grad_x = dx[None]

    small['ada_b'] = dmod
    flat = jnp.concatenate([jnp.stack(small[n]).reshape(-1) for n in SMALL])
    width = 8 * LANES
    npad = -flat.shape[0] % (8 * width)
    packed = jnp.pad(flat, (0, npad)).reshape(-1, width)
    parts = _comm("gather_small", [packed], 'gather')[0]

    def pack(d):
        f = jnp.concatenate([d[n].reshape(-1) for n in SMALL])
        return jnp.pad(f, (0, npad)).reshape(1, -1, width)

    res = _adamw("adamw_small", parts, pack(W), pack(M), pack(V), 0)
    off = 0
    for n in SMALL:
        size = W[n].size
        for key, val in zip("gdmv", res):
            out[n][key] = val.reshape(-1)[off:off + size].reshape(W[n].shape)
        off += size

    dmod_all = parts.reshape(NDEV, -1)[:, :L * 6 * D].reshape(NDEV, L, 6 * D)
    dm = lax.dynamic_slice_in_dim(dmod_all, me * n_ada, n_ada, axis=2).transpose(1, 0, 2)
    c_rep = jnp.broadcast_to(c_all[:, :, None], (NDEV, D, LANES))
    for key, val in zip("gdmv", _ada_update(c_rep, dm, ada_w, m_ada_w, v_ada_w)):
        out['ada_w'][key] = val

    def leaf(n, key):
        val = out[n][key]
        return jnp.stack(val) if isinstance(val, list) else val

    return (loss, grad_x, *[leaf(n, 'g') for n in WEIGHTS], *[leaf(n, 'd') for n in WEIGHTS],
            *[leaf(n, 'm') for n in WEIGHTS], *[leaf(n, 'v') for n in WEIGHTS])
```

```python
import numpy as np
import jax
import jax.numpy as jnp
from jax import lax
from jax.experimental import pallas as pl
from jax.experimental.pallas import tpu as pltpu

F32 = jnp.float32
BF16 = jnp.bfloat16
NDEV = 8
AXES = ("x", "y", "c")
LANES = 128
VMEM_LIMIT_BYTES = 48 * 1024 * 1024
NEG_INF = float("-inf")
ROW_TILE = 256

EPS = 1e-6
BLOCK = 128
FOX_DH = 128
FOX_HPS = 2
SWA_DH = 64
ROPE_DIM = 16
ROPE_THETA = 500000.0
FOX_SCALE = FOX_DH ** -0.5
SWA_SCALE = SWA_DH ** -0.5
GELU_C0 = 0.7978845608028654
GELU_C1 = 0.044715

ADAM_LR = 0.001
ADAM_B1 = 0.9
ADAM_B2 = 0.999
ADAM_EPS = 1e-08
ADAM_WD = 0.01
ADAM_STEP = 10

NN = (((1,), (0,)), ((), ()))
NT = (((1,), (1,)), ((), ()))
TN = (((0,), (0,)), ((), ()))

WEIGHTS = ['ada_w', 'ada_b', 'mix_pre_g', 'mix_post_g', 'ffn_pre_g', 'ffn_post_g', 'ffn_w_gu', 'ffn_w_down',
           'fox_w_in', 'fox_b_f', 'fox_w_out', 'sgu_w_in', 'sgu_ln_g', 'sgu_ln_b', 'sgu_w_s', 'sgu_b_s',
           'sgu_w_out', 'swa_w_in', 'swa_sinks', 'swa_w_out']
SMALL = ['ada_b', 'mix_pre_g', 'mix_post_g', 'ffn_pre_g', 'ffn_post_g', 'fox_b_f', 'sgu_ln_g', 'sgu_ln_b',
         'sgu_w_s', 'sgu_b_s', 'swa_sinks']


def _dot(a, b, dims):
    return lax.dot_general(a, b, dims, preferred_element_type=F32)


def _tile(n, pref, mult=LANES):
    t = (min(pref, n) // mult) * mult
    while t >= mult:
        if n % t == 0:
            return t
        t -= mult
    return n


def _params():
    return pltpu.CompilerParams(vmem_limit_bytes=VMEM_LIMIT_BYTES)


def _rep(a, n):
    return a if n == 1 else jnp.concatenate([a] * n, axis=-1)


def _vec_spec(d):
    return pl.BlockSpec((1, d), lambda *_: (0, 0))


def _sigmoid(z):
    return 1.0 / (1.0 + jnp.exp(-z))


def _gelu(z):
    t = jnp.tanh(GELU_C0 * (z + GELU_C1 * z * z * z))
    return 0.5 * z * (1.0 + t)


def _gelu_grad(z):
    t = jnp.tanh(GELU_C0 * (z + GELU_C1 * z * z * z))
    return 0.5 * (1.0 + t) + 0.5 * z * (1.0 - t * t) * GELU_C0 * (1.0 + 3.0 * GELU_C1 * z * z)


def _comm_out_shapes(arrs, gather):
    return [jax.ShapeDtypeStruct(((NDEV,) + a.shape) if gather else a.shape, a.dtype) for a in arrs]


def _comm_sems(n):
    return [pltpu.SemaphoreType.DMA((n,)), pltpu.SemaphoreType.DMA((n,)), pltpu.SemaphoreType.DMA((n,))]


def _me():
    x, y, c = lax.axis_index("x"), lax.axis_index("y"), lax.axis_index("c")
    return x, y, c, 4 * x + 2 * y + c


def _comm_start(ins, outs, gather, send_sems, recv_sems, local_sems):
    x, y, c, me = _me()
    for a in range(len(ins)):
        pltpu.make_async_copy(ins[a] if gather else ins[a].at[me], outs[a].at[me], local_sems.at[a]).start()
        for bits in range(1, NDEV):
            px = (1 - x) if bits & 4 else x
            py = (1 - y) if bits & 2 else y
            pc = (1 - c) if bits & 1 else c
            pltpu.make_async_remote_copy(
                src_ref=ins[a] if gather else ins[a].at[4 * px + 2 * py + pc], dst_ref=outs[a].at[me],
                send_sem=send_sems.at[a], recv_sem=recv_sems.at[a],
                device_id=(px, py, pc), device_id_type=pl.DeviceIdType.MESH).start()


def _comm_wait(ins, outs, gather, send_sems, recv_sems, local_sems):
    x, y, c, me = _me()
    for a in range(len(ins)):
        seven = outs[a].at[pl.ds(0, NDEV - 1)]
        pltpu.make_async_remote_copy(src_ref=seven, dst_ref=seven, send_sem=send_sems.at[a], recv_sem=recv_sems.at[a],
                                     device_id=(x, y, c), device_id_type=pl.DeviceIdType.MESH).wait()
        pltpu.make_async_copy(ins[a] if gather else ins[a].at[me], outs[a].at[me], local_sems.at[a]).wait()


def _comm(name, arrs, kind):
    n = len(arrs)
    gather = kind == 'gather'

    def body(*refs):
        ins, outs, sems = refs[:n], refs[n:2 * n], refs[2 * n:]
        _comm_start(ins, outs, gather, *sems)
        _comm_wait(ins, outs, gather, *sems)

    any_spec = pl.BlockSpec(memory_space=pl.ANY)
    return pl.pallas_call(
        body, name=name, out_shape=_comm_out_shapes(arrs, gather),
        in_specs=[any_spec] * n, out_specs=[any_spec] * n, scratch_shapes=_comm_sems(n),
    )(*arrs)


def _gather2_sems(n):
    return [pltpu.SemaphoreType.DMA((n,)) for _ in range(4)]


def _remote(src, dst, send_sem, recv_sem, device):
    return pltpu.make_async_remote_copy(src_ref=src, dst_ref=dst, send_sem=send_sem, recv_sem=recv_sem,
                                        device_id=device, device_id_type=pl.DeviceIdType.MESH)


def _gather2_start(ins, outs, send_sems, ici_sems, d2d_sems, local_sems):
    x, y, c, me = _me()
    for a in range(len(ins)):
        pltpu.make_async_copy(ins[a], outs[a].at[me], local_sems.at[a]).start()
        _remote(ins[a], outs[a].at[me], send_sems.at[a], d2d_sems.at[a], (x, y, 1 - c)).start()
        for px, py in ((1 - x, y), (x, 1 - y), (1 - x, 1 - y)):
            _remote(ins[a], outs[a].at[me], send_sems.at[a], ici_sems.at[a], (px, py, c)).start()


def _gather2_pass_on(ins, outs, send_sems, ici_sems, d2d_sems, local_sems):
    x, y, c, me = _me()
    for a in range(len(ins)):
        three = outs[a].at[pl.ds(0, 3)]
        _remote(three, three, send_sems.at[a], ici_sems.at[a], (x, y, c)).wait_recv()
        for px, py in ((1 - x, y), (x, 1 - y), (1 - x, 1 - y)):
            slot = outs[a].at[4 * px + 2 * py + c]
            _remote(slot, slot, send_sems.at[a], d2d_sems.at[a], (x, y, 1 - c)).start()


def _gather2_finish(ins, outs, send_sems, ici_sems, d2d_sems, local_sems):
    x, y, c, me = _me()
    for a in range(len(ins)):
        four, seven = outs[a].at[pl.ds(0, 4)], outs[a].at[pl.ds(0, NDEV - 1)]
        _remote(four, four, send_sems.at[a], d2d_sems.at[a], (x, y, c)).wait_recv()
        _remote(seven, seven, send_sems.at[a], d2d_sems.at[a], (x, y, c)).wait_send()
        pltpu.make_async_copy(ins[a], outs[a].at[me], local_sems.at[a]).wait()


def _gather2(name, arrs):
    n = len(arrs)

    def body(*refs):
        ins, outs, sems = refs[:n], refs[n:2 * n], refs[2 * n:]
        _gather2_start(ins, outs, *sems)
        _gather2_pass_on(ins, outs, *sems)
        _gather2_finish(ins, outs, *sems)

    any_spec = pl.BlockSpec(memory_space=pl.ANY)
    return pl.pallas_call(
        body, name=name, out_shape=_comm_out_shapes(arrs, True),
        in_specs=[any_spec] * n, out_specs=[any_spec] * n, scratch_shapes=_gather2_sems(n),
    )(*arrs)


MATMUL_VMEM_BUDGET = 38 * 1024 * 1024


def _matmul_tiles(mode, M, N, K, npairs, out_size):
    def uniq(vals):
        return sorted(set(vals), reverse=True)

    tms = uniq(_tile(M, p) for p in (1024, 512, 256))
    tns = uniq(_tile(N, p) for p in (1536, 1024, 512, 256))
    tks = uniq(_tile(K, p) for p in (2048, 1024)) if mode == 'tn' else [K] + uniq(_tile(K, p) for p in (2048, 1024))
    for tk in tks:
        best = None
        for tm in tms:
            for tn in tns:
                steps = K // tk
                need = (2 * npairs * (tm + tn) * tk * 2 + 2 * tm * tn * out_size + npairs * tm * tn * 4
                        + (tm * tn * 4 if steps > 1 else 0) + (tk * tm * 2 if mode == 'tn' else 0))
                if need <= MATMUL_VMEM_BUDGET and (best is None or (tm * tn, tm) > (best[0] * best[1], best[0])):
                    best = (tm, tn, tk)
        if best is not None:
            return best
    return _tile(M, 256), _tile(N, 256), _tile(K, 512)


def _matmul(name, pairs, mode, out_dtype, M, N, K, tm=None, tn=None, tk=None, a2a=()):
    if not (tm and tn and tk):
        tm, tn, tk = _matmul_tiles(mode, M, N, K, len(pairs), jnp.dtype(out_dtype).itemsize)
    nk = K // tk
    dims = {'nn': NN, 'nt': NT, 'tn': TN}[mode]
    in_specs, ops = [], []
    for a, b, ao, bo in pairs:
        if mode == 'tn':
            assert ao[0] % tk == 0 and ao[1] % tm == 0
            sa = pl.BlockSpec((tk, tm), lambda i, j, k, r=ao[0] // tk, c=ao[1] // tm: (k + r, i + c))
        else:
            assert ao[0] % tm == 0 and ao[1] % tk == 0
            sa = pl.BlockSpec((tm, tk), lambda i, j, k, r=ao[0] // tm, c=ao[1] // tk: (i + r, k + c))
        if mode == 'nt':
            assert bo[0] % tn == 0 and bo[1] % tk == 0
            sb = pl.BlockSpec((tn, tk), lambda i, j, k, r=bo[0] // tn, c=bo[1] // tk: (j + r, k + c))
        else:
            assert bo[0] % tk == 0 and bo[1] % tn == 0
            sb = pl.BlockSpec((tk, tn), lambda i, j, k, r=bo[0] // tk, c=bo[1] // tn: (k + r, j + c))
        in_specs += [sa, sb]
        ops += [a, b]
    npairs = len(pairs)
    nc = len(a2a)
    grid = (M // tm, N // tn, nk)

    def body(*refs):
        cin, o_ref, cout = refs[2 * npairs:2 * npairs + nc], refs[2 * npairs + nc], refs[2 * npairs + nc + 1:2 * npairs + 2 * nc + 1]
        scratch = refs[2 * npairs + 2 * nc + 1:]
        sems = scratch[1:] if nk > 1 else scratch
        i, j, k = pl.program_id(0), pl.program_id(1), pl.program_id(2)

        if nc:
            @pl.when(jnp.logical_and(jnp.logical_and(i == 0, j == 0), k == 0))
            def _():
                _comm_start(cin, cout, False, *sems)

        part = _dot(refs[0][...], refs[1][...], dims)
        for p in range(1, npairs):
            part = part + _dot(refs[2 * p][...], refs[2 * p + 1][...], dims)
        if nk == 1:
            o_ref[...] = part.astype(out_dtype)
        else:
            acc = scratch[0]

            @pl.when(k == 0)
            def _():
                acc[...] = part

            @pl.when(k > 0)
            def _():
                acc[...] += part

            @pl.when(k == nk - 1)
            def _():
                o_ref[...] = acc[...].astype(out_dtype)

        if nc:
            @pl.when(jnp.logical_and(jnp.logical_and(i == grid[0] - 1, j == grid[1] - 1), k == nk - 1))
            def _():
                _comm_wait(cin, cout, False, *sems)

    any_spec = pl.BlockSpec(memory_space=pl.ANY)
    res = pl.pallas_call(
        body, name=name, grid=grid,
        in_specs=in_specs + [any_spec] * nc,
        out_specs=[pl.BlockSpec((tm, tn), lambda i, j, k: (i, j))] + [any_spec] * nc,
        out_shape=[jax.ShapeDtypeStruct((M, N), out_dtype)] + _comm_out_shapes(a2a, False),
        scratch_shapes=([] if nk == 1 else [pltpu.VMEM((tm, tn), F32)]) + (_comm_sems(nc) if nc else []),
        compiler_params=_params(),
    )(*ops, *a2a)
    return (res[0], res[1:]) if nc else res[0]


def _mm(name, a, b, mode, out_dtype, a2a=()):
    if mode == 'nn':
        (M, K), N = a.shape, b.shape[1]
    elif mode == 'nt':
        (M, K), N = a.shape, b.shape[0]
    else:
        (K, M), N = a.shape, b.shape[1]
    return _matmul(name, [(a, b, (0, 0), (0, 0))], mode, out_dtype, M, N, K, a2a=a2a)


def _ffn_up(h, wgu):
    T, D = h.shape
    F = wgu.shape[1] // 2
    tm, tn = _tile(T, 1024), _tile(F, 512)

    def body(h_ref, wg_ref, wu_ref, g_ref, u_ref, a_ref):
        hv = h_ref[...]
        g = _dot(hv, wg_ref[...], NN)
        u = _dot(hv, wu_ref[...], NN)
        g_ref[...] = g
        u_ref[...] = u
        a_ref[...] = (g * _sigmoid(g) * u).astype(BF16)

    out = pl.BlockSpec((tm, tn), lambda i, j: (i, j))
    return pl.pallas_call(
        body, name="ffn_up", grid=(T // tm, F // tn),
        in_specs=[pl.BlockSpec((tm, D), lambda i, j: (i, 0)),
                  pl.BlockSpec((D, tn), lambda i, j: (0, j)),
                  pl.BlockSpec((D, tn), lambda i, j, o=F // tn: (0, j + o))],
        out_specs=[out, out, out],
        out_shape=[jax.ShapeDtypeStruct((T, F), F32), jax.ShapeDtypeStruct((T, F), F32),
                   jax.ShapeDtypeStruct((T, F), BF16)],
        compiler_params=_params(),
    )(h, wgu, wgu)


def _ffn_dact(dy, wd, g, u):
    T, D = dy.shape
    F = wd.shape[0]
    tm, tn = _tile(T, 1024), _tile(F, 512)

    nsplit = 2 if tn % (2 * LANES) == 0 else 1

    def body(dy_ref, wd_ref, g_ref, u_ref, dg_ref, du_ref):
        dyv = dy_ref[...]
        for c in range(nsplit):
            cols = slice(c * (tn // nsplit), (c + 1) * (tn // nsplit))
            da = _dot(dyv, wd_ref[cols, :], NT)
            g = g_ref[:, cols]
            sg = _sigmoid(g)
            dg_ref[:, cols] = (da * u_ref[:, cols] * (sg * (1.0 + g * (1.0 - sg)))).astype(BF16)
            du_ref[:, cols] = (da * (g * sg)).astype(BF16)

    blk = pl.BlockSpec((tm, tn), lambda i, j: (i, j))
    return pl.pallas_call(
        body, name="ffn_dact", grid=(T // tm, F // tn),
        in_specs=[pl.BlockSpec((tm, D), lambda i, j: (i, 0)), pl.BlockSpec((tn, D), lambda i, j: (j, 0)), blk, blk],
        out_specs=[blk, blk],
        out_shape=[jax.ShapeDtypeStruct((T, F), BF16)] * 2,
        compiler_params=_params(),
    )(dy, wd, g, u)


def _rstd(v):
    return lax.rsqrt(jnp.mean(v * v, axis=-1, keepdims=True) + EPS)


def _pre_fwd(x, g, sc, sh):
    T, D = x.shape
    tr = _tile(T, ROW_TILE, 8)

    def body(x_ref, g_ref, sc_ref, sh_ref, h_ref):
        xv = x_ref[...]
        r = xv * _rstd(xv) * g_ref[...]
        h_ref[...] = (r * (1.0 + sc_ref[...]) + sh_ref[...]).astype(BF16)

    row = pl.BlockSpec((tr, D), lambda i: (i, 0))
    return pl.pallas_call(
        body, name="pre_fwd", grid=(T // tr,),
        in_specs=[row, _vec_spec(D), _vec_spec(D), _vec_spec(D)], out_specs=row,
        out_shape=jax.ShapeDtypeStruct((T, D), BF16), compiler_params=_params(),
    )(x, g, sc, sh)


def _post_fwd(x, y, g, gate):
    T, D = x.shape
    tr = _tile(T, ROW_TILE, 8)

    def body(x_ref, y_ref, g_ref, gate_ref, o_ref):
        yv = y_ref[...]
        o_ref[...] = x_ref[...] + gate_ref[...] * (yv * _rstd(yv) * g_ref[...])

    row = pl.BlockSpec((tr, D), lambda i: (i, 0))
    return pl.pallas_call(
        body, name="post_fwd", grid=(T // tr,),
        in_specs=[row, row, _vec_spec(D), _vec_spec(D)], out_specs=row,
        out_shape=jax.ShapeDtypeStruct((T, D), F32), compiler_params=_params(),
    )(x, y, g, gate)


def _post_bwd(dx, y, g, gate):
    T, D = dx.shape
    tr = _tile(T, ROW_TILE, 8)

    def body(dx_ref, y_ref, g_ref, gate_ref, dy_ref, dgate_ref, dg_ref):
        @pl.when(pl.program_id(0) == 0)
        def _():
            dgate_ref[...] = jnp.zeros_like(dgate_ref)
            dg_ref[...] = jnp.zeros_like(dg_ref)

        yv, dxv = y_ref[...], dx_ref[...]
        rstd = _rstd(yv)
        yh = yv * rstd
        dgate_ref[...] += jnp.sum(dxv * (yh * g_ref[...]), axis=0, keepdims=True)
        dn = dxv * gate_ref[...]
        dg_ref[...] += jnp.sum(dn * yh, axis=0, keepdims=True)
        dyh = dn * g_ref[...]
        dy_ref[...] = (rstd * (dyh - yh * jnp.mean(dyh * yh, axis=-1, keepdims=True))).astype(BF16)

    row = pl.BlockSpec((tr, D), lambda i: (i, 0))
    vec = jax.ShapeDtypeStruct((1, D), F32)
    return pl.pallas_call(
        body, name="post_bwd", grid=(T // tr,),
        in_specs=[row, row, _vec_spec(D), _vec_spec(D)], out_specs=[row, _vec_spec(D), _vec_spec(D)],
        out_shape=[jax.ShapeDtypeStruct((T, D), BF16), vec, vec], compiler_params=_params(),
    )(dx, y, g, gate)


def _pre_bwd(dh, x, g, sc, dx_res):
    T, D = x.shape
    tr = _tile(T, ROW_TILE, 8)

    def body(dh_ref, x_ref, g_ref, sc_ref, dxr_ref, dx_ref, dsh_ref, dsc_ref, dg_ref):
        @pl.when(pl.program_id(0) == 0)
        def _():
            dsh_ref[...] = jnp.zeros_like(dsh_ref)
            dsc_ref[...] = jnp.zeros_like(dsc_ref)
            dg_ref[...] = jnp.zeros_like(dg_ref)

        xv, dhv = x_ref[...], dh_ref[...]
        rstd = _rstd(xv)
        xh = xv * rstd
        dsh_ref[...] += jnp.sum(dhv, axis=0, keepdims=True)
        dsc_ref[...] += jnp.sum(dhv * (xh * g_ref[...]), axis=0, keepdims=True)
        dr = dhv * (1.0 + sc_ref[...])
        dg_ref[...] += jnp.sum(dr * xh, axis=0, keepdims=True)
        dxh = dr * g_ref[...]
        dx_ref[...] = dxr_ref[...] + rstd * (dxh - xh * jnp.mean(dxh * xh, axis=-1, keepdims=True))

    row = pl.BlockSpec((tr, D), lambda i: (i, 0))
    vec = jax.ShapeDtypeStruct((1, D), F32)
    return pl.pallas_call(
        body, name="pre_bwd", grid=(T // tr,),
        in_specs=[row, row, _vec_spec(D), _vec_spec(D), row],
        out_specs=[row, _vec_spec(D), _vec_spec(D), _vec_spec(D)],
        out_shape=[jax.ShapeDtypeStruct((T, D), F32), vec, vec, vec], compiler_params=_params(),
    )(dh, x, g, sc, dx_res)


def _loss(x, target):
    T, D = x.shape
    tr = _tile(T, ROW_TILE, 8)

    def body(x_ref, t_ref, dx_ref, l_ref):
        @pl.when(pl.program_id(0) == 0)
        def _():
            l_ref[...] = jnp.zeros_like(l_ref)

        e = x_ref[...] - t_ref[...]
        dx_ref[...] = e / D
        rows = jnp.sum(e * e, axis=-1, keepdims=True)
        l_ref[...] += jnp.broadcast_to(jnp.sum(rows, axis=0, keepdims=True), (1, LANES))

    row = pl.BlockSpec((tr, D), lambda i: (i, 0))
    return pl.pallas_call(
        body, name="loss", grid=(T // tr,),
        in_specs=[row, row], out_specs=[row, _vec_spec(LANES)],
        out_shape=[jax.ShapeDtypeStruct((T, D), F32), jax.ShapeDtypeStruct((1, LANES), F32)],
        compiler_params=_params(),
    )(x, target)


def _split3(x):
    hi = x.astype(BF16)
    r = x - hi.astype(F32)
    mid = r.astype(BF16)
    lo = (r - mid.astype(F32)).astype(BF16)
    return hi, mid, lo


def _tri_sum(tri, x):
    hi, mid, lo = _split3(x)
    return _dot(tri, hi, NN) + _dot(tri, mid, NN) + _dot(tri, lo, NN)


def _fox_gate_fwd(fg, bf, H):
    T = fg.shape[0]
    tb = _tile(T, 512)

    def body(fg_ref, bf_ref, cum_ref, rep_ref, carry):
        @pl.when(pl.program_id(0) == 0)
        def _():
            carry[...] = jnp.zeros_like(carry)

        z = fg_ref[...] + bf_ref[...]
        logf = jnp.minimum(z, 0.0) - jnp.log(1.0 + jnp.exp(-jnp.abs(z)))
        row = lax.broadcasted_iota(jnp.int32, (tb, tb), 0)
        col = lax.broadcasted_iota(jnp.int32, (tb, tb), 1)
        cum = _tri_sum((row >= col).astype(BF16), logf) + carry[...]
        cum_ref[...] = cum
        carry[...] = cum_ref[pl.ds(tb - 1, 1), :]
        lane = lax.broadcasted_iota(jnp.int32, (tb, LANES), 1)
        for h in range(H):
            colv = jnp.sum(jnp.where(lane == h, cum, 0.0), axis=-1, keepdims=True)
            rep_ref[h] = jnp.broadcast_to(colv, (tb, LANES))

    return pl.pallas_call(
        body, name="fox_gate_fwd", grid=(T // tb,),
        in_specs=[pl.BlockSpec((tb, LANES), lambda i: (i, 0)), _vec_spec(LANES)],
        out_specs=[pl.BlockSpec((tb, LANES), lambda i: (i, 0)), pl.BlockSpec((H, tb, LANES), lambda i: (0, i, 0))],
        out_shape=[jax.ShapeDtypeStruct((T, LANES), F32), jax.ShapeDtypeStruct((H, T, LANES), F32)],
        scratch_shapes=[pltpu.VMEM((1, LANES), F32)], compiler_params=_params(),
    )(fg, bf)


def _fox_gate_bwd(dcum, fg, bf):
    T = fg.shape[0]
    tb = _tile(T, 512)
    nb = T // tb

    def body(dc_ref, fg_ref, bf_ref, dfg_ref, dbf_ref, carry):
        @pl.when(pl.program_id(0) == 0)
        def _():
            carry[...] = jnp.zeros_like(carry)
            dbf_ref[...] = jnp.zeros_like(dbf_ref)

        row = lax.broadcasted_iota(jnp.int32, (tb, tb), 0)
        col = lax.broadcasted_iota(jnp.int32, (tb, tb), 1)
        dc = dc_ref[...]
        dlogf = _tri_sum((row <= col).astype(BF16), dc) + carry[...]
        z = fg_ref[...] + bf_ref[...]
        dfg = dlogf * _sigmoid(-z)
        dfg_ref[...] = dfg
        dbf_ref[...] += jnp.sum(dfg, axis=0, keepdims=True)
        carry[...] += jnp.sum(dc, axis=0, keepdims=True)

    rev = pl.BlockSpec((tb, LANES), lambda i: (nb - 1 - i, 0))
    return pl.pallas_call(
        body, name="fox_gate_bwd", grid=(nb,),
        in_specs=[rev, rev, _vec_spec(LANES)], out_specs=[rev, _vec_spec(LANES)],
        out_shape=[jax.ShapeDtypeStruct((T, LANES), F32), jax.ShapeDtypeStruct((1, LANES), F32)],
        scratch_shapes=[pltpu.VMEM((1, LANES), F32)], compiler_params=_params(),
    )(dcum, fg, bf)


def _fox_blocks(T):
    tb = 512 if T >= 2048 else BLOCK
    return tb, T // tb


def _fox_fwd(name, qkv, cq_rep, ck, gather=()):
    T = qkv.shape[0]
    D = qkv.shape[1] // 3
    H = D // FOX_DH
    hps = FOX_HPS
    ng, wl = H // hps, hps * FOX_DH
    tb, nb = _fox_blocks(T)
    pairs = [(i, j) for i in range(nb) for j in range(i + 1)]
    qi = np.array([p[0] for p in pairs], np.int32)
    kj = np.array([p[1] for p in pairs], np.int32)
    npairs = len(pairs)
    nrep = tb // LANES
    nc = len(gather)

    def body(qi_ref, kj_ref, q_ref, k_ref, v_ref, cq_ref, ck_ref, *rest):
        cin, (o_ref, obf_ref, lse_ref), cout = rest[:nc], rest[nc:nc + 3], rest[nc + 3:2 * nc + 3]
        m_sc, l_sc, acc_sc = rest[2 * nc + 3:2 * nc + 6]
        sems = rest[2 * nc + 6:]
        g, p = pl.program_id(0), pl.program_id(1)
        i, j = qi_ref[p], kj_ref[p]

        if nc:
            @pl.when(jnp.logical_and(g == 0, p == 0))
            def _():
                _gather2_start(cin, cout, *sems)

        @pl.when(j == 0)
        def _():
            m_sc[...] = jnp.full_like(m_sc, NEG_INF)
            l_sc[...] = jnp.zeros_like(l_sc)
            acc_sc[...] = jnp.zeros_like(acc_sc)

        row = lax.broadcasted_iota(jnp.int32, (tb, tb), 0)
        col = lax.broadcasted_iota(jnp.int32, (tb, tb), 1)
        visible = jnp.logical_or(j < i, row >= col)
        for hh in range(hps):
            cols = slice(hh * FOX_DH, (hh + 1) * FOX_DH)
            s = _dot(q_ref[:, cols], k_ref[:, cols], NT) * FOX_SCALE
            s = jnp.where(visible, s + _rep(cq_ref[hh], nrep) - ck_ref[hh], NEG_INF)
            m_prev = m_sc[hh]
            m_new = jnp.maximum(m_prev, jnp.max(s, axis=-1, keepdims=True))
            alpha = jnp.exp(m_prev - m_new)
            pm = jnp.exp(s - _rep(m_new, nrep))
            l_sc[hh] = alpha * l_sc[hh] + jnp.sum(pm, axis=-1, keepdims=True)
            acc_sc[:, cols] = alpha * acc_sc[:, cols] + _dot(pm.astype(BF16), v_ref[:, cols], NN)
            m_sc[hh] = m_new

        @pl.when(j == i)
        def _():
            for hh in range(hps):
                cols = slice(hh * FOX_DH, (hh + 1) * FOX_DH)
                o = acc_sc[:, cols] / l_sc[hh]
                o_ref[:, cols] = o
                obf_ref[:, cols] = o.astype(BF16)
                lse_ref[hh] = m_sc[hh] + jnp.log(l_sc[hh])

        if nc:
            @pl.when(jnp.logical_and(g == ng - 1, p == npairs - 1))
            def _():
                _gather2_pass_on(cin, cout, *sems)
                _gather2_finish(cin, cout, *sems)

    any_spec = pl.BlockSpec(memory_space=pl.ANY)
    qblk = pl.BlockSpec((tb, wl), lambda g, p, qi, kj: (qi[p], g))
    qrep = pl.BlockSpec((hps, tb, LANES), lambda g, p, qi, kj: (g, qi[p], 0))
    grid_spec = pltpu.PrefetchScalarGridSpec(
        num_scalar_prefetch=2, grid=(ng, npairs),
        in_specs=[qblk,
                  pl.BlockSpec((tb, wl), lambda g, p, qi, kj: (kj[p], ng + g)),
                  pl.BlockSpec((tb, wl), lambda g, p, qi, kj: (kj[p], 2 * ng + g)),
                  qrep,
                  pl.BlockSpec((hps, 1, tb), lambda g, p, qi, kj: (g, 0, kj[p]))] + [any_spec] * nc,
        out_specs=[qblk, qblk, qrep] + [any_spec] * nc,
        scratch_shapes=[pltpu.VMEM((hps, tb, LANES), F32), pltpu.VMEM((hps, tb, LANES), F32),
                        pltpu.VMEM((tb, wl), F32)] + (_gather2_sems(nc) if nc else []))
    res = pl.pallas_call(
        body, name=name, grid_spec=grid_spec,
        out_shape=[jax.ShapeDtypeStruct((T, D), F32), jax.ShapeDtypeStruct((T, D), BF16),
                   jax.ShapeDtypeStruct((H, T, LANES), F32)] + _comm_out_shapes(gather, True),
        compiler_params=_params(),
    )(qi, kj, qkv, qkv, qkv, cq_rep, ck, *gather)
    return res[:3], res[3:]


def _fox_bwd_prep(do, o, lse_rep, cq_rep):
    T, D = do.shape
    H = D // FOX_DH
    tr = _tile(T, ROW_TILE, 8)

    def body(do_ref, o_ref, lse_ref, cq_ref, dob_ref, delta_ref, cql_ref):
        dov = do_ref[...]
        dob_ref[...] = dov.astype(BF16)
        prod = dov * o_ref[...]
        for h in range(H):
            d = jnp.sum(prod[:, h * FOX_DH:(h + 1) * FOX_DH], axis=-1, keepdims=True)
            delta_ref[h] = jnp.broadcast_to(d, (tr, LANES))
        cql_ref[...] = cq_ref[...] - lse_ref[...]

    row = pl.BlockSpec((tr, D), lambda i: (i, 0))
    rep = pl.BlockSpec((H, tr, LANES), lambda i: (0, i, 0))
    return pl.pallas_call(
        body, name="fox_bwd_prep", grid=(T // tr,),
        in_specs=[row, row, rep, rep], out_specs=[row, rep, rep],
        out_shape=[jax.ShapeDtypeStruct((T, D), BF16), jax.ShapeDtypeStruct((H, T, LANES), F32),
                   jax.ShapeDtypeStruct((H, T, LANES), F32)],
        compiler_params=_params(),
    )(do, o, lse_rep, cq_rep)


def _fox_bwd(name, qkv, dob, delta_rep, cql_rep, ck, a2a=()):
    T = qkv.shape[0]
    D = qkv.shape[1] // 3
    H = D // FOX_DH
    hps = FOX_HPS
    ng, wl = H // hps, hps * FOX_DH
    tb, nb = _fox_blocks(T)
    pairs = [(i, j) for j in range(nb) for i in range(j, nb)]
    qi = np.array([p[0] for p in pairs], np.int32)
    kj = np.array([p[1] for p in pairs], np.int32)
    npairs = len(pairs)
    nrep = tb // LANES
    nc = len(a2a)

    def body(qi_ref, kj_ref, q_ref, k_ref, v_ref, do_ref, delta_ref, cql_ref, ck_ref, *rest):
        cin, (dq_ref, dk_ref, dv_ref, dck_ref, dcq_ref), cout = rest[:nc], rest[nc:nc + 5], rest[nc + 5:2 * nc + 5]
        dq_acc, dk_acc, dv_acc, dc_acc = rest[2 * nc + 5:2 * nc + 9]
        sems = rest[2 * nc + 9:]
        g, p = pl.program_id(0), pl.program_id(1)
        i, j = qi_ref[p], kj_ref[p]

        @pl.when(jnp.logical_and(g == 0, p == 0))
        def _():
            dcq_ref[...] = jnp.zeros_like(dcq_ref)
            if nc:
                _comm_start(cin, cout, False, *sems)

        @pl.when(p == 0)
        def _():
            dq_acc[...] = jnp.zeros_like(dq_acc)

        @pl.when(i == j)
        def _():
            dk_acc[...] = jnp.zeros_like(dk_acc)
            dv_acc[...] = jnp.zeros_like(dv_acc)
            dc_acc[...] = jnp.zeros_like(dc_acc)

        row = lax.broadcasted_iota(jnp.int32, (tb, tb), 0)
        col = lax.broadcasted_iota(jnp.int32, (tb, tb), 1)
        visible = jnp.logical_or(j < i, row >= col)
        lane = lax.broadcasted_iota(jnp.int32, (tb, LANES), 1)
        rows = pl.ds(pl.multiple_of(i * tb, tb), tb)
        dcq = jnp.zeros((tb, LANES), F32)
        for hh in range(hps):
            cols = slice(hh * FOX_DH, (hh + 1) * FOX_DH)
            q, k, v, dov = q_ref[:, cols], k_ref[:, cols], v_ref[:, cols], do_ref[:, cols]
            s = _dot(q, k, NT) * FOX_SCALE + _rep(cql_ref[hh], nrep) - ck_ref[hh]
            pm = jnp.exp(jnp.where(visible, s, NEG_INF))
            dv_acc[:, cols] += _dot(pm.astype(BF16), dov, TN)
            ds = pm * (_dot(dov, v, NT) - _rep(delta_ref[hh], nrep))
            dsb = (ds * FOX_SCALE).astype(BF16)
            dk_acc[:, cols] += _dot(dsb, q, TN)
            dq_acc[rows, cols] += _dot(dsb, k, NN)
            dc_acc[hh] -= jnp.sum(ds, axis=0, keepdims=True)
            dcq = dcq + jnp.where(lane == g * hps + hh, jnp.sum(ds, axis=-1, keepdims=True), 0.0)
        dcq_ref[rows, :] += dcq

        @pl.when(i == nb - 1)
        def _():
            dk_ref[...] = dk_acc[...].astype(BF16)
            dv_ref[...] = dv_acc[...].astype(BF16)
            dck_ref[...] = dc_acc[...]

        @pl.when(p == npairs - 1)
        def _():
            dq_ref[...] = dq_acc[...].astype(BF16)

        if nc:
            @pl.when(jnp.logical_and(g == ng - 1, p == npairs - 1))
            def _():
                _comm_wait(cin, cout, False, *sems)

    any_spec = pl.BlockSpec(memory_space=pl.ANY)
    qblk = pl.BlockSpec((tb, wl), lambda g, p, qi, kj: (qi[p], g))
    qrep = pl.BlockSpec((hps, tb, LANES), lambda g, p, qi, kj: (g, qi[p], 0))
    kblk = pl.BlockSpec((tb, wl), lambda g, p, qi, kj: (kj[p], g))
    krow = pl.BlockSpec((hps, 1, tb), lambda g, p, qi, kj: (g, 0, kj[p]))
    grid_spec = pltpu.PrefetchScalarGridSpec(
        num_scalar_prefetch=2, grid=(ng, npairs),
        in_specs=[qblk,
                  pl.BlockSpec((tb, wl), lambda g, p, qi, kj: (kj[p], ng + g)),
                  pl.BlockSpec((tb, wl), lambda g, p, qi, kj: (kj[p], 2 * ng + g)),
                  qblk, qrep, qrep, krow] + [any_spec] * nc,
        out_specs=[pl.BlockSpec((T, wl), lambda g, p, qi, kj: (0, g)), kblk, kblk, krow,
                   pl.BlockSpec((T, LANES), lambda g, p, qi, kj: (0, 0))] + [any_spec] * nc,
        scratch_shapes=[pltpu.VMEM((T, wl), F32), pltpu.VMEM((tb, wl), F32), pltpu.VMEM((tb, wl), F32),
                        pltpu.VMEM((hps, 1, tb), F32)] + (_comm_sems(nc) if nc else []))
    act = jax.ShapeDtypeStruct((T, D), BF16)
    res = pl.pallas_call(
        body, name=name, grid_spec=grid_spec,
        out_shape=[act, act, act, jax.ShapeDtypeStruct((H, 1, T), F32),
                   jax.ShapeDtypeStruct((T, LANES), F32)] + _comm_out_shapes(a2a, False),
        compiler_params=_params(),
    )(qi, kj, qkv, qkv, qkv, dob, delta_rep, cql_rep, ck, *a2a)
    return res[:5], res[5:]


def _sgu_rows(T):
    return 2 * BLOCK if T % (2 * BLOCK) == 0 else BLOCK


def _sgu_norm(zv, g_ref, b_ref):
    vv = _gelu(zv)
    mu = jnp.mean(vv, axis=-1, keepdims=True)
    cen = vv - mu
    rstd = lax.rsqrt(jnp.mean(cen * cen, axis=-1, keepdims=True) + EPS)
    vh = cen * rstd
    return vh, rstd, vh * g_ref[...] + b_ref[...]


def _sgu_fwd(zpre, ln_g, ln_b, wsm, bs_rep):
    T = zpre.shape[0]
    W = zpre.shape[1] // 2
    G = W // BLOCK
    tr = _sgu_rows(T)

    def body(z_ref, g_ref, b_ref, ws_ref, bs_ref, o_ref):
        u = _gelu(z_ref[:, :W])
        _, _, vln = _sgu_norm(z_ref[:, W:], g_ref, b_ref)
        for c in range(tr // BLOCK):
            rows = slice(c * BLOCK, (c + 1) * BLOCK)
            for gi in range(G):
                cols = slice(gi * BLOCK, (gi + 1) * BLOCK)
                f = _dot(ws_ref[gi], vln[rows, cols].astype(BF16), NN) + bs_ref[gi]
                o_ref[rows, cols] = (u[rows, cols] * f).astype(BF16)

    full3 = pl.BlockSpec((G, BLOCK, BLOCK), lambda i: (0, 0, 0))
    return pl.pallas_call(
        body, name="sgu_fwd", grid=(T // tr,),
        in_specs=[pl.BlockSpec((tr, 2 * W), lambda i: (i, 0)), _vec_spec(W), _vec_spec(W), full3, full3],
        out_specs=pl.BlockSpec((tr, W), lambda i: (i, 0)),
        out_shape=jax.ShapeDtypeStruct((T, W), BF16), compiler_params=_params(),
    )(zpre, ln_g, ln_b, wsm, bs_rep)


def _sgu_bwd(dgt, zpre, ln_g, ln_b, wsm, wsmT, bs_rep):
    T = zpre.shape[0]
    W = zpre.shape[1] // 2
    G = W // BLOCK
    tr = BLOCK

    def body(dgt_ref, z_ref, g_ref, b_ref, ws_ref, wst_ref, bs_ref,
             dz_ref, dws_ref, dbs_ref, dlg_ref, dlb_ref, du_sc, dvln_sc):
        @pl.when(pl.program_id(0) == 0)
        def _():
            dws_ref[...] = jnp.zeros_like(dws_ref)
            dbs_ref[...] = jnp.zeros_like(dbs_ref)
            dlg_ref[...] = jnp.zeros_like(dlg_ref)
            dlb_ref[...] = jnp.zeros_like(dlb_ref)

        zu = z_ref[:, :W]
        zv = z_ref[:, W:]
        u = _gelu(zu)
        vh, rstd, vln = _sgu_norm(zv, g_ref, b_ref)
        dgtv = dgt_ref[...]
        trow = lax.broadcasted_iota(jnp.int32, (BLOCK, BLOCK), 0)
        tcol = lax.broadcasted_iota(jnp.int32, (BLOCK, BLOCK), 1)
        causal = trow >= tcol
        for c in range(tr // BLOCK):
            rows = slice(c * BLOCK, (c + 1) * BLOCK)
            for gi in range(G):
                cols = slice(gi * BLOCK, (gi + 1) * BLOCK)
                vb = vln[rows, cols].astype(BF16)
                f = _dot(ws_ref[gi], vb, NN) + bs_ref[gi]
                d = dgtv[rows, cols]
                du_sc[rows, cols] = d * f
                df = d * u[rows, cols]
                dfb = df.astype(BF16)
                dvln_sc[rows, cols] = _dot(wst_ref[gi], dfb, NN)
                dws_ref[gi] += jnp.where(causal, _dot(dfb, vb, NT), 0.0)
                dbs_ref[gi] += jnp.broadcast_to(jnp.sum(df, axis=-1, keepdims=True), (BLOCK, BLOCK))
        dvln = dvln_sc[...]
        dlg_ref[...] += jnp.sum(dvln * vh, axis=0, keepdims=True)
        dlb_ref[...] += jnp.sum(dvln, axis=0, keepdims=True)
        dvh = dvln * g_ref[...]
        dvv = rstd * (dvh - jnp.mean(dvh, axis=-1, keepdims=True)
                      - vh * jnp.mean(dvh * vh, axis=-1, keepdims=True))
        dz_ref[:, :W] = (du_sc[...] * _gelu_grad(zu)).astype(BF16)
        dz_ref[:, W:] = (dvv * _gelu_grad(zv)).astype(BF16)

    full3 = pl.BlockSpec((G, BLOCK, BLOCK), lambda i: (0, 0, 0))
    vec = jax.ShapeDtypeStruct((1, W), F32)
    acc3 = jax.ShapeDtypeStruct((G, BLOCK, BLOCK), F32)
    return pl.pallas_call(
        body, name="sgu_bwd", grid=(T // tr,),
        in_specs=[pl.BlockSpec((tr, W), lambda i: (i, 0)), pl.BlockSpec((tr, 2 * W), lambda i: (i, 0)),
                  _vec_spec(W), _vec_spec(W), full3, full3, full3],
        out_specs=[pl.BlockSpec((tr, 2 * W), lambda i: (i, 0)), full3, full3, _vec_spec(W), _vec_spec(W)],
        out_shape=[jax.ShapeDtypeStruct((T, 2 * W), BF16), acc3, acc3, vec, vec],
        scratch_shapes=[pltpu.VMEM((tr, W), F32), pltpu.VMEM((tr, W), F32)],
        compiler_params=_params(),
    )(dgt, zpre, ln_g, ln_b, wsm, wsmT, bs_rep)


def _rope(x, cos_t, sin_t):
    T, N = x.shape
    tr = _tile(T, ROW_TILE, 8)
    nrep = N // LANES
    half = ROPE_DIM // 2

    def body(x_ref, c_ref, s_ref, o_ref):
        xv = x_ref[...]
        lane = jnp.bitwise_and(lax.broadcasted_iota(jnp.int32, (tr, N), 1), SWA_DH - 1)
        partner = jnp.where(lane < half, -pltpu.roll(xv, N - half, 1), pltpu.roll(xv, half, 1))
        o_ref[...] = (xv * _rep(c_ref[...], nrep) + partner * _rep(s_ref[...], nrep)).astype(BF16)

    tab = pl.BlockSpec((tr, LANES), lambda i: (i, 0))
    row = pl.BlockSpec((tr, N), lambda i: (i, 0))
    return pl.pallas_call(
        body, name="rope", grid=(T // tr,), in_specs=[row, tab, tab], out_specs=row,
        out_shape=jax.ShapeDtypeStruct((T, N), BF16), compiler_params=_params(),
    )(x, cos_t, sin_t)


def _swa_tiles(T):
    sb = 4 if T >= 2048 else 2
    return sb, BLOCK * sb, T // (BLOCK * sb)


def _band_mask():
    row = lax.broadcasted_iota(jnp.int32, (BLOCK, 2 * BLOCK), 0)
    col = lax.broadcasted_iota(jnp.int32, (BLOCK, 2 * BLOCK), 1)
    return jnp.logical_and(col > row, col <= row + BLOCK), col


def _swa_specs(T, G):
    sb, tq, nq = _swa_tiles(T)
    q = pl.BlockSpec((None, tq, LANES), lambda h, i: (h, i, 0))
    kc = pl.BlockSpec((None, tq, LANES), lambda h, i: (h // G, i, 0))
    kp = pl.BlockSpec((None, BLOCK, LANES), lambda h, i: (h // G, jnp.maximum(i * sb - 1, 0), 0))
    return q, kc, kp


def _swa_band(b, i, kc_ref, kp_ref, vc_ref, vp_ref):
    rows = slice(b * BLOCK, (b + 1) * BLOCK)
    prev = slice((b - 1) * BLOCK, b * BLOCK)
    kprev = kp_ref[...] if b == 0 else kc_ref[prev, :]
    vprev = vp_ref[...] if b == 0 else vc_ref[prev, :]
    K = jnp.concatenate([kprev, kc_ref[rows, :]], axis=0)
    V = jnp.concatenate([vprev, vc_ref[rows, :]], axis=0)
    band, col = _band_mask()
    if b == 0:
        band = jnp.logical_and(band, jnp.logical_or(col >= BLOCK, i > 0))
    return rows, K, V, band


def _swa_fwd(qp, kp, vp, sinks):
    Hq, T, _ = qp.shape
    G = Hq // kp.shape[0]
    sb, tq, nq = _swa_tiles(T)

    def body(sink_ref, q_ref, kc_ref, kp_ref, vc_ref, vp_ref, o_ref, lse_ref):
        h, i = pl.program_id(0), pl.program_id(1)
        sink = sink_ref[h]
        for b in range(sb):
            rows, K, V, band = _swa_band(b, i, kc_ref, kp_ref, vc_ref, vp_ref)
            s = jnp.where(band, _dot(q_ref[rows, :], K, NT) * SWA_SCALE, NEG_INF)
            m = jnp.maximum(jnp.max(s, axis=-1, keepdims=True), sink)
            pm = jnp.exp(s - m)
            den = jnp.sum(pm, axis=-1, keepdims=True) + jnp.exp(sink - m)
            o_ref[rows, :] = _dot((pm / den).astype(BF16), V, NN)
            lse_ref[rows, :] = jnp.broadcast_to(m + jnp.log(den), (BLOCK, LANES))

    q, kc, kpv = _swa_specs(T, G)
    out = jax.ShapeDtypeStruct((Hq, T, LANES), F32)
    return pl.pallas_call(
        body, name="swa_fwd", grid=(Hq, nq),
        in_specs=[pl.BlockSpec(memory_space=pltpu.SMEM), q, kc, kpv, kc, kpv], out_specs=[q, q],
        out_shape=[out, out], compiler_params=_params(),
    )(sinks, qp, kp, kp, vp, vp)


def _swa_bwd_dq(qp, kp, vp, dop, op, lse_rep, sinks):
    Hq, T, _ = qp.shape
    G = Hq // kp.shape[0]
    sb, tq, nq = _swa_tiles(T)

    def body(sink_ref, q_ref, kc_ref, kp_ref, vc_ref, vp_ref, do_ref, o_ref, lse_ref, dq_ref, dsink_ref):
        h, i = pl.program_id(0), pl.program_id(1)
        sink = sink_ref[h]

        @pl.when(i == 0)
        def _():
            dsink_ref[...] = jnp.zeros_like(dsink_ref)

        for b in range(sb):
            rows, K, V, band = _swa_band(b, i, kc_ref, kp_ref, vc_ref, vp_ref)
            dov = do_ref[rows, :]
            delta = jnp.sum(dov * o_ref[rows, :], axis=-1, keepdims=True)
            lse = lse_ref[rows, :]
            s = jnp.where(band, _dot(q_ref[rows, :], K, NT) * SWA_SCALE, NEG_INF)
            pm = jnp.exp(s - _rep(lse, 2))
            dp = _dot(dov.astype(BF16), V, NT)
            ds = pm * (dp - delta)
            dq_ref[rows, :] = _dot((ds * SWA_SCALE).astype(BF16), K, NN)
            part = jnp.sum(jnp.exp(sink - lse) * delta, axis=0, keepdims=True)
            dsink_ref[...] -= jnp.broadcast_to(part, (8, LANES))

    q, kc, kpv = _swa_specs(T, G)
    return pl.pallas_call(
        body, name="swa_bwd_dq", grid=(Hq, nq),
        in_specs=[pl.BlockSpec(memory_space=pltpu.SMEM), q, kc, kpv, kc, kpv, q, q, q],
        out_specs=[q, pl.BlockSpec((None, 8, LANES), lambda h, i: (h, 0, 0))],
        out_shape=[jax.ShapeDtypeStruct((Hq, T, LANES), F32), jax.ShapeDtypeStruct((Hq, 8, LANES), F32)],
        compiler_params=_params(),
    )(sinks, qp, kp, kp, vp, vp, dop, op, lse_rep)


def _swa_bwd_dkv(qp, kp, vp, dop, op, lse_rep):
    Hq, T, _ = qp.shape
    Hk = kp.shape[0]
    G = Hq // Hk
    sb, tq, nq = _swa_tiles(T)
    nblk = T // BLOCK

    def body(k_ref, v_ref, q_ref, qn_ref, do_ref, don_ref, o_ref, on_ref, lse_ref, lsen_ref, dk_ref, dv_ref):
        i = pl.program_id(1)
        trow = lax.broadcasted_iota(jnp.int32, (2 * BLOCK, BLOCK), 0)
        scol = lax.broadcasted_iota(jnp.int32, (2 * BLOCK, BLOCK), 1)
        band0 = jnp.logical_and(trow >= scol, trow < scol + BLOCK)
        for b in range(sb):
            rows = slice(b * BLOCK, (b + 1) * BLOCK)
            nxt = slice((b + 1) * BLOCK, (b + 2) * BLOCK)
            last = b == sb - 1
            band = band0
            if last:
                band = jnp.logical_and(band0, jnp.logical_or(trow < BLOCK, i < nq - 1))
            kb, vb = k_ref[rows, :], v_ref[rows, :]
            dk = jnp.zeros((BLOCK, LANES), F32)
            dv = jnp.zeros((BLOCK, LANES), F32)
            for g in range(G):
                def two(cur, nx):
                    return jnp.concatenate([cur[g, rows, :], nx[g] if last else cur[g, nxt, :]], axis=0)
                Q, dov, ov, lse = two(q_ref, qn_ref), two(do_ref, don_ref), two(o_ref, on_ref), two(lse_ref, lsen_ref)
                delta = jnp.sum(dov * ov, axis=-1, keepdims=True)
                s = jnp.where(band, _dot(Q, kb, NT) * SWA_SCALE, NEG_INF)
                pm = jnp.exp(s - lse)
                dob = dov.astype(BF16)
                dv = dv + _dot(pm.astype(BF16), dob, TN)
                ds = pm * (_dot(dob, vb, NT) - delta)
                dk = dk + _dot((ds * SWA_SCALE).astype(BF16), Q, TN)
            dk_ref[rows, :] = dk
            dv_ref[rows, :] = dv

    kspec = pl.BlockSpec((None, tq, LANES), lambda h, i: (h, i, 0))
    cur = pl.BlockSpec((G, tq, LANES), lambda h, i: (h, i, 0))
    nxt = pl.BlockSpec((G, BLOCK, LANES), lambda h, i: (h, jnp.minimum((i + 1) * sb, nblk - 1), 0))
    out = jax.ShapeDtypeStruct((Hk, T, LANES), F32)
    return pl.pallas_call(
        body, name="swa_bwd_dkv", grid=(Hk, nq),
        in_specs=[kspec, kspec, cur, nxt, cur, nxt, cur, nxt, cur, nxt], out_specs=[kspec, kspec],
        out_shape=[out, out], compiler_params=_params(),
    )(kp, vp, qp, qp, dop, dop, op, op, lse_rep, lse_rep)


def _to_heads(a, nh):
    T = a.shape[0]
    a = a.reshape(T, nh, SWA_DH).transpose(1, 0, 2)
    return jnp.pad(a, ((0, 0), (0, 0), (0, LANES - SWA_DH)))


def _from_heads(a):
    nh, T, _ = a.shape
    return a[:, :, :SWA_DH].transpose(1, 0, 2).reshape(T, nh * SWA_DH)


def _adam(g, w, m, v):
    m2 = ADAM_B1 * m + (1.0 - ADAM_B1) * g
    v2 = ADAM_B2 * v + (1.0 - ADAM_B2) * (g * g)
    m_hat = m2 / (1.0 - ADAM_B1 ** ADAM_STEP)
    v_hat = v2 / (1.0 - ADAM_B2 ** ADAM_STEP)
    delta = -ADAM_LR * (m_hat / (jnp.sqrt(v_hat) + ADAM_EPS) + ADAM_WD * w)
    return delta, m2, v2


def _ada_fwd(c_all, w, b):
    L, D, n = w.shape
    tn = _tile(n, 768)

    def body(c_ref, w_ref, b_ref, o_ref):
        cv = c_ref[...]
        ca = (cv * _sigmoid(cv)).astype(BF16)
        o_ref[...] = _dot(ca, w_ref[...].astype(BF16), NN) + b_ref[...]

    return pl.pallas_call(
        body, name="ada_fwd", grid=(L, n // tn),
        in_specs=[pl.BlockSpec((NDEV, D), lambda l, j: (0, 0)), pl.BlockSpec((None, D, tn), lambda l, j: (l, 0, j)),
                  pl.BlockSpec((None, 1, tn), lambda l, j: (l, 0, j))],
        out_specs=pl.BlockSpec((None, NDEV, tn), lambda l, j: (l, 0, j)),
        out_shape=jax.ShapeDtypeStruct((L, NDEV, n), F32), compiler_params=_params(),
    )(c_all, w, b)


def _ada_update(c_rep, dm, w, m, v):
    L, D, n = w.shape
    tr = _tile(D, 256, 8)
    nrep = n // LANES

    def body(c_ref, dm_ref, w_ref, m_ref, v_ref, g_ref, d_ref, m2_ref, v2_ref):
        g = jnp.zeros((tr, n), F32)
        for b in range(NDEV):
            cv = c_ref[b]
            g = g + _rep(cv * _sigmoid(cv), nrep) * dm_ref[pl.ds(b, 1), :]
        g_ref[...] = g
        d_ref[...], m2_ref[...], v2_ref[...] = _adam(g, w_ref[...], m_ref[...], v_ref[...])

    blk = pl.BlockSpec((None, tr, n), lambda l, i: (l, i, 0))
    out = jax.ShapeDtypeStruct((L, D, n), F32)
    return pl.pallas_call(
        body, name="ada_update", grid=(L, D // tr),
        in_specs=[pl.BlockSpec((NDEV, tr, LANES), lambda l, i: (0, i, 0)),
                  pl.BlockSpec((None, NDEV, n), lambda l, i: (l, 0, 0)), blk, blk, blk],
        out_specs=[blk, blk, blk, blk], out_shape=[out, out, out, out], compiler_params=_params(),
    )(c_rep, dm, w, m, v)


def _adamw(name, parts, w, m, v, layer):
    P, R, C = parts.shape
    cpad = -(-C // LANES) * LANES
    per_row = cpad * (P * parts.dtype.itemsize + 7 * 4) * 2
    tr = _tile(R, max(8, (24 * 1024 * 1024 // per_row) // 8 * 8), 8)

    def body(p_ref, w_ref, m_ref, v_ref, g_ref, d_ref, m2_ref, v2_ref):
        g = p_ref[0].astype(F32)
        for s in range(1, P):
            g = g + p_ref[s].astype(F32)
        g_ref[...] = g
        d_ref[...], m2_ref[...], v2_ref[...] = _adam(g, w_ref[...], m_ref[...], v_ref[...])

    stk = pl.BlockSpec((None, tr, C), lambda i: (layer, i, 0))
    blk = pl.BlockSpec((tr, C), lambda i: (i, 0))
    out = jax.ShapeDtypeStruct((R, C), F32)
    return pl.pallas_call(
        body, name=name, grid=(R // tr,),
        in_specs=[pl.BlockSpec((P, tr, C), lambda i: (0, i, 0)), stk, stk, stk],
        out_specs=[blk, blk, blk, blk], out_shape=[out, out, out, out], compiler_params=_params(),
    )(parts, w, m, v)


def _colcat(a):
    s, k, n = a.shape
    return a.transpose(1, 0, 2).reshape(k, s * n)


def _colsplit(a):
    k, n8 = a.shape
    return a.reshape(k, NDEV, n8 // NDEV).transpose(1, 0, 2)


def _rowsplit(a):
    r, c = a.shape
    return a.reshape(NDEV, r // NDEV, c)


def kernel(x, c, positions, ada_w, ada_b, mix_pre_g, mix_post_g, ffn_pre_g, ffn_post_g, ffn_w_gu, ffn_w_down, fox_w_in, fox_b_f, fox_w_out, sgu_w_in, sgu_ln_g, sgu_ln_b, sgu_w_s, sgu_b_s, sgu_w_out, swa_w_in, swa_sinks, swa_w_out, loss_target, m_ada_w, m_ada_b, m_mix_pre_g, m_mix_post_g, m_ffn_pre_g, m_ffn_post_g, m_ffn_w_gu, m_ffn_w_down, m_fox_w_in, m_fox_b_f, m_fox_w_out, m_sgu_w_in, m_sgu_ln_g, m_sgu_ln_b, m_sgu_w_s, m_sgu_b_s, m_sgu_w_out, m_swa_w_in, m_swa_sinks, m_swa_w_out, v_ada_w, v_ada_b, v_mix_pre_g, v_mix_post_g, v_ffn_pre_g, v_ffn_post_g, v_ffn_w_gu, v_ffn_w_down, v_fox_w_in, v_fox_b_f, v_fox_w_out, v_sgu_w_in, v_sgu_ln_g, v_sgu_ln_b, v_sgu_w_s, v_sgu_b_s, v_sgu_w_out, v_swa_w_in, v_swa_sinks, v_swa_w_out):
    env = locals()
    W = {n: env[n] for n in WEIGHTS}
    M = {n: env["m_" + n] for n in WEIGHTS}
    V = {n: env["v_" + n] for n in WEIGHTS}

    me = 4 * lax.axis_index("x") + 2 * lax.axis_index("y") + lax.axis_index("c")
    _, T, D = x.shape
    L = ada_w.shape[0]
    n_ada = ada_w.shape[2]
    F = ffn_w_gu.shape[2] * NDEV // 2
    H = D // FOX_DH
    Hq = D // SWA_DH
    x0 = x[0]
    mixer = {0: 'fox', 1: 'sgu', 2: 'swa'}

    c_all = _comm("gather_c", [c], 'gather')[0].reshape(NDEV, D)
    ada_b_mine = lax.dynamic_slice_in_dim(ada_b, me * n_ada, n_ada, axis=1).reshape(L, 1, n_ada)
    mod_cols = _ada_fwd(c_all, ada_w, ada_b_mine)
    mod = _comm("a2a_mod", [mod_cols.transpose(1, 0, 2)], 'a2a')[0]
    mod = mod.transpose(1, 0, 2).reshape(L, 6, 1, D)

    inv = ROPE_THETA ** (-jnp.arange(0, ROPE_DIM, 2, dtype=F32) / ROPE_DIM)
    ang = positions[0].astype(F32)[:, None] * inv
    pad1 = jnp.ones((T, SWA_DH - ROPE_DIM), F32)
    cos64 = jnp.concatenate([jnp.cos(ang), jnp.cos(ang), pad1], axis=1)
    sin64 = jnp.concatenate([jnp.sin(ang), jnp.sin(ang), 0.0 * pad1], axis=1)
    cos_t = jnp.concatenate([cos64, cos64], axis=1)
    sin_t = jnp.concatenate([sin64, sin64], axis=1)

    fox_layers = [i for i in range(L) if mixer[i % 3] == 'fox']
    assert fox_layers and fox_layers[0] == 0

    def slice_of(i, role):
        kind, j = mixer[i % 3], i // 3
        src = {'wgu': (ffn_w_gu, i), 'wd': (ffn_w_down, i), 'win': (W[kind + '_w_in'], j), 'wout': (W[kind + '_w_out'], j)}[role]
        return src[0][src[1]].astype(BF16)

    def hosted_keys(f):
        later = [i for i in fox_layers if i > f]
        stop = later[0] if later else L
        keys = [(f, 'wout'), (f, 'wgu'), (f, 'wd')]
        for i in range(f + 1, stop):
            keys += [(i, 'wgu'), (i, 'wd'), (i, 'win'), (i, 'wout')]
        if later:
            keys += [(stop, 'win')]
        return keys

    raw, full = {}, {}
    first_keys = [(0, 'win')]
    raw.update(zip(first_keys, _gather2("gather_first", [slice_of(*k) for k in first_keys])))

    def wget(i, role):
        if (i, role) not in full:
            got = raw[(i, role)]
            full[(i, role)] = _colcat(got) if role in ('wgu', 'win') else got.reshape(-1, D)
        return full[(i, role)]

    saved = []
    xc = x0
    for i in range(L):
        kind, j = mixer[i % 3], i // 3
        s = dict(x_in=xc)
        sh_m, sc_m, g_m, sh_f, sc_f, g_f = [mod[i, t] for t in range(6)]
        h = _pre_fwd(xc, mix_pre_g[i:i + 1], sc_m, sh_m)
        s['h'] = h
        if kind == 'fox':
            wqkv = wget(i, 'win')[:, :3 * D]
            wf = jnp.pad(wget(i, 'win')[:, 3 * D:], ((0, 0), (0, LANES - H)))
            s['win_pad'] = jnp.concatenate([wqkv, wf], axis=1)
            bf = jnp.pad(fox_b_f[j:j + 1], ((0, 0), (0, LANES - H)))
            qkv = _mm("fox_qkv", h, wqkv, 'nn', BF16)
            fg = _mm("fox_fg", h, wf, 'nn', F32)
            cum, cq_rep = _fox_gate_fwd(fg, bf, H)
            ck = cum[:, :H].T.reshape(H, 1, T)
            keys = hosted_keys(i)
            (o, obf, lse_rep), got = _fox_fwd("fox_fwd%d" % j, qkv, cq_rep, ck, gather=[slice_of(*k) for k in keys])
            raw.update(zip(keys, got))
            s.update(qkv=qkv, fg=fg, bf=bf, cq_rep=cq_rep, ck=ck, o=o, lse_rep=lse_rep, mix_out=obf)
        elif kind == 'sgu':
            G = D // BLOCK
            causal = jnp.tril(jnp.ones((BLOCK, BLOCK), bool))
            wsm = jnp.where(causal[None], sgu_w_s[j], 0.0).astype(BF16)
            bs_rep = jnp.broadcast_to(sgu_b_s[j][:, :, None], (G, BLOCK, BLOCK))
            zpre = _mm("sgu_in", h, wget(i, 'win'), 'nn', F32)
            gated = _sgu_fwd(zpre, sgu_ln_g[j:j + 1], sgu_ln_b[j:j + 1], wsm, bs_rep)
            s.update(zpre=zpre, wsm=wsm, bs_rep=bs_rep, mix_out=gated)
        else:
            Hk = (wget(i, 'win').shape[1] // SWA_DH - Hq) // 2
            proj = _mm("swa_in", h, wget(i, 'win'), 'nn', F32)
            qr = _rope(proj[:, :Hq * SWA_DH], cos_t, sin_t)
            kr = _rope(proj[:, Hq * SWA_DH:(Hq + Hk) * SWA_DH], cos_t, sin_t)
            qp, kp = _to_heads(qr, Hq), _to_heads(kr, Hk)
            vp = _to_heads(proj[:, (Hq + Hk) * SWA_DH:].astype(BF16), Hk)
            op, lse_rep = _swa_fwd(qp, kp, vp, swa_sinks[j])
            s.update(qp=qp, kp=kp, vp=vp, op=op, lse_rep=lse_rep, Hk=Hk, mix_out=_from_heads(op).astype(BF16))
        y = _mm("mix_out", s['mix_out'], wget(i, 'wout'), 'nn', F32)
        x_mid = _post_fwd(xc, y, mix_post_g[i:i + 1], g_m)
        s.update(y_mix=y, x_mid=x_mid)
        h2 = _pre_fwd(x_mid, ffn_pre_g[i:i + 1], sc_f, sh_f)
        g, u, a = _ffn_up(h2, wget(i, 'wgu'))
        y2 = _mm("ffn_down", a, wget(i, 'wd'), 'nn', F32)
        xc = _post_fwd(x_mid, y2, ffn_post_g[i:i + 1], g_f)
        s.update(h2=h2, g=g, u=u, a=a, y_ffn=y2)
        saved.append(s)

    dx, lsum = _loss(xc, loss_target[0])
    loss = lax.psum(0.5 * lsum[0, 0] / D, AXES)

    small = {n: [None] * W[n].shape[0] for n in SMALL}
    dmod = [None] * L
    out = {n: dict(g=[None] * W[n].shape[0], d=[None] * W[n].shape[0], m=[None] * W[n].shape[0],
                   v=[None] * W[n].shape[0]) for n in WEIGHTS}

    pending = []

    def update(items, recv):
        for (name, idx, _), parts in zip(items, recv):
            res = _adamw("adamw_" + name, parts, W[name], M[name], V[name], idx)
            for key, val in zip("gdmv", res):
                out[name][key][idx] = val

    for i in reversed(range(L)):
        kind, j = mixer[i % 3], i // 3
        s = saved[i]
        sh_m, sc_m, g_m, sh_f, sc_f, g_f = [mod[i, t] for t in range(6)]
        dy2, dg_f, dpost_f = _post_bwd(dx, s['y_ffn'], ffn_post_g[i:i + 1], g_f)
        dwd = _mm("ffn_dwd", s['a'], dy2, 'tn', BF16)
        dg, du = _ffn_dact(dy2, wget(i, 'wd'), s['g'], s['u'])
        dwgu = jnp.concatenate([_mm("ffn_dwg", s['h2'], dg, 'tn', BF16), _mm("ffn_dwu", s['h2'], du, 'tn', BF16)], axis=1)
        dh2 = _matmul("ffn_dh", [(dg, wget(i, 'wgu'), (0, 0), (0, 0)), (du, wget(i, 'wgu'), (0, 0), (0, F))], 'nt', F32,
                      T, D, F)
        pending += [('ffn_w_gu', i, _colsplit(dwgu)), ('ffn_w_down', i, _rowsplit(dwd))]
        dx, dsh_f, dsc_f, dpre_f = _pre_bwd(dh2, s['x_mid'], ffn_pre_g[i:i + 1], sc_f, dx)
        dy, dg_m, dpost_m = _post_bwd(dx, s['y_mix'], mix_post_g[i:i + 1], g_m)
        dwout = _mm("mix_dwout", s['mix_out'], dy, 'tn', BF16)
        pending.append((kind + '_w_out', j, _rowsplit(dwout)))
        dmix = _mm("mix_dout", dy, wget(i, 'wout'), 'nt', F32)
        if kind == 'fox':
            dob, delta_rep, cql_rep = _fox_bwd_prep(dmix, s['o'], s['lse_rep'], s['cq_rep'])
            (dq, dk, dv, dck, dcq), recv = _fox_bwd("fox_bwd%d" % j, s['qkv'], dob, delta_rep, cql_rep, s['ck'],
                                                    a2a=[item[2] for item in pending])
            update(pending, recv)
            pending = []
            dcum = jnp.pad(dck.reshape(H, T).T, ((0, 0), (0, LANES - H))) + dcq
            dfg, dbf = _fox_gate_bwd(dcum, s['fg'], s['bf'])
            small['fox_b_f'][j] = dbf[0, :H]
            dproj = jnp.concatenate([dq, dk, dv, dfg.astype(BF16)], axis=1)
            dwin = _mm("fox_dwin", s['h'], dproj, 'tn', BF16)[:, :3 * D + H]
            pending.append((kind + '_w_in', j, _colsplit(dwin)))
            if i == 0:
                dh, recv = _mm("fox_dh_last", dproj, s['win_pad'], 'nt', F32, a2a=[item[2] for item in pending])
                update(pending, recv)
                pending = []
            else:
                dh = _mm("fox_dh", dproj, s['win_pad'], 'nt', F32)
        elif kind == 'sgu':
            wsmT = s['wsm'].transpose(0, 2, 1)
            dz, dws, dbs, dlg, dlb = _sgu_bwd(dmix, s['zpre'], sgu_ln_g[j:j + 1], sgu_ln_b[j:j + 1], s['wsm'], wsmT, s['bs_rep'])
            small['sgu_w_s'][j], small['sgu_b_s'][j] = dws, dbs[:, :, 0]
            small['sgu_ln_g'][j], small['sgu_ln_b'][j] = dlg[0], dlb[0]
            dwin = _mm("sgu_dwin", s['h'], dz, 'tn', BF16)
            pending.append((kind + '_w_in', j, _colsplit(dwin)))
            dh = _mm("sgu_dh", dz, wget(i, 'win'), 'nt', F32)
        else:
            Hk = s['Hk']
            dop = _to_heads(dmix, Hq)
            dqp, dsink = _swa_bwd_dq(s['qp'], s['kp'], s['vp'], dop, s['op'], s['lse_rep'], swa_sinks[j])
            dkp, dvp = _swa_bwd_dkv(s['qp'], s['kp'], s['vp'], dop, s['op'], s['lse_rep'])
            small['swa_sinks'][j] = dsink[:, 0, 0]
            dproj = jnp.concatenate([_rope(_from_heads(dqp), cos_t, -sin_t), _rope(_from_heads(dkp), cos_t, -sin_t),
                                     _from_heads(dvp).astype(BF16)], axis=1)
            dwin = _mm("swa_dwin", s['h'], dproj, 'tn', BF16)
            pending.append((kind + '_w_in', j, _colsplit(dwin)))
            dh = _mm("swa_dh", dproj, wget(i, 'win'), 'nt', F32)
        dx, dsh_m, dsc_m, dpre_m = _pre_bwd(dh, s['x_in'], mix_pre_g[i:i + 1], sc_m, dx)
        small['mix_pre_g'][i], small['mix_post_g'][i] = dpre_m[0], dpost_m[0]
        small['ffn_pre_g'][i], small['ffn_post_g'][i] = dpre_f[0], dpost_f[0]
        dmod[i] = jnp.concatenate([dsh_m, dsc_m, dg_m, dsh_f, dsc_f, dg_f], axis=1)[0]

    if pending:
        update(pending, _comm("a2a_last", [item[2] for item in pending], 'a2a'))
    grad_x = dx[None]

    small['ada_b'] = dmod
    flat = jnp.concatenate([jnp.stack(small[n]).reshape(-1) for n in SMALL])
    width = 8 * LANES
    npad = -flat.shape[0] % (8 * width)
    packed = jnp.pad(flat, (0, npad)).reshape(-1, width)
    parts = _comm("gather_small", [packed], 'gather')[0]

    def pack(d):
        f = jnp.concatenate([d[n].reshape(-1) for n in SMALL])
        return jnp.pad(f, (0, npad)).reshape(1, -1, width)

    res = _adamw("adamw_small", parts, pack(W), pack(M), pack(V), 0)
    off = 0
    for n in SMALL:
        size = W[n].size
        for key, val in zip("gdmv", res):
            out[n][key] = val.reshape(-1)[off:off + size].reshape(W[n].shape)
        off += size

    dmod_all = parts.reshape(NDEV, -1)[:, :L * 6 * D].reshape(NDEV, L, 6 * D)
    dm = lax.dynamic_slice_in_dim(dmod_all, me * n_ada, n_ada, axis=2).transpose(1, 0, 2)
    c_rep = jnp.broadcast_to(c_all[:, :, None], (NDEV, D, LANES))
    for key, val in zip("gdmv", _ada_update(c_rep, dm, ada_w, m_ada_w, v_ada_w)):
        out['ada_w'][key] = val

    def leaf(n, key):
        val = out[n][key]
        return jnp.stack(val) if isinstance(val, list) else val

    return (loss, grad_x, *[leaf(n, 'g') for n in WEIGHTS], *[leaf(n, 'd') for n in WEIGHTS],
            *[leaf(n, 'm') for n in WEIGHTS], *[leaf(n, 'v') for n in WEIGHTS])
```

```python
import numpy as np
import jax
import jax.numpy as jnp
from jax import lax
from jax.experimental import pallas as pl
from jax.experimental.pallas import tpu as pltpu

F32 = jnp.float32
BF16 = jnp.bfloat16
NDEV = 8
AXES = ("x", "y", "c")
LANES = 128
VMEM_LIMIT_BYTES = 48 * 1024 * 1024
NEG_INF = float("-inf")
ROW_TILE = 256

EPS = 1e-6
BLOCK = 128
FOX_DH = 128
FOX_HPS = 2
FOX_FWD_BLOCK = 1024
FOX_BWD_QUERIES = 1024
FOX_BWD_KEYS = 512
SWA_DH = 64
ROPE_DIM = 16
ROPE_THETA = 500000.0
FOX_SCALE = FOX_DH ** -0.5
SWA_SCALE = SWA_DH ** -0.5
GELU_C0 = 0.7978845608028654
GELU_C1 = 0.044715

ADAM_LR = 0.001
ADAM_B1 = 0.9
ADAM_B2 = 0.999
ADAM_EPS = 1e-08
ADAM_WD = 0.01
ADAM_STEP = 10

NN = (((1,), (0,)), ((), ()))
NT = (((1,), (1,)), ((), ()))
TN = (((0,), (0,)), ((), ()))

WEIGHTS = ['ada_w', 'ada_b', 'mix_pre_g', 'mix_post_g', 'ffn_pre_g', 'ffn_post_g', 'ffn_w_gu', 'ffn_w_down',
           'fox_w_in', 'fox_b_f', 'fox_w_out', 'sgu_w_in', 'sgu_ln_g', 'sgu_ln_b', 'sgu_w_s', 'sgu_b_s',
           'sgu_w_out', 'swa_w_in', 'swa_sinks', 'swa_w_out']
SMALL = ['ada_b', 'mix_pre_g', 'mix_post_g', 'ffn_pre_g', 'ffn_post_g', 'fox_b_f', 'sgu_ln_g', 'sgu_ln_b',
         'sgu_w_s', 'sgu_b_s', 'swa_sinks']


def _dot(a, b, dims):
    return lax.dot_general(a, b, dims, preferred_element_type=F32)


def _tile(n, pref, mult=LANES):
    t = (min(pref, n) // mult) * mult
    while t >= mult:
        if n % t == 0:
            return t
        t -= mult
    return n


def _params():
    return pltpu.CompilerParams(vmem_limit_bytes=VMEM_LIMIT_BYTES)


def _rep(a, n):
    return a if n == 1 else jnp.concatenate([a] * n, axis=-1)


def _vec_spec(d):
    return pl.BlockSpec((1, d), lambda *_: (0, 0))


def _sigmoid(z):
    return 1.0 / (1.0 + jnp.exp(-z))


def _gelu(z):
    t = jnp.tanh(GELU_C0 * (z + GELU_C1 * z * z * z))
    return 0.5 * z * (1.0 + t)


def _gelu_grad(z):
    t = jnp.tanh(GELU_C0 * (z + GELU_C1 * z * z * z))
    return 0.5 * (1.0 + t) + 0.5 * z * (1.0 - t * t) * GELU_C0 * (1.0 + 3.0 * GELU_C1 * z * z)


def _comm_out_shapes(arrs, gather):
    return [jax.ShapeDtypeStruct(((NDEV,) + a.shape) if gather else a.shape, a.dtype) for a in arrs]


def _comm_sems(n):
    return [pltpu.SemaphoreType.DMA((n,)), pltpu.SemaphoreType.DMA((n,)), pltpu.SemaphoreType.DMA((n,))]


def _me():
    x, y, c = lax.axis_index("x"), lax.axis_index("y"), lax.axis_index("c")
    return x, y, c, 4 * x + 2 * y + c


def _comm_start(ins, outs, gather, send_sems, recv_sems, local_sems):
    x, y, c, me = _me()
    for a in range(len(ins)):
        pltpu.make_async_copy(ins[a] if gather else ins[a].at[me], outs[a].at[me], local_sems.at[a]).start()
        for bits in range(1, NDEV):
            px = (1 - x) if bits & 4 else x
            py = (1 - y) if bits & 2 else y
            pc = (1 - c) if bits & 1 else c
            pltpu.make_async_remote_copy(
                src_ref=ins[a] if gather else ins[a].at[4 * px + 2 * py + pc], dst_ref=outs[a].at[me],
                send_sem=send_sems.at[a], recv_sem=recv_sems.at[a],
                device_id=(px, py, pc), device_id_type=pl.DeviceIdType.MESH).start()


def _comm_wait(ins, outs, gather, send_sems, recv_sems, local_sems):
    x, y, c, me = _me()
    for a in range(len(ins)):
        seven = outs[a].at[pl.ds(0, NDEV - 1)]
        pltpu.make_async_remote_copy(src_ref=seven, dst_ref=seven, send_sem=send_sems.at[a], recv_sem=recv_sems.at[a],
                                     device_id=(x, y, c), device_id_type=pl.DeviceIdType.MESH).wait()
        pltpu.make_async_copy(ins[a] if gather else ins[a].at[me], outs[a].at[me], local_sems.at[a]).wait()


def _comm(name, arrs, kind):
    n = len(arrs)
    gather = kind == 'gather'

    def body(*refs):
        ins, outs, sems = refs[:n], refs[n:2 * n], refs[2 * n:]
        _comm_start(ins, outs, gather, *sems)
        _comm_wait(ins, outs, gather, *sems)

    any_spec = pl.BlockSpec(memory_space=pl.ANY)
    return pl.pallas_call(
        body, name=name, out_shape=_comm_out_shapes(arrs, gather),
        in_specs=[any_spec] * n, out_specs=[any_spec] * n, scratch_shapes=_comm_sems(n),
    )(*arrs)


def _gather2_sems(n):
    return [pltpu.SemaphoreType.DMA((n,)) for _ in range(4)]


def _remote(src, dst, send_sem, recv_sem, device):
    return pltpu.make_async_remote_copy(src_ref=src, dst_ref=dst, send_sem=send_sem, recv_sem=recv_sem,
                                        device_id=device, device_id_type=pl.DeviceIdType.MESH)


def _gather2_start(ins, outs, send_sems, ici_sems, d2d_sems, local_sems):
    x, y, c, me = _me()
    for a in range(len(ins)):
        pltpu.make_async_copy(ins[a], outs[a].at[me], local_sems.at[a]).start()
        _remote(ins[a], outs[a].at[me], send_sems.at[a], d2d_sems.at[a], (x, y, 1 - c)).start()
        for px, py in ((1 - x, y), (x, 1 - y), (1 - x, 1 - y)):
            _remote(ins[a], outs[a].at[me], send_sems.at[a], ici_sems.at[a], (px, py, c)).start()


def _gather2_pass_on(ins, outs, send_sems, ici_sems, d2d_sems, local_sems):
    x, y, c, me = _me()
    for a in range(len(ins)):
        three = outs[a].at[pl.ds(0, 3)]
        _remote(three, three, send_sems.at[a], ici_sems.at[a], (x, y, c)).wait_recv()
        for px, py in ((1 - x, y), (x, 1 - y), (1 - x, 1 - y)):
            slot = outs[a].at[4 * px + 2 * py + c]
            _remote(slot, slot, send_sems.at[a], d2d_sems.at[a], (x, y, 1 - c)).start()


def _gather2_finish(ins, outs, send_sems, ici_sems, d2d_sems, local_sems):
    x, y, c, me = _me()
    for a in range(len(ins)):
        four, seven = outs[a].at[pl.ds(0, 4)], outs[a].at[pl.ds(0, NDEV - 1)]
        _remote(four, four, send_sems.at[a], d2d_sems.at[a], (x, y, c)).wait_recv()
        _remote(seven, seven, send_sems.at[a], d2d_sems.at[a], (x, y, c)).wait_send()
        pltpu.make_async_copy(ins[a], outs[a].at[me], local_sems.at[a]).wait()


def _gather2(name, arrs):
    n = len(arrs)

    def body(*refs):
        ins, outs, sems = refs[:n], refs[n:2 * n], refs[2 * n:]
        _gather2_start(ins, outs, *sems)
        _gather2_pass_on(ins, outs, *sems)
        _gather2_finish(ins, outs, *sems)

    any_spec = pl.BlockSpec(memory_space=pl.ANY)
    return pl.pallas_call(
        body, name=name, out_shape=_comm_out_shapes(arrs, True),
        in_specs=[any_spec] * n, out_specs=[any_spec] * n, scratch_shapes=_gather2_sems(n),
    )(*arrs)


MATMUL_VMEM_BUDGET = 38 * 1024 * 1024


def _matmul_tiles(mode, M, N, K, npairs, out_size):
    def uniq(vals):
        return sorted(set(vals), reverse=True)

    tms = uniq(_tile(M, p) for p in (1024, 512, 256))
    tns = uniq(_tile(N, p) for p in (1536, 1024, 512, 256))
    tks = uniq(_tile(K, p) for p in (2048, 1024)) if mode == 'tn' else [K] + uniq(_tile(K, p) for p in (2048, 1024))
    for tk in tks:
        best = None
        for tm in tms:
            for tn in tns:
                steps = K // tk
                need = (2 * npairs * (tm + tn) * tk * 2 + 2 * tm * tn * out_size + npairs * tm * tn * 4
                        + (tm * tn * 4 if steps > 1 else 0) + (tk * tm * 2 if mode == 'tn' else 0))
                if need <= MATMUL_VMEM_BUDGET and (best is None or (tm * tn, tm) > (best[0] * best[1], best[0])):
                    best = (tm, tn, tk)
        if best is not None:
            return best
    return _tile(M, 256), _tile(N, 256), _tile(K, 512)


def _matmul(name, pairs, mode, out_dtype, M, N, K, tm=None, tn=None, tk=None, a2a=()):
    if not (tm and tn and tk):
        tm, tn, tk = _matmul_tiles(mode, M, N, K, len(pairs), jnp.dtype(out_dtype).itemsize)
    nk = K // tk
    dims = {'nn': NN, 'nt': NT, 'tn': TN}[mode]
    in_specs, ops = [], []
    for a, b, ao, bo in pairs:
        if mode == 'tn':
            assert ao[0] % tk == 0 and ao[1] % tm == 0
            sa = pl.BlockSpec((tk, tm), lambda i, j, k, r=ao[0] // tk, c=ao[1] // tm: (k + r, i + c))
        else:
            assert ao[0] % tm == 0 and ao[1] % tk == 0
            sa = pl.BlockSpec((tm, tk), lambda i, j, k, r=ao[0] // tm, c=ao[1] // tk: (i + r, k + c))
        if mode == 'nt':
            assert bo[0] % tn == 0 and bo[1] % tk == 0
            sb = pl.BlockSpec((tn, tk), lambda i, j, k, r=bo[0] // tn, c=bo[1] // tk: (j + r, k + c))
        else:
            assert bo[0] % tk == 0 and bo[1] % tn == 0
            sb = pl.BlockSpec((tk, tn), lambda i, j, k, r=bo[0] // tk, c=bo[1] // tn: (k + r, j + c))
        in_specs += [sa, sb]
        ops += [a, b]
    npairs = len(pairs)
    nc = len(a2a)
    grid = (M // tm, N // tn, nk)

    def body(*refs):
        cin, o_ref, cout = refs[2 * npairs:2 * npairs + nc], refs[2 * npairs + nc], refs[2 * npairs + nc + 1:2 * npairs + 2 * nc + 1]
        scratch = refs[2 * npairs + 2 * nc + 1:]
        sems = scratch[1:] if nk > 1 else scratch
        i, j, k = pl.program_id(0), pl.program_id(1), pl.program_id(2)

        if nc:
            @pl.when(jnp.logical_and(jnp.logical_and(i == 0, j == 0), k == 0))
            def _():
                _comm_start(cin, cout, False, *sems)

        part = _dot(refs[0][...], refs[1][...], dims)
        for p in range(1, npairs):
            part = part + _dot(refs[2 * p][...], refs[2 * p + 1][...], dims)
        if nk == 1:
            o_ref[...] = part.astype(out_dtype)
        else:
            acc = scratch[0]

            @pl.when(k == 0)
            def _():
                acc[...] = part

            @pl.when(k > 0)
            def _():
                acc[...] += part

            @pl.when(k == nk - 1)
            def _():
                o_ref[...] = acc[...].astype(out_dtype)

        if nc:
            @pl.when(jnp.logical_and(jnp.logical_and(i == grid[0] - 1, j == grid[1] - 1), k == nk - 1))
            def _():
                _comm_wait(cin, cout, False, *sems)

    any_spec = pl.BlockSpec(memory_space=pl.ANY)
    res = pl.pallas_call(
        body, name=name, grid=grid,
        in_specs=in_specs + [any_spec] * nc,
        out_specs=[pl.BlockSpec((tm, tn), lambda i, j, k: (i, j))] + [any_spec] * nc,
        out_shape=[jax.ShapeDtypeStruct((M, N), out_dtype)] + _comm_out_shapes(a2a, False),
        scratch_shapes=([] if nk == 1 else [pltpu.VMEM((tm, tn), F32)]) + (_comm_sems(nc) if nc else []),
        compiler_params=_params(),
    )(*ops, *a2a)
    return (res[0], res[1:]) if nc else res[0]


def _mm(name, a, b, mode, out_dtype, a2a=()):
    if mode == 'nn':
        (M, K), N = a.shape, b.shape[1]
    elif mode == 'nt':
        (M, K), N = a.shape, b.shape[0]
    else:
        (K, M), N = a.shape, b.shape[1]
    return _matmul(name, [(a, b, (0, 0), (0, 0))], mode, out_dtype, M, N, K, a2a=a2a)


def _ffn_up(h, wgu):
    T, D = h.shape
    F = wgu.shape[1] // 2
    tm, tn = _tile(T, 1024), _tile(F, 512)

    def body(h_ref, wg_ref, wu_ref, g_ref, u_ref, a_ref):
        hv = h_ref[...]
        g = _dot(hv, wg_ref[...], NN)
        u = _dot(hv, wu_ref[...], NN)
        g_ref[...] = g
        u_ref[...] = u
        a_ref[...] = (g * _sigmoid(g) * u).astype(BF16)

    out = pl.BlockSpec((tm, tn), lambda i, j: (i, j))
    return pl.pallas_call(
        body, name="ffn_up", grid=(T // tm, F // tn),
        in_specs=[pl.BlockSpec((tm, D), lambda i, j: (i, 0)),
                  pl.BlockSpec((D, tn), lambda i, j: (0, j)),
                  pl.BlockSpec((D, tn), lambda i, j, o=F // tn: (0, j + o))],
        out_specs=[out, out, out],
        out_shape=[jax.ShapeDtypeStruct((T, F), F32), jax.ShapeDtypeStruct((T, F), F32),
                   jax.ShapeDtypeStruct((T, F), BF16)],
        compiler_params=_params(),
    )(h, wgu, wgu)


def _ffn_dact(dy, wd, g, u):
    T, D = dy.shape
    F = wd.shape[0]
    tm, tn = _tile(T, 1024), _tile(F, 512)

    nsplit = 2 if tn % (2 * LANES) == 0 else 1

    def body(dy_ref, wd_ref, g_ref, u_ref, dg_ref, du_ref):
        dyv = dy_ref[...]
        for c in range(nsplit):
            cols = slice(c * (tn // nsplit), (c + 1) * (tn // nsplit))
            da = _dot(dyv, wd_ref[cols, :], NT)
            g = g_ref[:, cols]
            sg = _sigmoid(g)
            dg_ref[:, cols] = (da * u_ref[:, cols] * (sg * (1.0 + g * (1.0 - sg)))).astype(BF16)
            du_ref[:, cols] = (da * (g * sg)).astype(BF16)

    blk = pl.BlockSpec((tm, tn), lambda i, j: (i, j))
    return pl.pallas_call(
        body, name="ffn_dact", grid=(T // tm, F // tn),
        in_specs=[pl.BlockSpec((tm, D), lambda i, j: (i, 0)), pl.BlockSpec((tn, D), lambda i, j: (j, 0)), blk, blk],
        out_specs=[blk, blk],
        out_shape=[jax.ShapeDtypeStruct((T, F), BF16)] * 2,
        compiler_params=_params(),
    )(dy, wd, g, u)


def _rstd(v):
    return lax.rsqrt(jnp.mean(v * v, axis=-1, keepdims=True) + EPS)


def _pre_fwd(x, g, sc, sh):
    T, D = x.shape
    tr = _tile(T, ROW_TILE, 8)

    def body(x_ref, g_ref, sc_ref, sh_ref, h_ref):
        xv = x_ref[...]
        r = xv * _rstd(xv) * g_ref[...]
        h_ref[...] = (r * (1.0 + sc_ref[...]) + sh_ref[...]).astype(BF16)

    row = pl.BlockSpec((tr, D), lambda i: (i, 0))
    return pl.pallas_call(
        body, name="pre_fwd", grid=(T // tr,),
        in_specs=[row, _vec_spec(D), _vec_spec(D), _vec_spec(D)], out_specs=row,
        out_shape=jax.ShapeDtypeStruct((T, D), BF16), compiler_params=_params(),
    )(x, g, sc, sh)


def _post_fwd(x, y, g, gate):
    T, D = x.shape
    tr = _tile(T, ROW_TILE, 8)

    def body(x_ref, y_ref, g_ref, gate_ref, o_ref):
        yv = y_ref[...]
        o_ref[...] = x_ref[...] + gate_ref[...] * (yv * _rstd(yv) * g_ref[...])

    row = pl.BlockSpec((tr, D), lambda i: (i, 0))
    return pl.pallas_call(
        body, name="post_fwd", grid=(T // tr,),
        in_specs=[row, row, _vec_spec(D), _vec_spec(D)], out_specs=row,
        out_shape=jax.ShapeDtypeStruct((T, D), F32), compiler_params=_params(),
    )(x, y, g, gate)


def _post_bwd(dx, y, g, gate):
    T, D = dx.shape
    tr = _tile(T, ROW_TILE, 8)

    def body(dx_ref, y_ref, g_ref, gate_ref, dy_ref, dgate_ref, dg_ref):
        @pl.when(pl.program_id(0) == 0)
        def _():
            dgate_ref[...] = jnp.zeros_like(dgate_ref)
            dg_ref[...] = jnp.zeros_like(dg_ref)

        yv, dxv = y_ref[...], dx_ref[...]
        rstd = _rstd(yv)
        yh = yv * rstd
        dgate_ref[...] += jnp.sum(dxv * (yh * g_ref[...]), axis=0, keepdims=True)
        dn = dxv * gate_ref[...]
        dg_ref[...] += jnp.sum(dn * yh, axis=0, keepdims=True)
        dyh = dn * g_ref[...]
        dy_ref[...] = (rstd * (dyh - yh * jnp.mean(dyh * yh, axis=-1, keepdims=True))).astype(BF16)

    row = pl.BlockSpec((tr, D), lambda i: (i, 0))
    vec = jax.ShapeDtypeStruct((1, D), F32)
    return pl.pallas_call(
        body, name="post_bwd", grid=(T // tr,),
        in_specs=[row, row, _vec_spec(D), _vec_spec(D)], out_specs=[row, _vec_spec(D), _vec_spec(D)],
        out_shape=[jax.ShapeDtypeStruct((T, D), BF16), vec, vec], compiler_params=_params(),
    )(dx, y, g, gate)


def _pre_bwd(dh, x, g, sc, dx_res):
    T, D = x.shape
    tr = _tile(T, ROW_TILE, 8)

    def body(dh_ref, x_ref, g_ref, sc_ref, dxr_ref, dx_ref, dsh_ref, dsc_ref, dg_ref):
        @pl.when(pl.program_id(0) == 0)
        def _():
            dsh_ref[...] = jnp.zeros_like(dsh_ref)
            dsc_ref[...] = jnp.zeros_like(dsc_ref)
            dg_ref[...] = jnp.zeros_like(dg_ref)

        xv, dhv = x_ref[...], dh_ref[...]
        rstd = _rstd(xv)
        xh = xv * rstd
        dsh_ref[...] += jnp.sum(dhv, axis=0, keepdims=True)
        dsc_ref[...] += jnp.sum(dhv * (xh * g_ref[...]), axis=0, keepdims=True)
        dr = dhv * (1.0 + sc_ref[...])
        dg_ref[...] += jnp.sum(dr * xh, axis=0, keepdims=True)
        dxh = dr * g_ref[...]
        dx_ref[...] = dxr_ref[...] + rstd * (dxh - xh * jnp.mean(dxh * xh, axis=-1, keepdims=True))

    row = pl.BlockSpec((tr, D), lambda i: (i, 0))
    vec = jax.ShapeDtypeStruct((1, D), F32)
    return pl.pallas_call(
        body, name="pre_bwd", grid=(T // tr,),
        in_specs=[row, row, _vec_spec(D), _vec_spec(D), row],
        out_specs=[row, _vec_spec(D), _vec_spec(D), _vec_spec(D)],
        out_shape=[jax.ShapeDtypeStruct((T, D), F32), vec, vec, vec], compiler_params=_params(),
    )(dh, x, g, sc, dx_res)


def _loss(x, target):
    T, D = x.shape
    tr = _tile(T, ROW_TILE, 8)

    def body(x_ref, t_ref, dx_ref, l_ref):
        @pl.when(pl.program_id(0) == 0)
        def _():
            l_ref[...] = jnp.zeros_like(l_ref)

        e = x_ref[...] - t_ref[...]
        dx_ref[...] = e / D
        rows = jnp.sum(e * e, axis=-1, keepdims=True)
        l_ref[...] += jnp.broadcast_to(jnp.sum(rows, axis=0, keepdims=True), (1, LANES))

    row = pl.BlockSpec((tr, D), lambda i: (i, 0))
    return pl.pallas_call(
        body, name="loss", grid=(T // tr,),
        in_specs=[row, row], out_specs=[row, _vec_spec(LANES)],
        out_shape=[jax.ShapeDtypeStruct((T, D), F32), jax.ShapeDtypeStruct((1, LANES), F32)],
        compiler_params=_params(),
    )(x, target)


def _split3(x):
    hi = x.astype(BF16)
    r = x - hi.astype(F32)
    mid = r.astype(BF16)
    lo = (r - mid.astype(F32)).astype(BF16)
    return hi, mid, lo


def _tri_sum(tri, x):
    hi, mid, lo = _split3(x)
    return _dot(tri, hi, NN) + _dot(tri, mid, NN) + _dot(tri, lo, NN)


def _fox_gate_fwd(fg, bf, H):
    T = fg.shape[0]
    tb = _tile(T, 512)

    def body(fg_ref, bf_ref, cum_ref, rep_ref, carry):
        @pl.when(pl.program_id(0) == 0)
        def _():
            carry[...] = jnp.zeros_like(carry)

        z = fg_ref[...] + bf_ref[...]
        logf = jnp.minimum(z, 0.0) - jnp.log(1.0 + jnp.exp(-jnp.abs(z)))
        row = lax.broadcasted_iota(jnp.int32, (tb, tb), 0)
        col = lax.broadcasted_iota(jnp.int32, (tb, tb), 1)
        cum = _tri_sum((row >= col).astype(BF16), logf) + carry[...]
        cum_ref[...] = cum
        carry[...] = cum_ref[pl.ds(tb - 1, 1), :]
        lane = lax.broadcasted_iota(jnp.int32, (tb, LANES), 1)
        for h in range(H):
            colv = jnp.sum(jnp.where(lane == h, cum, 0.0), axis=-1, keepdims=True)
            rep_ref[h] = jnp.broadcast_to(colv, (tb, LANES))

    return pl.pallas_call(
        body, name="fox_gate_fwd", grid=(T // tb,),
        in_specs=[pl.BlockSpec((tb, LANES), lambda i: (i, 0)), _vec_spec(LANES)],
        out_specs=[pl.BlockSpec((tb, LANES), lambda i: (i, 0)), pl.BlockSpec((H, tb, LANES), lambda i: (0, i, 0))],
        out_shape=[jax.ShapeDtypeStruct((T, LANES), F32), jax.ShapeDtypeStruct((H, T, LANES), F32)],
        scratch_shapes=[pltpu.VMEM((1, LANES), F32)], compiler_params=_params(),
    )(fg, bf)


def _fox_gate_bwd(dcum, fg, bf):
    T = fg.shape[0]
    tb = _tile(T, 512)
    nb = T // tb

    def body(dc_ref, fg_ref, bf_ref, dfg_ref, dbf_ref, carry):
        @pl.when(pl.program_id(0) == 0)
        def _():
            carry[...] = jnp.zeros_like(carry)
            dbf_ref[...] = jnp.zeros_like(dbf_ref)

        row = lax.broadcasted_iota(jnp.int32, (tb, tb), 0)
        col = lax.broadcasted_iota(jnp.int32, (tb, tb), 1)
        dc = dc_ref[...]
        dlogf = _tri_sum((row <= col).astype(BF16), dc) + carry[...]
        z = fg_ref[...] + bf_ref[...]
        dfg = dlogf * _sigmoid(-z)
        dfg_ref[...] = dfg
        dbf_ref[...] += jnp.sum(dfg, axis=0, keepdims=True)
        carry[...] += jnp.sum(dc, axis=0, keepdims=True)

    rev = pl.BlockSpec((tb, LANES), lambda i: (nb - 1 - i, 0))
    return pl.pallas_call(
        body, name="fox_gate_bwd", grid=(nb,),
        in_specs=[rev, rev, _vec_spec(LANES)], out_specs=[rev, _vec_spec(LANES)],
        out_shape=[jax.ShapeDtypeStruct((T, LANES), F32), jax.ShapeDtypeStruct((1, LANES), F32)],
        scratch_shapes=[pltpu.VMEM((1, LANES), F32)], compiler_params=_params(),
    )(dcum, fg, bf)


def _fox_blocks(T, pref=512):
    tb = pref if T >= 2 * pref else BLOCK
    return tb, T // tb


def _fox_fwd(name, qkv, cq_rep, ck, gather=()):
    T = qkv.shape[0]
    D = qkv.shape[1] // 3
    H = D // FOX_DH
    hps = FOX_HPS
    ng, wl = H // hps, hps * FOX_DH
    tb, nb = _fox_blocks(T, FOX_FWD_BLOCK)
    pairs = [(i, j) for i in range(nb) for j in range(i + 1)]
    qi = np.array([p[0] for p in pairs], np.int32)
    kj = np.array([p[1] for p in pairs], np.int32)
    npairs = len(pairs)
    nrep = tb // LANES
    nc = len(gather)

    def body(qi_ref, kj_ref, q_ref, k_ref, v_ref, cq_ref, ck_ref, *rest):
        cin, (o_ref, obf_ref, lse_ref), cout = rest[:nc], rest[nc:nc + 3], rest[nc + 3:2 * nc + 3]
        m_sc, l_sc, acc_sc = rest[2 * nc + 3:2 * nc + 6]
        sems = rest[2 * nc + 6:]
        g, p = pl.program_id(0), pl.program_id(1)
        i, j = qi_ref[p], kj_ref[p]

        if nc:
            @pl.when(jnp.logical_and(g == 0, p == 0))
            def _():
                _gather2_start(cin, cout, *sems)

        @pl.when(j == 0)
        def _():
            m_sc[...] = jnp.full_like(m_sc, NEG_INF)
            l_sc[...] = jnp.zeros_like(l_sc)
            acc_sc[...] = jnp.zeros_like(acc_sc)

        row = lax.broadcasted_iota(jnp.int32, (tb, tb), 0)
        col = lax.broadcasted_iota(jnp.int32, (tb, tb), 1)
        visible = jnp.logical_or(j < i, row >= col)
        for hh in range(hps):
            cols = slice(hh * FOX_DH, (hh + 1) * FOX_DH)
            s = _dot(q_ref[:, cols], k_ref[:, cols], NT) * FOX_SCALE
            s = jnp.where(visible, s + _rep(cq_ref[hh], nrep) - ck_ref[hh], NEG_INF)
            m_prev = m_sc[hh]
            m_new = jnp.maximum(m_prev, jnp.max(s, axis=-1, keepdims=True))
            alpha = jnp.exp(m_prev - m_new)
            pm = jnp.exp(s - _rep(m_new, nrep))
            l_sc[hh] = alpha * l_sc[hh] + jnp.sum(pm, axis=-1, keepdims=True)
            acc_sc[:, cols] = alpha * acc_sc[:, cols] + _dot(pm.astype(BF16), v_ref[:, cols], NN)
            m_sc[hh] = m_new

        @pl.when(j == i)
        def _():
            for hh in range(hps):
                cols = slice(hh * FOX_DH, (hh + 1) * FOX_DH)
                o = acc_sc[:, cols] / l_sc[hh]
                o_ref[:, cols] = o
                obf_ref[:, cols] = o.astype(BF16)
                lse_ref[hh] = m_sc[hh] + jnp.log(l_sc[hh])

        if nc:
            @pl.when(jnp.logical_and(g == ng - 1, p == npairs - 1))
            def _():
                _gather2_pass_on(cin, cout, *sems)
                _gather2_finish(cin, cout, *sems)

    any_spec = pl.BlockSpec(memory_space=pl.ANY)
    qblk = pl.BlockSpec((tb, wl), lambda g, p, qi, kj: (qi[p], g))
    qrep = pl.BlockSpec((hps, tb, LANES), lambda g, p, qi, kj: (g, qi[p], 0))
    grid_spec = pltpu.PrefetchScalarGridSpec(
        num_scalar_prefetch=2, grid=(ng, npairs),
        in_specs=[qblk,
                  pl.BlockSpec((tb, wl), lambda g, p, qi, kj: (kj[p], ng + g)),
                  pl.BlockSpec((tb, wl), lambda g, p, qi, kj: (kj[p], 2 * ng + g)),
                  qrep,
                  pl.BlockSpec((hps, 1, tb), lambda g, p, qi, kj: (g, 0, kj[p]))] + [any_spec] * nc,
        out_specs=[qblk, qblk, qrep] + [any_spec] * nc,
        scratch_shapes=[pltpu.VMEM((hps, tb, LANES), F32), pltpu.VMEM((hps, tb, LANES), F32),
                        pltpu.VMEM((tb, wl), F32)] + (_gather2_sems(nc) if nc else []))
    res = pl.pallas_call(
        body, name=name, grid_spec=grid_spec,
        out_shape=[jax.ShapeDtypeStruct((T, D), F32), jax.ShapeDtypeStruct((T, D), BF16),
                   jax.ShapeDtypeStruct((H, T, LANES), F32)] + _comm_out_shapes(gather, True),
        compiler_params=_params(),
    )(qi, kj, qkv, qkv, qkv, cq_rep, ck, *gather)
    return res[:3], res[3:]


def _fox_bwd_prep(do, o, lse_rep, cq_rep):
    T, D = do.shape
    H = D // FOX_DH
    tr = _tile(T, ROW_TILE, 8)

    def body(do_ref, o_ref, lse_ref, cq_ref, dob_ref, delta_ref, cql_ref):
        dov = do_ref[...]
        dob_ref[...] = dov.astype(BF16)
        prod = dov * o_ref[...]
        for h in range(H):
            d = jnp.sum(prod[:, h * FOX_DH:(h + 1) * FOX_DH], axis=-1, keepdims=True)
            delta_ref[h] = jnp.broadcast_to(d, (tr, LANES))
        cql_ref[...] = cq_ref[...] - lse_ref[...]

    row = pl.BlockSpec((tr, D), lambda i: (i, 0))
    rep = pl.BlockSpec((H, tr, LANES), lambda i: (0, i, 0))
    return pl.pallas_call(
        body, name="fox_bwd_prep", grid=(T // tr,),
        in_specs=[row, row, rep, rep], out_specs=[row, rep, rep],
        out_shape=[jax.ShapeDtypeStruct((T, D), BF16), jax.ShapeDtypeStruct((H, T, LANES), F32),
                   jax.ShapeDtypeStruct((H, T, LANES), F32)],
        compiler_params=_params(),
    )(do, o, lse_rep, cq_rep)


def _fox_bwd(name, qkv, dob, delta_rep, cql_rep, ck, a2a=()):
    T = qkv.shape[0]
    D = qkv.shape[1] // 3
    H = D // FOX_DH
    hps = FOX_HPS
    ng, wl = H // hps, hps * FOX_DH
    tk, nbk = _fox_blocks(T, FOX_BWD_KEYS)
    tq = max(_fox_blocks(T, FOX_BWD_QUERIES)[0], tk)
    nbq, ratio = T // tq, tq // tk
    pairs = [(i, j) for j in range(nbk) for i in range(j // ratio, nbq)]
    qi = np.array([p[0] for p in pairs], np.int32)
    kj = np.array([p[1] for p in pairs], np.int32)
    npairs = len(pairs)
    nrep = tk // LANES
    nc = len(a2a)

    def body(qi_ref, kj_ref, q_ref, k_ref, v_ref, do_ref, delta_ref, cql_ref, ck_ref, *rest):
        cin, (dq_ref, dk_ref, dv_ref, dck_ref, dcq_ref), cout = rest[:nc], rest[nc:nc + 5], rest[nc + 5:2 * nc + 5]
        dq_acc, dk_acc, dv_acc, dc_acc = rest[2 * nc + 5:2 * nc + 9]
        sems = rest[2 * nc + 9:]
        g, p = pl.program_id(0), pl.program_id(1)
        i, j = qi_ref[p], kj_ref[p]

        @pl.when(jnp.logical_and(g == 0, p == 0))
        def _():
            dcq_ref[...] = jnp.zeros_like(dcq_ref)
            if nc:
                _comm_start(cin, cout, False, *sems)

        @pl.when(p == 0)
        def _():
            dq_acc[...] = jnp.zeros_like(dq_acc)

        @pl.when(i == j // ratio)
        def _():
            dk_acc[...] = jnp.zeros_like(dk_acc)
            dv_acc[...] = jnp.zeros_like(dv_acc)
            dc_acc[...] = jnp.zeros_like(dc_acc)

        row = lax.broadcasted_iota(jnp.int32, (tq, tk), 0)
        col = lax.broadcasted_iota(jnp.int32, (tq, tk), 1)
        visible = jnp.logical_or((j + 1) * tk <= i * tq, row + i * tq >= col + j * tk)
        lane = lax.broadcasted_iota(jnp.int32, (tq, LANES), 1)
        rows = pl.ds(pl.multiple_of(i * tq, tq), tq)
        dcq = jnp.zeros((tq, LANES), F32)
        for hh in range(hps):
            cols = slice(hh * FOX_DH, (hh + 1) * FOX_DH)
            q, k, v, dov = q_ref[:, cols], k_ref[:, cols], v_ref[:, cols], do_ref[:, cols]
            s = _dot(q, k, NT) * FOX_SCALE + _rep(cql_ref[hh], nrep) - ck_ref[hh]
            pm = jnp.exp(jnp.where(visible, s, NEG_INF))
            dv_acc[:, cols] += _dot(pm.astype(BF16), dov, TN)
            ds = pm * (_dot(dov, v, NT) - _rep(delta_ref[hh], nrep))
            dsb = (ds * FOX_SCALE).astype(BF16)
            dk_acc[:, cols] += _dot(dsb, q, TN)
            dq_acc[rows, cols] += _dot(dsb, k, NN)
            dc_acc[hh] -= jnp.sum(ds, axis=0, keepdims=True)
            dcq = dcq + jnp.where(lane == g * hps + hh, jnp.sum(ds, axis=-1, keepdims=True), 0.0)
        dcq_ref[rows, :] += dcq

        @pl.when(i == nbq - 1)
        def _():
            dk_ref[...] = dk_acc[...].astype(BF16)
            dv_ref[...] = dv_acc[...].astype(BF16)
            dck_ref[...] = dc_acc[...]

        @pl.when(p == npairs - 1)
        def _():
            dq_ref[...] = dq_acc[...].astype(BF16)

        if nc:
            @pl.when(jnp.logical_and(g == ng - 1, p == npairs - 1))
            def _():
                _comm_wait(cin, cout, False, *sems)

    any_spec = pl.BlockSpec(memory_space=pl.ANY)
    qblk = pl.BlockSpec((tq, wl), lambda g, p, qi, kj: (qi[p], g))
    qrep = pl.BlockSpec((hps, tq, LANES), lambda g, p, qi, kj: (g, qi[p], 0))
    kblk = pl.BlockSpec((tk, wl), lambda g, p, qi, kj: (kj[p], g))
    krow = pl.BlockSpec((hps, 1, tk), lambda g, p, qi, kj: (g, 0, kj[p]))
    grid_spec = pltpu.PrefetchScalarGridSpec(
        num_scalar_prefetch=2, grid=(ng, npairs),
        in_specs=[qblk,
                  pl.BlockSpec((tk, wl), lambda g, p, qi, kj: (kj[p], ng + g)),
                  pl.BlockSpec((tk, wl), lambda g, p, qi, kj: (kj[p], 2 * ng + g)),
                  qblk, qrep, qrep, krow] + [any_spec] * nc,
        out_specs=[pl.BlockSpec((T, wl), lambda g, p, qi, kj: (0, g)), kblk, kblk, krow,
                   pl.BlockSpec((T, LANES), lambda g, p, qi, kj: (0, 0))] + [any_spec] * nc,
        scratch_shapes=[pltpu.VMEM((T, wl), F32), pltpu.VMEM((tk, wl), F32), pltpu.VMEM((tk, wl), F32),
                        pltpu.VMEM((hps, 1, tk), F32)] + (_comm_sems(nc) if nc else []))
    act = jax.ShapeDtypeStruct((T, D), BF16)
    res = pl.pallas_call(
        body, name=name, grid_spec=grid_spec,
        out_shape=[act, act, act, jax.ShapeDtypeStruct((H, 1, T), F32),
                   jax.ShapeDtypeStruct((T, LANES), F32)] + _comm_out_shapes(a2a, False),
        compiler_params=_params(),
    )(qi, kj, qkv, qkv, qkv, dob, delta_rep, cql_rep, ck, *a2a)
    return res[:5], res[5:]


def _sgu_rows(T):
    return 2 * BLOCK if T % (2 * BLOCK) == 0 else BLOCK


def _sgu_norm(zv, g_ref, b_ref):
    vv = _gelu(zv)
    mu = jnp.mean(vv, axis=-1, keepdims=True)
    cen = vv - mu
    rstd = lax.rsqrt(jnp.mean(cen * cen, axis=-1, keepdims=True) + EPS)
    vh = cen * rstd
    return vh, rstd, vh * g_ref[...] + b_ref[...]


def _sgu_fwd(zpre, ln_g, ln_b, wsm, bs_rep):
    T = zpre.shape[0]
    W = zpre.shape[1] // 2
    G = W // BLOCK
    tr = _sgu_rows(T)

    def body(z_ref, g_ref, b_ref, ws_ref, bs_ref, o_ref):
        u = _gelu(z_ref[:, :W])
        _, _, vln = _sgu_norm(z_ref[:, W:], g_ref, b_ref)
        for c in range(tr // BLOCK):
            rows = slice(c * BLOCK, (c + 1) * BLOCK)
            for gi in range(G):
                cols = slice(gi * BLOCK, (gi + 1) * BLOCK)
                f = _dot(ws_ref[gi], vln[rows, cols].astype(BF16), NN) + bs_ref[gi]
                o_ref[rows, cols] = (u[rows, cols] * f).astype(BF16)

    full3 = pl.BlockSpec((G, BLOCK, BLOCK), lambda i: (0, 0, 0))
    return pl.pallas_call(
        body, name="sgu_fwd", grid=(T // tr,),
        in_specs=[pl.BlockSpec((tr, 2 * W), lambda i: (i, 0)), _vec_spec(W), _vec_spec(W), full3, full3],
        out_specs=pl.BlockSpec((tr, W), lambda i: (i, 0)),
        out_shape=jax.ShapeDtypeStruct((T, W), BF16), compiler_params=_params(),
    )(zpre, ln_g, ln_b, wsm, bs_rep)


def _sgu_bwd(dgt, zpre, ln_g, ln_b, wsm, wsmT, bs_rep):
    T = zpre.shape[0]
    W = zpre.shape[1] // 2
    G = W // BLOCK
    tr = BLOCK

    def body(dgt_ref, z_ref, g_ref, b_ref, ws_ref, wst_ref, bs_ref,
             dz_ref, dws_ref, dbs_ref, dlg_ref, dlb_ref, du_sc, dvln_sc):
        @pl.when(pl.program_id(0) == 0)
        def _():
            dws_ref[...] = jnp.zeros_like(dws_ref)
            dbs_ref[...] = jnp.zeros_like(dbs_ref)
            dlg_ref[...] = jnp.zeros_like(dlg_ref)
            dlb_ref[...] = jnp.zeros_like(dlb_ref)

        zu = z_ref[:, :W]
        zv = z_ref[:, W:]
        u = _gelu(zu)
        vh, rstd, vln = _sgu_norm(zv, g_ref, b_ref)
        dgtv = dgt_ref[...]
        trow = lax.broadcasted_iota(jnp.int32, (BLOCK, BLOCK), 0)
        tcol = lax.broadcasted_iota(jnp.int32, (BLOCK, BLOCK), 1)
        causal = trow >= tcol
        for c in range(tr // BLOCK):
            rows = slice(c * BLOCK, (c + 1) * BLOCK)
            for gi in range(G):
                cols = slice(gi * BLOCK, (gi + 1) * BLOCK)
                vb = vln[rows, cols].astype(BF16)
                f = _dot(ws_ref[gi], vb, NN) + bs_ref[gi]
                d = dgtv[rows, cols]
                du_sc[rows, cols] = d * f
                df = d * u[rows, cols]
                dfb = df.astype(BF16)
                dvln_sc[rows, cols] = _dot(wst_ref[gi], dfb, NN)
                dws_ref[gi] += jnp.where(causal, _dot(dfb, vb, NT), 0.0)
                dbs_ref[gi] += jnp.broadcast_to(jnp.sum(df, axis=-1, keepdims=True), (BLOCK, BLOCK))
        dvln = dvln_sc[...]
        dlg_ref[...] += jnp.sum(dvln * vh, axis=0, keepdims=True)
        dlb_ref[...] += jnp.sum(dvln, axis=0, keepdims=True)
        dvh = dvln * g_ref[...]
        dvv = rstd * (dvh - jnp.mean(dvh, axis=-1, keepdims=True)
                      - vh * jnp.mean(dvh * vh, axis=-1, keepdims=True))
        dz_ref[:, :W] = (du_sc[...] * _gelu_grad(zu)).astype(BF16)
        dz_ref[:, W:] = (dvv * _gelu_grad(zv)).astype(BF16)

    full3 = pl.BlockSpec((G, BLOCK, BLOCK), lambda i: (0, 0, 0))
    vec = jax.ShapeDtypeStruct((1, W), F32)
    acc3 = jax.ShapeDtypeStruct((G, BLOCK, BLOCK), F32)
    return pl.pallas_call(
        body, name="sgu_bwd", grid=(T // tr,),
        in_specs=[pl.BlockSpec((tr, W), lambda i: (i, 0)), pl.BlockSpec((tr, 2 * W), lambda i: (i, 0)),
                  _vec_spec(W), _vec_spec(W), full3, full3, full3],
        out_specs=[pl.BlockSpec((tr, 2 * W), lambda i: (i, 0)), full3, full3, _vec_spec(W), _vec_spec(W)],
        out_shape=[jax.ShapeDtypeStruct((T, 2 * W), BF16), acc3, acc3, vec, vec],
        scratch_shapes=[pltpu.VMEM((tr, W), F32), pltpu.VMEM((tr, W), F32)],
        compiler_params=_params(),
    )(dgt, zpre, ln_g, ln_b, wsm, wsmT, bs_rep)


def _rope(x, cos_t, sin_t):
    T, N = x.shape
    tr = _tile(T, ROW_TILE, 8)
    nrep = N // LANES
    half = ROPE_DIM // 2

    def body(x_ref, c_ref, s_ref, o_ref):
        xv = x_ref[...]
        lane = jnp.bitwise_and(lax.broadcasted_iota(jnp.int32, (tr, N), 1), SWA_DH - 1)
        partner = jnp.where(lane < half, -pltpu.roll(xv, N - half, 1), pltpu.roll(xv, half, 1))
        o_ref[...] = (xv * _rep(c_ref[...], nrep) + partner * _rep(s_ref[...], nrep)).astype(BF16)

    tab = pl.BlockSpec((tr, LANES), lambda i: (i, 0))
    row = pl.BlockSpec((tr, N), lambda i: (i, 0))
    return pl.pallas_call(
        body, name="rope", grid=(T // tr,), in_specs=[row, tab, tab], out_specs=row,
        out_shape=jax.ShapeDtypeStruct((T, N), BF16), compiler_params=_params(),
    )(x, cos_t, sin_t)


def _swa_tiles(T):
    sb = 4 if T >= 2048 else 2
    return sb, BLOCK * sb, T // (BLOCK * sb)


def _band_mask():
    row = lax.broadcasted_iota(jnp.int32, (BLOCK, 2 * BLOCK), 0)
    col = lax.broadcasted_iota(jnp.int32, (BLOCK, 2 * BLOCK), 1)
    return jnp.logical_and(col > row, col <= row + BLOCK), col


def _swa_specs(T, G):
    sb, tq, nq = _swa_tiles(T)
    q = pl.BlockSpec((None, tq, LANES), lambda h, i: (h, i, 0))
    kc = pl.BlockSpec((None, tq, LANES), lambda h, i: (h // G, i, 0))
    kp = pl.BlockSpec((None, BLOCK, LANES), lambda h, i: (h // G, jnp.maximum(i * sb - 1, 0), 0))
    return q, kc, kp


def _swa_band(b, i, kc_ref, kp_ref, vc_ref, vp_ref):
    rows = slice(b * BLOCK, (b + 1) * BLOCK)
    prev = slice((b - 1) * BLOCK, b * BLOCK)
    kprev = kp_ref[...] if b == 0 else kc_ref[prev, :]
    vprev = vp_ref[...] if b == 0 else vc_ref[prev, :]
    K = jnp.concatenate([kprev, kc_ref[rows, :]], axis=0)
    V = jnp.concatenate([vprev, vc_ref[rows, :]], axis=0)
    band, col = _band_mask()
    if b == 0:
        band = jnp.logical_and(band, jnp.logical_or(col >= BLOCK, i > 0))
    return rows, K, V, band


def _swa_fwd(qp, kp, vp, sinks):
    Hq, T, _ = qp.shape
    G = Hq // kp.shape[0]
    sb, tq, nq = _swa_tiles(T)

    def body(sink_ref, q_ref, kc_ref, kp_ref, vc_ref, vp_ref, o_ref, lse_ref):
        h, i = pl.program_id(0), pl.program_id(1)
        sink = sink_ref[h]
        for b in range(sb):
            rows, K, V, band = _swa_band(b, i, kc_ref, kp_ref, vc_ref, vp_ref)
            s = jnp.where(band, _dot(q_ref[rows, :], K, NT) * SWA_SCALE, NEG_INF)
            m = jnp.maximum(jnp.max(s, axis=-1, keepdims=True), sink)
            pm = jnp.exp(s - m)
            den = jnp.sum(pm, axis=-1, keepdims=True) + jnp.exp(sink - m)
            o_ref[rows, :] = _dot((pm / den).astype(BF16), V, NN)
            lse_ref[rows, :] = jnp.broadcast_to(m + jnp.log(den), (BLOCK, LANES))

    q, kc, kpv = _swa_specs(T, G)
    out = jax.ShapeDtypeStruct((Hq, T, LANES), F32)
    return pl.pallas_call(
        body, name="swa_fwd", grid=(Hq, nq),
        in_specs=[pl.BlockSpec(memory_space=pltpu.SMEM), q, kc, kpv, kc, kpv], out_specs=[q, q],
        out_shape=[out, out], compiler_params=_params(),
    )(sinks, qp, kp, kp, vp, vp)


def _swa_bwd_dq(qp, kp, vp, dop, op, lse_rep, sinks):
    Hq, T, _ = qp.shape
    G = Hq // kp.shape[0]
    sb, tq, nq = _swa_tiles(T)

    def body(sink_ref, q_ref, kc_ref, kp_ref, vc_ref, vp_ref, do_ref, o_ref, lse_ref, dq_ref, dsink_ref):
        h, i = pl.program_id(0), pl.program_id(1)
        sink = sink_ref[h]

        @pl.when(i == 0)
        def _():
            dsink_ref[...] = jnp.zeros_like(dsink_ref)

        for b in range(sb):
            rows, K, V, band = _swa_band(b, i, kc_ref, kp_ref, vc_ref, vp_ref)
            dov = do_ref[rows, :]
            delta = jnp.sum(dov * o_ref[rows, :], axis=-1, keepdims=True)
            lse = lse_ref[rows, :]
            s = jnp.where(band, _dot(q_ref[rows, :], K, NT) * SWA_SCALE, NEG_INF)
            pm = jnp.exp(s - _rep(lse, 2))
            dp = _dot(dov.astype(BF16), V, NT)
            ds = pm * (dp - delta)
            dq_ref[rows, :] = _dot((ds * SWA_SCALE).astype(BF16), K, NN)
            part = jnp.sum(jnp.exp(sink - lse) * delta, axis=0, keepdims=True)
            dsink_ref[...] -= jnp.broadcast_to(part, (8, LANES))

    q, kc, kpv = _swa_specs(T, G)
    return pl.pallas_call(
        body, name="swa_bwd_dq", grid=(Hq, nq),
        in_specs=[pl.BlockSpec(memory_space=pltpu.SMEM), q, kc, kpv, kc, kpv, q, q, q],
        out_specs=[q, pl.BlockSpec((None, 8, LANES), lambda h, i: (h, 0, 0))],
        out_shape=[jax.ShapeDtypeStruct((Hq, T, LANES), F32), jax.ShapeDtypeStruct((Hq, 8, LANES), F32)],
        compiler_params=_params(),
    )(sinks, qp, kp, kp, vp, vp, dop, op, lse_rep)


def _swa_bwd_dkv(qp, kp, vp, dop, op, lse_rep):
    Hq, T, _ = qp.shape
    Hk = kp.shape[0]
    G = Hq // Hk
    sb, tq, nq = _swa_tiles(T)
    nblk = T // BLOCK

    def body(k_ref, v_ref, q_ref, qn_ref, do_ref, don_ref, o_ref, on_ref, lse_ref, lsen_ref, dk_ref, dv_ref):
        i = pl.program_id(1)
        trow = lax.broadcasted_iota(jnp.int32, (2 * BLOCK, BLOCK), 0)
        scol = lax.broadcasted_iota(jnp.int32, (2 * BLOCK, BLOCK), 1)
        band0 = jnp.logical_and(trow >= scol, trow < scol + BLOCK)
        for b in range(sb):
            rows = slice(b * BLOCK, (b + 1) * BLOCK)
            nxt = slice((b + 1) * BLOCK, (b + 2) * BLOCK)
            last = b == sb - 1
            band = band0
            if last:
                band = jnp.logical_and(band0, jnp.logical_or(trow < BLOCK, i < nq - 1))
            kb, vb = k_ref[rows, :], v_ref[rows, :]
            dk = jnp.zeros((BLOCK, LANES), F32)
            dv = jnp.zeros((BLOCK, LANES), F32)
            for g in range(G):
                def two(cur, nx):
                    return jnp.concatenate([cur[g, rows, :], nx[g] if last else cur[g, nxt, :]], axis=0)
                Q, dov, ov, lse = two(q_ref, qn_ref), two(do_ref, don_ref), two(o_ref, on_ref), two(lse_ref, lsen_ref)
                delta = jnp.sum(dov * ov, axis=-1, keepdims=True)
                s = jnp.where(band, _dot(Q, kb, NT) * SWA_SCALE, NEG_INF)
                pm = jnp.exp(s - lse)
                dob = dov.astype(BF16)
                dv = dv + _dot(pm.astype(BF16), dob, TN)
                ds = pm * (_dot(dob, vb, NT) - delta)
                dk = dk + _dot((ds * SWA_SCALE).astype(BF16), Q, TN)
            dk_ref[rows, :] = dk
            dv_ref[rows, :] = dv

    kspec = pl.BlockSpec((None, tq, LANES), lambda h, i: (h, i, 0))
    cur = pl.BlockSpec((G, tq, LANES), lambda h, i: (h, i, 0))
    nxt = pl.BlockSpec((G, BLOCK, LANES), lambda h, i: (h, jnp.minimum((i + 1) * sb, nblk - 1), 0))
    out = jax.ShapeDtypeStruct((Hk, T, LANES), F32)
    return pl.pallas_call(
        body, name="swa_bwd_dkv", grid=(Hk, nq),
        in_specs=[kspec, kspec, cur, nxt, cur, nxt, cur, nxt, cur, nxt], out_specs=[kspec, kspec],
        out_shape=[out, out], compiler_params=_params(),
    )(kp, vp, qp, qp, dop, dop, op, op, lse_rep, lse_rep)


def _to_heads(a, nh):
    T = a.shape[0]
    a = a.reshape(T, nh, SWA_DH).transpose(1, 0, 2)
    return jnp.pad(a, ((0, 0), (0, 0), (0, LANES - SWA_DH)))


def _from_heads(a):
    nh, T, _ = a.shape
    return a[:, :, :SWA_DH].transpose(1, 0, 2).reshape(T, nh * SWA_DH)


def _adam(g, w, m, v):
    m2 = ADAM_B1 * m + (1.0 - ADAM_B1) * g
    v2 = ADAM_B2 * v + (1.0 - ADAM_B2) * (g * g)
    m_hat = m2 / (1.0 - ADAM_B1 ** ADAM_STEP)
    v_hat = v2 / (1.0 - ADAM_B2 ** ADAM_STEP)
    delta = -ADAM_LR * (m_hat / (jnp.sqrt(v_hat) + ADAM_EPS) + ADAM_WD * w)
    return delta, m2, v2


def _ada_fwd(c_all, w, b):
    L, D, n = w.shape
    tn = _tile(n, 768)

    def body(c_ref, w_ref, b_ref, o_ref):
        cv = c_ref[...]
        ca = (cv * _sigmoid(cv)).astype(BF16)
        o_ref[...] = _dot(ca, w_ref[...].astype(BF16), NN) + b_ref[...]

    return pl.pallas_call(
        body, name="ada_fwd", grid=(L, n // tn),
        in_specs=[pl.BlockSpec((NDEV, D), lambda l, j: (0, 0)), pl.BlockSpec((None, D, tn), lambda l, j: (l, 0, j)),
                  pl.BlockSpec((None, 1, tn), lambda l, j: (l, 0, j))],
        out_specs=pl.BlockSpec((None, NDEV, tn), lambda l, j: (l, 0, j)),
        out_shape=jax.ShapeDtypeStruct((L, NDEV, n), F32), compiler_params=_params(),
    )(c_all, w, b)


def _ada_update(c_rep, dm, w, m, v):
    L, D, n = w.shape
    tr = _tile(D, 256, 8)
    nrep = n // LANES

    def body(c_ref, dm_ref, w_ref, m_ref, v_ref, g_ref, d_ref, m2_ref, v2_ref):
        g = jnp.zeros((tr, n), F32)
        for b in range(NDEV):
            cv = c_ref[b]
            g = g + _rep(cv * _sigmoid(cv), nrep) * dm_ref[pl.ds(b, 1), :]
        g_ref[...] = g
        d_ref[...], m2_ref[...], v2_ref[...] = _adam(g, w_ref[...], m_ref[...], v_ref[...])

    blk = pl.BlockSpec((None, tr, n), lambda l, i: (l, i, 0))
    out = jax.ShapeDtypeStruct((L, D, n), F32)
    return pl.pallas_call(
        body, name="ada_update", grid=(L, D // tr),
        in_specs=[pl.BlockSpec((NDEV, tr, LANES), lambda l, i: (0, i, 0)),
                  pl.BlockSpec((None, NDEV, n), lambda l, i: (l, 0, 0)), blk, blk, blk],
        out_specs=[blk, blk, blk, blk], out_shape=[out, out, out, out], compiler_params=_params(),
    )(c_rep, dm, w, m, v)


def _adamw(name, parts, w, m, v, layer):
    P, R, C = parts.shape
    cpad = -(-C // LANES) * LANES
    per_row = cpad * (P * parts.dtype.itemsize + 7 * 4) * 2
    tr = _tile(R, max(8, (24 * 1024 * 1024 // per_row) // 8 * 8), 8)

    def body(p_ref, w_ref, m_ref, v_ref, g_ref, d_ref, m2_ref, v2_ref):
        g = p_ref[0].astype(F32)
        for s in range(1, P):
            g = g + p_ref[s].astype(F32)
        g_ref[...] = g
        d_ref[...], m2_ref[...], v2_ref[...] = _adam(g, w_ref[...], m_ref[...], v_ref[...])

    stk = pl.BlockSpec((None, tr, C), lambda i: (layer, i, 0))
    blk = pl.BlockSpec((tr, C), lambda i: (i, 0))
    out = jax.ShapeDtypeStruct((R, C), F32)
    return pl.pallas_call(
        body, name=name, grid=(R // tr,),
        in_specs=[pl.BlockSpec((P, tr, C), lambda i: (0, i, 0)), stk, stk, stk],
        out_specs=[blk, blk, blk, blk], out_shape=[out, out, out, out], compiler_params=_params(),
    )(parts, w, m, v)


def _colcat(a):
    s, k, n = a.shape
    return a.transpose(1, 0, 2).reshape(k, s * n)


def _colsplit(a):
    k, n8 = a.shape
    return a.reshape(k, NDEV, n8 // NDEV).transpose(1, 0, 2)


def _rowsplit(a):
    r, c = a.shape
    return a.reshape(NDEV, r // NDEV, c)


def kernel(x, c, positions, ada_w, ada_b, mix_pre_g, mix_post_g, ffn_pre_g, ffn_post_g, ffn_w_gu, ffn_w_down, fox_w_in, fox_b_f, fox_w_out, sgu_w_in, sgu_ln_g, sgu_ln_b, sgu_w_s, sgu_b_s, sgu_w_out, swa_w_in, swa_sinks, swa_w_out, loss_target, m_ada_w, m_ada_b, m_mix_pre_g, m_mix_post_g, m_ffn_pre_g, m_ffn_post_g, m_ffn_w_gu, m_ffn_w_down, m_fox_w_in, m_fox_b_f, m_fox_w_out, m_sgu_w_in, m_sgu_ln_g, m_sgu_ln_b, m_sgu_w_s, m_sgu_b_s, m_sgu_w_out, m_swa_w_in, m_swa_sinks, m_swa_w_out, v_ada_w, v_ada_b, v_mix_pre_g, v_mix_post_g, v_ffn_pre_g, v_ffn_post_g, v_ffn_w_gu, v_ffn_w_down, v_fox_w_in, v_fox_b_f, v_fox_w_out, v_sgu_w_in, v_sgu_ln_g, v_sgu_ln_b, v_sgu_w_s, v_sgu_b_s, v_sgu_w_out, v_swa_w_in, v_swa_sinks, v_swa_w_out):
    env = locals()
    W = {n: env[n] for n in WEIGHTS}
    M = {n: env["m_" + n] for n in WEIGHTS}
    V = {n: env["v_" + n] for n in WEIGHTS}

    me = 4 * lax.axis_index("x") + 2 * lax.axis_index("y") + lax.axis_index("c")
    _, T, D = x.shape
    L = ada_w.shape[0]
    n_ada = ada_w.shape[2]
    F = ffn_w_gu.shape[2] * NDEV // 2
    H = D // FOX_DH
    Hq = D // SWA_DH
    x0 = x[0]
    mixer = {0: 'fox', 1: 'sgu', 2: 'swa'}

    c_all = _comm("gather_c", [c], 'gather')[0].reshape(NDEV, D)
    ada_b_mine = lax.dynamic_slice_in_dim(ada_b, me * n_ada, n_ada, axis=1).reshape(L, 1, n_ada)
    mod_cols = _ada_fwd(c_all, ada_w, ada_b_mine)
    mod = _comm("a2a_mod", [mod_cols.transpose(1, 0, 2)], 'a2a')[0]
    mod = mod.transpose(1, 0, 2).reshape(L, 6, 1, D)

    inv = ROPE_THETA ** (-jnp.arange(0, ROPE_DIM, 2, dtype=F32) / ROPE_DIM)
    ang = positions[0].astype(F32)[:, None] * inv
    pad1 = jnp.ones((T, SWA_DH - ROPE_DIM), F32)
    cos64 = jnp.concatenate([jnp.cos(ang), jnp.cos(ang), pad1], axis=1)
    sin64 = jnp.concatenate([jnp.sin(ang), jnp.sin(ang), 0.0 * pad1], axis=1)
    cos_t = jnp.concatenate([cos64, cos64], axis=1)
    sin_t = jnp.concatenate([sin64, sin64], axis=1)

    fox_layers = [i for i in range(L) if mixer[i % 3] == 'fox']
    assert fox_layers and fox_layers[0] == 0

    def slice_of(i, role):
        kind, j = mixer[i % 3], i // 3
        src = {'wgu': (ffn_w_gu, i), 'wd': (ffn_w_down, i), 'win': (W[kind + '_w_in'], j), 'wout': (W[kind + '_w_out'], j)}[role]
        return src[0][src[1]].astype(BF16)

    def hosted_keys(f):
        later = [i for i in fox_layers if i > f]
        stop = later[0] if later else L
        keys = [(f, 'wout'), (f, 'wgu'), (f, 'wd')]
        for i in range(f + 1, stop):
            keys += [(i, 'wgu'), (i, 'wd'), (i, 'win'), (i, 'wout')]
        if later:
            keys += [(stop, 'win')]
        return keys

    raw, full = {}, {}
    first_keys = [(0, 'win')]
    raw.update(zip(first_keys, _gather2("gather_first", [slice_of(*k) for k in first_keys])))

    def wget(i, role):
        if (i, role) not in full:
            got = raw[(i, role)]
            full[(i, role)] = _colcat(got) if role in ('wgu', 'win') else got.reshape(-1, D)
        return full[(i, role)]

    saved = []
    xc = x0
    for i in range(L):
        kind, j = mixer[i % 3], i // 3
        s = dict(x_in=xc)
        sh_m, sc_m, g_m, sh_f, sc_f, g_f = [mod[i, t] for t in range(6)]
        h = _pre_fwd(xc, mix_pre_g[i:i + 1], sc_m, sh_m)
        s['h'] = h
        if kind == 'fox':
            wqkv = wget(i, 'win')[:, :3 * D]
            wf = jnp.pad(wget(i, 'win')[:, 3 * D:], ((0, 0), (0, LANES - H)))
            s['win_pad'] = jnp.concatenate([wqkv, wf], axis=1)
            bf = jnp.pad(fox_b_f[j:j + 1], ((0, 0), (0, LANES - H)))
            qkv = _mm("fox_qkv", h, wqkv, 'nn', BF16)
            fg = _mm("fox_fg", h, wf, 'nn', F32)
            cum, cq_rep = _fox_gate_fwd(fg, bf, H)
            ck = cum[:, :H].T.reshape(H, 1, T)
            keys = hosted_keys(i)
            (o, obf, lse_rep), got = _fox_fwd("fox_fwd%d" % j, qkv, cq_rep, ck, gather=[slice_of(*k) for k in keys])
            raw.update(zip(keys, got))
            s.update(qkv=qkv, fg=fg, bf=bf, cq_rep=cq_rep, ck=ck, o=o, lse_rep=lse_rep, mix_out=obf)
        elif kind == 'sgu':
            G = D // BLOCK
            causal = jnp.tril(jnp.ones((BLOCK, BLOCK), bool))
            wsm = jnp.where(causal[None], sgu_w_s[j], 0.0).astype(BF16)
            bs_rep = jnp.broadcast_to(sgu_b_s[j][:, :, None], (G, BLOCK, BLOCK))
            zpre = _mm("sgu_in", h, wget(i, 'win'), 'nn', F32)
            gated = _sgu_fwd(zpre, sgu_ln_g[j:j + 1], sgu_ln_b[j:j + 1], wsm, bs_rep)
            s.update(zpre=zpre, wsm=wsm, bs_rep=bs_rep, mix_out=gated)
        else:
            Hk = (wget(i, 'win').shape[1] // SWA_DH - Hq) // 2
            proj = _mm("swa_in", h, wget(i, 'win'), 'nn', F32)
            qr = _rope(proj[:, :Hq * SWA_DH], cos_t, sin_t)
            kr = _rope(proj[:, Hq * SWA_DH:(Hq + Hk) * SWA_DH], cos_t, sin_t)
            qp, kp = _to_heads(qr, Hq), _to_heads(kr, Hk)
            vp = _to_heads(proj[:, (Hq + Hk) * SWA_DH:].astype(BF16), Hk)
            op, lse_rep = _swa_fwd(qp, kp, vp, swa_sinks[j])
            s.update(qp=qp, kp=kp, vp=vp, op=op, lse_rep=lse_rep, Hk=Hk, mix_out=_from_heads(op).astype(BF16))
        y = _mm("mix_out", s['mix_out'], wget(i, 'wout'), 'nn', F32)
        x_mid = _post_fwd(xc, y, mix_post_g[i:i + 1], g_m)
        s.update(y_mix=y, x_mid=x_mid)
        h2 = _pre_fwd(x_mid, ffn_pre_g[i:i + 1], sc_f, sh_f)
        g, u, a = _ffn_up(h2, wget(i, 'wgu'))
        y2 = _mm("ffn_down", a, wget(i, 'wd'), 'nn', F32)
        xc = _post_fwd(x_mid, y2, ffn_post_g[i:i + 1], g_f)
        s.update(h2=h2, g=g, u=u, a=a, y_ffn=y2)
        saved.append(s)

    dx, lsum = _loss(xc, loss_target[0])
    loss = lax.psum(0.5 * lsum[0, 0] / D, AXES)

    small = {n: [None] * W[n].shape[0] for n in SMALL}
    dmod = [None] * L
    out = {n: dict(g=[None] * W[n].shape[0], d=[None] * W[n].shape[0], m=[None] * W[n].shape[0],
                   v=[None] * W[n].shape[0]) for n in WEIGHTS}

    pending = []

    def update(items, recv):
        for (name, idx, _), parts in zip(items, recv):
            res = _adamw("adamw_" + name, parts, W[name], M[name], V[name], idx)
            for key, val in zip("gdmv", res):
                out[name][key][idx] = val

    for i in reversed(range(L)):
        kind, j = mixer[i % 3], i // 3
        s = saved[i]
        sh_m, sc_m, g_m, sh_f, sc_f, g_f = [mod[i, t] for t in range(6)]
        dy2, dg_f, dpost_f = _post_bwd(dx, s['y_ffn'], ffn_post_g[i:i + 1], g_f)
        dwd = _mm("ffn_dwd", s['a'], dy2, 'tn', BF16)
        dg, du = _ffn_dact(dy2, wget(i, 'wd'), s['g'], s['u'])
        dwgu = jnp.concatenate([_mm("ffn_dwg", s['h2'], dg, 'tn', BF16), _mm("ffn_dwu", s['h2'], du, 'tn', BF16)], axis=1)
        dh2 = _matmul("ffn_dh", [(dg, wget(i, 'wgu'), (0, 0), (0, 0)), (du, wget(i, 'wgu'), (0, 0), (0, F))], 'nt', F32,
                      T, D, F)
        pending += [('ffn_w_gu', i, _colsplit(dwgu)), ('ffn_w_down', i, _rowsplit(dwd))]
        dx, dsh_f, dsc_f, dpre_f = _pre_bwd(dh2, s['x_mid'], ffn_pre_g[i:i + 1], sc_f, dx)
        dy, dg_m, dpost_m = _post_bwd(dx, s['y_mix'], mix_post_g[i:i + 1], g_m)
        dwout = _mm("mix_dwout", s['mix_out'], dy, 'tn', BF16)
        pending.append((kind + '_w_out', j, _rowsplit(dwout)))
        dmix = _mm("mix_dout", dy, wget(i, 'wout'), 'nt', F32)
        if kind == 'fox':
            dob, delta_rep, cql_rep = _fox_bwd_prep(dmix, s['o'], s['lse_rep'], s['cq_rep'])
            (dq, dk, dv, dck, dcq), recv = _fox_bwd("fox_bwd%d" % j, s['qkv'], dob, delta_rep, cql_rep, s['ck'],
                                                    a2a=[item[2] for item in pending])
            update(pending, recv)
            pending = []
            dcum = jnp.pad(dck.reshape(H, T).T, ((0, 0), (0, LANES - H))) + dcq
            dfg, dbf = _fox_gate_bwd(dcum, s['fg'], s['bf'])
            small['fox_b_f'][j] = dbf[0, :H]
            dproj = jnp.concatenate([dq, dk, dv, dfg.astype(BF16)], axis=1)
            dwin = _mm("fox_dwin", s['h'], dproj, 'tn', BF16)[:, :3 * D + H]
            pending.append((kind + '_w_in', j, _colsplit(dwin)))
            if i == 0:
                dh, recv = _mm("fox_dh_last", dproj, s['win_pad'], 'nt', F32, a2a=[item[2] for item in pending])
                update(pending, recv)
                pending = []
            else:
                dh = _mm("fox_dh", dproj, s['win_pad'], 'nt', F32)
        elif kind == 'sgu':
            wsmT = s['wsm'].transpose(0, 2, 1)
            dz, dws, dbs, dlg, dlb = _sgu_bwd(dmix, s['zpre'], sgu_ln_g[j:j + 1], sgu_ln_b[j:j + 1], s['wsm'], wsmT, s['bs_rep'])
            small['sgu_w_s'][j], small['sgu_b_s'][j] = dws, dbs[:, :, 0]
            small['sgu_ln_g'][j], small['sgu_ln_b'][j] = dlg[0], dlb[0]
            dwin = _mm("sgu_dwin", s['h'], dz, 'tn', BF16)
            pending.append((kind + '_w_in', j, _colsplit(dwin)))
            dh = _mm("sgu_dh", dz, wget(i, 'win'), 'nt', F32)
        else:
            Hk = s['Hk']
            dop = _to_heads(dmix, Hq)
            dqp, dsink = _swa_bwd_dq(s['qp'], s['kp'], s['vp'], dop, s['op'], s['lse_rep'], swa_sinks[j])
            dkp, dvp = _swa_bwd_dkv(s['qp'], s['kp'], s['vp'], dop, s['op'], s['lse_rep'])
            small['swa_sinks'][j] = dsink[:, 0, 0]
            dproj = jnp.concatenate([_rope(_from_heads(dqp), cos_t, -sin_t), _rope(_from_heads(dkp), cos_t, -sin_t),
                                     _from_heads(dvp).astype(BF16)], axis=1)
            dwin = _mm("swa_dwin", s['h'], dproj, 'tn', BF16)
            pending.append((kind + '_w_in', j, _colsplit(dwin)))
            dh = _mm("swa_dh", dproj, wget(i, 'win'), 'nt', F32)
        dx, dsh_m, dsc_m, dpre_m = _pre_bwd(dh, s['x_in'], mix_pre_g[i:i + 1], sc_m, dx)
        small['mix_pre_g'][i], small['mix_post_g'][i] = dpre_m[0], dpost_m[0]
        small['ffn_pre_g'][i], small['ffn_post_g'][i] = dpre_f[0], dpost_f[0]
        dmod[i] = jnp.concatenate([dsh_m, dsc_m, dg_m, dsh_f, dsc_f, dg_f], axis=1)[0]

    if pending:
        update(pending, _comm("a2a_last", [item[2] for item in pending], 'a2a'))
    grad_x = dx[None]

    small['ada_b'] = dmod
    flat = jnp.concatenate([jnp.stack(small[n]).reshape(-1) for n in SMALL])
    width = 8 * LANES
    npad = -flat.shape[0] % (8 * width)
    packed = jnp.pad(flat, (0, npad)).reshape(-1, width)
    parts = _comm("gather_small", [packed], 'gather')[0]

    def pack(d):
        f = jnp.concatenate([d[n].reshape(-1) for n in SMALL])
        return jnp.pad(f, (0, npad)).reshape(1, -1, width)

    res = _adamw("adamw_small", parts, pack(W), pack(M), pack(V), 0)
    off = 0
    for n in SMALL:
        size = W[n].size
        for key, val in zip("gdmv", res):
            out[n][key] = val.reshape(-1)[off:off + size].reshape(W[n].shape)
        off += size

    dmod_all = parts.reshape(NDEV, -1)[:, :L * 6 * D].reshape(NDEV, L, 6 * D)
    dm = lax.dynamic_slice_in_dim(dmod_all, me * n_ada, n_ada, axis=2).transpose(1, 0, 2)
    c_rep = jnp.broadcast_to(c_all[:, :, None], (NDEV, D, LANES))
    for key, val in zip("gdmv", _ada_update(c_rep, dm, ada_w, m_ada_w, v_ada_w)):
        out['ada_w'][key] = val

    def leaf(n, key):
        val = out[n][key]
        return jnp.stack(val) if isinstance(val, list) else val

    return (loss, grad_x, *[leaf(n, 'g') for n in WEIGHTS], *[leaf(n, 'd') for n in WEIGHTS],
            *[leaf(n, 'm') for n in WEIGHTS], *[leaf(n, 'v') for n in WEIGHTS])
```

```python
import numpy as np
import jax
import jax.numpy as jnp
from jax import lax
from jax.experimental import pallas as pl
from jax.experimental.pallas import tpu as pltpu

F32 = jnp.float32
BF16 = jnp.bfloat16
NDEV = 8
AXES = ("x", "y", "c")
LANES = 128
VMEM_LIMIT_BYTES = 48 * 1024 * 1024
NEG_INF = float("-inf")
ROW_TILE = 256

EPS = 1e-6
BLOCK = 128
FOX_DH = 128
FOX_HPS = 2
FOX_FWD_BLOCK = 1024
FOX_BWD_QUERIES = 1024
FOX_BWD_KEYS = 512
HOSTED_GATHER_PARAMS = 36 * 2 ** 20
HOSTED_A2A_PARAMS = 22 * 2 ** 20
SWA_DH = 64
ROPE_DIM = 16
ROPE_THETA = 500000.0
FOX_SCALE = FOX_DH ** -0.5
SWA_SCALE = SWA_DH ** -0.5
GELU_C0 = 0.7978845608028654
GELU_C1 = 0.044715

ADAM_LR = 0.001
ADAM_B1 = 0.9
ADAM_B2 = 0.999
ADAM_EPS = 1e-08
ADAM_WD = 0.01
ADAM_STEP = 10

NN = (((1,), (0,)), ((), ()))
NT = (((1,), (1,)), ((), ()))
TN = (((0,), (0,)), ((), ()))

WEIGHTS = ['ada_w', 'ada_b', 'mix_pre_g', 'mix_post_g', 'ffn_pre_g', 'ffn_post_g', 'ffn_w_gu', 'ffn_w_down',
           'fox_w_in', 'fox_b_f', 'fox_w_out', 'sgu_w_in', 'sgu_ln_g', 'sgu_ln_b', 'sgu_w_s', 'sgu_b_s',
           'sgu_w_out', 'swa_w_in', 'swa_sinks', 'swa_w_out']
SMALL = ['ada_b', 'mix_pre_g', 'mix_post_g', 'ffn_pre_g', 'ffn_post_g', 'fox_b_f', 'sgu_ln_g', 'sgu_ln_b',
         'sgu_w_s', 'sgu_b_s', 'swa_sinks']


def _dot(a, b, dims):
    return lax.dot_general(a, b, dims, preferred_element_type=F32)


def _tile(n, pref, mult=LANES):
    t = (min(pref, n) // mult) * mult
    while t >= mult:
        if n % t == 0:
            return t
        t -= mult
    return n


def _params():
    return pltpu.CompilerParams(vmem_limit_bytes=VMEM_LIMIT_BYTES)


def _rep(a, n):
    return a if n == 1 else jnp.concatenate([a] * n, axis=-1)


def _vec_spec(d):
    return pl.BlockSpec((1, d), lambda *_: (0, 0))


def _sigmoid(z):
    return 1.0 / (1.0 + jnp.exp(-z))


def _gelu(z):
    t = jnp.tanh(GELU_C0 * (z + GELU_C1 * z * z * z))
    return 0.5 * z * (1.0 + t)


def _gelu_grad(z):
    t = jnp.tanh(GELU_C0 * (z + GELU_C1 * z * z * z))
    return 0.5 * (1.0 + t) + 0.5 * z * (1.0 - t * t) * GELU_C0 * (1.0 + 3.0 * GELU_C1 * z * z)


def _comm_out_shapes(arrs, gather):
    return [jax.ShapeDtypeStruct(((NDEV,) + a.shape) if gather else a.shape, a.dtype) for a in arrs]


def _comm_sems(n):
    return [pltpu.SemaphoreType.DMA((n,)), pltpu.SemaphoreType.DMA((n,)), pltpu.SemaphoreType.DMA((n,))]


def _me():
    x, y, c = lax.axis_index("x"), lax.axis_index("y"), lax.axis_index("c")
    return x, y, c, 4 * x + 2 * y + c


def _comm_start(ins, outs, gather, send_sems, recv_sems, local_sems):
    x, y, c, me = _me()
    for a in range(len(ins)):
        pltpu.make_async_copy(ins[a] if gather else ins[a].at[me], outs[a].at[me], local_sems.at[a]).start()
        for bits in range(1, NDEV):
            px = (1 - x) if bits & 4 else x
            py = (1 - y) if bits & 2 else y
            pc = (1 - c) if bits & 1 else c
            pltpu.make_async_remote_copy(
                src_ref=ins[a] if gather else ins[a].at[4 * px + 2 * py + pc], dst_ref=outs[a].at[me],
                send_sem=send_sems.at[a], recv_sem=recv_sems.at[a],
                device_id=(px, py, pc), device_id_type=pl.DeviceIdType.MESH).start()


def _comm_wait(ins, outs, gather, send_sems, recv_sems, local_sems):
    x, y, c, me = _me()
    for a in range(len(ins)):
        seven = outs[a].at[pl.ds(0, NDEV - 1)]
        pltpu.make_async_remote_copy(src_ref=seven, dst_ref=seven, send_sem=send_sems.at[a], recv_sem=recv_sems.at[a],
                                     device_id=(x, y, c), device_id_type=pl.DeviceIdType.MESH).wait()
        pltpu.make_async_copy(ins[a] if gather else ins[a].at[me], outs[a].at[me], local_sems.at[a]).wait()


def _comm(name, arrs, kind):
    n = len(arrs)
    gather = kind == 'gather'

    def body(*refs):
        ins, outs, sems = refs[:n], refs[n:2 * n], refs[2 * n:]
        _comm_start(ins, outs, gather, *sems)
        _comm_wait(ins, outs, gather, *sems)

    any_spec = pl.BlockSpec(memory_space=pl.ANY)
    return pl.pallas_call(
        body, name=name, out_shape=_comm_out_shapes(arrs, gather),
        in_specs=[any_spec] * n, out_specs=[any_spec] * n, scratch_shapes=_comm_sems(n),
    )(*arrs)


def _gather2_sems(n):
    return [pltpu.SemaphoreType.DMA((n,)) for _ in range(4)]


def _remote(src, dst, send_sem, recv_sem, device):
    return pltpu.make_async_remote_copy(src_ref=src, dst_ref=dst, send_sem=send_sem, recv_sem=recv_sem,
                                        device_id=device, device_id_type=pl.DeviceIdType.MESH)


def _gather2_start(ins, outs, send_sems, ici_sems, d2d_sems, local_sems):
    x, y, c, me = _me()
    for a in range(len(ins)):
        pltpu.make_async_copy(ins[a], outs[a].at[me], local_sems.at[a]).start()
        _remote(ins[a], outs[a].at[me], send_sems.at[a], d2d_sems.at[a], (x, y, 1 - c)).start()
        for px, py in ((1 - x, y), (x, 1 - y), (1 - x, 1 - y)):
            _remote(ins[a], outs[a].at[me], send_sems.at[a], ici_sems.at[a], (px, py, c)).start()


def _gather2_pass_on(ins, outs, send_sems, ici_sems, d2d_sems, local_sems):
    x, y, c, me = _me()
    for a in range(len(ins)):
        three = outs[a].at[pl.ds(0, 3)]
        _remote(three, three, send_sems.at[a], ici_sems.at[a], (x, y, c)).wait_recv()
        for px, py in ((1 - x, y), (x, 1 - y), (1 - x, 1 - y)):
            slot = outs[a].at[4 * px + 2 * py + c]
            _remote(slot, slot, send_sems.at[a], d2d_sems.at[a], (x, y, 1 - c)).start()


def _gather2_finish(ins, outs, send_sems, ici_sems, d2d_sems, local_sems):
    x, y, c, me = _me()
    for a in range(len(ins)):
        four, seven = outs[a].at[pl.ds(0, 4)], outs[a].at[pl.ds(0, NDEV - 1)]
        _remote(four, four, send_sems.at[a], d2d_sems.at[a], (x, y, c)).wait_recv()
        _remote(seven, seven, send_sems.at[a], d2d_sems.at[a], (x, y, c)).wait_send()
        pltpu.make_async_copy(ins[a], outs[a].at[me], local_sems.at[a]).wait()


def _gather2(name, arrs):
    n = len(arrs)

    def body(*refs):
        ins, outs, sems = refs[:n], refs[n:2 * n], refs[2 * n:]
        _gather2_start(ins, outs, *sems)
        _gather2_pass_on(ins, outs, *sems)
        _gather2_finish(ins, outs, *sems)

    any_spec = pl.BlockSpec(memory_space=pl.ANY)
    return pl.pallas_call(
        body, name=name, out_shape=_comm_out_shapes(arrs, True),
        in_specs=[any_spec] * n, out_specs=[any_spec] * n, scratch_shapes=_gather2_sems(n),
    )(*arrs)


MATMUL_VMEM_BUDGET = 38 * 1024 * 1024


def _matmul_tiles(mode, M, N, K, npairs, out_size):
    def uniq(vals):
        return sorted(set(vals), reverse=True)

    tms = uniq(_tile(M, p) for p in (1024, 512, 256))
    tns = uniq(_tile(N, p) for p in (1536, 1024, 512, 256))
    tks = uniq(_tile(K, p) for p in (2048, 1024)) if mode == 'tn' else [K] + uniq(_tile(K, p) for p in (2048, 1024))
    for tk in tks:
        best = None
        for tm in tms:
            for tn in tns:
                steps = K // tk
                need = (2 * npairs * (tm + tn) * tk * 2 + 2 * tm * tn * out_size + npairs * tm * tn * 4
                        + (tm * tn * 4 if steps > 1 else 0) + (tk * tm * 2 if mode == 'tn' else 0))
                if need <= MATMUL_VMEM_BUDGET and (best is None or (tm * tn, tm) > (best[0] * best[1], best[0])):
                    best = (tm, tn, tk)
        if best is not None:
            return best
    return _tile(M, 256), _tile(N, 256), _tile(K, 512)


def _matmul(name, pairs, mode, out_dtype, M, N, K, tm=None, tn=None, tk=None, a2a=()):
    if not (tm and tn and tk):
        tm, tn, tk = _matmul_tiles(mode, M, N, K, len(pairs), jnp.dtype(out_dtype).itemsize)
    nk = K // tk
    dims = {'nn': NN, 'nt': NT, 'tn': TN}[mode]
    in_specs, ops = [], []
    for a, b, ao, bo in pairs:
        if mode == 'tn':
            assert ao[0] % tk == 0 and ao[1] % tm == 0
            sa = pl.BlockSpec((tk, tm), lambda i, j, k, r=ao[0] // tk, c=ao[1] // tm: (k + r, i + c))
        else:
            assert ao[0] % tm == 0 and ao[1] % tk == 0
            sa = pl.BlockSpec((tm, tk), lambda i, j, k, r=ao[0] // tm, c=ao[1] // tk: (i + r, k + c))
        if mode == 'nt':
            assert bo[0] % tn == 0 and bo[1] % tk == 0
            sb = pl.BlockSpec((tn, tk), lambda i, j, k, r=bo[0] // tn, c=bo[1] // tk: (j + r, k + c))
        else:
            assert bo[0] % tk == 0 and bo[1] % tn == 0
            sb = pl.BlockSpec((tk, tn), lambda i, j, k, r=bo[0] // tk, c=bo[1] // tn: (k + r, j + c))
        in_specs += [sa, sb]
        ops += [a, b]
    npairs = len(pairs)
    nc = len(a2a)
    grid = (M // tm, N // tn, nk)

    def body(*refs):
        cin, o_ref, cout = refs[2 * npairs:2 * npairs + nc], refs[2 * npairs + nc], refs[2 * npairs + nc + 1:2 * npairs + 2 * nc + 1]
        scratch = refs[2 * npairs + 2 * nc + 1:]
        sems = scratch[1:] if nk > 1 else scratch
        i, j, k = pl.program_id(0), pl.program_id(1), pl.program_id(2)

        if nc:
            @pl.when(jnp.logical_and(jnp.logical_and(i == 0, j == 0), k == 0))
            def _():
                _comm_start(cin, cout, False, *sems)

        part = _dot(refs[0][...], refs[1][...], dims)
        for p in range(1, npairs):
            part = part + _dot(refs[2 * p][...], refs[2 * p + 1][...], dims)
        if nk == 1:
            o_ref[...] = part.astype(out_dtype)
        else:
            acc = scratch[0]

            @pl.when(k == 0)
            def _():
                acc[...] = part

            @pl.when(k > 0)
            def _():
                acc[...] += part

            @pl.when(k == nk - 1)
            def _():
                o_ref[...] = acc[...].astype(out_dtype)

        if nc:
            @pl.when(jnp.logical_and(jnp.logical_and(i == grid[0] - 1, j == grid[1] - 1), k == nk - 1))
            def _():
                _comm_wait(cin, cout, False, *sems)

    any_spec = pl.BlockSpec(memory_space=pl.ANY)
    res = pl.pallas_call(
        body, name=name, grid=grid,
        in_specs=in_specs + [any_spec] * nc,
        out_specs=[pl.BlockSpec((tm, tn), lambda i, j, k: (i, j))] + [any_spec] * nc,
        out_shape=[jax.ShapeDtypeStruct((M, N), out_dtype)] + _comm_out_shapes(a2a, False),
        scratch_shapes=([] if nk == 1 else [pltpu.VMEM((tm, tn), F32)]) + (_comm_sems(nc) if nc else []),
        compiler_params=_params(),
    )(*ops, *a2a)
    return (res[0], res[1:]) if nc else res[0]


def _mm(name, a, b, mode, out_dtype, a2a=()):
    if mode == 'nn':
        (M, K), N = a.shape, b.shape[1]
    elif mode == 'nt':
        (M, K), N = a.shape, b.shape[0]
    else:
        (K, M), N = a.shape, b.shape[1]
    return _matmul(name, [(a, b, (0, 0), (0, 0))], mode, out_dtype, M, N, K, a2a=a2a)


def _ffn_up(name, h, wgu, gather=()):
    T, D = h.shape
    F = wgu.shape[1] // 2
    tm, tn = _tile(T, 1024), _tile(F, 512)
    grid = (T // tm, F // tn)
    nc = len(gather)

    def body(h_ref, wg_ref, wu_ref, *rest):
        cin, (g_ref, u_ref, a_ref), cout, sems = rest[:nc], rest[nc:nc + 3], rest[nc + 3:2 * nc + 3], rest[2 * nc + 3:]
        i, j = pl.program_id(0), pl.program_id(1)

        if nc:
            @pl.when(jnp.logical_and(i == 0, j == 0))
            def _():
                _gather2_start(cin, cout, *sems)

        hv = h_ref[...]
        g = _dot(hv, wg_ref[...], NN)
        u = _dot(hv, wu_ref[...], NN)
        g_ref[...] = g
        u_ref[...] = u
        a_ref[...] = (g * _sigmoid(g) * u).astype(BF16)

        if nc:
            @pl.when(jnp.logical_and(i == grid[0] - 1, j == grid[1] - 1))
            def _():
                _gather2_pass_on(cin, cout, *sems)
                _gather2_finish(cin, cout, *sems)

    any_spec = pl.BlockSpec(memory_space=pl.ANY)
    out = pl.BlockSpec((tm, tn), lambda i, j: (i, j))
    res = pl.pallas_call(
        body, name=name, grid=grid,
        in_specs=[pl.BlockSpec((tm, D), lambda i, j: (i, 0)),
                  pl.BlockSpec((D, tn), lambda i, j: (0, j)),
                  pl.BlockSpec((D, tn), lambda i, j, o=F // tn: (0, j + o))] + [any_spec] * nc,
        out_specs=[out, out, out] + [any_spec] * nc,
        out_shape=[jax.ShapeDtypeStruct((T, F), F32), jax.ShapeDtypeStruct((T, F), F32),
                   jax.ShapeDtypeStruct((T, F), BF16)] + _comm_out_shapes(gather, True),
        scratch_shapes=_gather2_sems(nc) if nc else [],
        compiler_params=_params(),
    )(h, wgu, wgu, *gather)
    return res[:3], res[3:]


def _ffn_dact(dy, wd, g, u):
    T, D = dy.shape
    F = wd.shape[0]
    tm, tn = _tile(T, 1024), _tile(F, 512)

    nsplit = 2 if tn % (2 * LANES) == 0 else 1

    def body(dy_ref, wd_ref, g_ref, u_ref, dg_ref, du_ref):
        dyv = dy_ref[...]
        for c in range(nsplit):
            cols = slice(c * (tn // nsplit), (c + 1) * (tn // nsplit))
            da = _dot(dyv, wd_ref[cols, :], NT)
            g = g_ref[:, cols]
            sg = _sigmoid(g)
            dg_ref[:, cols] = (da * u_ref[:, cols] * (sg * (1.0 + g * (1.0 - sg)))).astype(BF16)
            du_ref[:, cols] = (da * (g * sg)).astype(BF16)

    blk = pl.BlockSpec((tm, tn), lambda i, j: (i, j))
    return pl.pallas_call(
        body, name="ffn_dact", grid=(T // tm, F // tn),
        in_specs=[pl.BlockSpec((tm, D), lambda i, j: (i, 0)), pl.BlockSpec((tn, D), lambda i, j: (j, 0)), blk, blk],
        out_specs=[blk, blk],
        out_shape=[jax.ShapeDtypeStruct((T, F), BF16)] * 2,
        compiler_params=_params(),
    )(dy, wd, g, u)


def _rstd(v):
    return lax.rsqrt(jnp.mean(v * v, axis=-1, keepdims=True) + EPS)


def _pre_fwd(x, g, sc, sh):
    T, D = x.shape
    tr = _tile(T, ROW_TILE, 8)

    def body(x_ref, g_ref, sc_ref, sh_ref, h_ref):
        xv = x_ref[...]
        r = xv * _rstd(xv) * g_ref[...]
        h_ref[...] = (r * (1.0 + sc_ref[...]) + sh_ref[...]).astype(BF16)

    row = pl.BlockSpec((tr, D), lambda i: (i, 0))
    return pl.pallas_call(
        body, name="pre_fwd", grid=(T // tr,),
        in_specs=[row, _vec_spec(D), _vec_spec(D), _vec_spec(D)], out_specs=row,
        out_shape=jax.ShapeDtypeStruct((T, D), BF16), compiler_params=_params(),
    )(x, g, sc, sh)


def _post_fwd(x, y, g, gate):
    T, D = x.shape
    tr = _tile(T, ROW_TILE, 8)

    def body(x_ref, y_ref, g_ref, gate_ref, o_ref):
        yv = y_ref[...]
        o_ref[...] = x_ref[...] + gate_ref[...] * (yv * _rstd(yv) * g_ref[...])

    row = pl.BlockSpec((tr, D), lambda i: (i, 0))
    return pl.pallas_call(
        body, name="post_fwd", grid=(T // tr,),
        in_specs=[row, row, _vec_spec(D), _vec_spec(D)], out_specs=row,
        out_shape=jax.ShapeDtypeStruct((T, D), F32), compiler_params=_params(),
    )(x, y, g, gate)


def _post_bwd(dx, y, g, gate):
    T, D = dx.shape
    tr = _tile(T, ROW_TILE, 8)

    def body(dx_ref, y_ref, g_ref, gate_ref, dy_ref, dgate_ref, dg_ref):
        @pl.when(pl.program_id(0) == 0)
        def _():
            dgate_ref[...] = jnp.zeros_like(dgate_ref)
            dg_ref[...] = jnp.zeros_like(dg_ref)

        yv, dxv = y_ref[...], dx_ref[...]
        rstd = _rstd(yv)
        yh = yv * rstd
        dgate_ref[...] += jnp.sum(dxv * (yh * g_ref[...]), axis=0, keepdims=True)
        dn = dxv * gate_ref[...]
        dg_ref[...] += jnp.sum(dn * yh, axis=0, keepdims=True)
        dyh = dn * g_ref[...]
        dy_ref[...] = (rstd * (dyh - yh * jnp.mean(dyh * yh, axis=-1, keepdims=True))).astype(BF16)

    row = pl.BlockSpec((tr, D), lambda i: (i, 0))
    vec = jax.ShapeDtypeStruct((1, D), F32)
    return pl.pallas_call(
        body, name="post_bwd", grid=(T // tr,),
        in_specs=[row, row, _vec_spec(D), _vec_spec(D)], out_specs=[row, _vec_spec(D), _vec_spec(D)],
        out_shape=[jax.ShapeDtypeStruct((T, D), BF16), vec, vec], compiler_params=_params(),
    )(dx, y, g, gate)


def _pre_bwd(dh, x, g, sc, dx_res):
    T, D = x.shape
    tr = _tile(T, ROW_TILE, 8)

    def body(dh_ref, x_ref, g_ref, sc_ref, dxr_ref, dx_ref, dsh_ref, dsc_ref, dg_ref):
        @pl.when(pl.program_id(0) == 0)
        def _():
            dsh_ref[...] = jnp.zeros_like(dsh_ref)
            dsc_ref[...] = jnp.zeros_like(dsc_ref)
            dg_ref[...] = jnp.zeros_like(dg_ref)

        xv, dhv = x_ref[...], dh_ref[...]
        rstd = _rstd(xv)
        xh = xv * rstd
        dsh_ref[...] += jnp.sum(dhv, axis=0, keepdims=True)
        dsc_ref[...] += jnp.sum(dhv * (xh * g_ref[...]), axis=0, keepdims=True)
        dr = dhv * (1.0 + sc_ref[...])
        dg_ref[...] += jnp.sum(dr * xh, axis=0, keepdims=True)
        dxh = dr * g_ref[...]
        dx_ref[...] = dxr_ref[...] + rstd * (dxh - xh * jnp.mean(dxh * xh, axis=-1, keepdims=True))

    row = pl.BlockSpec((tr, D), lambda i: (i, 0))
    vec = jax.ShapeDtypeStruct((1, D), F32)
    return pl.pallas_call(
        body, name="pre_bwd", grid=(T // tr,),
        in_specs=[row, row, _vec_spec(D), _vec_spec(D), row],
        out_specs=[row, _vec_spec(D), _vec_spec(D), _vec_spec(D)],
        out_shape=[jax.ShapeDtypeStruct((T, D), F32), vec, vec, vec], compiler_params=_params(),
    )(dh, x, g, sc, dx_res)


def _loss(x, target):
    T, D = x.shape
    tr = _tile(T, ROW_TILE, 8)

    def body(x_ref, t_ref, dx_ref, l_ref):
        @pl.when(pl.program_id(0) == 0)
        def _():
            l_ref[...] = jnp.zeros_like(l_ref)

        e = x_ref[...] - t_ref[...]
        dx_ref[...] = e / D
        rows = jnp.sum(e * e, axis=-1, keepdims=True)
        l_ref[...] += jnp.broadcast_to(jnp.sum(rows, axis=0, keepdims=True), (1, LANES))

    row = pl.BlockSpec((tr, D), lambda i: (i, 0))
    return pl.pallas_call(
        body, name="loss", grid=(T // tr,),
        in_specs=[row, row], out_specs=[row, _vec_spec(LANES)],
        out_shape=[jax.ShapeDtypeStruct((T, D), F32), jax.ShapeDtypeStruct((1, LANES), F32)],
        compiler_params=_params(),
    )(x, target)


def _split3(x):
    hi = x.astype(BF16)
    r = x - hi.astype(F32)
    mid = r.astype(BF16)
    lo = (r - mid.astype(F32)).astype(BF16)
    return hi, mid, lo


def _tri_sum(tri, x):
    hi, mid, lo = _split3(x)
    return _dot(tri, hi, NN) + _dot(tri, mid, NN) + _dot(tri, lo, NN)


def _fox_gate_fwd(fg, bf, H):
    T = fg.shape[0]
    tb = _tile(T, 512)

    def body(fg_ref, bf_ref, cum_ref, rep_ref, carry):
        @pl.when(pl.program_id(0) == 0)
        def _():
            carry[...] = jnp.zeros_like(carry)

        z = fg_ref[...] + bf_ref[...]
        logf = jnp.minimum(z, 0.0) - jnp.log(1.0 + jnp.exp(-jnp.abs(z)))
        row = lax.broadcasted_iota(jnp.int32, (tb, tb), 0)
        col = lax.broadcasted_iota(jnp.int32, (tb, tb), 1)
        cum = _tri_sum((row >= col).astype(BF16), logf) + carry[...]
        cum_ref[...] = cum
        carry[...] = cum_ref[pl.ds(tb - 1, 1), :]
        lane = lax.broadcasted_iota(jnp.int32, (tb, LANES), 1)
        for h in range(H):
            colv = jnp.sum(jnp.where(lane == h, cum, 0.0), axis=-1, keepdims=True)
            rep_ref[h] = jnp.broadcast_to(colv, (tb, LANES))

    return pl.pallas_call(
        body, name="fox_gate_fwd", grid=(T // tb,),
        in_specs=[pl.BlockSpec((tb, LANES), lambda i: (i, 0)), _vec_spec(LANES)],
        out_specs=[pl.BlockSpec((tb, LANES), lambda i: (i, 0)), pl.BlockSpec((H, tb, LANES), lambda i: (0, i, 0))],
        out_shape=[jax.ShapeDtypeStruct((T, LANES), F32), jax.ShapeDtypeStruct((H, T, LANES), F32)],
        scratch_shapes=[pltpu.VMEM((1, LANES), F32)], compiler_params=_params(),
    )(fg, bf)


def _fox_gate_bwd(dcum, fg, bf):
    T = fg.shape[0]
    tb = _tile(T, 512)
    nb = T // tb

    def body(dc_ref, fg_ref, bf_ref, dfg_ref, dbf_ref, carry):
        @pl.when(pl.program_id(0) == 0)
        def _():
            carry[...] = jnp.zeros_like(carry)
            dbf_ref[...] = jnp.zeros_like(dbf_ref)

        row = lax.broadcasted_iota(jnp.int32, (tb, tb), 0)
        col = lax.broadcasted_iota(jnp.int32, (tb, tb), 1)
        dc = dc_ref[...]
        dlogf = _tri_sum((row <= col).astype(BF16), dc) + carry[...]
        z = fg_ref[...] + bf_ref[...]
        dfg = dlogf * _sigmoid(-z)
        dfg_ref[...] = dfg
        dbf_ref[...] += jnp.sum(dfg, axis=0, keepdims=True)
        carry[...] += jnp.sum(dc, axis=0, keepdims=True)

    rev = pl.BlockSpec((tb, LANES), lambda i: (nb - 1 - i, 0))
    return pl.pallas_call(
        body, name="fox_gate_bwd", grid=(nb,),
        in_specs=[rev, rev, _vec_spec(LANES)], out_specs=[rev, _vec_spec(LANES)],
        out_shape=[jax.ShapeDtypeStruct((T, LANES), F32), jax.ShapeDtypeStruct((1, LANES), F32)],
        scratch_shapes=[pltpu.VMEM((1, LANES), F32)], compiler_params=_params(),
    )(dcum, fg, bf)


def _fox_blocks(T, pref=512):
    tb = pref if T >= 2 * pref else BLOCK
    return tb, T // tb


def _fox_fwd(name, qkv, cq_rep, ck, gather=()):
    T = qkv.shape[0]
    D = qkv.shape[1] // 3
    H = D // FOX_DH
    hps = FOX_HPS
    ng, wl = H // hps, hps * FOX_DH
    tb, nb = _fox_blocks(T, FOX_FWD_BLOCK)
    pairs = [(i, j) for i in range(nb) for j in range(i + 1)]
    qi = np.array([p[0] for p in pairs], np.int32)
    kj = np.array([p[1] for p in pairs], np.int32)
    npairs = len(pairs)
    nrep = tb // LANES
    nc = len(gather)

    def body(qi_ref, kj_ref, q_ref, k_ref, v_ref, cq_ref, ck_ref, *rest):
        cin, (o_ref, obf_ref, lse_ref), cout = rest[:nc], rest[nc:nc + 3], rest[nc + 3:2 * nc + 3]
        m_sc, l_sc, acc_sc = rest[2 * nc + 3:2 * nc + 6]
        sems = rest[2 * nc + 6:]
        g, p = pl.program_id(0), pl.program_id(1)
        i, j = qi_ref[p], kj_ref[p]

        if nc:
            @pl.when(jnp.logical_and(g == 0, p == 0))
            def _():
                _gather2_start(cin, cout, *sems)

        @pl.when(j == 0)
        def _():
            m_sc[...] = jnp.full_like(m_sc, NEG_INF)
            l_sc[...] = jnp.zeros_like(l_sc)
            acc_sc[...] = jnp.zeros_like(acc_sc)

        row = lax.broadcasted_iota(jnp.int32, (tb, tb), 0)
        col = lax.broadcasted_iota(jnp.int32, (tb, tb), 1)
        visible = jnp.logical_or(j < i, row >= col)
        for hh in range(hps):
            cols = slice(hh * FOX_DH, (hh + 1) * FOX_DH)
            s = _dot(q_ref[:, cols], k_ref[:, cols], NT) * FOX_SCALE
            s = jnp.where(visible, s + _rep(cq_ref[hh], nrep) - ck_ref[hh], NEG_INF)
            m_prev = m_sc[hh]
            m_new = jnp.maximum(m_prev, jnp.max(s, axis=-1, keepdims=True))
            alpha = jnp.exp(m_prev - m_new)
            pm = jnp.exp(s - _rep(m_new, nrep))
            l_sc[hh] = alpha * l_sc[hh] + jnp.sum(pm, axis=-1, keepdims=True)
            acc_sc[:, cols] = alpha * acc_sc[:, cols] + _dot(pm.astype(BF16), v_ref[:, cols], NN)
            m_sc[hh] = m_new

        @pl.when(j == i)
        def _():
            for hh in range(hps):
                cols = slice(hh * FOX_DH, (hh + 1) * FOX_DH)
                o = acc_sc[:, cols] / l_sc[hh]
                o_ref[:, cols] = o
                obf_ref[:, cols] = o.astype(BF16)
                lse_ref[hh] = m_sc[hh] + jnp.log(l_sc[hh])

        if nc:
            @pl.when(jnp.logical_and(g == ng - 1, p == npairs - 1))
            def _():
                _gather2_pass_on(cin, cout, *sems)
                _gather2_finish(cin, cout, *sems)

    any_spec = pl.BlockSpec(memory_space=pl.ANY)
    qblk = pl.BlockSpec((tb, wl), lambda g, p, qi, kj: (qi[p], g))
    qrep = pl.BlockSpec((hps, tb, LANES), lambda g, p, qi, kj: (g, qi[p], 0))
    grid_spec = pltpu.PrefetchScalarGridSpec(
        num_scalar_prefetch=2, grid=(ng, npairs),
        in_specs=[qblk,
                  pl.BlockSpec((tb, wl), lambda g, p, qi, kj: (kj[p], ng + g)),
                  pl.BlockSpec((tb, wl), lambda g, p, qi, kj: (kj[p], 2 * ng + g)),
                  qrep,
                  pl.BlockSpec((hps, 1, tb), lambda g, p, qi, kj: (g, 0, kj[p]))] + [any_spec] * nc,
        out_specs=[qblk, qblk, qrep] + [any_spec] * nc,
        scratch_shapes=[pltpu.VMEM((hps, tb, LANES), F32), pltpu.VMEM((hps, tb, LANES), F32),
                        pltpu.VMEM((tb, wl), F32)] + (_gather2_sems(nc) if nc else []))
    res = pl.pallas_call(
        body, name=name, grid_spec=grid_spec,
        out_shape=[jax.ShapeDtypeStruct((T, D), F32), jax.ShapeDtypeStruct((T, D), BF16),
                   jax.ShapeDtypeStruct((H, T, LANES), F32)] + _comm_out_shapes(gather, True),
        compiler_params=_params(),
    )(qi, kj, qkv, qkv, qkv, cq_rep, ck, *gather)
    return res[:3], res[3:]


def _fox_bwd_prep(do, o, lse_rep, cq_rep):
    T, D = do.shape
    H = D // FOX_DH
    tr = _tile(T, ROW_TILE, 8)

    def body(do_ref, o_ref, lse_ref, cq_ref, dob_ref, delta_ref, cql_ref):
        dov = do_ref[...]
        dob_ref[...] = dov.astype(BF16)
        prod = dov * o_ref[...]
        for h in range(H):
            d = jnp.sum(prod[:, h * FOX_DH:(h + 1) * FOX_DH], axis=-1, keepdims=True)
            delta_ref[h] = jnp.broadcast_to(d, (tr, LANES))
        cql_ref[...] = cq_ref[...] - lse_ref[...]

    row = pl.BlockSpec((tr, D), lambda i: (i, 0))
    rep = pl.BlockSpec((H, tr, LANES), lambda i: (0, i, 0))
    return pl.pallas_call(
        body, name="fox_bwd_prep", grid=(T // tr,),
        in_specs=[row, row, rep, rep], out_specs=[row, rep, rep],
        out_shape=[jax.ShapeDtypeStruct((T, D), BF16), jax.ShapeDtypeStruct((H, T, LANES), F32),
                   jax.ShapeDtypeStruct((H, T, LANES), F32)],
        compiler_params=_params(),
    )(do, o, lse_rep, cq_rep)


def _fox_bwd(name, qkv, dob, delta_rep, cql_rep, ck, a2a=()):
    T = qkv.shape[0]
    D = qkv.shape[1] // 3
    H = D // FOX_DH
    hps = FOX_HPS
    ng, wl = H // hps, hps * FOX_DH
    tk, nbk = _fox_blocks(T, FOX_BWD_KEYS)
    tq = max(_fox_blocks(T, FOX_BWD_QUERIES)[0], tk)
    nbq, ratio = T // tq, tq // tk
    pairs = [(i, j) for j in range(nbk) for i in range(j // ratio, nbq)]
    qi = np.array([p[0] for p in pairs], np.int32)
    kj = np.array([p[1] for p in pairs], np.int32)
    npairs = len(pairs)
    nrep = tk // LANES
    nc = len(a2a)

    def body(qi_ref, kj_ref, q_ref, k_ref, v_ref, do_ref, delta_ref, cql_ref, ck_ref, *rest):
        cin, (dq_ref, dk_ref, dv_ref, dck_ref, dcq_ref), cout = rest[:nc], rest[nc:nc + 5], rest[nc + 5:2 * nc + 5]
        dq_acc, dk_acc, dv_acc, dc_acc = rest[2 * nc + 5:2 * nc + 9]
        sems = rest[2 * nc + 9:]
        g, p = pl.program_id(0), pl.program_id(1)
        i, j = qi_ref[p], kj_ref[p]

        @pl.when(jnp.logical_and(g == 0, p == 0))
        def _():
            dcq_ref[...] = jnp.zeros_like(dcq_ref)
            if nc:
                _comm_start(cin, cout, False, *sems)

        @pl.when(p == 0)
        def _():
            dq_acc[...] = jnp.zeros_like(dq_acc)

        @pl.when(i == j // ratio)
        def _():
            dk_acc[...] = jnp.zeros_like(dk_acc)
            dv_acc[...] = jnp.zeros_like(dv_acc)
            dc_acc[...] = jnp.zeros_like(dc_acc)

        row = lax.broadcasted_iota(jnp.int32, (tq, tk), 0)
        col = lax.broadcasted_iota(jnp.int32, (tq, tk), 1)
        visible = jnp.logical_or((j + 1) * tk <= i * tq, row + i * tq >= col + j * tk)
        lane = lax.broadcasted_iota(jnp.int32, (tq, LANES), 1)
        rows = pl.ds(pl.multiple_of(i * tq, tq), tq)
        dcq = jnp.zeros((tq, LANES), F32)
        for hh in range(hps):
            cols = slice(hh * FOX_DH, (hh + 1) * FOX_DH)
            q, k, v, dov = q_ref[:, cols], k_ref[:, cols], v_ref[:, cols], do_ref[:, cols]
            s = _dot(q, k, NT) * FOX_SCALE + _rep(cql_ref[hh], nrep) - ck_ref[hh]
            pm = jnp.exp(jnp.where(visible, s, NEG_INF))
            dv_acc[:, cols] += _dot(pm.astype(BF16), dov, TN)
            ds = pm * (_dot(dov, v, NT) - _rep(delta_ref[hh], nrep))
            dsb = (ds * FOX_SCALE).astype(BF16)
            dk_acc[:, cols] += _dot(dsb, q, TN)
            dq_acc[rows, cols] += _dot(dsb, k, NN)
            dc_acc[hh] -= jnp.sum(ds, axis=0, keepdims=True)
            dcq = dcq + jnp.where(lane == g * hps + hh, jnp.sum(ds, axis=-1, keepdims=True), 0.0)
        dcq_ref[rows, :] += dcq

        @pl.when(i == nbq - 1)
        def _():
            dk_ref[...] = dk_acc[...].astype(BF16)
            dv_ref[...] = dv_acc[...].astype(BF16)
            dck_ref[...] = dc_acc[...]

        @pl.when(p == npairs - 1)
        def _():
            dq_ref[...] = dq_acc[...].astype(BF16)

        if nc:
            @pl.when(jnp.logical_and(g == ng - 1, p == npairs - 1))
            def _():
                _comm_wait(cin, cout, False, *sems)

    any_spec = pl.BlockSpec(memory_space=pl.ANY)
    qblk = pl.BlockSpec((tq, wl), lambda g, p, qi, kj: (qi[p], g))
    qrep = pl.BlockSpec((hps, tq, LANES), lambda g, p, qi, kj: (g, qi[p], 0))
    kblk = pl.BlockSpec((tk, wl), lambda g, p, qi, kj: (kj[p], g))
    krow = pl.BlockSpec((hps, 1, tk), lambda g, p, qi, kj: (g, 0, kj[p]))
    grid_spec = pltpu.PrefetchScalarGridSpec(
        num_scalar_prefetch=2, grid=(ng, npairs),
        in_specs=[qblk,
                  pl.BlockSpec((tk, wl), lambda g, p, qi, kj: (kj[p], ng + g)),
                  pl.BlockSpec((tk, wl), lambda g, p, qi, kj: (kj[p], 2 * ng + g)),
                  qblk, qrep, qrep, krow] + [any_spec] * nc,
        out_specs=[pl.BlockSpec((T, wl), lambda g, p, qi, kj: (0, g)), kblk, kblk, krow,
                   pl.BlockSpec((T, LANES), lambda g, p, qi, kj: (0, 0))] + [any_spec] * nc,
        scratch_shapes=[pltpu.VMEM((T, wl), F32), pltpu.VMEM((tk, wl), F32), pltpu.VMEM((tk, wl), F32),
                        pltpu.VMEM((hps, 1, tk), F32)] + (_comm_sems(nc) if nc else []))
    act = jax.ShapeDtypeStruct((T, D), BF16)
    res = pl.pallas_call(
        body, name=name, grid_spec=grid_spec,
        out_shape=[act, act, act, jax.ShapeDtypeStruct((H, 1, T), F32),
                   jax.ShapeDtypeStruct((T, LANES), F32)] + _comm_out_shapes(a2a, False),
        compiler_params=_params(),
    )(qi, kj, qkv, qkv, qkv, dob, delta_rep, cql_rep, ck, *a2a)
    return res[:5], res[5:]


def _sgu_rows(T):
    return 2 * BLOCK if T % (2 * BLOCK) == 0 else BLOCK


def _sgu_norm(zv, g_ref, b_ref):
    vv = _gelu(zv)
    mu = jnp.mean(vv, axis=-1, keepdims=True)
    cen = vv - mu
    rstd = lax.rsqrt(jnp.mean(cen * cen, axis=-1, keepdims=True) + EPS)
    vh = cen * rstd
    return vh, rstd, vh * g_ref[...] + b_ref[...]


def _sgu_fwd(zpre, ln_g, ln_b, wsm, bs_rep):
    T = zpre.shape[0]
    W = zpre.shape[1] // 2
    G = W // BLOCK
    tr = _sgu_rows(T)

    def body(z_ref, g_ref, b_ref, ws_ref, bs_ref, o_ref):
        u = _gelu(z_ref[:, :W])
        _, _, vln = _sgu_norm(z_ref[:, W:], g_ref, b_ref)
        for c in range(tr // BLOCK):
            rows = slice(c * BLOCK, (c + 1) * BLOCK)
            for gi in range(G):
                cols = slice(gi * BLOCK, (gi + 1) * BLOCK)
                f = _dot(ws_ref[gi], vln[rows, cols].astype(BF16), NN) + bs_ref[gi]
                o_ref[rows, cols] = (u[rows, cols] * f).astype(BF16)

    full3 = pl.BlockSpec((G, BLOCK, BLOCK), lambda i: (0, 0, 0))
    return pl.pallas_call(
        body, name="sgu_fwd", grid=(T // tr,),
        in_specs=[pl.BlockSpec((tr, 2 * W), lambda i: (i, 0)), _vec_spec(W), _vec_spec(W), full3, full3],
        out_specs=pl.BlockSpec((tr, W), lambda i: (i, 0)),
        out_shape=jax.ShapeDtypeStruct((T, W), BF16), compiler_params=_params(),
    )(zpre, ln_g, ln_b, wsm, bs_rep)


def _sgu_bwd(dgt, zpre, ln_g, ln_b, wsm, wsmT, bs_rep):
    T = zpre.shape[0]
    W = zpre.shape[1] // 2
    G = W // BLOCK
    tr = BLOCK

    def body(dgt_ref, z_ref, g_ref, b_ref, ws_ref, wst_ref, bs_ref,
             dz_ref, dws_ref, dbs_ref, dlg_ref, dlb_ref, du_sc, dvln_sc):
        @pl.when(pl.program_id(0) == 0)
        def _():
            dws_ref[...] = jnp.zeros_like(dws_ref)
            dbs_ref[...] = jnp.zeros_like(dbs_ref)
            dlg_ref[...] = jnp.zeros_like(dlg_ref)
            dlb_ref[...] = jnp.zeros_like(dlb_ref)

        zu = z_ref[:, :W]
        zv = z_ref[:, W:]
        u = _gelu(zu)
        vh, rstd, vln = _sgu_norm(zv, g_ref, b_ref)
        dgtv = dgt_ref[...]
        trow = lax.broadcasted_iota(jnp.int32, (BLOCK, BLOCK), 0)
        tcol = lax.broadcasted_iota(jnp.int32, (BLOCK, BLOCK), 1)
        causal = trow >= tcol
        for c in range(tr // BLOCK):
            rows = slice(c * BLOCK, (c + 1) * BLOCK)
            for gi in range(G):
                cols = slice(gi * BLOCK, (gi + 1) * BLOCK)
                vb = vln[rows, cols].astype(BF16)
                f = _dot(ws_ref[gi], vb, NN) + bs_ref[gi]
                d = dgtv[rows, cols]
                du_sc[rows, cols] = d * f
                df = d * u[rows, cols]
                dfb = df.astype(BF16)
                dvln_sc[rows, cols] = _dot(wst_ref[gi], dfb, NN)
                dws_ref[gi] += jnp.where(causal, _dot(dfb, vb, NT), 0.0)
                dbs_ref[gi] += jnp.broadcast_to(jnp.sum(df, axis=-1, keepdims=True), (BLOCK, BLOCK))
        dvln = dvln_sc[...]
        dlg_ref[...] += jnp.sum(dvln * vh, axis=0, keepdims=True)
        dlb_ref[...] += jnp.sum(dvln, axis=0, keepdims=True)
        dvh = dvln * g_ref[...]
        dvv = rstd * (dvh - jnp.mean(dvh, axis=-1, keepdims=True)
                      - vh * jnp.mean(dvh * vh, axis=-1, keepdims=True))
        dz_ref[:, :W] = (du_sc[...] * _gelu_grad(zu)).astype(BF16)
        dz_ref[:, W:] = (dvv * _gelu_grad(zv)).astype(BF16)

    full3 = pl.BlockSpec((G, BLOCK, BLOCK), lambda i: (0, 0, 0))
    vec = jax.ShapeDtypeStruct((1, W), F32)
    acc3 = jax.ShapeDtypeStruct((G, BLOCK, BLOCK), F32)
    return pl.pallas_call(
        body, name="sgu_bwd", grid=(T // tr,),
        in_specs=[pl.BlockSpec((tr, W), lambda i: (i, 0)), pl.BlockSpec((tr, 2 * W), lambda i: (i, 0)),
                  _vec_spec(W), _vec_spec(W), full3, full3, full3],
        out_specs=[pl.BlockSpec((tr, 2 * W), lambda i: (i, 0)), full3, full3, _vec_spec(W), _vec_spec(W)],
        out_shape=[jax.ShapeDtypeStruct((T, 2 * W), BF16), acc3, acc3, vec, vec],
        scratch_shapes=[pltpu.VMEM((tr, W), F32), pltpu.VMEM((tr, W), F32)],
        compiler_params=_params(),
    )(dgt, zpre, ln_g, ln_b, wsm, wsmT, bs_rep)


def _rope(x, cos_t, sin_t):
    T, N = x.shape
    tr = _tile(T, ROW_TILE, 8)
    nrep = N // LANES
    half = ROPE_DIM // 2

    def body(x_ref, c_ref, s_ref, o_ref):
        xv = x_ref[...]
        lane = jnp.bitwise_and(lax.broadcasted_iota(jnp.int32, (tr, N), 1), SWA_DH - 1)
        partner = jnp.where(lane < half, -pltpu.roll(xv, N - half, 1), pltpu.roll(xv, half, 1))
        o_ref[...] = (xv * _rep(c_ref[...], nrep) + partner * _rep(s_ref[...], nrep)).astype(BF16)

    tab = pl.BlockSpec((tr, LANES), lambda i: (i, 0))
    row = pl.BlockSpec((tr, N), lambda i: (i, 0))
    return pl.pallas_call(
        body, name="rope", grid=(T // tr,), in_specs=[row, tab, tab], out_specs=row,
        out_shape=jax.ShapeDtypeStruct((T, N), BF16), compiler_params=_params(),
    )(x, cos_t, sin_t)


def _swa_tiles(T):
    sb = 4 if T >= 2048 else 2
    return sb, BLOCK * sb, T // (BLOCK * sb)


def _band_mask():
    row = lax.broadcasted_iota(jnp.int32, (BLOCK, 2 * BLOCK), 0)
    col = lax.broadcasted_iota(jnp.int32, (BLOCK, 2 * BLOCK), 1)
    return jnp.logical_and(col > row, col <= row + BLOCK), col


def _swa_specs(T, G):
    sb, tq, nq = _swa_tiles(T)
    q = pl.BlockSpec((None, tq, LANES), lambda h, i: (h, i, 0))
    kc = pl.BlockSpec((None, tq, LANES), lambda h, i: (h // G, i, 0))
    kp = pl.BlockSpec((None, BLOCK, LANES), lambda h, i: (h // G, jnp.maximum(i * sb - 1, 0), 0))
    return q, kc, kp


def _swa_band(b, i, kc_ref, kp_ref, vc_ref, vp_ref):
    rows = slice(b * BLOCK, (b + 1) * BLOCK)
    prev = slice((b - 1) * BLOCK, b * BLOCK)
    kprev = kp_ref[...] if b == 0 else kc_ref[prev, :]
    vprev = vp_ref[...] if b == 0 else vc_ref[prev, :]
    K = jnp.concatenate([kprev, kc_ref[rows, :]], axis=0)
    V = jnp.concatenate([vprev, vc_ref[rows, :]], axis=0)
    band, col = _band_mask()
    if b == 0:
        band = jnp.logical_and(band, jnp.logical_or(col >= BLOCK, i > 0))
    return rows, K, V, band


def _swa_fwd(qp, kp, vp, sinks):
    Hq, T, _ = qp.shape
    G = Hq // kp.shape[0]
    sb, tq, nq = _swa_tiles(T)

    def body(sink_ref, q_ref, kc_ref, kp_ref, vc_ref, vp_ref, o_ref, lse_ref):
        h, i = pl.program_id(0), pl.program_id(1)
        sink = sink_ref[h]
        for b in range(sb):
            rows, K, V, band = _swa_band(b, i, kc_ref, kp_ref, vc_ref, vp_ref)
            s = jnp.where(band, _dot(q_ref[rows, :], K, NT) * SWA_SCALE, NEG_INF)
            m = jnp.maximum(jnp.max(s, axis=-1, keepdims=True), sink)
            pm = jnp.exp(s - m)
            den = jnp.sum(pm, axis=-1, keepdims=True) + jnp.exp(sink - m)
            o_ref[rows, :] = _dot((pm / den).astype(BF16), V, NN)
            lse_ref[rows, :] = jnp.broadcast_to(m + jnp.log(den), (BLOCK, LANES))

    q, kc, kpv = _swa_specs(T, G)
    out = jax.ShapeDtypeStruct((Hq, T, LANES), F32)
    return pl.pallas_call(
        body, name="swa_fwd", grid=(Hq, nq),
        in_specs=[pl.BlockSpec(memory_space=pltpu.SMEM), q, kc, kpv, kc, kpv], out_specs=[q, q],
        out_shape=[out, out], compiler_params=_params(),
    )(sinks, qp, kp, kp, vp, vp)


def _swa_bwd_dq(qp, kp, vp, dop, op, lse_rep, sinks):
    Hq, T, _ = qp.shape
    G = Hq // kp.shape[0]
    sb, tq, nq = _swa_tiles(T)

    def body(sink_ref, q_ref, kc_ref, kp_ref, vc_ref, vp_ref, do_ref, o_ref, lse_ref, dq_ref, dsink_ref):
        h, i = pl.program_id(0), pl.program_id(1)
        sink = sink_ref[h]

        @pl.when(i == 0)
        def _():
            dsink_ref[...] = jnp.zeros_like(dsink_ref)

        for b in range(sb):
            rows, K, V, band = _swa_band(b, i, kc_ref, kp_ref, vc_ref, vp_ref)
            dov = do_ref[rows, :]
            delta = jnp.sum(dov * o_ref[rows, :], axis=-1, keepdims=True)
            lse = lse_ref[rows, :]
            s = jnp.where(band, _dot(q_ref[rows, :], K, NT) * SWA_SCALE, NEG_INF)
            pm = jnp.exp(s - _rep(lse, 2))
            dp = _dot(dov.astype(BF16), V, NT)
            ds = pm * (dp - delta)
            dq_ref[rows, :] = _dot((ds * SWA_SCALE).astype(BF16), K, NN)
            part = jnp.sum(jnp.exp(sink - lse) * delta, axis=0, keepdims=True)
            dsink_ref[...] -= jnp.broadcast_to(part, (8, LANES))

    q, kc, kpv = _swa_specs(T, G)
    return pl.pallas_call(
        body, name="swa_bwd_dq", grid=(Hq, nq),
        in_specs=[pl.BlockSpec(memory_space=pltpu.SMEM), q, kc, kpv, kc, kpv, q, q, q],
        out_specs=[q, pl.BlockSpec((None, 8, LANES), lambda h, i: (h, 0, 0))],
        out_shape=[jax.ShapeDtypeStruct((Hq, T, LANES), F32), jax.ShapeDtypeStruct((Hq, 8, LANES), F32)],
        compiler_params=_params(),
    )(sinks, qp, kp, kp, vp, vp, dop, op, lse_rep)


def _swa_bwd_dkv(qp, kp, vp, dop, op, lse_rep):
    Hq, T, _ = qp.shape
    Hk = kp.shape[0]
    G = Hq // Hk
    sb, tq, nq = _swa_tiles(T)
    nblk = T // BLOCK

    def body(k_ref, v_ref, q_ref, qn_ref, do_ref, don_ref, o_ref, on_ref, lse_ref, lsen_ref, dk_ref, dv_ref):
        i = pl.program_id(1)
        trow = lax.broadcasted_iota(jnp.int32, (2 * BLOCK, BLOCK), 0)
        scol = lax.broadcasted_iota(jnp.int32, (2 * BLOCK, BLOCK), 1)
        band0 = jnp.logical_and(trow >= scol, trow < scol + BLOCK)
        for b in range(sb):
            rows = slice(b * BLOCK, (b + 1) * BLOCK)
            nxt = slice((b + 1) * BLOCK, (b + 2) * BLOCK)
            last = b == sb - 1
            band = band0
            if last:
                band = jnp.logical_and(band0, jnp.logical_or(trow < BLOCK, i < nq - 1))
            kb, vb = k_ref[rows, :], v_ref[rows, :]
            dk = jnp.zeros((BLOCK, LANES), F32)
            dv = jnp.zeros((BLOCK, LANES), F32)
            for g in range(G):
                def two(cur, nx):
                    return jnp.concatenate([cur[g, rows, :], nx[g] if last else cur[g, nxt, :]], axis=0)
                Q, dov, ov, lse = two(q_ref, qn_ref), two(do_ref, don_ref), two(o_ref, on_ref), two(lse_ref, lsen_ref)
                delta = jnp.sum(dov * ov, axis=-1, keepdims=True)
                s = jnp.where(band, _dot(Q, kb, NT) * SWA_SCALE, NEG_INF)
                pm = jnp.exp(s - lse)
                dob = dov.astype(BF16)
                dv = dv + _dot(pm.astype(BF16), dob, TN)
                ds = pm * (_dot(dob, vb, NT) - delta)
                dk = dk + _dot((ds * SWA_SCALE).astype(BF16), Q, TN)
            dk_ref[rows, :] = dk
            dv_ref[rows, :] = dv

    kspec = pl.BlockSpec((None, tq, LANES), lambda h, i: (h, i, 0))
    cur = pl.BlockSpec((G, tq, LANES), lambda h, i: (h, i, 0))
    nxt = pl.BlockSpec((G, BLOCK, LANES), lambda h, i: (h, jnp.minimum((i + 1) * sb, nblk - 1), 0))
    out = jax.ShapeDtypeStruct((Hk, T, LANES), F32)
    return pl.pallas_call(
        body, name="swa_bwd_dkv", grid=(Hk, nq),
        in_specs=[kspec, kspec, cur, nxt, cur, nxt, cur, nxt, cur, nxt], out_specs=[kspec, kspec],
        out_shape=[out, out], compiler_params=_params(),
    )(kp, vp, qp, qp, dop, dop, op, op, lse_rep, lse_rep)


def _to_heads(a, nh):
    T = a.shape[0]
    a = a.reshape(T, nh, SWA_DH).transpose(1, 0, 2)
    return jnp.pad(a, ((0, 0), (0, 0), (0, LANES - SWA_DH)))


def _from_heads(a):
    nh, T, _ = a.shape
    return a[:, :, :SWA_DH].transpose(1, 0, 2).reshape(T, nh * SWA_DH)


def _adam(g, w, m, v):
    m2 = ADAM_B1 * m + (1.0 - ADAM_B1) * g
    v2 = ADAM_B2 * v + (1.0 - ADAM_B2) * (g * g)
    m_hat = m2 / (1.0 - ADAM_B1 ** ADAM_STEP)
    v_hat = v2 / (1.0 - ADAM_B2 ** ADAM_STEP)
    delta = -ADAM_LR * (m_hat / (jnp.sqrt(v_hat) + ADAM_EPS) + ADAM_WD * w)
    return delta, m2, v2


def _ada_fwd(c_all, w, b):
    L, D, n = w.shape
    tn = _tile(n, 768)

    def body(c_ref, w_ref, b_ref, o_ref):
        cv = c_ref[...]
        ca = (cv * _sigmoid(cv)).astype(BF16)
        o_ref[...] = _dot(ca, w_ref[...].astype(BF16), NN) + b_ref[...]

    return pl.pallas_call(
        body, name="ada_fwd", grid=(L, n // tn),
        in_specs=[pl.BlockSpec((NDEV, D), lambda l, j: (0, 0)), pl.BlockSpec((None, D, tn), lambda l, j: (l, 0, j)),
                  pl.BlockSpec((None, 1, tn), lambda l, j: (l, 0, j))],
        out_specs=pl.BlockSpec((None, NDEV, tn), lambda l, j: (l, 0, j)),
        out_shape=jax.ShapeDtypeStruct((L, NDEV, n), F32), compiler_params=_params(),
    )(c_all, w, b)


def _ada_update(c_rep, dm, w, m, v):
    L, D, n = w.shape
    tr = _tile(D, 256, 8)
    nrep = n // LANES

    def body(c_ref, dm_ref, w_ref, m_ref, v_ref, g_ref, d_ref, m2_ref, v2_ref):
        g = jnp.zeros((tr, n), F32)
        for b in range(NDEV):
            cv = c_ref[b]
            g = g + _rep(cv * _sigmoid(cv), nrep) * dm_ref[pl.ds(b, 1), :]
        g_ref[...] = g
        d_ref[...], m2_ref[...], v2_ref[...] = _adam(g, w_ref[...], m_ref[...], v_ref[...])

    blk = pl.BlockSpec((None, tr, n), lambda l, i: (l, i, 0))
    out = jax.ShapeDtypeStruct((L, D, n), F32)
    return pl.pallas_call(
        body, name="ada_update", grid=(L, D // tr),
        in_specs=[pl.BlockSpec((NDEV, tr, LANES), lambda l, i: (0, i, 0)),
                  pl.BlockSpec((None, NDEV, n), lambda l, i: (l, 0, 0)), blk, blk, blk],
        out_specs=[blk, blk, blk, blk], out_shape=[out, out, out, out], compiler_params=_params(),
    )(c_rep, dm, w, m, v)


def _adamw(name, parts, w, m, v, layer, stacked=None):
    P, R, C = parts.shape
    Lw = w.shape[0]
    cpad = -(-C // LANES) * LANES
    per_row = cpad * (P * parts.dtype.itemsize + 7 * 4) * 2
    tr = _tile(R, max(8, (24 * 1024 * 1024 // per_row) // 8 * 8), 8)
    if stacked is None:
        stacked = [lax.empty((Lw, R, C), F32) for _ in range(4)]

    def body(p_ref, w_ref, m_ref, v_ref, *rest):
        g_ref, d_ref, m2_ref, v2_ref = rest[4:]
        g = p_ref[0].astype(F32)
        for s in range(1, P):
            g = g + p_ref[s].astype(F32)
        g_ref[...] = g
        d_ref[...], m2_ref[...], v2_ref[...] = _adam(g, w_ref[...], m_ref[...], v_ref[...])

    stk = pl.BlockSpec((None, tr, C), lambda i: (layer, i, 0))
    any_spec = pl.BlockSpec(memory_space=pl.ANY)
    out = jax.ShapeDtypeStruct((Lw, R, C), F32)
    return pl.pallas_call(
        body, name=name, grid=(R // tr,),
        in_specs=[pl.BlockSpec((P, tr, C), lambda i: (0, i, 0)), stk, stk, stk] + [any_spec] * 4,
        out_specs=[stk, stk, stk, stk], out_shape=[out, out, out, out],
        input_output_aliases={4: 0, 5: 1, 6: 2, 7: 3}, compiler_params=_params(),
    )(parts, w, m, v, *stacked)


def _colcat(a):
    s, k, n = a.shape
    return a.transpose(1, 0, 2).reshape(k, s * n)


def _colsplit(a):
    k, n8 = a.shape
    return a.reshape(k, NDEV, n8 // NDEV).transpose(1, 0, 2)


def _rowsplit(a):
    r, c = a.shape
    return a.reshape(NDEV, r // NDEV, c)


def kernel(x, c, positions, ada_w, ada_b, mix_pre_g, mix_post_g, ffn_pre_g, ffn_post_g, ffn_w_gu, ffn_w_down, fox_w_in, fox_b_f, fox_w_out, sgu_w_in, sgu_ln_g, sgu_ln_b, sgu_w_s, sgu_b_s, sgu_w_out, swa_w_in, swa_sinks, swa_w_out, loss_target, m_ada_w, m_ada_b, m_mix_pre_g, m_mix_post_g, m_ffn_pre_g, m_ffn_post_g, m_ffn_w_gu, m_ffn_w_down, m_fox_w_in, m_fox_b_f, m_fox_w_out, m_sgu_w_in, m_sgu_ln_g, m_sgu_ln_b, m_sgu_w_s, m_sgu_b_s, m_sgu_w_out, m_swa_w_in, m_swa_sinks, m_swa_w_out, v_ada_w, v_ada_b, v_mix_pre_g, v_mix_post_g, v_ffn_pre_g, v_ffn_post_g, v_ffn_w_gu, v_ffn_w_down, v_fox_w_in, v_fox_b_f, v_fox_w_out, v_sgu_w_in, v_sgu_ln_g, v_sgu_ln_b, v_sgu_w_s, v_sgu_b_s, v_sgu_w_out, v_swa_w_in, v_swa_sinks, v_swa_w_out):
    env = locals()
    W = {n: env[n] for n in WEIGHTS}
    M = {n: env["m_" + n] for n in WEIGHTS}
    V = {n: env["v_" + n] for n in WEIGHTS}

    me = 4 * lax.axis_index("x") + 2 * lax.axis_index("y") + lax.axis_index("c")
    _, T, D = x.shape
    L = ada_w.shape[0]
    n_ada = ada_w.shape[2]
    F = ffn_w_gu.shape[2] * NDEV // 2
    H = D // FOX_DH
    Hq = D // SWA_DH
    x0 = x[0]
    mixer = {0: 'fox', 1: 'sgu', 2: 'swa'}

    c_all = _comm("gather_c", [c], 'gather')[0].reshape(NDEV, D)
    ada_b_mine = lax.dynamic_slice_in_dim(ada_b, me * n_ada, n_ada, axis=1).reshape(L, 1, n_ada)
    mod_cols = _ada_fwd(c_all, ada_w, ada_b_mine)
    mod = _comm("a2a_mod", [mod_cols.transpose(1, 0, 2)], 'a2a')[0]
    mod = mod.transpose(1, 0, 2).reshape(L, 6, 1, D)

    inv = ROPE_THETA ** (-jnp.arange(0, ROPE_DIM, 2, dtype=F32) / ROPE_DIM)
    ang = positions[0].astype(F32)[:, None] * inv
    pad1 = jnp.ones((T, SWA_DH - ROPE_DIM), F32)
    cos64 = jnp.concatenate([jnp.cos(ang), jnp.cos(ang), pad1], axis=1)
    sin64 = jnp.concatenate([jnp.sin(ang), jnp.sin(ang), 0.0 * pad1], axis=1)
    cos_t = jnp.concatenate([cos64, cos64], axis=1)
    sin_t = jnp.concatenate([sin64, sin64], axis=1)

    fox_layers = [i for i in range(L) if mixer[i % 3] == 'fox']
    assert fox_layers and fox_layers[0] == 0

    def slice_of(i, role):
        kind, j = mixer[i % 3], i // 3
        src = {'wgu': (ffn_w_gu, i), 'wd': (ffn_w_down, i), 'win': (W[kind + '_w_in'], j), 'wout': (W[kind + '_w_out'], j)}[role]
        return src[0][src[1]].astype(BF16)

    def nparams(key):
        return int(np.prod(slice_of(*key).shape)) * NDEV

    def gather_plan():
        plan = {}
        for f in fox_layers:
            later = [i for i in fox_layers if i > f]
            stop = later[0] if later else L
            keys = [(f, 'wout'), (f, 'wgu'), (f, 'wd')]
            for i in range(f + 1, stop):
                keys += [(i, 'wgu'), (i, 'wd'), (i, 'win'), (i, 'wout')]
            if later:
                keys += [(stop, 'win')]
            near = [k for k in keys if k[0] <= f + 1]
            far = [k for k in keys if k[0] > f + 1]
            plan[('fox', f)] = near
            for i in range(f, stop):
                take, total = [], 0
                while far and (not take or total + nparams(far[0]) <= HOSTED_GATHER_PARAMS):
                    total += nparams(far[0])
                    take.append(far.pop(0))
                plan[('ffn', i)] = take
            assert not far
        return plan

    plan = gather_plan()
    raw, full = {}, {}
    first_keys = [(0, 'win')]
    raw.update(zip(first_keys, _gather2("gather_first", [slice_of(*k) for k in first_keys])))

    def wget(i, role):
        if (i, role) not in full:
            got = raw[(i, role)]
            full[(i, role)] = _colcat(got) if role in ('wgu', 'win') else got.reshape(-1, D)
        return full[(i, role)]

    saved = []
    xc = x0
    for i in range(L):
        kind, j = mixer[i % 3], i // 3
        s = dict(x_in=xc)
        sh_m, sc_m, g_m, sh_f, sc_f, g_f = [mod[i, t] for t in range(6)]
        h = _pre_fwd(xc, mix_pre_g[i:i + 1], sc_m, sh_m)
        s['h'] = h
        if kind == 'fox':
            wqkv = wget(i, 'win')[:, :3 * D]
            wf = jnp.pad(wget(i, 'win')[:, 3 * D:], ((0, 0), (0, LANES - H)))
            s['win_pad'] = jnp.concatenate([wqkv, wf], axis=1)
            bf = jnp.pad(fox_b_f[j:j + 1], ((0, 0), (0, LANES - H)))
            qkv = _mm("fox_qkv", h, wqkv, 'nn', BF16)
            fg = _mm("fox_fg", h, wf, 'nn', F32)
            cum, cq_rep = _fox_gate_fwd(fg, bf, H)
            ck = cum[:, :H].T.reshape(H, 1, T)
            keys = plan[('fox', i)]
            (o, obf, lse_rep), got = _fox_fwd("fox_fwd%d" % j, qkv, cq_rep, ck, gather=[slice_of(*k) for k in keys])
            raw.update(zip(keys, got))
            s.update(qkv=qkv, fg=fg, bf=bf, cq_rep=cq_rep, ck=ck, o=o, lse_rep=lse_rep, mix_out=obf)
        elif kind == 'sgu':
            G = D // BLOCK
            causal = jnp.tril(jnp.ones((BLOCK, BLOCK), bool))
            wsm = jnp.where(causal[None], sgu_w_s[j], 0.0).astype(BF16)
            bs_rep = jnp.broadcast_to(sgu_b_s[j][:, :, None], (G, BLOCK, BLOCK))
            zpre = _mm("sgu_in", h, wget(i, 'win'), 'nn', F32)
            gated = _sgu_fwd(zpre, sgu_ln_g[j:j + 1], sgu_ln_b[j:j + 1], wsm, bs_rep)
            s.update(zpre=zpre, wsm=wsm, bs_rep=bs_rep, mix_out=gated)
        else:
            Hk = (wget(i, 'win').shape[1] // SWA_DH - Hq) // 2
            proj = _mm("swa_in", h, wget(i, 'win'), 'nn', F32)
            qr = _rope(proj[:, :Hq * SWA_DH], cos_t, sin_t)
            kr = _rope(proj[:, Hq * SWA_DH:(Hq + Hk) * SWA_DH], cos_t, sin_t)
            qp, kp = _to_heads(qr, Hq), _to_heads(kr, Hk)
            vp = _to_heads(proj[:, (Hq + Hk) * SWA_DH:].astype(BF16), Hk)
            op, lse_rep = _swa_fwd(qp, kp, vp, swa_sinks[j])
            s.update(qp=qp, kp=kp, vp=vp, op=op, lse_rep=lse_rep, Hk=Hk, mix_out=_from_heads(op).astype(BF16))
        y = _mm("mix_out", s['mix_out'], wget(i, 'wout'), 'nn', F32)
        x_mid = _post_fwd(xc, y, mix_post_g[i:i + 1], g_m)
        s.update(y_mix=y, x_mid=x_mid)
        h2 = _pre_fwd(x_mid, ffn_pre_g[i:i + 1], sc_f, sh_f)
        keys = plan[('ffn', i)]
        (g, u, a), got = _ffn_up("ffn_up_g%d" % i if keys else "ffn_up", h2, wget(i, 'wgu'), gather=[slice_of(*k) for k in keys])
        raw.update(zip(keys, got))
        y2 = _mm("ffn_down", a, wget(i, 'wd'), 'nn', F32)
        xc = _post_fwd(x_mid, y2, ffn_post_g[i:i + 1], g_f)
        s.update(h2=h2, g=g, u=u, a=a, y_ffn=y2)
        saved.append(s)

    dx, lsum = _loss(xc, loss_target[0])
    loss = lax.psum(0.5 * lsum[0, 0] / D, AXES)

    small = {n: [None] * W[n].shape[0] for n in SMALL}
    dmod = [None] * L
    pending = []
    stacks = {}

    def update(items, recv):
        for (name, idx, _), parts in zip(items, recv):
            stacks[name] = _adamw("adamw_" + name, parts, W[name], M[name], V[name], idx, stacked=stacks.get(name))

    for i in reversed(range(L)):
        kind, j = mixer[i % 3], i // 3
        s = saved[i]
        sh_m, sc_m, g_m, sh_f, sc_f, g_f = [mod[i, t] for t in range(6)]
        dy2, dg_f, dpost_f = _post_bwd(dx, s['y_ffn'], ffn_post_g[i:i + 1], g_f)
        dwd = _mm("ffn_dwd", s['a'], dy2, 'tn', BF16)
        dg, du = _ffn_dact(dy2, wget(i, 'wd'), s['g'], s['u'])
        dwgu = jnp.concatenate([_mm("ffn_dwg", s['h2'], dg, 'tn', BF16), _mm("ffn_dwu", s['h2'], du, 'tn', BF16)], axis=1)
        pending += [('ffn_w_gu', i, _colsplit(dwgu)), ('ffn_w_down', i, _rowsplit(dwd))]
        take, total = [], 0
        while i < L - 1 and pending and (not take or total + pending[0][2].size <= HOSTED_A2A_PARAMS):
            total += pending[0][2].size
            take.append(pending.pop(0))
        dh_pairs = [(dg, wget(i, 'wgu'), (0, 0), (0, 0)), (du, wget(i, 'wgu'), (0, 0), (0, F))]
        if take:
            dh2, recv = _matmul("ffn_dh_x%d" % i, dh_pairs, 'nt', F32, T, D, F, a2a=[item[2] for item in take])
            update(take, recv)
        else:
            dh2 = _matmul("ffn_dh", dh_pairs, 'nt', F32, T, D, F)
        dx, dsh_f, dsc_f, dpre_f = _pre_bwd(dh2, s['x_mid'], ffn_pre_g[i:i + 1], sc_f, dx)
        dy, dg_m, dpost_m = _post_bwd(dx, s['y_mix'], mix_post_g[i:i + 1], g_m)
        dwout = _mm("mix_dwout", s['mix_out'], dy, 'tn', BF16)
        pending.append((kind + '_w_out', j, _rowsplit(dwout)))
        dmix = _mm("mix_dout", dy, wget(i, 'wout'), 'nt', F32)
        if kind == 'fox':
            dob, delta_rep, cql_rep = _fox_bwd_prep(dmix, s['o'], s['lse_rep'], s['cq_rep'])
            (dq, dk, dv, dck, dcq), recv = _fox_bwd("fox_bwd%d" % j, s['qkv'], dob, delta_rep, cql_rep, s['ck'],
                                                    a2a=[item[2] for item in pending])
            update(pending, recv)
            pending = []
            dcum = jnp.pad(dck.reshape(H, T).T, ((0, 0), (0, LANES - H))) + dcq
            dfg, dbf = _fox_gate_bwd(dcum, s['fg'], s['bf'])
            small['fox_b_f'][j] = dbf[0, :H]
            dproj = jnp.concatenate([dq, dk, dv, dfg.astype(BF16)], axis=1)
            dwin = _mm("fox_dwin", s['h'], dproj, 'tn', BF16)[:, :3 * D + H]
            pending.append((kind + '_w_in', j, _colsplit(dwin)))
            if i == 0:
                dh, recv = _mm("fox_dh_last", dproj, s['win_pad'], 'nt', F32, a2a=[item[2] for item in pending])
                update(pending, recv)
                pending = []
            else:
                dh = _mm("fox_dh", dproj, s['win_pad'], 'nt', F32)
        elif kind == 'sgu':
            wsmT = s['wsm'].transpose(0, 2, 1)
            dz, dws, dbs, dlg, dlb = _sgu_bwd(dmix, s['zpre'], sgu_ln_g[j:j + 1], sgu_ln_b[j:j + 1], s['wsm'], wsmT, s['bs_rep'])
            small['sgu_w_s'][j], small['sgu_b_s'][j] = dws, dbs[:, :, 0]
            small['sgu_ln_g'][j], small['sgu_ln_b'][j] = dlg[0], dlb[0]
            dwin = _mm("sgu_dwin", s['h'], dz, 'tn', BF16)
            pending.append((kind + '_w_in', j, _colsplit(dwin)))
            dh = _mm("sgu_dh", dz, wget(i, 'win'), 'nt', F32)
        else:
            Hk = s['Hk']
            dop = _to_heads(dmix, Hq)
            dqp, dsink = _swa_bwd_dq(s['qp'], s['kp'], s['vp'], dop, s['op'], s['lse_rep'], swa_sinks[j])
            dkp, dvp = _swa_bwd_dkv(s['qp'], s['kp'], s['vp'], dop, s['op'], s['lse_rep'])
            small['swa_sinks'][j] = dsink[:, 0, 0]
            dproj = jnp.concatenate([_rope(_from_heads(dqp), cos_t, -sin_t), _rope(_from_heads(dkp), cos_t, -sin_t),
                                     _from_heads(dvp).astype(BF16)], axis=1)
            dwin = _mm("swa_dwin", s['h'], dproj, 'tn', BF16)
            pending.append((kind + '_w_in', j, _colsplit(dwin)))
            dh = _mm("swa_dh", dproj, wget(i, 'win'), 'nt', F32)
        dx, dsh_m, dsc_m, dpre_m = _pre_bwd(dh, s['x_in'], mix_pre_g[i:i + 1], sc_m, dx)
        small['mix_pre_g'][i], small['mix_post_g'][i] = dpre_m[0], dpost_m[0]
        small['ffn_pre_g'][i], small['ffn_post_g'][i] = dpre_f[0], dpost_f[0]
        dmod[i] = jnp.concatenate([dsh_m, dsc_m, dg_m, dsh_f, dsc_f, dg_f], axis=1)[0]

    if pending:
        update(pending, _comm("a2a_last", [item[2] for item in pending], 'a2a'))
    grad_x = dx[None]

    small['ada_b'] = dmod
    flat = jnp.concatenate([jnp.stack(small[n]).reshape(-1) for n in SMALL])
    width = 8 * LANES
    npad = -flat.shape[0] % (8 * width)
    packed = jnp.pad(flat, (0, npad)).reshape(-1, width)
    parts = _comm("gather_small", [packed], 'gather')[0]

    def pack(d):
        f = jnp.concatenate([d[n].reshape(-1) for n in SMALL])
        return jnp.pad(f, (0, npad)).reshape(1, -1, width)

    res = _adamw("adamw_small", parts, pack(W), pack(M), pack(V), 0)
    off = 0
    for n in SMALL:
        size = W[n].size
        stacks[n] = [val.reshape(-1)[off:off + size].reshape(W[n].shape) for val in res]
        off += size

    dmod_all = parts.reshape(NDEV, -1)[:, :L * 6 * D].reshape(NDEV, L, 6 * D)
    dm = lax.dynamic_slice_in_dim(dmod_all, me * n_ada, n_ada, axis=2).transpose(1, 0, 2)
    c_rep = jnp.broadcast_to(c_all[:, :, None], (NDEV, D, LANES))
    stacks['ada_w'] = _ada_update(c_rep, dm, ada_w, m_ada_w, v_ada_w)

    return (loss, grad_x, *[stacks[n][t] for t in range(4) for n in WEIGHTS])
```

```python
import numpy as np
import jax
import jax.numpy as jnp
from jax import lax
from jax.experimental import pallas as pl
from jax.experimental.pallas import tpu as pltpu

F32 = jnp.float32
BF16 = jnp.bfloat16
NDEV = 8
AXES = ("x", "y", "c")
LANES = 128
VMEM_LIMIT_BYTES = 48 * 1024 * 1024
NEG_INF = float("-inf")
ROW_TILE = 256

EPS = 1e-6
BLOCK = 128
FOX_DH = 128
FOX_HPS = 2
FOX_FWD_BLOCK = 1024
FOX_BWD_QUERIES = 1024
FOX_BWD_KEYS = 512
HOSTED_GATHER_PARAMS = 36 * 2 ** 20
HOSTED_A2A_PARAMS = 22 * 2 ** 20
SWA_DH = 64
ROPE_DIM = 16
ROPE_THETA = 500000.0
FOX_SCALE = FOX_DH ** -0.5
SWA_SCALE = SWA_DH ** -0.5
GELU_C0 = 0.7978845608028654
GELU_C1 = 0.044715

ADAM_LR = 0.001
ADAM_B1 = 0.9
ADAM_B2 = 0.999
ADAM_EPS = 1e-08
ADAM_WD = 0.01
ADAM_STEP = 10

NN = (((1,), (0,)), ((), ()))
NT = (((1,), (1,)), ((), ()))
TN = (((0,), (0,)), ((), ()))

WEIGHTS = ['ada_w', 'ada_b', 'mix_pre_g', 'mix_post_g', 'ffn_pre_g', 'ffn_post_g', 'ffn_w_gu', 'ffn_w_down',
           'fox_w_in', 'fox_b_f', 'fox_w_out', 'sgu_w_in', 'sgu_ln_g', 'sgu_ln_b', 'sgu_w_s', 'sgu_b_s',
           'sgu_w_out', 'swa_w_in', 'swa_sinks', 'swa_w_out']
SMALL = ['ada_b', 'mix_pre_g', 'mix_post_g', 'ffn_pre_g', 'ffn_post_g', 'fox_b_f', 'sgu_ln_g', 'sgu_ln_b',
         'sgu_w_s', 'sgu_b_s', 'swa_sinks']


def _dot(a, b, dims):
    return lax.dot_general(a, b, dims, preferred_element_type=F32)


def _tile(n, pref, mult=LANES):
    t = (min(pref, n) // mult) * mult
    while t >= mult:
        if n % t == 0:
            return t
        t -= mult
    return n


def _params():
    return pltpu.CompilerParams(vmem_limit_bytes=VMEM_LIMIT_BYTES)


def _rep(a, n):
    return a if n == 1 else jnp.concatenate([a] * n, axis=-1)


def _vec_spec(d):
    return pl.BlockSpec((1, d), lambda *_: (0, 0))


def _sigmoid(z):
    return 1.0 / (1.0 + jnp.exp(-z))


def _gelu(z):
    t = jnp.tanh(GELU_C0 * (z + GELU_C1 * z * z * z))
    return 0.5 * z * (1.0 + t)


def _gelu_grad(z):
    t = jnp.tanh(GELU_C0 * (z + GELU_C1 * z * z * z))
    return 0.5 * (1.0 + t) + 0.5 * z * (1.0 - t * t) * GELU_C0 * (1.0 + 3.0 * GELU_C1 * z * z)


def _comm_out_shapes(arrs, gather):
    return [jax.ShapeDtypeStruct(((NDEV,) + a.shape) if gather else a.shape, a.dtype) for a in arrs]


def _comm_sems(n):
    return [pltpu.SemaphoreType.DMA((n,)), pltpu.SemaphoreType.DMA((n,)), pltpu.SemaphoreType.DMA((n,))]


def _me():
    x, y, c = lax.axis_index("x"), lax.axis_index("y"), lax.axis_index("c")
    return x, y, c, 4 * x + 2 * y + c


def _comm_start(ins, outs, gather, send_sems, recv_sems, local_sems):
    x, y, c, me = _me()
    for a in range(len(ins)):
        pltpu.make_async_copy(ins[a] if gather else ins[a].at[me], outs[a].at[me], local_sems.at[a]).start()
        for bits in range(1, NDEV):
            px = (1 - x) if bits & 4 else x
            py = (1 - y) if bits & 2 else y
            pc = (1 - c) if bits & 1 else c
            pltpu.make_async_remote_copy(
                src_ref=ins[a] if gather else ins[a].at[4 * px + 2 * py + pc], dst_ref=outs[a].at[me],
                send_sem=send_sems.at[a], recv_sem=recv_sems.at[a],
                device_id=(px, py, pc), device_id_type=pl.DeviceIdType.MESH).start()


def _comm_wait(ins, outs, gather, send_sems, recv_sems, local_sems):
    x, y, c, me = _me()
    for a in range(len(ins)):
        seven = outs[a].at[pl.ds(0, NDEV - 1)]
        pltpu.make_async_remote_copy(src_ref=seven, dst_ref=seven, send_sem=send_sems.at[a], recv_sem=recv_sems.at[a],
                                     device_id=(x, y, c), device_id_type=pl.DeviceIdType.MESH).wait()
        pltpu.make_async_copy(ins[a] if gather else ins[a].at[me], outs[a].at[me], local_sems.at[a]).wait()


def _comm(name, arrs, kind):
    n = len(arrs)
    gather = kind == 'gather'

    def body(*refs):
        ins, outs, sems = refs[:n], refs[n:2 * n], refs[2 * n:]
        _comm_start(ins, outs, gather, *sems)
        _comm_wait(ins, outs, gather, *sems)

    any_spec = pl.BlockSpec(memory_space=pl.ANY)
    return pl.pallas_call(
        body, name=name, out_shape=_comm_out_shapes(arrs, gather),
        in_specs=[any_spec] * n, out_specs=[any_spec] * n, scratch_shapes=_comm_sems(n),
    )(*arrs)


def _gather2_sems(n):
    return [pltpu.SemaphoreType.DMA((n,)) for _ in range(4)]


def _remote(src, dst, send_sem, recv_sem, device):
    return pltpu.make_async_remote_copy(src_ref=src, dst_ref=dst, send_sem=send_sem, recv_sem=recv_sem,
                                        device_id=device, device_id_type=pl.DeviceIdType.MESH)


def _gather2_start(ins, outs, send_sems, ici_sems, d2d_sems, local_sems):
    x, y, c, me = _me()
    for a in range(len(ins)):
        pltpu.make_async_copy(ins[a], outs[a].at[me], local_sems.at[a]).start()
        _remote(ins[a], outs[a].at[me], send_sems.at[a], d2d_sems.at[a], (x, y, 1 - c)).start()
        for px, py in ((1 - x, y), (x, 1 - y), (1 - x, 1 - y)):
            _remote(ins[a], outs[a].at[me], send_sems.at[a], ici_sems.at[a], (px, py, c)).start()


def _gather2_pass_on(ins, outs, send_sems, ici_sems, d2d_sems, local_sems):
    x, y, c, me = _me()
    for a in range(len(ins)):
        three = outs[a].at[pl.ds(0, 3)]
        _remote(three, three, send_sems.at[a], ici_sems.at[a], (x, y, c)).wait_recv()
        for px, py in ((1 - x, y), (x, 1 - y), (1 - x, 1 - y)):
            slot = outs[a].at[4 * px + 2 * py + c]
            _remote(slot, slot, send_sems.at[a], d2d_sems.at[a], (x, y, 1 - c)).start()


def _gather2_finish(ins, outs, send_sems, ici_sems, d2d_sems, local_sems):
    x, y, c, me = _me()
    for a in range(len(ins)):
        four, seven = outs[a].at[pl.ds(0, 4)], outs[a].at[pl.ds(0, NDEV - 1)]
        _remote(four, four, send_sems.at[a], d2d_sems.at[a], (x, y, c)).wait_recv()
        _remote(seven, seven, send_sems.at[a], d2d_sems.at[a], (x, y, c)).wait_send()
        pltpu.make_async_copy(ins[a], outs[a].at[me], local_sems.at[a]).wait()


def _gather2(name, arrs):
    n = len(arrs)

    def body(*refs):
        ins, outs, sems = refs[:n], refs[n:2 * n], refs[2 * n:]
        _gather2_start(ins, outs, *sems)
        _gather2_pass_on(ins, outs, *sems)
        _gather2_finish(ins, outs, *sems)

    any_spec = pl.BlockSpec(memory_space=pl.ANY)
    return pl.pallas_call(
        body, name=name, out_shape=_comm_out_shapes(arrs, True),
        in_specs=[any_spec] * n, out_specs=[any_spec] * n, scratch_shapes=_gather2_sems(n),
    )(*arrs)


MATMUL_VMEM_BUDGET = 38 * 1024 * 1024


def _matmul_tiles(mode, M, N, K, npairs, out_size):
    def uniq(vals):
        return sorted(set(vals), reverse=True)

    tms = uniq(_tile(M, p) for p in (1024, 512, 256))
    tns = uniq(_tile(N, p) for p in (1536, 1024, 512, 256))
    tks = uniq(_tile(K, p) for p in (2048, 1024)) if mode == 'tn' else [K] + uniq(_tile(K, p) for p in (2048, 1024))
    for tk in tks:
        best = None
        for tm in tms:
            for tn in tns:
                steps = K // tk
                need = (2 * npairs * (tm + tn) * tk * 2 + 2 * tm * tn * out_size + npairs * tm * tn * 4
                        + (tm * tn * 4 if steps > 1 else 0) + (tk * tm * 2 if mode == 'tn' else 0))
                if need <= MATMUL_VMEM_BUDGET and (best is None or (tm * tn, tm) > (best[0] * best[1], best[0])):
                    best = (tm, tn, tk)
        if best is not None:
            return best
    return _tile(M, 256), _tile(N, 256), _tile(K, 512)


def _matmul(name, pairs, mode, out_dtype, M, N, K, tm=None, tn=None, tk=None, a2a=()):
    if not (tm and tn and tk):
        tm, tn, tk = _matmul_tiles(mode, M, N, K, len(pairs), jnp.dtype(out_dtype).itemsize)
    nk = K // tk
    dims = {'nn': NN, 'nt': NT, 'tn': TN}[mode]
    in_specs, ops = [], []
    for a, b, ao, bo in pairs:
        if mode == 'tn':
            assert ao[0] % tk == 0 and ao[1] % tm == 0
            sa = pl.BlockSpec((tk, tm), lambda i, j, k, r=ao[0] // tk, c=ao[1] // tm: (k + r, i + c))
        else:
            assert ao[0] % tm == 0 and ao[1] % tk == 0
            sa = pl.BlockSpec((tm, tk), lambda i, j, k, r=ao[0] // tm, c=ao[1] // tk: (i + r, k + c))
        if mode == 'nt':
            assert bo[0] % tn == 0 and bo[1] % tk == 0
            sb = pl.BlockSpec((tn, tk), lambda i, j, k, r=bo[0] // tn, c=bo[1] // tk: (j + r, k + c))
        else:
            assert bo[0] % tk == 0 and bo[1] % tn == 0
            sb = pl.BlockSpec((tk, tn), lambda i, j, k, r=bo[0] // tk, c=bo[1] // tn: (k + r, j + c))
        in_specs += [sa, sb]
        ops += [a, b]
    npairs = len(pairs)
    nc = len(a2a)
    grid = (M // tm, N // tn, nk)

    def body(*refs):
        cin, o_ref, cout = refs[2 * npairs:2 * npairs + nc], refs[2 * npairs + nc], refs[2 * npairs + nc + 1:2 * npairs + 2 * nc + 1]
        scratch = refs[2 * npairs + 2 * nc + 1:]
        sems = scratch[1:] if nk > 1 else scratch
        i, j, k = pl.program_id(0), pl.program_id(1), pl.program_id(2)

        if nc:
            @pl.when(jnp.logical_and(jnp.logical_and(i == 0, j == 0), k == 0))
            def _():
                _comm_start(cin, cout, False, *sems)

        part = _dot(refs[0][...], refs[1][...], dims)
        for p in range(1, npairs):
            part = part + _dot(refs[2 * p][...], refs[2 * p + 1][...], dims)
        if nk == 1:
            o_ref[...] = part.astype(out_dtype)
        else:
            acc = scratch[0]

            @pl.when(k == 0)
            def _():
                acc[...] = part

            @pl.when(k > 0)
            def _():
                acc[...] += part

            @pl.when(k == nk - 1)
            def _():
                o_ref[...] = acc[...].astype(out_dtype)

        if nc:
            @pl.when(jnp.logical_and(jnp.logical_and(i == grid[0] - 1, j == grid[1] - 1), k == nk - 1))
            def _():
                _comm_wait(cin, cout, False, *sems)

    any_spec = pl.BlockSpec(memory_space=pl.ANY)
    res = pl.pallas_call(
        body, name=name, grid=grid,
        in_specs=in_specs + [any_spec] * nc,
        out_specs=[pl.BlockSpec((tm, tn), lambda i, j, k: (i, j))] + [any_spec] * nc,
        out_shape=[jax.ShapeDtypeStruct((M, N), out_dtype)] + _comm_out_shapes(a2a, False),
        scratch_shapes=([] if nk == 1 else [pltpu.VMEM((tm, tn), F32)]) + (_comm_sems(nc) if nc else []),
        compiler_params=_params(),
    )(*ops, *a2a)
    return (res[0], res[1:]) if nc else res[0]


def _mm(name, a, b, mode, out_dtype, a2a=()):
    if mode == 'nn':
        (M, K), N = a.shape, b.shape[1]
    elif mode == 'nt':
        (M, K), N = a.shape, b.shape[0]
    else:
        (K, M), N = a.shape, b.shape[1]
    return _matmul(name, [(a, b, (0, 0), (0, 0))], mode, out_dtype, M, N, K, a2a=a2a)


def _ffn_up(name, h, wgu, gather=()):
    T, D = h.shape
    F = wgu.shape[1] // 2
    tm, tn = _tile(T, 1024), _tile(F, 512)
    grid = (T // tm, F // tn)
    nc = len(gather)

    def body(h_ref, wg_ref, wu_ref, *rest):
        cin, (g_ref, u_ref, a_ref), cout, sems = rest[:nc], rest[nc:nc + 3], rest[nc + 3:2 * nc + 3], rest[2 * nc + 3:]
        i, j = pl.program_id(0), pl.program_id(1)

        if nc:
            @pl.when(jnp.logical_and(i == 0, j == 0))
            def _():
                _gather2_start(cin, cout, *sems)

        hv = h_ref[...]
        g = _dot(hv, wg_ref[...], NN)
        u = _dot(hv, wu_ref[...], NN)
        g_ref[...] = g
        u_ref[...] = u
        a_ref[...] = (g * _sigmoid(g) * u).astype(BF16)

        if nc:
            @pl.when(jnp.logical_and(i == grid[0] - 1, j == grid[1] - 1))
            def _():
                _gather2_pass_on(cin, cout, *sems)
                _gather2_finish(cin, cout, *sems)

    any_spec = pl.BlockSpec(memory_space=pl.ANY)
    out = pl.BlockSpec((tm, tn), lambda i, j: (i, j))
    res = pl.pallas_call(
        body, name=name, grid=grid,
        in_specs=[pl.BlockSpec((tm, D), lambda i, j: (i, 0)),
                  pl.BlockSpec((D, tn), lambda i, j: (0, j)),
                  pl.BlockSpec((D, tn), lambda i, j, o=F // tn: (0, j + o))] + [any_spec] * nc,
        out_specs=[out, out, out] + [any_spec] * nc,
        out_shape=[jax.ShapeDtypeStruct((T, F), F32), jax.ShapeDtypeStruct((T, F), F32),
                   jax.ShapeDtypeStruct((T, F), BF16)] + _comm_out_shapes(gather, True),
        scratch_shapes=_gather2_sems(nc) if nc else [],
        compiler_params=_params(),
    )(h, wgu, wgu, *gather)
    return res[:3], res[3:]


def _ffn_dact(dy, wd, g, u):
    T, D = dy.shape
    F = wd.shape[0]
    tm, tn = _tile(T, 1024), _tile(F, 512)

    nsplit = 2 if tn % (2 * LANES) == 0 else 1

    def body(dy_ref, wd_ref, g_ref, u_ref, dg_ref, du_ref):
        dyv = dy_ref[...]
        for c in range(nsplit):
            cols = slice(c * (tn // nsplit), (c + 1) * (tn // nsplit))
            da = _dot(dyv, wd_ref[cols, :], NT)
            g = g_ref[:, cols]
            sg = _sigmoid(g)
            dg_ref[:, cols] = (da * u_ref[:, cols] * (sg * (1.0 + g * (1.0 - sg)))).astype(BF16)
            du_ref[:, cols] = (da * (g * sg)).astype(BF16)

    blk = pl.BlockSpec((tm, tn), lambda i, j: (i, j))
    return pl.pallas_call(
        body, name="ffn_dact", grid=(T // tm, F // tn),
        in_specs=[pl.BlockSpec((tm, D), lambda i, j: (i, 0)), pl.BlockSpec((tn, D), lambda i, j: (j, 0)), blk, blk],
        out_specs=[blk, blk],
        out_shape=[jax.ShapeDtypeStruct((T, F), BF16)] * 2,
        compiler_params=_params(),
    )(dy, wd, g, u)


def _rstd(v):
    return lax.rsqrt(jnp.mean(v * v, axis=-1, keepdims=True) + EPS)


def _pre_fwd(x, g, sc, sh):
    T, D = x.shape
    tr = _tile(T, ROW_TILE, 8)

    def body(x_ref, g_ref, sc_ref, sh_ref, h_ref):
        xv = x_ref[...]
        r = xv * _rstd(xv) * g_ref[...]
        h_ref[...] = (r * (1.0 + sc_ref[...]) + sh_ref[...]).astype(BF16)

    row = pl.BlockSpec((tr, D), lambda i: (i, 0))
    return pl.pallas_call(
        body, name="pre_fwd", grid=(T // tr,),
        in_specs=[row, _vec_spec(D), _vec_spec(D), _vec_spec(D)], out_specs=row,
        out_shape=jax.ShapeDtypeStruct((T, D), BF16), compiler_params=_params(),
    )(x, g, sc, sh)


def _post_fwd(x, y, g, gate):
    T, D = x.shape
    tr = _tile(T, ROW_TILE, 8)

    def body(x_ref, y_ref, g_ref, gate_ref, o_ref):
        yv = y_ref[...]
        o_ref[...] = x_ref[...] + gate_ref[...] * (yv * _rstd(yv) * g_ref[...])

    row = pl.BlockSpec((tr, D), lambda i: (i, 0))
    return pl.pallas_call(
        body, name="post_fwd", grid=(T // tr,),
        in_specs=[row, row, _vec_spec(D), _vec_spec(D)], out_specs=row,
        out_shape=jax.ShapeDtypeStruct((T, D), F32), compiler_params=_params(),
    )(x, y, g, gate)


def _post_bwd(dx, y, g, gate):
    T, D = dx.shape
    tr = _tile(T, ROW_TILE, 8)

    def body(dx_ref, y_ref, g_ref, gate_ref, dy_ref, dgate_ref, dg_ref):
        @pl.when(pl.program_id(0) == 0)
        def _():
            dgate_ref[...] = jnp.zeros_like(dgate_ref)
            dg_ref[...] = jnp.zeros_like(dg_ref)

        yv, dxv = y_ref[...], dx_ref[...]
        rstd = _rstd(yv)
        yh = yv * rstd
        dgate_ref[...] += jnp.sum(dxv * (yh * g_ref[...]), axis=0, keepdims=True)
        dn = dxv * gate_ref[...]
        dg_ref[...] += jnp.sum(dn * yh, axis=0, keepdims=True)
        dyh = dn * g_ref[...]
        dy_ref[...] = (rstd * (dyh - yh * jnp.mean(dyh * yh, axis=-1, keepdims=True))).astype(BF16)

    row = pl.BlockSpec((tr, D), lambda i: (i, 0))
    vec = jax.ShapeDtypeStruct((1, D), F32)
    return pl.pallas_call(
        body, name="post_bwd", grid=(T // tr,),
        in_specs=[row, row, _vec_spec(D), _vec_spec(D)], out_specs=[row, _vec_spec(D), _vec_spec(D)],
        out_shape=[jax.ShapeDtypeStruct((T, D), BF16), vec, vec], compiler_params=_params(),
    )(dx, y, g, gate)


def _pre_bwd(dh, x, g, sc, dx_res):
    T, D = x.shape
    tr = _tile(T, ROW_TILE, 8)

    def body(dh_ref, x_ref, g_ref, sc_ref, dxr_ref, dx_ref, dsh_ref, dsc_ref, dg_ref):
        @pl.when(pl.program_id(0) == 0)
        def _():
            dsh_ref[...] = jnp.zeros_like(dsh_ref)
            dsc_ref[...] = jnp.zeros_like(dsc_ref)
            dg_ref[...] = jnp.zeros_like(dg_ref)

        xv, dhv = x_ref[...], dh_ref[...]
        rstd = _rstd(xv)
        xh = xv * rstd
        dsh_ref[...] += jnp.sum(dhv, axis=0, keepdims=True)
        dsc_ref[...] += jnp.sum(dhv * (xh * g_ref[...]), axis=0, keepdims=True)
        dr = dhv * (1.0 + sc_ref[...])
        dg_ref[...] += jnp.sum(dr * xh, axis=0, keepdims=True)
        dxh = dr * g_ref[...]
        dx_ref[...] = dxr_ref[...] + rstd * (dxh - xh * jnp.mean(dxh * xh, axis=-1, keepdims=True))

    row = pl.BlockSpec((tr, D), lambda i: (i, 0))
    vec = jax.ShapeDtypeStruct((1, D), F32)
    return pl.pallas_call(
        body, name="pre_bwd", grid=(T // tr,),
        in_specs=[row, row, _vec_spec(D), _vec_spec(D), row],
        out_specs=[row, _vec_spec(D), _vec_spec(D), _vec_spec(D)],
        out_shape=[jax.ShapeDtypeStruct((T, D), F32), vec, vec, vec], compiler_params=_params(),
    )(dh, x, g, sc, dx_res)


def _loss(x, target):
    T, D = x.shape
    tr = _tile(T, ROW_TILE, 8)

    def body(x_ref, t_ref, dx_ref, l_ref):
        @pl.when(pl.program_id(0) == 0)
        def _():
            l_ref[...] = jnp.zeros_like(l_ref)

        e = x_ref[...] - t_ref[...]
        dx_ref[...] = e / D
        rows = jnp.sum(e * e, axis=-1, keepdims=True)
        l_ref[...] += jnp.broadcast_to(jnp.sum(rows, axis=0, keepdims=True), (1, LANES))

    row = pl.BlockSpec((tr, D), lambda i: (i, 0))
    return pl.pallas_call(
        body, name="loss", grid=(T // tr,),
        in_specs=[row, row], out_specs=[row, _vec_spec(LANES)],
        out_shape=[jax.ShapeDtypeStruct((T, D), F32), jax.ShapeDtypeStruct((1, LANES), F32)],
        compiler_params=_params(),
    )(x, target)


def _split3(x):
    hi = x.astype(BF16)
    r = x - hi.astype(F32)
    mid = r.astype(BF16)
    lo = (r - mid.astype(F32)).astype(BF16)
    return hi, mid, lo


def _tri_sum(tri, x):
    hi, mid, lo = _split3(x)
    return _dot(tri, hi, NN) + _dot(tri, mid, NN) + _dot(tri, lo, NN)


def _fox_gate_fwd(fg, bf, H):
    T = fg.shape[0]
    tb = _tile(T, 512)

    def body(fg_ref, bf_ref, cum_ref, rep_ref, carry):
        @pl.when(pl.program_id(0) == 0)
        def _():
            carry[...] = jnp.zeros_like(carry)

        z = fg_ref[...] + bf_ref[...]
        logf = jnp.minimum(z, 0.0) - jnp.log(1.0 + jnp.exp(-jnp.abs(z)))
        row = lax.broadcasted_iota(jnp.int32, (tb, tb), 0)
        col = lax.broadcasted_iota(jnp.int32, (tb, tb), 1)
        cum = _tri_sum((row >= col).astype(BF16), logf) + carry[...]
        cum_ref[...] = cum
        carry[...] = cum_ref[pl.ds(tb - 1, 1), :]
        lane = lax.broadcasted_iota(jnp.int32, (tb, LANES), 1)
        for h in range(H):
            colv = jnp.sum(jnp.where(lane == h, cum, 0.0), axis=-1, keepdims=True)
            rep_ref[h] = jnp.broadcast_to(colv, (tb, LANES))

    return pl.pallas_call(
        body, name="fox_gate_fwd", grid=(T // tb,),
        in_specs=[pl.BlockSpec((tb, LANES), lambda i: (i, 0)), _vec_spec(LANES)],
        out_specs=[pl.BlockSpec((tb, LANES), lambda i: (i, 0)), pl.BlockSpec((H, tb, LANES), lambda i: (0, i, 0))],
        out_shape=[jax.ShapeDtypeStruct((T, LANES), F32), jax.ShapeDtypeStruct((H, T, LANES), F32)],
        scratch_shapes=[pltpu.VMEM((1, LANES), F32)], compiler_params=_params(),
    )(fg, bf)


def _fox_gate_bwd(dcum, fg, bf):
    T = fg.shape[0]
    tb = _tile(T, 512)
    nb = T // tb

    def body(dc_ref, fg_ref, bf_ref, dfg_ref, dbf_ref, carry):
        @pl.when(pl.program_id(0) == 0)
        def _():
            carry[...] = jnp.zeros_like(carry)
            dbf_ref[...] = jnp.zeros_like(dbf_ref)

        row = lax.broadcasted_iota(jnp.int32, (tb, tb), 0)
        col = lax.broadcasted_iota(jnp.int32, (tb, tb), 1)
        dc = dc_ref[...]
        dlogf = _tri_sum((row <= col).astype(BF16), dc) + carry[...]
        z = fg_ref[...] + bf_ref[...]
        dfg = dlogf * _sigmoid(-z)
        dfg_ref[...] = dfg
        dbf_ref[...] += jnp.sum(dfg, axis=0, keepdims=True)
        carry[...] += jnp.sum(dc, axis=0, keepdims=True)

    rev = pl.BlockSpec((tb, LANES), lambda i: (nb - 1 - i, 0))
    return pl.pallas_call(
        body, name="fox_gate_bwd", grid=(nb,),
        in_specs=[rev, rev, _vec_spec(LANES)], out_specs=[rev, _vec_spec(LANES)],
        out_shape=[jax.ShapeDtypeStruct((T, LANES), F32), jax.ShapeDtypeStruct((1, LANES), F32)],
        scratch_shapes=[pltpu.VMEM((1, LANES), F32)], compiler_params=_params(),
    )(dcum, fg, bf)


def _fox_blocks(T, pref=512):
    tb = pref if T >= 2 * pref else BLOCK
    return tb, T // tb


def _fox_fwd(name, qkv, cq_rep, ck, gather=()):
    T = qkv.shape[0]
    D = qkv.shape[1] // 3
    H = D // FOX_DH
    hps = FOX_HPS
    ng, wl = H // hps, hps * FOX_DH
    tb, nb = _fox_blocks(T, FOX_FWD_BLOCK)
    pairs = [(i, j) for i in range(nb) for j in range(i + 1)]
    qi = np.array([p[0] for p in pairs], np.int32)
    kj = np.array([p[1] for p in pairs], np.int32)
    npairs = len(pairs)
    nrep = tb // LANES
    nc = len(gather)

    def body(qi_ref, kj_ref, q_ref, k_ref, v_ref, cq_ref, ck_ref, *rest):
        cin, (o_ref, obf_ref, lse_ref), cout = rest[:nc], rest[nc:nc + 3], rest[nc + 3:2 * nc + 3]
        m_sc, l_sc, acc_sc = rest[2 * nc + 3:2 * nc + 6]
        sems = rest[2 * nc + 6:]
        g, p = pl.program_id(0), pl.program_id(1)
        i, j = qi_ref[p], kj_ref[p]

        if nc:
            @pl.when(jnp.logical_and(g == 0, p == 0))
            def _():
                _gather2_start(cin, cout, *sems)

        @pl.when(j == 0)
        def _():
            m_sc[...] = jnp.full_like(m_sc, NEG_INF)
            l_sc[...] = jnp.zeros_like(l_sc)
            acc_sc[...] = jnp.zeros_like(acc_sc)

        row = lax.broadcasted_iota(jnp.int32, (tb, tb), 0)
        col = lax.broadcasted_iota(jnp.int32, (tb, tb), 1)
        visible = jnp.logical_or(j < i, row >= col)
        for hh in range(hps):
            cols = slice(hh * FOX_DH, (hh + 1) * FOX_DH)
            s = _dot(q_ref[:, cols], k_ref[:, cols], NT) * FOX_SCALE
            s = jnp.where(visible, s + _rep(cq_ref[hh], nrep) - ck_ref[hh], NEG_INF)
            m_prev = m_sc[hh]
            m_new = jnp.maximum(m_prev, jnp.max(s, axis=-1, keepdims=True))
            alpha = jnp.exp(m_prev - m_new)
            pm = jnp.exp(s - _rep(m_new, nrep))
            l_sc[hh] = alpha * l_sc[hh] + jnp.sum(pm, axis=-1, keepdims=True)
            acc_sc[:, cols] = alpha * acc_sc[:, cols] + _dot(pm.astype(BF16), v_ref[:, cols], NN)
            m_sc[hh] = m_new

        @pl.when(j == i)
        def _():
            for hh in range(hps):
                cols = slice(hh * FOX_DH, (hh + 1) * FOX_DH)
                o = acc_sc[:, cols] / l_sc[hh]
                o_ref[:, cols] = o
                obf_ref[:, cols] = o.astype(BF16)
                lse_ref[hh] = m_sc[hh] + jnp.log(l_sc[hh])

        if nc:
            @pl.when(jnp.logical_and(g == ng - 1, p == npairs - 1))
            def _():
                _gather2_pass_on(cin, cout, *sems)
                _gather2_finish(cin, cout, *sems)

    any_spec = pl.BlockSpec(memory_space=pl.ANY)
    qblk = pl.BlockSpec((tb, wl), lambda g, p, qi, kj: (qi[p], g))
    qrep = pl.BlockSpec((hps, tb, LANES), lambda g, p, qi, kj: (g, qi[p], 0))
    grid_spec = pltpu.PrefetchScalarGridSpec(
        num_scalar_prefetch=2, grid=(ng, npairs),
        in_specs=[qblk,
                  pl.BlockSpec((tb, wl), lambda g, p, qi, kj: (kj[p], ng + g)),
                  pl.BlockSpec((tb, wl), lambda g, p, qi, kj: (kj[p], 2 * ng + g)),
                  qrep,
                  pl.BlockSpec((hps, 1, tb), lambda g, p, qi, kj: (g, 0, kj[p]))] + [any_spec] * nc,
        out_specs=[qblk, qblk, qrep] + [any_spec] * nc,
        scratch_shapes=[pltpu.VMEM((hps, tb, LANES), F32), pltpu.VMEM((hps, tb, LANES), F32),
                        pltpu.VMEM((tb, wl), F32)] + (_gather2_sems(nc) if nc else []))
    res = pl.pallas_call(
        body, name=name, grid_spec=grid_spec,
        out_shape=[jax.ShapeDtypeStruct((T, D), F32), jax.ShapeDtypeStruct((T, D), BF16),
                   jax.ShapeDtypeStruct((H, T, LANES), F32)] + _comm_out_shapes(gather, True),
        compiler_params=_params(),
    )(qi, kj, qkv, qkv, qkv, cq_rep, ck, *gather)
    return res[:3], res[3:]


def _fox_bwd_prep(do, o, lse_rep, cq_rep):
    T, D = do.shape
    H = D // FOX_DH
    tr = _tile(T, ROW_TILE, 8)

    def body(do_ref, o_ref, lse_ref, cq_ref, dob_ref, delta_ref, cql_ref):
        dov = do_ref[...]
        dob_ref[...] = dov.astype(BF16)
        prod = dov * o_ref[...]
        for h in range(H):
            d = jnp.sum(prod[:, h * FOX_DH:(h + 1) * FOX_DH], axis=-1, keepdims=True)
            delta_ref[h] = jnp.broadcast_to(d, (tr, LANES))
        cql_ref[...] = cq_ref[...] - lse_ref[...]

    row = pl.BlockSpec((tr, D), lambda i: (i, 0))
    rep = pl.BlockSpec((H, tr, LANES), lambda i: (0, i, 0))
    return pl.pallas_call(
        body, name="fox_bwd_prep", grid=(T // tr,),
        in_specs=[row, row, rep, rep], out_specs=[row, rep, rep],
        out_shape=[jax.ShapeDtypeStruct((T, D), BF16), jax.ShapeDtypeStruct((H, T, LANES), F32),
                   jax.ShapeDtypeStruct((H, T, LANES), F32)],
        compiler_params=_params(),
    )(do, o, lse_rep, cq_rep)


def _fox_bwd(name, qkv, dob, delta_rep, cql_rep, ck, a2a=()):
    T = qkv.shape[0]
    D = qkv.shape[1] // 3
    H = D // FOX_DH
    hps = FOX_HPS
    ng, wl = H // hps, hps * FOX_DH
    tk, nbk = _fox_blocks(T, FOX_BWD_KEYS)
    tq = max(_fox_blocks(T, FOX_BWD_QUERIES)[0], tk)
    nbq, ratio = T // tq, tq // tk
    pairs = [(i, j) for j in range(nbk) for i in range(j // ratio, nbq)]
    qi = np.array([p[0] for p in pairs], np.int32)
    kj = np.array([p[1] for p in pairs], np.int32)
    npairs = len(pairs)
    nrep = tk // LANES
    nc = len(a2a)

    def body(qi_ref, kj_ref, q_ref, k_ref, v_ref, do_ref, delta_ref, cql_ref, ck_ref, *rest):
        cin, (dq_ref, dk_ref, dv_ref, dck_ref, dcq_ref), cout = rest[:nc], rest[nc:nc + 5], rest[nc + 5:2 * nc + 5]
        dq_acc, dk_acc, dv_acc, dc_acc = rest[2 * nc + 5:2 * nc + 9]
        sems = rest[2 * nc + 9:]
        g, p = pl.program_id(0), pl.program_id(1)
        i, j = qi_ref[p], kj_ref[p]

        @pl.when(jnp.logical_and(g == 0, p == 0))
        def _():
            dcq_ref[...] = jnp.zeros_like(dcq_ref)
            if nc:
                _comm_start(cin, cout, False, *sems)

        @pl.when(p == 0)
        def _():
            dq_acc[...] = jnp.zeros_like(dq_acc)

        @pl.when(i == j // ratio)
        def _():
            dk_acc[...] = jnp.zeros_like(dk_acc)
            dv_acc[...] = jnp.zeros_like(dv_acc)
            dc_acc[...] = jnp.zeros_like(dc_acc)

        row = lax.broadcasted_iota(jnp.int32, (tq, tk), 0)
        col = lax.broadcasted_iota(jnp.int32, (tq, tk), 1)
        visible = jnp.logical_or((j + 1) * tk <= i * tq, row + i * tq >= col + j * tk)
        lane = lax.broadcasted_iota(jnp.int32, (tq, LANES), 1)
        rows = pl.ds(pl.multiple_of(i * tq, tq), tq)
        dcq = jnp.zeros((tq, LANES), F32)
        for hh in range(hps):
            cols = slice(hh * FOX_DH, (hh + 1) * FOX_DH)
            q, k, v, dov = q_ref[:, cols], k_ref[:, cols], v_ref[:, cols], do_ref[:, cols]
            s = _dot(q, k, NT) * FOX_SCALE + _rep(cql_ref[hh], nrep) - ck_ref[hh]
            pm = jnp.exp(jnp.where(visible, s, NEG_INF))
            dv_acc[:, cols] += _dot(pm.astype(BF16), dov, TN)
            ds = pm * (_dot(dov, v, NT) - _rep(delta_ref[hh], nrep))
            dsb = (ds * FOX_SCALE).astype(BF16)
            dk_acc[:, cols] += _dot(dsb, q, TN)
            dq_acc[rows, cols] += _dot(dsb, k, NN)
            dc_acc[hh] -= jnp.sum(ds, axis=0, keepdims=True)
            dcq = dcq + jnp.where(lane == g * hps + hh, jnp.sum(ds, axis=-1, keepdims=True), 0.0)
        dcq_ref[rows, :] += dcq

        @pl.when(i == nbq - 1)
        def _():
            dk_ref[...] = dk_acc[...].astype(BF16)
            dv_ref[...] = dv_acc[...].astype(BF16)
            dck_ref[...] = dc_acc[...]

        @pl.when(p == npairs - 1)
        def _():
            dq_ref[...] = dq_acc[...].astype(BF16)

        if nc:
            @pl.when(jnp.logical_and(g == ng - 1, p == npairs - 1))
            def _():
                _comm_wait(cin, cout, False, *sems)

    any_spec = pl.BlockSpec(memory_space=pl.ANY)
    qblk = pl.BlockSpec((tq, wl), lambda g, p, qi, kj: (qi[p], g))
    qrep = pl.BlockSpec((hps, tq, LANES), lambda g, p, qi, kj: (g, qi[p], 0))
    kblk = pl.BlockSpec((tk, wl), lambda g, p, qi, kj: (kj[p], g))
    krow = pl.BlockSpec((hps, 1, tk), lambda g, p, qi, kj: (g, 0, kj[p]))
    grid_spec = pltpu.PrefetchScalarGridSpec(
        num_scalar_prefetch=2, grid=(ng, npairs),
        in_specs=[qblk,
                  pl.BlockSpec((tk, wl), lambda g, p, qi, kj: (kj[p], ng + g)),
                  pl.BlockSpec((tk, wl), lambda g, p, qi, kj: (kj[p], 2 * ng + g)),
                  qblk, qrep, qrep, krow] + [any_spec] * nc,
        out_specs=[pl.BlockSpec((T, wl), lambda g, p, qi, kj: (0, g)), kblk, kblk, krow,
                   pl.BlockSpec((T, LANES), lambda g, p, qi, kj: (0, 0))] + [any_spec] * nc,
        scratch_shapes=[pltpu.VMEM((T, wl), F32), pltpu.VMEM((tk, wl), F32), pltpu.VMEM((tk, wl), F32),
                        pltpu.VMEM((hps, 1, tk), F32)] + (_comm_sems(nc) if nc else []))
    act = jax.ShapeDtypeStruct((T, D), BF16)
    res = pl.pallas_call(
        body, name=name, grid_spec=grid_spec,
        out_shape=[act, act, act, jax.ShapeDtypeStruct((H, 1, T), F32),
                   jax.ShapeDtypeStruct((T, LANES), F32)] + _comm_out_shapes(a2a, False),
        compiler_params=_params(),
    )(qi, kj, qkv, qkv, qkv, dob, delta_rep, cql_rep, ck, *a2a)
    return res[:5], res[5:]


def _sgu_rows(T):
    return 2 * BLOCK if T % (2 * BLOCK) == 0 else BLOCK


def _sgu_norm(zv, g_ref, b_ref):
    vv = _gelu(zv)
    mu = jnp.mean(vv, axis=-1, keepdims=True)
    cen = vv - mu
    rstd = lax.rsqrt(jnp.mean(cen * cen, axis=-1, keepdims=True) + EPS)
    vh = cen * rstd
    return vh, rstd, vh * g_ref[...] + b_ref[...]


def _sgu_fwd(zpre, ln_g, ln_b, wsm, bs_rep):
    T = zpre.shape[0]
    W = zpre.shape[1] // 2
    G = W // BLOCK
    tr = _sgu_rows(T)

    def body(z_ref, g_ref, b_ref, ws_ref, bs_ref, o_ref):
        u = _gelu(z_ref[:, :W])
        _, _, vln = _sgu_norm(z_ref[:, W:], g_ref, b_ref)
        for c in range(tr // BLOCK):
            rows = slice(c * BLOCK, (c + 1) * BLOCK)
            for gi in range(G):
                cols = slice(gi * BLOCK, (gi + 1) * BLOCK)
                f = _dot(ws_ref[gi], vln[rows, cols].astype(BF16), NN) + bs_ref[gi]
                o_ref[rows, cols] = (u[rows, cols] * f).astype(BF16)

    full3 = pl.BlockSpec((G, BLOCK, BLOCK), lambda i: (0, 0, 0))
    return pl.pallas_call(
        body, name="sgu_fwd", grid=(T // tr,),
        in_specs=[pl.BlockSpec((tr, 2 * W), lambda i: (i, 0)), _vec_spec(W), _vec_spec(W), full3, full3],
        out_specs=pl.BlockSpec((tr, W), lambda i: (i, 0)),
        out_shape=jax.ShapeDtypeStruct((T, W), BF16), compiler_params=_params(),
    )(zpre, ln_g, ln_b, wsm, bs_rep)


def _sgu_bwd(dgt, zpre, ln_g, ln_b, wsm, wsmT, bs_rep):
    T = zpre.shape[0]
    W = zpre.shape[1] // 2
    G = W // BLOCK
    tr = BLOCK

    def body(dgt_ref, z_ref, g_ref, b_ref, ws_ref, wst_ref, bs_ref,
             dz_ref, dws_ref, dbs_ref, dlg_ref, dlb_ref, du_sc, dvln_sc):
        @pl.when(pl.program_id(0) == 0)
        def _():
            dws_ref[...] = jnp.zeros_like(dws_ref)
            dbs_ref[...] = jnp.zeros_like(dbs_ref)
            dlg_ref[...] = jnp.zeros_like(dlg_ref)
            dlb_ref[...] = jnp.zeros_like(dlb_ref)

        zu = z_ref[:, :W]
        zv = z_ref[:, W:]
        u = _gelu(zu)
        vh, rstd, vln = _sgu_norm(zv, g_ref, b_ref)
        dgtv = dgt_ref[...]
        trow = lax.broadcasted_iota(jnp.int32, (BLOCK, BLOCK), 0)
        tcol = lax.broadcasted_iota(jnp.int32, (BLOCK, BLOCK), 1)
        causal = trow >= tcol
        for c in range(tr // BLOCK):
            rows = slice(c * BLOCK, (c + 1) * BLOCK)
            for gi in range(G):
                cols = slice(gi * BLOCK, (gi + 1) * BLOCK)
                vb = vln[rows, cols].astype(BF16)
                f = _dot(ws_ref[gi], vb, NN) + bs_ref[gi]
                d = dgtv[rows, cols]
                du_sc[rows, cols] = d * f
                df = d * u[rows, cols]
                dfb = df.astype(BF16)
                dvln_sc[rows, cols] = _dot(wst_ref[gi], dfb, NN)
                dws_ref[gi] += jnp.where(causal, _dot(dfb, vb, NT), 0.0)
                dbs_ref[gi] += jnp.broadcast_to(jnp.sum(df, axis=-1, keepdims=True), (BLOCK, BLOCK))
        dvln = dvln_sc[...]
        dlg_ref[...] += jnp.sum(dvln * vh, axis=0, keepdims=True)
        dlb_ref[...] += jnp.sum(dvln, axis=0, keepdims=True)
        dvh = dvln * g_ref[...]
        dvv = rstd * (dvh - jnp.mean(dvh, axis=-1, keepdims=True)
                      - vh * jnp.mean(dvh * vh, axis=-1, keepdims=True))
        dz_ref[:, :W] = (du_sc[...] * _gelu_grad(zu)).astype(BF16)
        dz_ref[:, W:] = (dvv * _gelu_grad(zv)).astype(BF16)

    full3 = pl.BlockSpec((G, BLOCK, BLOCK), lambda i: (0, 0, 0))
    vec = jax.ShapeDtypeStruct((1, W), F32)
    acc3 = jax.ShapeDtypeStruct((G, BLOCK, BLOCK), F32)
    return pl.pallas_call(
        body, name="sgu_bwd", grid=(T // tr,),
        in_specs=[pl.BlockSpec((tr, W), lambda i: (i, 0)), pl.BlockSpec((tr, 2 * W), lambda i: (i, 0)),
                  _vec_spec(W), _vec_spec(W), full3, full3, full3],
        out_specs=[pl.BlockSpec((tr, 2 * W), lambda i: (i, 0)), full3, full3, _vec_spec(W), _vec_spec(W)],
        out_shape=[jax.ShapeDtypeStruct((T, 2 * W), BF16), acc3, acc3, vec, vec],
        scratch_shapes=[pltpu.VMEM((tr, W), F32), pltpu.VMEM((tr, W), F32)],
        compiler_params=_params(),
    )(dgt, zpre, ln_g, ln_b, wsm, wsmT, bs_rep)


def _rope(x, cos_t, sin_t):
    T, N = x.shape
    tr = _tile(T, ROW_TILE, 8)
    nrep = N // LANES
    half = ROPE_DIM // 2

    def body(x_ref, c_ref, s_ref, o_ref):
        xv = x_ref[...]
        lane = jnp.bitwise_and(lax.broadcasted_iota(jnp.int32, (tr, N), 1), SWA_DH - 1)
        partner = jnp.where(lane < half, -pltpu.roll(xv, N - half, 1), pltpu.roll(xv, half, 1))
        o_ref[...] = (xv * _rep(c_ref[...], nrep) + partner * _rep(s_ref[...], nrep)).astype(BF16)

    tab = pl.BlockSpec((tr, LANES), lambda i: (i, 0))
    row = pl.BlockSpec((tr, N), lambda i: (i, 0))
    return pl.pallas_call(
        body, name="rope", grid=(T // tr,), in_specs=[row, tab, tab], out_specs=row,
        out_shape=jax.ShapeDtypeStruct((T, N), BF16), compiler_params=_params(),
    )(x, cos_t, sin_t)


def _swa_tiles(T):
    sb = 4 if T >= 2048 else 2
    return sb, BLOCK * sb, T // (BLOCK * sb)


def _band_mask():
    row = lax.broadcasted_iota(jnp.int32, (BLOCK, 2 * BLOCK), 0)
    col = lax.broadcasted_iota(jnp.int32, (BLOCK, 2 * BLOCK), 1)
    return jnp.logical_and(col > row, col <= row + BLOCK), col


def _swa_specs(T, G):
    sb, tq, nq = _swa_tiles(T)
    q = pl.BlockSpec((G, tq, LANES), lambda h, i: (h, i, 0))
    kc = pl.BlockSpec((None, tq, LANES), lambda h, i: (h, i, 0))
    kp = pl.BlockSpec((None, BLOCK, LANES), lambda h, i: (h, jnp.maximum(i * sb - 1, 0), 0))
    return q, kc, kp


def _swa_band(b, i, kc_ref, kp_ref, vc_ref, vp_ref):
    rows = slice(b * BLOCK, (b + 1) * BLOCK)
    prev = slice((b - 1) * BLOCK, b * BLOCK)
    kprev = kp_ref[...] if b == 0 else kc_ref[prev, :]
    vprev = vp_ref[...] if b == 0 else vc_ref[prev, :]
    K = jnp.concatenate([kprev, kc_ref[rows, :]], axis=0)
    V = jnp.concatenate([vprev, vc_ref[rows, :]], axis=0)
    band, col = _band_mask()
    if b == 0:
        band = jnp.logical_and(band, jnp.logical_or(col >= BLOCK, i > 0))
    return rows, K, V, band


def _stack_heads(ref, rows, G):
    return jnp.concatenate([ref[g, rows, :] for g in range(G)], axis=0)


def _swa_fwd(qp, kp, vp, sinks):
    Hq, T, _ = qp.shape
    Hk = kp.shape[0]
    G = Hq // Hk
    sb, tq, nq = _swa_tiles(T)

    def body(sink_ref, q_ref, kc_ref, kp_ref, vc_ref, vp_ref, o_ref, lse_ref):
        h, i = pl.program_id(0), pl.program_id(1)
        head_of_row = lax.broadcasted_iota(jnp.int32, (G * BLOCK, 1), 0) // BLOCK
        sink = jnp.zeros((G * BLOCK, 1), F32)
        for g in range(G):
            sink = jnp.where(head_of_row == g, sink_ref[h * G + g], sink)
        for b in range(sb):
            rows, K, V, band = _swa_band(b, i, kc_ref, kp_ref, vc_ref, vp_ref)
            band = jnp.concatenate([band] * G, axis=0)
            s = jnp.where(band, _dot(_stack_heads(q_ref, rows, G), K, NT) * SWA_SCALE, NEG_INF)
            m = jnp.maximum(jnp.max(s, axis=-1, keepdims=True), sink)
            pm = jnp.exp(s - m)
            den = jnp.sum(pm, axis=-1, keepdims=True) + jnp.exp(sink - m)
            o = _dot((pm / den).astype(BF16), V, NN)
            lse = jnp.broadcast_to(m + jnp.log(den), (G * BLOCK, LANES))
            for g in range(G):
                o_ref[g, rows, :] = o[g * BLOCK:(g + 1) * BLOCK]
                lse_ref[g, rows, :] = lse[g * BLOCK:(g + 1) * BLOCK]

    q, kc, kpv = _swa_specs(T, G)
    out = jax.ShapeDtypeStruct((Hq, T, LANES), F32)
    return pl.pallas_call(
        body, name="swa_fwd", grid=(Hk, nq),
        in_specs=[pl.BlockSpec(memory_space=pltpu.SMEM), q, kc, kpv, kc, kpv], out_specs=[q, q],
        out_shape=[out, out], compiler_params=_params(),
    )(sinks, qp, kp, kp, vp, vp)


def _swa_bwd_dq(qp, kp, vp, dop, op, lse_rep, sinks):
    Hq, T, _ = qp.shape
    Hk = kp.shape[0]
    G = Hq // Hk
    sb, tq, nq = _swa_tiles(T)

    def body(sink_ref, q_ref, kc_ref, kp_ref, vc_ref, vp_ref, do_ref, o_ref, lse_ref, dq_ref, dsink_ref):
        h, i = pl.program_id(0), pl.program_id(1)

        @pl.when(i == 0)
        def _():
            dsink_ref[...] = jnp.zeros_like(dsink_ref)

        for b in range(sb):
            rows, K, V, band = _swa_band(b, i, kc_ref, kp_ref, vc_ref, vp_ref)
            band = jnp.concatenate([band] * G, axis=0)
            dov = _stack_heads(do_ref, rows, G)
            delta = jnp.sum(dov * _stack_heads(o_ref, rows, G), axis=-1, keepdims=True)
            lse = _stack_heads(lse_ref, rows, G)
            s = jnp.where(band, _dot(_stack_heads(q_ref, rows, G), K, NT) * SWA_SCALE, NEG_INF)
            pm = jnp.exp(s - _rep(lse, 2))
            ds = pm * (_dot(dov.astype(BF16), V, NT) - delta)
            dq = _dot((ds * SWA_SCALE).astype(BF16), K, NN)
            for g in range(G):
                head = slice(g * BLOCK, (g + 1) * BLOCK)
                dq_ref[g, rows, :] = dq[head]
                part = jnp.sum(jnp.exp(sink_ref[h * G + g] - lse[head]) * delta[head], axis=0, keepdims=True)
                dsink_ref[g] -= jnp.broadcast_to(part, (8, LANES))

    q, kc, kpv = _swa_specs(T, G)
    return pl.pallas_call(
        body, name="swa_bwd_dq", grid=(Hk, nq),
        in_specs=[pl.BlockSpec(memory_space=pltpu.SMEM), q, kc, kpv, kc, kpv, q, q, q],
        out_specs=[q, pl.BlockSpec((G, 8, LANES), lambda h, i: (h, 0, 0))],
        out_shape=[jax.ShapeDtypeStruct((Hq, T, LANES), F32), jax.ShapeDtypeStruct((Hq, 8, LANES), F32)],
        compiler_params=_params(),
    )(sinks, qp, kp, kp, vp, vp, dop, op, lse_rep)


def _swa_bwd_dkv(qp, kp, vp, dop, op, lse_rep):
    Hq, T, _ = qp.shape
    Hk = kp.shape[0]
    G = Hq // Hk
    sb, tq, nq = _swa_tiles(T)
    nblk = T // BLOCK

    def body(k_ref, v_ref, q_ref, qn_ref, do_ref, don_ref, o_ref, on_ref, lse_ref, lsen_ref, dk_ref, dv_ref):
        i = pl.program_id(1)
        trow = lax.broadcasted_iota(jnp.int32, (2 * BLOCK, BLOCK), 0)
        scol = lax.broadcasted_iota(jnp.int32, (2 * BLOCK, BLOCK), 1)
        band0 = jnp.logical_and(trow >= scol, trow < scol + BLOCK)
        for b in range(sb):
            rows = slice(b * BLOCK, (b + 1) * BLOCK)
            nxt = slice((b + 1) * BLOCK, (b + 2) * BLOCK)
            last = b == sb - 1
            band = band0
            if last:
                band = jnp.logical_and(band0, jnp.logical_or(trow < BLOCK, i < nq - 1))
            kb, vb = k_ref[rows, :], v_ref[rows, :]

            def bands(cur, nx):
                return jnp.concatenate([piece for g in range(G)
                                        for piece in (cur[g, rows, :], nx[g] if last else cur[g, nxt, :])], axis=0)

            Q, dov, lse = bands(q_ref, qn_ref), bands(do_ref, don_ref), bands(lse_ref, lsen_ref)
            delta = jnp.sum(dov * bands(o_ref, on_ref), axis=-1, keepdims=True)
            s = jnp.where(jnp.concatenate([band] * G, axis=0), _dot(Q, kb, NT) * SWA_SCALE, NEG_INF)
            pm = jnp.exp(s - lse)
            dob = dov.astype(BF16)
            dv_ref[rows, :] = _dot(pm.astype(BF16), dob, TN)
            ds = pm * (_dot(dob, vb, NT) - delta)
            dk_ref[rows, :] = _dot((ds * SWA_SCALE).astype(BF16), Q, TN)

    kspec = pl.BlockSpec((None, tq, LANES), lambda h, i: (h, i, 0))
    cur = pl.BlockSpec((G, tq, LANES), lambda h, i: (h, i, 0))
    nxt = pl.BlockSpec((G, BLOCK, LANES), lambda h, i: (h, jnp.minimum((i + 1) * sb, nblk - 1), 0))
    out = jax.ShapeDtypeStruct((Hk, T, LANES), F32)
    return pl.pallas_call(
        body, name="swa_bwd_dkv", grid=(Hk, nq),
        in_specs=[kspec, kspec, cur, nxt, cur, nxt, cur, nxt, cur, nxt], out_specs=[kspec, kspec],
        out_shape=[out, out], compiler_params=_params(),
    )(kp, vp, qp, qp, dop, dop, op, op, lse_rep, lse_rep)


def _to_heads(a, nh):
    T = a.shape[0]
    a = a.reshape(T, nh, SWA_DH).transpose(1, 0, 2)
    return jnp.pad(a, ((0, 0), (0, 0), (0, LANES - SWA_DH)))


def _from_heads(a):
    nh, T, _ = a.shape
    return a[:, :, :SWA_DH].transpose(1, 0, 2).reshape(T, nh * SWA_DH)


def _adam(g, w, m, v):
    m2 = ADAM_B1 * m + (1.0 - ADAM_B1) * g
    v2 = ADAM_B2 * v + (1.0 - ADAM_B2) * (g * g)
    m_hat = m2 / (1.0 - ADAM_B1 ** ADAM_STEP)
    v_hat = v2 / (1.0 - ADAM_B2 ** ADAM_STEP)
    delta = -ADAM_LR * (m_hat / (jnp.sqrt(v_hat) + ADAM_EPS) + ADAM_WD * w)
    return delta, m2, v2


def _ada_fwd(c_all, w, b):
    L, D, n = w.shape
    tn = _tile(n, 768)

    def body(c_ref, w_ref, b_ref, o_ref):
        cv = c_ref[...]
        ca = (cv * _sigmoid(cv)).astype(BF16)
        o_ref[...] = _dot(ca, w_ref[...].astype(BF16), NN) + b_ref[...]

    return pl.pallas_call(
        body, name="ada_fwd", grid=(L, n // tn),
        in_specs=[pl.BlockSpec((NDEV, D), lambda l, j: (0, 0)), pl.BlockSpec((None, D, tn), lambda l, j: (l, 0, j)),
                  pl.BlockSpec((None, 1, tn), lambda l, j: (l, 0, j))],
        out_specs=pl.BlockSpec((None, NDEV, tn), lambda l, j: (l, 0, j)),
        out_shape=jax.ShapeDtypeStruct((L, NDEV, n), F32), compiler_params=_params(),
    )(c_all, w, b)


def _ada_update(c_rep, dm, w, m, v):
    L, D, n = w.shape
    tr = _tile(D, 256, 8)
    nrep = n // LANES

    def body(c_ref, dm_ref, w_ref, m_ref, v_ref, g_ref, d_ref, m2_ref, v2_ref):
        g = jnp.zeros((tr, n), F32)
        for b in range(NDEV):
            cv = c_ref[b]
            g = g + _rep(cv * _sigmoid(cv), nrep) * dm_ref[pl.ds(b, 1), :]
        g_ref[...] = g
        d_ref[...], m2_ref[...], v2_ref[...] = _adam(g, w_ref[...], m_ref[...], v_ref[...])

    blk = pl.BlockSpec((None, tr, n), lambda l, i: (l, i, 0))
    out = jax.ShapeDtypeStruct((L, D, n), F32)
    return pl.pallas_call(
        body, name="ada_update", grid=(L, D // tr),
        in_specs=[pl.BlockSpec((NDEV, tr, LANES), lambda l, i: (0, i, 0)),
                  pl.BlockSpec((None, NDEV, n), lambda l, i: (l, 0, 0)), blk, blk, blk],
        out_specs=[blk, blk, blk, blk], out_shape=[out, out, out, out], compiler_params=_params(),
    )(c_rep, dm, w, m, v)


def _adamw(name, parts, w, m, v, layer, stacked=None):
    P, R, C = parts.shape
    Lw = w.shape[0]
    cpad = -(-C // LANES) * LANES
    per_row = cpad * (P * parts.dtype.itemsize + 7 * 4) * 2
    tr = _tile(R, max(8, (24 * 1024 * 1024 // per_row) // 8 * 8), 8)
    if stacked is None:
        stacked = [lax.empty((Lw, R, C), F32) for _ in range(4)]

    def body(p_ref, w_ref, m_ref, v_ref, *rest):
        g_ref, d_ref, m2_ref, v2_ref = rest[4:]
        g = p_ref[0].astype(F32)
        for s in range(1, P):
            g = g + p_ref[s].astype(F32)
        g_ref[...] = g
        d_ref[...], m2_ref[...], v2_ref[...] = _adam(g, w_ref[...], m_ref[...], v_ref[...])

    stk = pl.BlockSpec((None, tr, C), lambda i: (layer, i, 0))
    any_spec = pl.BlockSpec(memory_space=pl.ANY)
    out = jax.ShapeDtypeStruct((Lw, R, C), F32)
    return pl.pallas_call(
        body, name=name, grid=(R // tr,),
        in_specs=[pl.BlockSpec((P, tr, C), lambda i: (0, i, 0)), stk, stk, stk] + [any_spec] * 4,
        out_specs=[stk, stk, stk, stk], out_shape=[out, out, out, out],
        input_output_aliases={4: 0, 5: 1, 6: 2, 7: 3}, compiler_params=_params(),
    )(parts, w, m, v, *stacked)


def _colcat(a):
    s, k, n = a.shape
    return a.transpose(1, 0, 2).reshape(k, s * n)


def _colsplit(a):
    k, n8 = a.shape
    return a.reshape(k, NDEV, n8 // NDEV).transpose(1, 0, 2)


def _rowsplit(a):
    r, c = a.shape
    return a.reshape(NDEV, r // NDEV, c)


def kernel(x, c, positions, ada_w, ada_b, mix_pre_g, mix_post_g, ffn_pre_g, ffn_post_g, ffn_w_gu, ffn_w_down, fox_w_in, fox_b_f, fox_w_out, sgu_w_in, sgu_ln_g, sgu_ln_b, sgu_w_s, sgu_b_s, sgu_w_out, swa_w_in, swa_sinks, swa_w_out, loss_target, m_ada_w, m_ada_b, m_mix_pre_g, m_mix_post_g, m_ffn_pre_g, m_ffn_post_g, m_ffn_w_gu, m_ffn_w_down, m_fox_w_in, m_fox_b_f, m_fox_w_out, m_sgu_w_in, m_sgu_ln_g, m_sgu_ln_b, m_sgu_w_s, m_sgu_b_s, m_sgu_w_out, m_swa_w_in, m_swa_sinks, m_swa_w_out, v_ada_w, v_ada_b, v_mix_pre_g, v_mix_post_g, v_ffn_pre_g, v_ffn_post_g, v_ffn_w_gu, v_ffn_w_down, v_fox_w_in, v_fox_b_f, v_fox_w_out, v_sgu_w_in, v_sgu_ln_g, v_sgu_ln_b, v_sgu_w_s, v_sgu_b_s, v_sgu_w_out, v_swa_w_in, v_swa_sinks, v_swa_w_out):
    env = locals()
    W = {n: env[n] for n in WEIGHTS}
    M = {n: env["m_" + n] for n in WEIGHTS}
    V = {n: env["v_" + n] for n in WEIGHTS}

    me = 4 * lax.axis_index("x") + 2 * lax.axis_index("y") + lax.axis_index("c")
    _, T, D = x.shape
    L = ada_w.shape[0]
    n_ada = ada_w.shape[2]
    F = ffn_w_gu.shape[2] * NDEV // 2
    H = D // FOX_DH
    Hq = D // SWA_DH
    x0 = x.reshape(T, D)
    mixer = {0: 'fox', 1: 'sgu', 2: 'swa'}

    c_all = _comm("gather_c", [c], 'gather')[0].reshape(NDEV, D)
    ada_b_mine = lax.dynamic_slice_in_dim(ada_b, me * n_ada, n_ada, axis=1).reshape(L, 1, n_ada)
    mod_cols = _ada_fwd(c_all, ada_w, ada_b_mine)
    mod = _comm("a2a_mod", [mod_cols.transpose(1, 0, 2)], 'a2a')[0]
    mod = mod.transpose(1, 0, 2).reshape(L, 6, 1, D)

    inv = ROPE_THETA ** (-jnp.arange(0, ROPE_DIM, 2, dtype=F32) / ROPE_DIM)
    ang = positions[0].astype(F32)[:, None] * inv
    pad1 = jnp.ones((T, SWA_DH - ROPE_DIM), F32)
    cos64 = jnp.concatenate([jnp.cos(ang), jnp.cos(ang), pad1], axis=1)
    sin64 = jnp.concatenate([jnp.sin(ang), jnp.sin(ang), 0.0 * pad1], axis=1)
    cos_t = jnp.concatenate([cos64, cos64], axis=1)
    sin_t = jnp.concatenate([sin64, sin64], axis=1)

    fox_layers = [i for i in range(L) if mixer[i % 3] == 'fox']
    assert fox_layers and fox_layers[0] == 0

    def slice_of(i, role):
        kind, j = mixer[i % 3], i // 3
        src = {'wgu': (ffn_w_gu, i), 'wd': (ffn_w_down, i), 'win': (W[kind + '_w_in'], j), 'wout': (W[kind + '_w_out'], j)}[role]
        return src[0][src[1]].astype(BF16)

    def nparams(key):
        return int(np.prod(slice_of(*key).shape)) * NDEV

    def gather_plan():
        plan = {}
        for f in fox_layers:
            later = [i for i in fox_layers if i > f]
            stop = later[0] if later else L
            keys = [(f, 'wout'), (f, 'wgu'), (f, 'wd')]
            for i in range(f + 1, stop):
                keys += [(i, 'wgu'), (i, 'wd'), (i, 'win'), (i, 'wout')]
            if later:
                keys += [(stop, 'win')]
            near = [k for k in keys if k[0] <= f + 1]
            far = [k for k in keys if k[0] > f + 1]
            plan[('fox', f)] = near
            for i in range(f, stop):
                take, total = [], 0
                while far and (not take or total + nparams(far[0]) <= HOSTED_GATHER_PARAMS):
                    total += nparams(far[0])
                    take.append(far.pop(0))
                plan[('ffn', i)] = take
            assert not far
        return plan

    plan = gather_plan()
    raw, full = {}, {}
    first_keys = [(0, 'win')]
    raw.update(zip(first_keys, _gather2("gather_first", [slice_of(*k) for k in first_keys])))

    def wget(i, role):
        if (i, role) not in full:
            got = raw[(i, role)]
            full[(i, role)] = _colcat(got) if role in ('wgu', 'win') else got.reshape(-1, D)
        return full[(i, role)]

    saved = []
    xc = x0
    for i in range(L):
        kind, j = mixer[i % 3], i // 3
        s = dict(x_in=xc)
        sh_m, sc_m, g_m, sh_f, sc_f, g_f = [mod[i, t] for t in range(6)]
        h = _pre_fwd(xc, mix_pre_g[i:i + 1], sc_m, sh_m)
        s['h'] = h
        if kind == 'fox':
            wqkv = wget(i, 'win')[:, :3 * D]
            wf = jnp.pad(wget(i, 'win')[:, 3 * D:], ((0, 0), (0, LANES - H)))
            s['win_pad'] = jnp.concatenate([wqkv, wf], axis=1)
            bf = jnp.pad(fox_b_f[j:j + 1], ((0, 0), (0, LANES - H)))
            qkv = _mm("fox_qkv", h, wqkv, 'nn', BF16)
            fg = _mm("fox_fg", h, wf, 'nn', F32)
            cum, cq_rep = _fox_gate_fwd(fg, bf, H)
            ck = cum[:, :H].T.reshape(H, 1, T)
            keys = plan[('fox', i)]
            (o, obf, lse_rep), got = _fox_fwd("fox_fwd%d" % j, qkv, cq_rep, ck, gather=[slice_of(*k) for k in keys])
            raw.update(zip(keys, got))
            s.update(qkv=qkv, fg=fg, bf=bf, cq_rep=cq_rep, ck=ck, o=o, lse_rep=lse_rep, mix_out=obf)
        elif kind == 'sgu':
            G = D // BLOCK
            causal = jnp.tril(jnp.ones((BLOCK, BLOCK), bool))
            wsm = jnp.where(causal[None], sgu_w_s[j], 0.0).astype(BF16)
            bs_rep = jnp.broadcast_to(sgu_b_s[j][:, :, None], (G, BLOCK, BLOCK))
            zpre = _mm("sgu_in", h, wget(i, 'win'), 'nn', F32)
            gated = _sgu_fwd(zpre, sgu_ln_g[j:j + 1], sgu_ln_b[j:j + 1], wsm, bs_rep)
            s.update(zpre=zpre, wsm=wsm, bs_rep=bs_rep, mix_out=gated)
        else:
            Hk = (wget(i, 'win').shape[1] // SWA_DH - Hq) // 2
            proj = _mm("swa_in", h, wget(i, 'win'), 'nn', F32)
            qr = _rope(proj[:, :Hq * SWA_DH], cos_t, sin_t)
            kr = _rope(proj[:, Hq * SWA_DH:(Hq + Hk) * SWA_DH], cos_t, sin_t)
            qp, kp = _to_heads(qr, Hq), _to_heads(kr, Hk)
            vp = _to_heads(proj[:, (Hq + Hk) * SWA_DH:].astype(BF16), Hk)
            op, lse_rep = _swa_fwd(qp, kp, vp, swa_sinks[j])
            s.update(qp=qp, kp=kp, vp=vp, op=op, lse_rep=lse_rep, Hk=Hk, mix_out=_from_heads(op).astype(BF16))
        y = _mm("mix_out", s['mix_out'], wget(i, 'wout'), 'nn', F32)
        x_mid = _post_fwd(xc, y, mix_post_g[i:i + 1], g_m)
        s.update(y_mix=y, x_mid=x_mid)
        h2 = _pre_fwd(x_mid, ffn_pre_g[i:i + 1], sc_f, sh_f)
        keys = plan[('ffn', i)]
        (g, u, a), got = _ffn_up("ffn_up_g%d" % i if keys else "ffn_up", h2, wget(i, 'wgu'), gather=[slice_of(*k) for k in keys])
        raw.update(zip(keys, got))
        y2 = _mm("ffn_down", a, wget(i, 'wd'), 'nn', F32)
        xc = _post_fwd(x_mid, y2, ffn_post_g[i:i + 1], g_f)
        s.update(h2=h2, g=g, u=u, a=a, y_ffn=y2)
        saved.append(s)

    dx, lsum = _loss(xc, loss_target.reshape(T, D))
    loss = lax.psum(0.5 * lsum[0, 0] / D, AXES)

    small = {n: [None] * W[n].shape[0] for n in SMALL}
    dmod = [None] * L
    pending = []
    stacks = {}

    def update(items, recv):
        for (name, idx, _), parts in zip(items, recv):
            stacks[name] = _adamw("adamw_" + name, parts, W[name], M[name], V[name], idx, stacked=stacks.get(name))

    for i in reversed(range(L)):
        kind, j = mixer[i % 3], i // 3
        s = saved[i]
        sh_m, sc_m, g_m, sh_f, sc_f, g_f = [mod[i, t] for t in range(6)]
        dy2, dg_f, dpost_f = _post_bwd(dx, s['y_ffn'], ffn_post_g[i:i + 1], g_f)
        dwd = _mm("ffn_dwd", s['a'], dy2, 'tn', BF16)
        dg, du = _ffn_dact(dy2, wget(i, 'wd'), s['g'], s['u'])
        dwgu = jnp.concatenate([_mm("ffn_dwg", s['h2'], dg, 'tn', BF16), _mm("ffn_dwu", s['h2'], du, 'tn', BF16)], axis=1)
        pending += [('ffn_w_gu', i, _colsplit(dwgu)), ('ffn_w_down', i, _rowsplit(dwd))]
        take, total = [], 0
        while i < L - 1 and pending and (not take or total + pending[0][2].size <= HOSTED_A2A_PARAMS):
            total += pending[0][2].size
            take.append(pending.pop(0))
        dh_pairs = [(dg, wget(i, 'wgu'), (0, 0), (0, 0)), (du, wget(i, 'wgu'), (0, 0), (0, F))]
        if take:
            dh2, recv = _matmul("ffn_dh_x%d" % i, dh_pairs, 'nt', F32, T, D, F, a2a=[item[2] for item in take])
            update(take, recv)
        else:
            dh2 = _matmul("ffn_dh", dh_pairs, 'nt', F32, T, D, F)
        dx, dsh_f, dsc_f, dpre_f = _pre_bwd(dh2, s['x_mid'], ffn_pre_g[i:i + 1], sc_f, dx)
        dy, dg_m, dpost_m = _post_bwd(dx, s['y_mix'], mix_post_g[i:i + 1], g_m)
        dwout = _mm("mix_dwout", s['mix_out'], dy, 'tn', BF16)
        pending.append((kind + '_w_out', j, _rowsplit(dwout)))
        dmix = _mm("mix_dout", dy, wget(i, 'wout'), 'nt', F32)
        if kind == 'fox':
            dob, delta_rep, cql_rep = _fox_bwd_prep(dmix, s['o'], s['lse_rep'], s['cq_rep'])
            (dq, dk, dv, dck, dcq), recv = _fox_bwd("fox_bwd%d" % j, s['qkv'], dob, delta_rep, cql_rep, s['ck'],
                                                    a2a=[item[2] for item in pending])
            update(pending, recv)
            pending = []
            dcum = jnp.pad(dck.reshape(H, T).T, ((0, 0), (0, LANES - H))) + dcq
            dfg, dbf = _fox_gate_bwd(dcum, s['fg'], s['bf'])
            small['fox_b_f'][j] = dbf[0, :H]
            dproj = jnp.concatenate([dq, dk, dv, dfg.astype(BF16)], axis=1)
            dwin = _mm("fox_dwin", s['h'], dproj, 'tn', BF16)[:, :3 * D + H]
            pending.append((kind + '_w_in', j, _colsplit(dwin)))
            if i == 0:
                dh, recv = _mm("fox_dh_last", dproj, s['win_pad'], 'nt', F32, a2a=[item[2] for item in pending])
                update(pending, recv)
                pending = []
            else:
                dh = _mm("fox_dh", dproj, s['win_pad'], 'nt', F32)
        elif kind == 'sgu':
            wsmT = s['wsm'].transpose(0, 2, 1)
            dz, dws, dbs, dlg, dlb = _sgu_bwd(dmix, s['zpre'], sgu_ln_g[j:j + 1], sgu_ln_b[j:j + 1], s['wsm'], wsmT, s['bs_rep'])
            small['sgu_w_s'][j], small['sgu_b_s'][j] = dws, dbs[:, :, 0]
            small['sgu_ln_g'][j], small['sgu_ln_b'][j] = dlg[0], dlb[0]
            dwin = _mm("sgu_dwin", s['h'], dz, 'tn', BF16)
            pending.append((kind + '_w_in', j, _colsplit(dwin)))
            dh = _mm("sgu_dh", dz, wget(i, 'win'), 'nt', F32)
        else:
            Hk = s['Hk']
            dop = _to_heads(dmix, Hq)
            dqp, dsink = _swa_bwd_dq(s['qp'], s['kp'], s['vp'], dop, s['op'], s['lse_rep'], swa_sinks[j])
            dkp, dvp = _swa_bwd_dkv(s['qp'], s['kp'], s['vp'], dop, s['op'], s['lse_rep'])
            small['swa_sinks'][j] = dsink[:, 0, 0]
            dproj = jnp.concatenate([_rope(_from_heads(dqp), cos_t, -sin_t), _rope(_from_heads(dkp), cos_t, -sin_t),
                                     _from_heads(dvp).astype(BF16)], axis=1)
            dwin = _mm("swa_dwin", s['h'], dproj, 'tn', BF16)
            pending.append((kind + '_w_in', j, _colsplit(dwin)))
            dh = _mm("swa_dh", dproj, wget(i, 'win'), 'nt', F32)
        dx, dsh_m, dsc_m, dpre_m = _pre_bwd(dh, s['x_in'], mix_pre_g[i:i + 1], sc_m, dx)
        small['mix_pre_g'][i], small['mix_post_g'][i] = dpre_m[0], dpost_m[0]
        small['ffn_pre_g'][i], small['ffn_post_g'][i] = dpre_f[0], dpost_f[0]
        dmod[i] = jnp.concatenate([dsh_m, dsc_m, dg_m, dsh_f, dsc_f, dg_f], axis=1)[0]

    if pending:
        update(pending, _comm("a2a_last", [item[2] for item in pending], 'a2a'))
    grad_x = dx.reshape(1, T, D)

    small['ada_b'] = dmod
    flat = jnp.concatenate([jnp.stack(small[n]).reshape(-1) for n in SMALL])
    width = 8 * LANES
    npad = -flat.shape[0] % (8 * width)
    packed = jnp.pad(flat, (0, npad)).reshape(-1, width)
    parts = _comm("gather_small", [packed], 'gather')[0]

    def pack(d):
        f = jnp.concatenate([d[n].reshape(-1) for n in SMALL])
        return jnp.pad(f, (0, npad)).reshape(1, -1, width)

    res = _adamw("adamw_small", parts, pack(W), pack(M), pack(V), 0)
    off = 0
    for n in SMALL:
        size = W[n].size
        stacks[n] = [val.reshape(-1)[off:off + size].reshape(W[n].shape) for val in res]
        off += size

    dmod_all = parts.reshape(NDEV, -1)[:, :L * 6 * D].reshape(NDEV, L, 6 * D)
    dm = lax.dynamic_slice_in_dim(dmod_all, me * n_ada, n_ada, axis=2).transpose(1, 0, 2)
    c_rep = jnp.broadcast_to(c_all[:, :, None], (NDEV, D, LANES))
    stacks['ada_w'] = _ada_update(c_rep, dm, ada_w, m_ada_w, v_ada_w)

    return (loss, grad_x, *[stacks[n][t] for t in range(4) for n in WEIGHTS])
```

```python
import numpy as np
import jax
import jax.numpy as jnp
from jax import lax
from jax.experimental import pallas as pl
from jax.experimental.pallas import tpu as pltpu

F32 = jnp.float32
BF16 = jnp.bfloat16
NDEV = 8
AXES = ("x", "y", "c")
LANES = 128
VMEM_LIMIT_BYTES = 48 * 1024 * 1024
NEG_INF = float("-inf")
ROW_TILE = 256

EPS = 1e-6
BLOCK = 128
FOX_DH = 128
FOX_HPS = 2
FOX_FWD_BLOCK = 1024
FOX_BWD_QUERIES = 1024
FOX_BWD_KEYS = 512
HOSTED_GATHER_PARAMS = 36 * 2 ** 20
HOSTED_A2A_PARAMS = 22 * 2 ** 20
SWA_DH = 64
ROPE_DIM = 16
ROPE_THETA = 500000.0
FOX_SCALE = FOX_DH ** -0.5
SWA_SCALE = SWA_DH ** -0.5
GELU_C0 = 0.7978845608028654
GELU_C1 = 0.044715

ADAM_LR = 0.001
ADAM_B1 = 0.9
ADAM_B2 = 0.999
ADAM_EPS = 1e-08
ADAM_WD = 0.01
ADAM_STEP = 10

NN = (((1,), (0,)), ((), ()))
NT = (((1,), (1,)), ((), ()))
TN = (((0,), (0,)), ((), ()))

WEIGHTS = ['ada_w', 'ada_b', 'mix_pre_g', 'mix_post_g', 'ffn_pre_g', 'ffn_post_g', 'ffn_w_gu', 'ffn_w_down',
           'fox_w_in', 'fox_b_f', 'fox_w_out', 'sgu_w_in', 'sgu_ln_g', 'sgu_ln_b', 'sgu_w_s', 'sgu_b_s',
           'sgu_w_out', 'swa_w_in', 'swa_sinks', 'swa_w_out']
SMALL = ['ada_b', 'mix_pre_g', 'mix_post_g', 'ffn_pre_g', 'ffn_post_g', 'fox_b_f', 'sgu_ln_g', 'sgu_ln_b',
         'sgu_w_s', 'sgu_b_s', 'swa_sinks']


def _dot(a, b, dims):
    return lax.dot_general(a, b, dims, preferred_element_type=F32)


def _tile(n, pref, mult=LANES):
    t = (min(pref, n) // mult) * mult
    while t >= mult:
        if n % t == 0:
            return t
        t -= mult
    return n


def _params():
    return pltpu.CompilerParams(vmem_limit_bytes=VMEM_LIMIT_BYTES)


def _rep(a, n):
    return a if n == 1 else jnp.concatenate([a] * n, axis=-1)


def _vec_spec(d):
    return pl.BlockSpec((1, d), lambda *_: (0, 0))


def _sigmoid(z):
    return 1.0 / (1.0 + jnp.exp(-z))


def _gelu(z):
    t = jnp.tanh(GELU_C0 * (z + GELU_C1 * z * z * z))
    return 0.5 * z * (1.0 + t)


def _gelu_grad(z):
    t = jnp.tanh(GELU_C0 * (z + GELU_C1 * z * z * z))
    return 0.5 * (1.0 + t) + 0.5 * z * (1.0 - t * t) * GELU_C0 * (1.0 + 3.0 * GELU_C1 * z * z)


def _comm_out_shapes(arrs, gather):
    return [jax.ShapeDtypeStruct(((NDEV,) + a.shape) if gather else a.shape, a.dtype) for a in arrs]


def _comm_sems(n):
    return [pltpu.SemaphoreType.DMA((n,)), pltpu.SemaphoreType.DMA((n,)), pltpu.SemaphoreType.DMA((n,))]


def _me():
    x, y, c = lax.axis_index("x"), lax.axis_index("y"), lax.axis_index("c")
    return x, y, c, 4 * x + 2 * y + c


def _comm_start(ins, outs, gather, send_sems, recv_sems, local_sems):
    x, y, c, me = _me()
    for a in range(len(ins)):
        pltpu.make_async_copy(ins[a] if gather else ins[a].at[me], outs[a].at[me], local_sems.at[a]).start()
        for bits in range(1, NDEV):
            px = (1 - x) if bits & 4 else x
            py = (1 - y) if bits & 2 else y
            pc = (1 - c) if bits & 1 else c
            pltpu.make_async_remote_copy(
                src_ref=ins[a] if gather else ins[a].at[4 * px + 2 * py + pc], dst_ref=outs[a].at[me],
                send_sem=send_sems.at[a], recv_sem=recv_sems.at[a],
                device_id=(px, py, pc), device_id_type=pl.DeviceIdType.MESH).start()


def _comm_wait(ins, outs, gather, send_sems, recv_sems, local_sems):
    x, y, c, me = _me()
    for a in range(len(ins)):
        seven = outs[a].at[pl.ds(0, NDEV - 1)]
        pltpu.make_async_remote_copy(src_ref=seven, dst_ref=seven, send_sem=send_sems.at[a], recv_sem=recv_sems.at[a],
                                     device_id=(x, y, c), device_id_type=pl.DeviceIdType.MESH).wait()
        pltpu.make_async_copy(ins[a] if gather else ins[a].at[me], outs[a].at[me], local_sems.at[a]).wait()


def _comm(name, arrs, kind):
    n = len(arrs)
    gather = kind == 'gather'

    def body(*refs):
        ins, outs, sems = refs[:n], refs[n:2 * n], refs[2 * n:]
        _comm_start(ins, outs, gather, *sems)
        _comm_wait(ins, outs, gather, *sems)

    any_spec = pl.BlockSpec(memory_space=pl.ANY)
    return pl.pallas_call(
        body, name=name, out_shape=_comm_out_shapes(arrs, gather),
        in_specs=[any_spec] * n, out_specs=[any_spec] * n, scratch_shapes=_comm_sems(n),
    )(*arrs)


def _gather2_sems(n):
    return [pltpu.SemaphoreType.DMA((n,)) for _ in range(4)]


def _remote(src, dst, send_sem, recv_sem, device):
    return pltpu.make_async_remote_copy(src_ref=src, dst_ref=dst, send_sem=send_sem, recv_sem=recv_sem,
                                        device_id=device, device_id_type=pl.DeviceIdType.MESH)


def _gather2_start(ins, outs, send_sems, ici_sems, d2d_sems, local_sems):
    x, y, c, me = _me()
    for a in range(len(ins)):
        pltpu.make_async_copy(ins[a], outs[a].at[me], local_sems.at[a]).start()
        _remote(ins[a], outs[a].at[me], send_sems.at[a], d2d_sems.at[a], (x, y, 1 - c)).start()
        for px, py in ((1 - x, y), (x, 1 - y), (1 - x, 1 - y)):
            _remote(ins[a], outs[a].at[me], send_sems.at[a], ici_sems.at[a], (px, py, c)).start()


def _gather2_pass_on(ins, outs, send_sems, ici_sems, d2d_sems, local_sems):
    x, y, c, me = _me()
    for a in range(len(ins)):
        three = outs[a].at[pl.ds(0, 3)]
        _remote(three, three, send_sems.at[a], ici_sems.at[a], (x, y, c)).wait_recv()
        for px, py in ((1 - x, y), (x, 1 - y), (1 - x, 1 - y)):
            slot = outs[a].at[4 * px + 2 * py + c]
            _remote(slot, slot, send_sems.at[a], d2d_sems.at[a], (x, y, 1 - c)).start()


def _gather2_finish(ins, outs, send_sems, ici_sems, d2d_sems, local_sems):
    x, y, c, me = _me()
    for a in range(len(ins)):
        four, seven = outs[a].at[pl.ds(0, 4)], outs[a].at[pl.ds(0, NDEV - 1)]
        _remote(four, four, send_sems.at[a], d2d_sems.at[a], (x, y, c)).wait_recv()
        _remote(seven, seven, send_sems.at[a], d2d_sems.at[a], (x, y, c)).wait_send()
        pltpu.make_async_copy(ins[a], outs[a].at[me], local_sems.at[a]).wait()


def _gather2(name, arrs):
    n = len(arrs)

    def body(*refs):
        ins, outs, sems = refs[:n], refs[n:2 * n], refs[2 * n:]
        _gather2_start(ins, outs, *sems)
        _gather2_pass_on(ins, outs, *sems)
        _gather2_finish(ins, outs, *sems)

    any_spec = pl.BlockSpec(memory_space=pl.ANY)
    return pl.pallas_call(
        body, name=name, out_shape=_comm_out_shapes(arrs, True),
        in_specs=[any_spec] * n, out_specs=[any_spec] * n, scratch_shapes=_gather2_sems(n),
    )(*arrs)


MATMUL_VMEM_BUDGET = 38 * 1024 * 1024


def _matmul_tiles(mode, M, N, K, npairs, out_size):
    def uniq(vals):
        return sorted(set(vals), reverse=True)

    tms = uniq(_tile(M, p) for p in (1024, 512, 256))
    tns = uniq(_tile(N, p) for p in (1536, 1024, 512, 256))
    tks = uniq(_tile(K, p) for p in (2048, 1024)) if mode == 'tn' else [K] + uniq(_tile(K, p) for p in (2048, 1024))
    for tk in tks:
        best = None
        for tm in tms:
            for tn in tns:
                steps = K // tk
                need = (2 * npairs * (tm + tn) * tk * 2 + 2 * tm * tn * out_size + npairs * tm * tn * 4
                        + (tm * tn * 4 if steps > 1 else 0) + (tk * tm * 2 if mode == 'tn' else 0))
                if need <= MATMUL_VMEM_BUDGET and (best is None or (tm * tn, tm) > (best[0] * best[1], best[0])):
                    best = (tm, tn, tk)
        if best is not None:
            return best
    return _tile(M, 256), _tile(N, 256), _tile(K, 512)


def _matmul(name, pairs, mode, out_dtype, M, N, K, tm=None, tn=None, tk=None, a2a=(), slots=None):
    if not (tm and tn and tk):
        tm, tn, tk = _matmul_tiles(mode, M, N, K, len(pairs), jnp.dtype(out_dtype).itemsize)
    nk = K // tk
    dims = {'nn': NN, 'nt': NT, 'tn': TN}[mode]
    in_specs, ops = [], []
    for a, b, ao, bo in pairs:
        if mode == 'tn':
            assert ao[0] % tk == 0 and ao[1] % tm == 0
            sa = pl.BlockSpec((tk, tm), lambda i, j, k, r=ao[0] // tk, c=ao[1] // tm: (k + r, i + c))
        else:
            assert ao[0] % tm == 0 and ao[1] % tk == 0
            sa = pl.BlockSpec((tm, tk), lambda i, j, k, r=ao[0] // tm, c=ao[1] // tk: (i + r, k + c))
        if mode == 'nt':
            assert bo[0] % tn == 0 and bo[1] % tk == 0
            sb = pl.BlockSpec((tn, tk), lambda i, j, k, r=bo[0] // tn, c=bo[1] // tk: (j + r, k + c))
        else:
            assert bo[0] % tk == 0 and bo[1] % tn == 0
            sb = pl.BlockSpec((tk, tn), lambda i, j, k, r=bo[0] // tk, c=bo[1] // tn: (k + r, j + c))
        in_specs += [sa, sb]
        ops += [a, b]
    npairs = len(pairs)
    nc = len(a2a)
    grid = (M // tm, N // tn, nk)
    earlier = [slots[2]] if slots and slots[2] is not None else []
    first_out = 2 * npairs + nc + len(earlier)

    def body(*refs):
        cin, o_ref, cout = refs[2 * npairs:2 * npairs + nc], refs[first_out], refs[first_out + 1:first_out + nc + 1]
        scratch = refs[first_out + nc + 1:]
        sems = scratch[1:] if nk > 1 else scratch
        i, j, k = pl.program_id(0), pl.program_id(1), pl.program_id(2)

        if nc:
            @pl.when(jnp.logical_and(jnp.logical_and(i == 0, j == 0), k == 0))
            def _():
                _comm_start(cin, cout, False, *sems)

        part = _dot(refs[0][...], refs[1][...], dims)
        for p in range(1, npairs):
            part = part + _dot(refs[2 * p][...], refs[2 * p + 1][...], dims)
        if nk == 1:
            o_ref[...] = part.astype(out_dtype)
        else:
            acc = scratch[0]

            @pl.when(k == 0)
            def _():
                acc[...] = part

            @pl.when(k > 0)
            def _():
                acc[...] += part

            @pl.when(k == nk - 1)
            def _():
                o_ref[...] = acc[...].astype(out_dtype)

        if nc:
            @pl.when(jnp.logical_and(jnp.logical_and(i == grid[0] - 1, j == grid[1] - 1), k == nk - 1))
            def _():
                _comm_wait(cin, cout, False, *sems)

    any_spec = pl.BlockSpec(memory_space=pl.ANY)
    if slots:
        out_spec = pl.BlockSpec((None, tm, tn), lambda i, j, k, base=slots[1]: (j + base, i, 0))
        out_shape = jax.ShapeDtypeStruct((slots[0], M, tn), out_dtype)
    else:
        out_spec = pl.BlockSpec((tm, tn), lambda i, j, k: (i, j))
        out_shape = jax.ShapeDtypeStruct((M, N), out_dtype)
    res = pl.pallas_call(
        body, name=name, grid=grid,
        in_specs=in_specs + [any_spec] * (nc + len(earlier)),
        out_specs=[out_spec] + [any_spec] * nc,
        out_shape=[out_shape] + _comm_out_shapes(a2a, False),
        scratch_shapes=([] if nk == 1 else [pltpu.VMEM((tm, tn), F32)]) + (_comm_sems(nc) if nc else []),
        input_output_aliases={first_out - 1: 0} if earlier else {},
        compiler_params=_params(),
    )(*ops, *a2a, *earlier)
    return (res[0], res[1:]) if nc else res[0]


def _mm(name, a, b, mode, out_dtype, a2a=()):
    if mode == 'nn':
        (M, K), N = a.shape, b.shape[1]
    elif mode == 'nt':
        (M, K), N = a.shape, b.shape[0]
    else:
        (K, M), N = a.shape, b.shape[1]
    return _matmul(name, [(a, b, (0, 0), (0, 0))], mode, out_dtype, M, N, K, a2a=a2a)


def _ffn_up(name, h, wgu, gather=()):
    T, D = h.shape
    F = wgu.shape[1] // 2
    tm, tn = _tile(T, 1024), _tile(F, 512)
    grid = (T // tm, F // tn)
    nc = len(gather)

    def body(h_ref, wg_ref, wu_ref, *rest):
        cin, (g_ref, u_ref, a_ref), cout, sems = rest[:nc], rest[nc:nc + 3], rest[nc + 3:2 * nc + 3], rest[2 * nc + 3:]
        i, j = pl.program_id(0), pl.program_id(1)

        if nc:
            @pl.when(jnp.logical_and(i == 0, j == 0))
            def _():
                _gather2_start(cin, cout, *sems)

        hv = h_ref[...]
        g = _dot(hv, wg_ref[...], NN)
        u = _dot(hv, wu_ref[...], NN)
        g_ref[...] = g
        u_ref[...] = u
        a_ref[...] = (g * _sigmoid(g) * u).astype(BF16)

        if nc:
            @pl.when(jnp.logical_and(i == grid[0] - 1, j == grid[1] - 1))
            def _():
                _gather2_pass_on(cin, cout, *sems)
                _gather2_finish(cin, cout, *sems)

    any_spec = pl.BlockSpec(memory_space=pl.ANY)
    out = pl.BlockSpec((tm, tn), lambda i, j: (i, j))
    res = pl.pallas_call(
        body, name=name, grid=grid,
        in_specs=[pl.BlockSpec((tm, D), lambda i, j: (i, 0)),
                  pl.BlockSpec((D, tn), lambda i, j: (0, j)),
                  pl.BlockSpec((D, tn), lambda i, j, o=F // tn: (0, j + o))] + [any_spec] * nc,
        out_specs=[out, out, out] + [any_spec] * nc,
        out_shape=[jax.ShapeDtypeStruct((T, F), F32), jax.ShapeDtypeStruct((T, F), F32),
                   jax.ShapeDtypeStruct((T, F), BF16)] + _comm_out_shapes(gather, True),
        scratch_shapes=_gather2_sems(nc) if nc else [],
        compiler_params=_params(),
    )(h, wgu, wgu, *gather)
    return res[:3], res[3:]


def _ffn_dact(dy, wd, g, u):
    T, D = dy.shape
    F = wd.shape[0]
    tm, tn = _tile(T, 1024), _tile(F, 512)

    nsplit = 2 if tn % (2 * LANES) == 0 else 1

    def body(dy_ref, wd_ref, g_ref, u_ref, dg_ref, du_ref):
        dyv = dy_ref[...]
        for c in range(nsplit):
            cols = slice(c * (tn // nsplit), (c + 1) * (tn // nsplit))
            da = _dot(dyv, wd_ref[cols, :], NT)
            g = g_ref[:, cols]
            sg = _sigmoid(g)
            dg_ref[:, cols] = (da * u_ref[:, cols] * (sg * (1.0 + g * (1.0 - sg)))).astype(BF16)
            du_ref[:, cols] = (da * (g * sg)).astype(BF16)

    blk = pl.BlockSpec((tm, tn), lambda i, j: (i, j))
    return pl.pallas_call(
        body, name="ffn_dact", grid=(T // tm, F // tn),
        in_specs=[pl.BlockSpec((tm, D), lambda i, j: (i, 0)), pl.BlockSpec((tn, D), lambda i, j: (j, 0)), blk, blk],
        out_specs=[blk, blk],
        out_shape=[jax.ShapeDtypeStruct((T, F), BF16)] * 2,
        compiler_params=_params(),
    )(dy, wd, g, u)


def _rstd(v):
    return lax.rsqrt(jnp.mean(v * v, axis=-1, keepdims=True) + EPS)


def _pre_fwd(x, g, sc, sh):
    T, D = x.shape
    tr = _tile(T, ROW_TILE, 8)

    def body(x_ref, g_ref, sc_ref, sh_ref, h_ref):
        xv = x_ref[...]
        r = xv * _rstd(xv) * g_ref[...]
        h_ref[...] = (r * (1.0 + sc_ref[...]) + sh_ref[...]).astype(BF16)

    row = pl.BlockSpec((tr, D), lambda i: (i, 0))
    return pl.pallas_call(
        body, name="pre_fwd", grid=(T // tr,),
        in_specs=[row, _vec_spec(D), _vec_spec(D), _vec_spec(D)], out_specs=row,
        out_shape=jax.ShapeDtypeStruct((T, D), BF16), compiler_params=_params(),
    )(x, g, sc, sh)


def _post_fwd(x, y, g, gate):
    T, D = x.shape
    tr = _tile(T, ROW_TILE, 8)

    def body(x_ref, y_ref, g_ref, gate_ref, o_ref):
        yv = y_ref[...]
        o_ref[...] = x_ref[...] + gate_ref[...] * (yv * _rstd(yv) * g_ref[...])

    row = pl.BlockSpec((tr, D), lambda i: (i, 0))
    return pl.pallas_call(
        body, name="post_fwd", grid=(T // tr,),
        in_specs=[row, row, _vec_spec(D), _vec_spec(D)], out_specs=row,
        out_shape=jax.ShapeDtypeStruct((T, D), F32), compiler_params=_params(),
    )(x, y, g, gate)


def _post_bwd(dx, y, g, gate):
    T, D = dx.shape
    tr = _tile(T, ROW_TILE, 8)

    def body(dx_ref, y_ref, g_ref, gate_ref, dy_ref, dgate_ref, dg_ref):
        @pl.when(pl.program_id(0) == 0)
        def _():
            dgate_ref[...] = jnp.zeros_like(dgate_ref)
            dg_ref[...] = jnp.zeros_like(dg_ref)

        yv, dxv = y_ref[...], dx_ref[...]
        rstd = _rstd(yv)
        yh = yv * rstd
        dgate_ref[...] += jnp.sum(dxv * (yh * g_ref[...]), axis=0, keepdims=True)
        dn = dxv * gate_ref[...]
        dg_ref[...] += jnp.sum(dn * yh, axis=0, keepdims=True)
        dyh = dn * g_ref[...]
        dy_ref[...] = (rstd * (dyh - yh * jnp.mean(dyh * yh, axis=-1, keepdims=True))).astype(BF16)

    row = pl.BlockSpec((tr, D), lambda i: (i, 0))
    vec = jax.ShapeDtypeStruct((1, D), F32)
    return pl.pallas_call(
        body, name="post_bwd", grid=(T // tr,),
        in_specs=[row, row, _vec_spec(D), _vec_spec(D)], out_specs=[row, _vec_spec(D), _vec_spec(D)],
        out_shape=[jax.ShapeDtypeStruct((T, D), BF16), vec, vec], compiler_params=_params(),
    )(dx, y, g, gate)


def _pre_bwd(dh, x, g, sc, dx_res):
    T, D = x.shape
    tr = _tile(T, ROW_TILE, 8)

    def body(dh_ref, x_ref, g_ref, sc_ref, dxr_ref, dx_ref, dsh_ref, dsc_ref, dg_ref):
        @pl.when(pl.program_id(0) == 0)
        def _():
            dsh_ref[...] = jnp.zeros_like(dsh_ref)
            dsc_ref[...] = jnp.zeros_like(dsc_ref)
            dg_ref[...] = jnp.zeros_like(dg_ref)

        xv, dhv = x_ref[...], dh_ref[...]
        rstd = _rstd(xv)
        xh = xv * rstd
        dsh_ref[...] += jnp.sum(dhv, axis=0, keepdims=True)
        dsc_ref[...] += jnp.sum(dhv * (xh * g_ref[...]), axis=0, keepdims=True)
        dr = dhv * (1.0 + sc_ref[...])
        dg_ref[...] += jnp.sum(dr * xh, axis=0, keepdims=True)
        dxh = dr * g_ref[...]
        dx_ref[...] = dxr_ref[...] + rstd * (dxh - xh * jnp.mean(dxh * xh, axis=-1, keepdims=True))

    row = pl.BlockSpec((tr, D), lambda i: (i, 0))
    vec = jax.ShapeDtypeStruct((1, D), F32)
    return pl.pallas_call(
        body, name="pre_bwd", grid=(T // tr,),
        in_specs=[row, row, _vec_spec(D), _vec_spec(D), row],
        out_specs=[row, _vec_spec(D), _vec_spec(D), _vec_spec(D)],
        out_shape=[jax.ShapeDtypeStruct((T, D), F32), vec, vec, vec], compiler_params=_params(),
    )(dh, x, g, sc, dx_res)


def _loss(x, target):
    T, D = x.shape
    tr = _tile(T, ROW_TILE, 8)

    def body(x_ref, t_ref, dx_ref, l_ref):
        @pl.when(pl.program_id(0) == 0)
        def _():
            l_ref[...] = jnp.zeros_like(l_ref)

        e = x_ref[...] - t_ref[...]
        dx_ref[...] = e / D
        rows = jnp.sum(e * e, axis=-1, keepdims=True)
        l_ref[...] += jnp.broadcast_to(jnp.sum(rows, axis=0, keepdims=True), (1, LANES))

    row = pl.BlockSpec((tr, D), lambda i: (i, 0))
    return pl.pallas_call(
        body, name="loss", grid=(T // tr,),
        in_specs=[row, row], out_specs=[row, _vec_spec(LANES)],
        out_shape=[jax.ShapeDtypeStruct((T, D), F32), jax.ShapeDtypeStruct((1, LANES), F32)],
        compiler_params=_params(),
    )(x, target)


def _split3(x):
    hi = x.astype(BF16)
    r = x - hi.astype(F32)
    mid = r.astype(BF16)
    lo = (r - mid.astype(F32)).astype(BF16)
    return hi, mid, lo


def _tri_sum(tri, x):
    hi, mid, lo = _split3(x)
    return _dot(tri, hi, NN) + _dot(tri, mid, NN) + _dot(tri, lo, NN)


def _fox_gate_fwd(fg, bf, H):
    T = fg.shape[0]
    tb = _tile(T, 512)

    def body(fg_ref, bf_ref, cum_ref, rep_ref, carry):
        @pl.when(pl.program_id(0) == 0)
        def _():
            carry[...] = jnp.zeros_like(carry)

        z = fg_ref[...] + bf_ref[...]
        logf = jnp.minimum(z, 0.0) - jnp.log(1.0 + jnp.exp(-jnp.abs(z)))
        row = lax.broadcasted_iota(jnp.int32, (tb, tb), 0)
        col = lax.broadcasted_iota(jnp.int32, (tb, tb), 1)
        cum = _tri_sum((row >= col).astype(BF16), logf) + carry[...]
        cum_ref[...] = cum
        carry[...] = cum_ref[pl.ds(tb - 1, 1), :]
        lane = lax.broadcasted_iota(jnp.int32, (tb, LANES), 1)
        for h in range(H):
            colv = jnp.sum(jnp.where(lane == h, cum, 0.0), axis=-1, keepdims=True)
            rep_ref[h] = jnp.broadcast_to(colv, (tb, LANES))

    return pl.pallas_call(
        body, name="fox_gate_fwd", grid=(T // tb,),
        in_specs=[pl.BlockSpec((tb, LANES), lambda i: (i, 0)), _vec_spec(LANES)],
        out_specs=[pl.BlockSpec((tb, LANES), lambda i: (i, 0)), pl.BlockSpec((H, tb, LANES), lambda i: (0, i, 0))],
        out_shape=[jax.ShapeDtypeStruct((T, LANES), F32), jax.ShapeDtypeStruct((H, T, LANES), F32)],
        scratch_shapes=[pltpu.VMEM((1, LANES), F32)], compiler_params=_params(),
    )(fg, bf)


def _fox_gate_bwd(dcum, fg, bf):
    T = fg.shape[0]
    tb = _tile(T, 512)
    nb = T // tb

    def body(dc_ref, fg_ref, bf_ref, dfg_ref, dbf_ref, carry):
        @pl.when(pl.program_id(0) == 0)
        def _():
            carry[...] = jnp.zeros_like(carry)
            dbf_ref[...] = jnp.zeros_like(dbf_ref)

        row = lax.broadcasted_iota(jnp.int32, (tb, tb), 0)
        col = lax.broadcasted_iota(jnp.int32, (tb, tb), 1)
        dc = dc_ref[...]
        dlogf = _tri_sum((row <= col).astype(BF16), dc) + carry[...]
        z = fg_ref[...] + bf_ref[...]
        dfg = dlogf * _sigmoid(-z)
        dfg_ref[...] = dfg
        dbf_ref[...] += jnp.sum(dfg, axis=0, keepdims=True)
        carry[...] += jnp.sum(dc, axis=0, keepdims=True)

    rev = pl.BlockSpec((tb, LANES), lambda i: (nb - 1 - i, 0))
    return pl.pallas_call(
        body, name="fox_gate_bwd", grid=(nb,),
        in_specs=[rev, rev, _vec_spec(LANES)], out_specs=[rev, _vec_spec(LANES)],
        out_shape=[jax.ShapeDtypeStruct((T, LANES), F32), jax.ShapeDtypeStruct((1, LANES), F32)],
        scratch_shapes=[pltpu.VMEM((1, LANES), F32)], compiler_params=_params(),
    )(dcum, fg, bf)


def _fox_blocks(T, pref=512):
    tb = pref if T >= 2 * pref else BLOCK
    return tb, T // tb


def _fox_fwd(name, qkv, cq_rep, ck, gather=()):
    T = qkv.shape[0]
    D = qkv.shape[1] // 3
    H = D // FOX_DH
    hps = FOX_HPS
    ng, wl = H // hps, hps * FOX_DH
    tb, nb = _fox_blocks(T, FOX_FWD_BLOCK)
    pairs = [(i, j) for i in range(nb) for j in range(i + 1)]
    qi = np.array([p[0] for p in pairs], np.int32)
    kj = np.array([p[1] for p in pairs], np.int32)
    npairs = len(pairs)
    nrep = tb // LANES
    nc = len(gather)

    def body(qi_ref, kj_ref, q_ref, k_ref, v_ref, cq_ref, ck_ref, *rest):
        cin, (o_ref, obf_ref, lse_ref), cout = rest[:nc], rest[nc:nc + 3], rest[nc + 3:2 * nc + 3]
        m_sc, l_sc, acc_sc = rest[2 * nc + 3:2 * nc + 6]
        sems = rest[2 * nc + 6:]
        g, p = pl.program_id(0), pl.program_id(1)
        i, j = qi_ref[p], kj_ref[p]

        if nc:
            @pl.when(jnp.logical_and(g == 0, p == 0))
            def _():
                _gather2_start(cin, cout, *sems)

        @pl.when(j == 0)
        def _():
            m_sc[...] = jnp.full_like(m_sc, NEG_INF)
            l_sc[...] = jnp.zeros_like(l_sc)
            acc_sc[...] = jnp.zeros_like(acc_sc)

        row = lax.broadcasted_iota(jnp.int32, (tb, tb), 0)
        col = lax.broadcasted_iota(jnp.int32, (tb, tb), 1)
        visible = jnp.logical_or(j < i, row >= col)
        for hh in range(hps):
            cols = slice(hh * FOX_DH, (hh + 1) * FOX_DH)
            s = _dot(q_ref[:, cols], k_ref[:, cols], NT) * FOX_SCALE
            s = jnp.where(visible, s + _rep(cq_ref[hh], nrep) - ck_ref[hh], NEG_INF)
            m_prev = m_sc[hh]
            m_new = jnp.maximum(m_prev, jnp.max(s, axis=-1, keepdims=True))
            alpha = jnp.exp(m_prev - m_new)
            pm = jnp.exp(s - _rep(m_new, nrep))
            l_sc[hh] = alpha * l_sc[hh] + jnp.sum(pm, axis=-1, keepdims=True)
            acc_sc[:, cols] = alpha * acc_sc[:, cols] + _dot(pm.astype(BF16), v_ref[:, cols], NN)
            m_sc[hh] = m_new

        @pl.when(j == i)
        def _():
            for hh in range(hps):
                cols = slice(hh * FOX_DH, (hh + 1) * FOX_DH)
                o = acc_sc[:, cols] / l_sc[hh]
                o_ref[:, cols] = o
                obf_ref[:, cols] = o.astype(BF16)
                lse_ref[hh] = m_sc[hh] + jnp.log(l_sc[hh])

        if nc:
            @pl.when(jnp.logical_and(g == ng - 1, p == npairs - 1))
            def _():
                _gather2_pass_on(cin, cout, *sems)
                _gather2_finish(cin, cout, *sems)

    any_spec = pl.BlockSpec(memory_space=pl.ANY)
    qblk = pl.BlockSpec((tb, wl), lambda g, p, qi, kj: (qi[p], g))
    qrep = pl.BlockSpec((hps, tb, LANES), lambda g, p, qi, kj: (g, qi[p], 0))
    grid_spec = pltpu.PrefetchScalarGridSpec(
        num_scalar_prefetch=2, grid=(ng, npairs),
        in_specs=[qblk,
                  pl.BlockSpec((tb, wl), lambda g, p, qi, kj: (kj[p], ng + g)),
                  pl.BlockSpec((tb, wl), lambda g, p, qi, kj: (kj[p], 2 * ng + g)),
                  qrep,
                  pl.BlockSpec((hps, 1, tb), lambda g, p, qi, kj: (g, 0, kj[p]))] + [any_spec] * nc,
        out_specs=[qblk, qblk, qrep] + [any_spec] * nc,
        scratch_shapes=[pltpu.VMEM((hps, tb, LANES), F32), pltpu.VMEM((hps, tb, LANES), F32),
                        pltpu.VMEM((tb, wl), F32)] + (_gather2_sems(nc) if nc else []))
    res = pl.pallas_call(
        body, name=name, grid_spec=grid_spec,
        out_shape=[jax.ShapeDtypeStruct((T, D), F32), jax.ShapeDtypeStruct((T, D), BF16),
                   jax.ShapeDtypeStruct((H, T, LANES), F32)] + _comm_out_shapes(gather, True),
        compiler_params=_params(),
    )(qi, kj, qkv, qkv, qkv, cq_rep, ck, *gather)
    return res[:3], res[3:]


def _fox_bwd_prep(do, o, lse_rep, cq_rep):
    T, D = do.shape
    H = D // FOX_DH
    tr = _tile(T, ROW_TILE, 8)

    def body(do_ref, o_ref, lse_ref, cq_ref, dob_ref, delta_ref, cql_ref):
        dov = do_ref[...]
        dob_ref[...] = dov.astype(BF16)
        prod = dov * o_ref[...]
        for h in range(H):
            d = jnp.sum(prod[:, h * FOX_DH:(h + 1) * FOX_DH], axis=-1, keepdims=True)
            delta_ref[h] = jnp.broadcast_to(d, (tr, LANES))
        cql_ref[...] = cq_ref[...] - lse_ref[...]

    row = pl.BlockSpec((tr, D), lambda i: (i, 0))
    rep = pl.BlockSpec((H, tr, LANES), lambda i: (0, i, 0))
    return pl.pallas_call(
        body, name="fox_bwd_prep", grid=(T // tr,),
        in_specs=[row, row, rep, rep], out_specs=[row, rep, rep],
        out_shape=[jax.ShapeDtypeStruct((T, D), BF16), jax.ShapeDtypeStruct((H, T, LANES), F32),
                   jax.ShapeDtypeStruct((H, T, LANES), F32)],
        compiler_params=_params(),
    )(do, o, lse_rep, cq_rep)


def _fox_bwd(name, qkv, dob, delta_rep, cql_rep, ck, a2a=()):
    T = qkv.shape[0]
    D = qkv.shape[1] // 3
    H = D // FOX_DH
    hps = FOX_HPS
    ng, wl = H // hps, hps * FOX_DH
    tk, nbk = _fox_blocks(T, FOX_BWD_KEYS)
    tq = max(_fox_blocks(T, FOX_BWD_QUERIES)[0], tk)
    nbq, ratio = T // tq, tq // tk
    pairs = [(i, j) for j in range(nbk) for i in range(j // ratio, nbq)]
    qi = np.array([p[0] for p in pairs], np.int32)
    kj = np.array([p[1] for p in pairs], np.int32)
    npairs = len(pairs)
    nrep = tk // LANES
    nc = len(a2a)

    def body(qi_ref, kj_ref, q_ref, k_ref, v_ref, do_ref, delta_ref, cql_ref, ck_ref, *rest):
        cin, (dq_ref, dk_ref, dv_ref, dck_ref, dcq_ref), cout = rest[:nc], rest[nc:nc + 5], rest[nc + 5:2 * nc + 5]
        dq_acc, dk_acc, dv_acc, dc_acc = rest[2 * nc + 5:2 * nc + 9]
        sems = rest[2 * nc + 9:]
        g, p = pl.program_id(0), pl.program_id(1)
        i, j = qi_ref[p], kj_ref[p]

        @pl.when(jnp.logical_and(g == 0, p == 0))
        def _():
            dcq_ref[...] = jnp.zeros_like(dcq_ref)
            if nc:
                _comm_start(cin, cout, False, *sems)

        @pl.when(p == 0)
        def _():
            dq_acc[...] = jnp.zeros_like(dq_acc)

        @pl.when(i == j // ratio)
        def _():
            dk_acc[...] = jnp.zeros_like(dk_acc)
            dv_acc[...] = jnp.zeros_like(dv_acc)
            dc_acc[...] = jnp.zeros_like(dc_acc)

        row = lax.broadcasted_iota(jnp.int32, (tq, tk), 0)
        col = lax.broadcasted_iota(jnp.int32, (tq, tk), 1)
        visible = jnp.logical_or((j + 1) * tk <= i * tq, row + i * tq >= col + j * tk)
        lane = lax.broadcasted_iota(jnp.int32, (tq, LANES), 1)
        rows = pl.ds(pl.multiple_of(i * tq, tq), tq)
        dcq = jnp.zeros((tq, LANES), F32)
        for hh in range(hps):
            cols = slice(hh * FOX_DH, (hh + 1) * FOX_DH)
            q, k, v, dov = q_ref[:, cols], k_ref[:, cols], v_ref[:, cols], do_ref[:, cols]
            s = _dot(q, k, NT) * FOX_SCALE + _rep(cql_ref[hh], nrep) - ck_ref[hh]
            pm = jnp.exp(jnp.where(visible, s, NEG_INF))
            dv_acc[:, cols] += _dot(pm.astype(BF16), dov, TN)
            ds = pm * (_dot(dov, v, NT) - _rep(delta_ref[hh], nrep))
            dsb = (ds * FOX_SCALE).astype(BF16)
            dk_acc[:, cols] += _dot(dsb, q, TN)
            dq_acc[rows, cols] += _dot(dsb, k, NN)
            dc_acc[hh] -= jnp.sum(ds, axis=0, keepdims=True)
            dcq = dcq + jnp.where(lane == g * hps + hh, jnp.sum(ds, axis=-1, keepdims=True), 0.0)
        dcq_ref[rows, :] += dcq

        @pl.when(i == nbq - 1)
        def _():
            dk_ref[...] = dk_acc[...].astype(BF16)
            dv_ref[...] = dv_acc[...].astype(BF16)
            dck_ref[...] = dc_acc[...]

        @pl.when(p == npairs - 1)
        def _():
            dq_ref[...] = dq_acc[...].astype(BF16)

        if nc:
            @pl.when(jnp.logical_and(g == ng - 1, p == npairs - 1))
            def _():
                _comm_wait(cin, cout, False, *sems)

    any_spec = pl.BlockSpec(memory_space=pl.ANY)
    qblk = pl.BlockSpec((tq, wl), lambda g, p, qi, kj: (qi[p], g))
    qrep = pl.BlockSpec((hps, tq, LANES), lambda g, p, qi, kj: (g, qi[p], 0))
    kblk = pl.BlockSpec((tk, wl), lambda g, p, qi, kj: (kj[p], g))
    krow = pl.BlockSpec((hps, 1, tk), lambda g, p, qi, kj: (g, 0, kj[p]))
    grid_spec = pltpu.PrefetchScalarGridSpec(
        num_scalar_prefetch=2, grid=(ng, npairs),
        in_specs=[qblk,
                  pl.BlockSpec((tk, wl), lambda g, p, qi, kj: (kj[p], ng + g)),
                  pl.BlockSpec((tk, wl), lambda g, p, qi, kj: (kj[p], 2 * ng + g)),
                  qblk, qrep, qrep, krow] + [any_spec] * nc,
        out_specs=[pl.BlockSpec((T, wl), lambda g, p, qi, kj: (0, g)), kblk, kblk, krow,
                   pl.BlockSpec((T, LANES), lambda g, p, qi, kj: (0, 0))] + [any_spec] * nc,
        scratch_shapes=[pltpu.VMEM((T, wl), F32), pltpu.VMEM((tk, wl), F32), pltpu.VMEM((tk, wl), F32),
                        pltpu.VMEM((hps, 1, tk), F32)] + (_comm_sems(nc) if nc else []))
    act = jax.ShapeDtypeStruct((T, D), BF16)
    res = pl.pallas_call(
        body, name=name, grid_spec=grid_spec,
        out_shape=[act, act, act, jax.ShapeDtypeStruct((H, 1, T), F32),
                   jax.ShapeDtypeStruct((T, LANES), F32)] + _comm_out_shapes(a2a, False),
        compiler_params=_params(),
    )(qi, kj, qkv, qkv, qkv, dob, delta_rep, cql_rep, ck, *a2a)
    return res[:5], res[5:]


def _sgu_rows(T):
    return 2 * BLOCK if T % (2 * BLOCK) == 0 else BLOCK


def _sgu_norm(zv, g_ref, b_ref):
    vv = _gelu(zv)
    mu = jnp.mean(vv, axis=-1, keepdims=True)
    cen = vv - mu
    rstd = lax.rsqrt(jnp.mean(cen * cen, axis=-1, keepdims=True) + EPS)
    vh = cen * rstd
    return vh, rstd, vh * g_ref[...] + b_ref[...]


def _sgu_fwd(zpre, ln_g, ln_b, wsm, bs_rep):
    T = zpre.shape[0]
    W = zpre.shape[1] // 2
    G = W // BLOCK
    tr = _sgu_rows(T)

    def body(z_ref, g_ref, b_ref, ws_ref, bs_ref, o_ref):
        u = _gelu(z_ref[:, :W])
        _, _, vln = _sgu_norm(z_ref[:, W:], g_ref, b_ref)
        for c in range(tr // BLOCK):
            rows = slice(c * BLOCK, (c + 1) * BLOCK)
            for gi in range(G):
                cols = slice(gi * BLOCK, (gi + 1) * BLOCK)
                f = _dot(ws_ref[gi], vln[rows, cols].astype(BF16), NN) + bs_ref[gi]
                o_ref[rows, cols] = (u[rows, cols] * f).astype(BF16)

    full3 = pl.BlockSpec((G, BLOCK, BLOCK), lambda i: (0, 0, 0))
    return pl.pallas_call(
        body, name="sgu_fwd", grid=(T // tr,),
        in_specs=[pl.BlockSpec((tr, 2 * W), lambda i: (i, 0)), _vec_spec(W), _vec_spec(W), full3, full3],
        out_specs=pl.BlockSpec((tr, W), lambda i: (i, 0)),
        out_shape=jax.ShapeDtypeStruct((T, W), BF16), compiler_params=_params(),
    )(zpre, ln_g, ln_b, wsm, bs_rep)


def _sgu_bwd(dgt, zpre, ln_g, ln_b, wsm, wsmT, bs_rep):
    T = zpre.shape[0]
    W = zpre.shape[1] // 2
    G = W // BLOCK
    tr = BLOCK

    def body(dgt_ref, z_ref, g_ref, b_ref, ws_ref, wst_ref, bs_ref,
             dz_ref, dws_ref, dbs_ref, dlg_ref, dlb_ref, du_sc, dvln_sc):
        @pl.when(pl.program_id(0) == 0)
        def _():
            dws_ref[...] = jnp.zeros_like(dws_ref)
            dbs_ref[...] = jnp.zeros_like(dbs_ref)
            dlg_ref[...] = jnp.zeros_like(dlg_ref)
            dlb_ref[...] = jnp.zeros_like(dlb_ref)

        zu = z_ref[:, :W]
        zv = z_ref[:, W:]
        u = _gelu(zu)
        vh, rstd, vln = _sgu_norm(zv, g_ref, b_ref)
        dgtv = dgt_ref[...]
        trow = lax.broadcasted_iota(jnp.int32, (BLOCK, BLOCK), 0)
        tcol = lax.broadcasted_iota(jnp.int32, (BLOCK, BLOCK), 1)
        causal = trow >= tcol
        for c in range(tr // BLOCK):
            rows = slice(c * BLOCK, (c + 1) * BLOCK)
            for gi in range(G):
                cols = slice(gi * BLOCK, (gi + 1) * BLOCK)
                vb = vln[rows, cols].astype(BF16)
                f = _dot(ws_ref[gi], vb, NN) + bs_ref[gi]
                d = dgtv[rows, cols]
                du_sc[rows, cols] = d * f
                df = d * u[rows, cols]
                dfb = df.astype(BF16)
                dvln_sc[rows, cols] = _dot(wst_ref[gi], dfb, NN)
                dws_ref[gi] += jnp.where(causal, _dot(dfb, vb, NT), 0.0)
                dbs_ref[gi] += jnp.broadcast_to(jnp.sum(df, axis=-1, keepdims=True), (BLOCK, BLOCK))
        dvln = dvln_sc[...]
        dlg_ref[...] += jnp.sum(dvln * vh, axis=0, keepdims=True)
        dlb_ref[...] += jnp.sum(dvln, axis=0, keepdims=True)
        dvh = dvln * g_ref[...]
        dvv = rstd * (dvh - jnp.mean(dvh, axis=-1, keepdims=True)
                      - vh * jnp.mean(dvh * vh, axis=-1, keepdims=True))
        dz_ref[:, :W] = (du_sc[...] * _gelu_grad(zu)).astype(BF16)
        dz_ref[:, W:] = (dvv * _gelu_grad(zv)).astype(BF16)

    full3 = pl.BlockSpec((G, BLOCK, BLOCK), lambda i: (0, 0, 0))
    vec = jax.ShapeDtypeStruct((1, W), F32)
    acc3 = jax.ShapeDtypeStruct((G, BLOCK, BLOCK), F32)
    return pl.pallas_call(
        body, name="sgu_bwd", grid=(T // tr,),
        in_specs=[pl.BlockSpec((tr, W), lambda i: (i, 0)), pl.BlockSpec((tr, 2 * W), lambda i: (i, 0)),
                  _vec_spec(W), _vec_spec(W), full3, full3, full3],
        out_specs=[pl.BlockSpec((tr, 2 * W), lambda i: (i, 0)), full3, full3, _vec_spec(W), _vec_spec(W)],
        out_shape=[jax.ShapeDtypeStruct((T, 2 * W), BF16), acc3, acc3, vec, vec],
        scratch_shapes=[pltpu.VMEM((tr, W), F32), pltpu.VMEM((tr, W), F32)],
        compiler_params=_params(),
    )(dgt, zpre, ln_g, ln_b, wsm, wsmT, bs_rep)


def _rope(x, cos_t, sin_t):
    T, N = x.shape
    tr = _tile(T, ROW_TILE, 8)
    nrep = N // LANES
    half = ROPE_DIM // 2

    def body(x_ref, c_ref, s_ref, o_ref):
        xv = x_ref[...]
        lane = jnp.bitwise_and(lax.broadcasted_iota(jnp.int32, (tr, N), 1), SWA_DH - 1)
        partner = jnp.where(lane < half, -pltpu.roll(xv, N - half, 1), pltpu.roll(xv, half, 1))
        o_ref[...] = (xv * _rep(c_ref[...], nrep) + partner * _rep(s_ref[...], nrep)).astype(BF16)

    tab = pl.BlockSpec((tr, LANES), lambda i: (i, 0))
    row = pl.BlockSpec((tr, N), lambda i: (i, 0))
    return pl.pallas_call(
        body, name="rope", grid=(T // tr,), in_specs=[row, tab, tab], out_specs=row,
        out_shape=jax.ShapeDtypeStruct((T, N), BF16), compiler_params=_params(),
    )(x, cos_t, sin_t)


def _swa_tiles(T):
    sb = 4 if T >= 2048 else 2
    return sb, BLOCK * sb, T // (BLOCK * sb)


def _band_mask():
    row = lax.broadcasted_iota(jnp.int32, (BLOCK, 2 * BLOCK), 0)
    col = lax.broadcasted_iota(jnp.int32, (BLOCK, 2 * BLOCK), 1)
    return jnp.logical_and(col > row, col <= row + BLOCK), col


def _swa_specs(T, G):
    sb, tq, nq = _swa_tiles(T)
    q = pl.BlockSpec((G, tq, LANES), lambda h, i: (h, i, 0))
    kc = pl.BlockSpec((None, tq, LANES), lambda h, i: (h, i, 0))
    kp = pl.BlockSpec((None, BLOCK, LANES), lambda h, i: (h, jnp.maximum(i * sb - 1, 0), 0))
    return q, kc, kp


def _swa_band(b, i, kc_ref, kp_ref, vc_ref, vp_ref):
    rows = slice(b * BLOCK, (b + 1) * BLOCK)
    prev = slice((b - 1) * BLOCK, b * BLOCK)
    kprev = kp_ref[...] if b == 0 else kc_ref[prev, :]
    vprev = vp_ref[...] if b == 0 else vc_ref[prev, :]
    K = jnp.concatenate([kprev, kc_ref[rows, :]], axis=0)
    V = jnp.concatenate([vprev, vc_ref[rows, :]], axis=0)
    band, col = _band_mask()
    if b == 0:
        band = jnp.logical_and(band, jnp.logical_or(col >= BLOCK, i > 0))
    return rows, K, V, band


def _stack_heads(ref, rows, G):
    return jnp.concatenate([ref[g, rows, :] for g in range(G)], axis=0)


def _swa_fwd(qp, kp, vp, sinks):
    Hq, T, _ = qp.shape
    Hk = kp.shape[0]
    G = Hq // Hk
    sb, tq, nq = _swa_tiles(T)

    def body(sink_ref, q_ref, kc_ref, kp_ref, vc_ref, vp_ref, o_ref, lse_ref):
        h, i = pl.program_id(0), pl.program_id(1)
        head_of_row = lax.broadcasted_iota(jnp.int32, (G * BLOCK, 1), 0) // BLOCK
        sink = jnp.zeros((G * BLOCK, 1), F32)
        for g in range(G):
            sink = jnp.where(head_of_row == g, sink_ref[h * G + g], sink)
        for b in range(sb):
            rows, K, V, band = _swa_band(b, i, kc_ref, kp_ref, vc_ref, vp_ref)
            band = jnp.concatenate([band] * G, axis=0)
            s = jnp.where(band, _dot(_stack_heads(q_ref, rows, G), K, NT) * SWA_SCALE, NEG_INF)
            m = jnp.maximum(jnp.max(s, axis=-1, keepdims=True), sink)
            pm = jnp.exp(s - m)
            den = jnp.sum(pm, axis=-1, keepdims=True) + jnp.exp(sink - m)
            o = _dot((pm / den).astype(BF16), V, NN)
            lse = jnp.broadcast_to(m + jnp.log(den), (G * BLOCK, LANES))
            for g in range(G):
                o_ref[g, rows, :] = o[g * BLOCK:(g + 1) * BLOCK]
                lse_ref[g, rows, :] = lse[g * BLOCK:(g + 1) * BLOCK]

    q, kc, kpv = _swa_specs(T, G)
    out = jax.ShapeDtypeStruct((Hq, T, LANES), F32)
    return pl.pallas_call(
        body, name="swa_fwd", grid=(Hk, nq),
        in_specs=[pl.BlockSpec(memory_space=pltpu.SMEM), q, kc, kpv, kc, kpv], out_specs=[q, q],
        out_shape=[out, out], compiler_params=_params(),
    )(sinks, qp, kp, kp, vp, vp)


def _swa_bwd_dq(qp, kp, vp, dop, op, lse_rep, sinks):
    Hq, T, _ = qp.shape
    Hk = kp.shape[0]
    G = Hq // Hk
    sb, tq, nq = _swa_tiles(T)

    def body(sink_ref, q_ref, kc_ref, kp_ref, vc_ref, vp_ref, do_ref, o_ref, lse_ref, dq_ref, dsink_ref):
        h, i = pl.program_id(0), pl.program_id(1)

        @pl.when(i == 0)
        def _():
            dsink_ref[...] = jnp.zeros_like(dsink_ref)

        for b in range(sb):
            rows, K, V, band = _swa_band(b, i, kc_ref, kp_ref, vc_ref, vp_ref)
            band = jnp.concatenate([band] * G, axis=0)
            dov = _stack_heads(do_ref, rows, G)
            delta = jnp.sum(dov * _stack_heads(o_ref, rows, G), axis=-1, keepdims=True)
            lse = _stack_heads(lse_ref, rows, G)
            s = jnp.where(band, _dot(_stack_heads(q_ref, rows, G), K, NT) * SWA_SCALE, NEG_INF)
            pm = jnp.exp(s - _rep(lse, 2))
            ds = pm * (_dot(dov.astype(BF16), V, NT) - delta)
            dq = _dot((ds * SWA_SCALE).astype(BF16), K, NN)
            for g in range(G):
                head = slice(g * BLOCK, (g + 1) * BLOCK)
                dq_ref[g, rows, :] = dq[head]
                part = jnp.sum(jnp.exp(sink_ref[h * G + g] - lse[head]) * delta[head], axis=0, keepdims=True)
                dsink_ref[g] -= jnp.broadcast_to(part, (8, LANES))

    q, kc, kpv = _swa_specs(T, G)
    return pl.pallas_call(
        body, name="swa_bwd_dq", grid=(Hk, nq),
        in_specs=[pl.BlockSpec(memory_space=pltpu.SMEM), q, kc, kpv, kc, kpv, q, q, q],
        out_specs=[q, pl.BlockSpec((G, 8, LANES), lambda h, i: (h, 0, 0))],
        out_shape=[jax.ShapeDtypeStruct((Hq, T, LANES), F32), jax.ShapeDtypeStruct((Hq, 8, LANES), F32)],
        compiler_params=_params(),
    )(sinks, qp, kp, kp, vp, vp, dop, op, lse_rep)


def _swa_bwd_dkv(qp, kp, vp, dop, op, lse_rep):
    Hq, T, _ = qp.shape
    Hk = kp.shape[0]
    G = Hq // Hk
    sb, tq, nq = _swa_tiles(T)
    nblk = T // BLOCK

    def body(k_ref, v_ref, q_ref, qn_ref, do_ref, don_ref, o_ref, on_ref, lse_ref, lsen_ref, dk_ref, dv_ref):
        i = pl.program_id(1)
        trow = lax.broadcasted_iota(jnp.int32, (2 * BLOCK, BLOCK), 0)
        scol = lax.broadcasted_iota(jnp.int32, (2 * BLOCK, BLOCK), 1)
        band0 = jnp.logical_and(trow >= scol, trow < scol + BLOCK)
        for b in range(sb):
            rows = slice(b * BLOCK, (b + 1) * BLOCK)
            nxt = slice((b + 1) * BLOCK, (b + 2) * BLOCK)
            last = b == sb - 1
            band = band0
            if last:
                band = jnp.logical_and(band0, jnp.logical_or(trow < BLOCK, i < nq - 1))
            kb, vb = k_ref[rows, :], v_ref[rows, :]

            def bands(cur, nx):
                return jnp.concatenate([piece for g in range(G)
                                        for piece in (cur[g, rows, :], nx[g] if last else cur[g, nxt, :])], axis=0)

            Q, dov, lse = bands(q_ref, qn_ref), bands(do_ref, don_ref), bands(lse_ref, lsen_ref)
            delta = jnp.sum(dov * bands(o_ref, on_ref), axis=-1, keepdims=True)
            s = jnp.where(jnp.concatenate([band] * G, axis=0), _dot(Q, kb, NT) * SWA_SCALE, NEG_INF)
            pm = jnp.exp(s - lse)
            dob = dov.astype(BF16)
            dv_ref[rows, :] = _dot(pm.astype(BF16), dob, TN)
            ds = pm * (_dot(dob, vb, NT) - delta)
            dk_ref[rows, :] = _dot((ds * SWA_SCALE).astype(BF16), Q, TN)

    kspec = pl.BlockSpec((None, tq, LANES), lambda h, i: (h, i, 0))
    cur = pl.BlockSpec((G, tq, LANES), lambda h, i: (h, i, 0))
    nxt = pl.BlockSpec((G, BLOCK, LANES), lambda h, i: (h, jnp.minimum((i + 1) * sb, nblk - 1), 0))
    out = jax.ShapeDtypeStruct((Hk, T, LANES), F32)
    return pl.pallas_call(
        body, name="swa_bwd_dkv", grid=(Hk, nq),
        in_specs=[kspec, kspec, cur, nxt, cur, nxt, cur, nxt, cur, nxt], out_specs=[kspec, kspec],
        out_shape=[out, out], compiler_params=_params(),
    )(kp, vp, qp, qp, dop, dop, op, op, lse_rep, lse_rep)


def _to_heads(a, nh):
    T = a.shape[0]
    a = a.reshape(T, nh, SWA_DH).transpose(1, 0, 2)
    return jnp.pad(a, ((0, 0), (0, 0), (0, LANES - SWA_DH)))


def _from_heads(a):
    nh, T, _ = a.shape
    return a[:, :, :SWA_DH].transpose(1, 0, 2).reshape(T, nh * SWA_DH)


def _adam(g, w, m, v):
    m2 = ADAM_B1 * m + (1.0 - ADAM_B1) * g
    v2 = ADAM_B2 * v + (1.0 - ADAM_B2) * (g * g)
    m_hat = m2 / (1.0 - ADAM_B1 ** ADAM_STEP)
    v_hat = v2 / (1.0 - ADAM_B2 ** ADAM_STEP)
    delta = -ADAM_LR * (m_hat / (jnp.sqrt(v_hat) + ADAM_EPS) + ADAM_WD * w)
    return delta, m2, v2


def _ada_fwd(c_all, w, b):
    L, D, n = w.shape
    tn = _tile(n, 768)

    def body(c_ref, w_ref, b_ref, o_ref):
        cv = c_ref[...]
        ca = (cv * _sigmoid(cv)).astype(BF16)
        o_ref[...] = _dot(ca, w_ref[...].astype(BF16), NN) + b_ref[...]

    return pl.pallas_call(
        body, name="ada_fwd", grid=(L, n // tn),
        in_specs=[pl.BlockSpec((NDEV, D), lambda l, j: (0, 0)), pl.BlockSpec((None, D, tn), lambda l, j: (l, 0, j)),
                  pl.BlockSpec((None, 1, tn), lambda l, j: (l, 0, j))],
        out_specs=pl.BlockSpec((None, NDEV, tn), lambda l, j: (l, 0, j)),
        out_shape=jax.ShapeDtypeStruct((L, NDEV, n), F32), compiler_params=_params(),
    )(c_all, w, b)


def _ada_update(c_rep, dm, w, m, v):
    L, D, n = w.shape
    tr = _tile(D, 256, 8)
    nrep = n // LANES

    def body(c_ref, dm_ref, w_ref, m_ref, v_ref, g_ref, d_ref, m2_ref, v2_ref):
        g = jnp.zeros((tr, n), F32)
        for b in range(NDEV):
            cv = c_ref[b]
            g = g + _rep(cv * _sigmoid(cv), nrep) * dm_ref[pl.ds(b, 1), :]
        g_ref[...] = g
        d_ref[...], m2_ref[...], v2_ref[...] = _adam(g, w_ref[...], m_ref[...], v_ref[...])

    blk = pl.BlockSpec((None, tr, n), lambda l, i: (l, i, 0))
    out = jax.ShapeDtypeStruct((L, D, n), F32)
    return pl.pallas_call(
        body, name="ada_update", grid=(L, D // tr),
        in_specs=[pl.BlockSpec((NDEV, tr, LANES), lambda l, i: (0, i, 0)),
                  pl.BlockSpec((None, NDEV, n), lambda l, i: (l, 0, 0)), blk, blk, blk],
        out_specs=[blk, blk, blk, blk], out_shape=[out, out, out, out], compiler_params=_params(),
    )(c_rep, dm, w, m, v)


def _adamw(name, parts, w, m, v, layer, stacked=None):
    P, R, C = parts.shape
    Lw = w.shape[0]
    cpad = -(-C // LANES) * LANES
    per_row = cpad * (P * parts.dtype.itemsize + 7 * 4) * 2
    tr = _tile(R, max(8, (24 * 1024 * 1024 // per_row) // 8 * 8), 8)
    if stacked is None:
        stacked = [lax.empty((Lw, R, C), F32) for _ in range(4)]

    def body(p_ref, w_ref, m_ref, v_ref, *rest):
        g_ref, d_ref, m2_ref, v2_ref = rest[4:]
        g = p_ref[0].astype(F32)
        for s in range(1, P):
            g = g + p_ref[s].astype(F32)
        g_ref[...] = g
        d_ref[...], m2_ref[...], v2_ref[...] = _adam(g, w_ref[...], m_ref[...], v_ref[...])

    stk = pl.BlockSpec((None, tr, C), lambda i: (layer, i, 0))
    any_spec = pl.BlockSpec(memory_space=pl.ANY)
    out = jax.ShapeDtypeStruct((Lw, R, C), F32)
    return pl.pallas_call(
        body, name=name, grid=(R // tr,),
        in_specs=[pl.BlockSpec((P, tr, C), lambda i: (0, i, 0)), stk, stk, stk] + [any_spec] * 4,
        out_specs=[stk, stk, stk, stk], out_shape=[out, out, out, out],
        input_output_aliases={4: 0, 5: 1, 6: 2, 7: 3}, compiler_params=_params(),
    )(parts, w, m, v, *stacked)


def _colcat(a):
    s, k, n = a.shape
    return a.transpose(1, 0, 2).reshape(k, s * n)


def _colsplit(a):
    k, n8 = a.shape
    return a.reshape(k, NDEV, n8 // NDEV).transpose(1, 0, 2)


def _rowsplit(a):
    r, c = a.shape
    return a.reshape(NDEV, r // NDEV, c)


def kernel(x, c, positions, ada_w, ada_b, mix_pre_g, mix_post_g, ffn_pre_g, ffn_post_g, ffn_w_gu, ffn_w_down, fox_w_in, fox_b_f, fox_w_out, sgu_w_in, sgu_ln_g, sgu_ln_b, sgu_w_s, sgu_b_s, sgu_w_out, swa_w_in, swa_sinks, swa_w_out, loss_target, m_ada_w, m_ada_b, m_mix_pre_g, m_mix_post_g, m_ffn_pre_g, m_ffn_post_g, m_ffn_w_gu, m_ffn_w_down, m_fox_w_in, m_fox_b_f, m_fox_w_out, m_sgu_w_in, m_sgu_ln_g, m_sgu_ln_b, m_sgu_w_s, m_sgu_b_s, m_sgu_w_out, m_swa_w_in, m_swa_sinks, m_swa_w_out, v_ada_w, v_ada_b, v_mix_pre_g, v_mix_post_g, v_ffn_pre_g, v_ffn_post_g, v_ffn_w_gu, v_ffn_w_down, v_fox_w_in, v_fox_b_f, v_fox_w_out, v_sgu_w_in, v_sgu_ln_g, v_sgu_ln_b, v_sgu_w_s, v_sgu_b_s, v_sgu_w_out, v_swa_w_in, v_swa_sinks, v_swa_w_out):
    env = locals()
    W = {n: env[n] for n in WEIGHTS}
    M = {n: env["m_" + n] for n in WEIGHTS}
    V = {n: env["v_" + n] for n in WEIGHTS}

    me = 4 * lax.axis_index("x") + 2 * lax.axis_index("y") + lax.axis_index("c")
    _, T, D = x.shape
    L = ada_w.shape[0]
    n_ada = ada_w.shape[2]
    F = ffn_w_gu.shape[2] * NDEV // 2
    H = D // FOX_DH
    Hq = D // SWA_DH
    x0 = x.reshape(T, D)
    mixer = {0: 'fox', 1: 'sgu', 2: 'swa'}

    c_all = _comm("gather_c", [c], 'gather')[0].reshape(NDEV, D)
    ada_b_mine = lax.dynamic_slice_in_dim(ada_b, me * n_ada, n_ada, axis=1).reshape(L, 1, n_ada)
    mod_cols = _ada_fwd(c_all, ada_w, ada_b_mine)
    mod = _comm("a2a_mod", [mod_cols.transpose(1, 0, 2)], 'a2a')[0]
    mod = mod.transpose(1, 0, 2).reshape(L, 6, 1, D)

    inv = ROPE_THETA ** (-jnp.arange(0, ROPE_DIM, 2, dtype=F32) / ROPE_DIM)
    ang = positions[0].astype(F32)[:, None] * inv
    pad1 = jnp.ones((T, SWA_DH - ROPE_DIM), F32)
    cos64 = jnp.concatenate([jnp.cos(ang), jnp.cos(ang), pad1], axis=1)
    sin64 = jnp.concatenate([jnp.sin(ang), jnp.sin(ang), 0.0 * pad1], axis=1)
    cos_t = jnp.concatenate([cos64, cos64], axis=1)
    sin_t = jnp.concatenate([sin64, sin64], axis=1)

    fox_layers = [i for i in range(L) if mixer[i % 3] == 'fox']
    assert fox_layers and fox_layers[0] == 0

    def slice_of(i, role):
        kind, j = mixer[i % 3], i // 3
        src = {'wgu': (ffn_w_gu, i), 'wd': (ffn_w_down, i), 'win': (W[kind + '_w_in'], j), 'wout': (W[kind + '_w_out'], j)}[role]
        return src[0][src[1]].astype(BF16)

    def nparams(key):
        return int(np.prod(slice_of(*key).shape)) * NDEV

    def gather_plan():
        plan = {}
        for f in fox_layers:
            later = [i for i in fox_layers if i > f]
            stop = later[0] if later else L
            keys = [(f, 'wout'), (f, 'wgu'), (f, 'wd')]
            for i in range(f + 1, stop):
                keys += [(i, 'wgu'), (i, 'wd'), (i, 'win'), (i, 'wout')]
            if later:
                keys += [(stop, 'win')]
            near = [k for k in keys if k[0] <= f + 1]
            far = [k for k in keys if k[0] > f + 1]
            plan[('fox', f)] = near
            for i in range(f, stop):
                take, total = [], 0
                while far and (not take or total + nparams(far[0]) <= HOSTED_GATHER_PARAMS):
                    total += nparams(far[0])
                    take.append(far.pop(0))
                plan[('ffn', i)] = take
            assert not far
        return plan

    plan = gather_plan()
    raw, full = {}, {}
    first_keys = [(0, 'win')]
    raw.update(zip(first_keys, _gather2("gather_first", [slice_of(*k) for k in first_keys])))

    def wget(i, role):
        if (i, role) not in full:
            got = raw[(i, role)]
            full[(i, role)] = _colcat(got) if role in ('wgu', 'win') else got.reshape(-1, D)
        return full[(i, role)]

    saved = []
    xc = x0
    for i in range(L):
        kind, j = mixer[i % 3], i // 3
        s = dict(x_in=xc)
        sh_m, sc_m, g_m, sh_f, sc_f, g_f = [mod[i, t] for t in range(6)]
        h = _pre_fwd(xc, mix_pre_g[i:i + 1], sc_m, sh_m)
        s['h'] = h
        if kind == 'fox':
            wqkv = wget(i, 'win')[:, :3 * D]
            wf = jnp.pad(wget(i, 'win')[:, 3 * D:], ((0, 0), (0, LANES - H)))
            s['win_pad'] = jnp.concatenate([wqkv, wf], axis=1)
            bf = jnp.pad(fox_b_f[j:j + 1], ((0, 0), (0, LANES - H)))
            qkv = _mm("fox_qkv", h, wqkv, 'nn', BF16)
            fg = _mm("fox_fg", h, wf, 'nn', F32)
            cum, cq_rep = _fox_gate_fwd(fg, bf, H)
            ck = cum[:, :H].T.reshape(H, 1, T)
            keys = plan[('fox', i)]
            (o, obf, lse_rep), got = _fox_fwd("fox_fwd%d" % j, qkv, cq_rep, ck, gather=[slice_of(*k) for k in keys])
            raw.update(zip(keys, got))
            s.update(qkv=qkv, fg=fg, bf=bf, cq_rep=cq_rep, ck=ck, o=o, lse_rep=lse_rep, mix_out=obf)
        elif kind == 'sgu':
            G = D // BLOCK
            causal = jnp.tril(jnp.ones((BLOCK, BLOCK), bool))
            wsm = jnp.where(causal[None], sgu_w_s[j], 0.0).astype(BF16)
            bs_rep = jnp.broadcast_to(sgu_b_s[j][:, :, None], (G, BLOCK, BLOCK))
            zpre = _mm("sgu_in", h, wget(i, 'win'), 'nn', F32)
            gated = _sgu_fwd(zpre, sgu_ln_g[j:j + 1], sgu_ln_b[j:j + 1], wsm, bs_rep)
            s.update(zpre=zpre, wsm=wsm, bs_rep=bs_rep, mix_out=gated)
        else:
            Hk = (wget(i, 'win').shape[1] // SWA_DH - Hq) // 2
            proj = _mm("swa_in", h, wget(i, 'win'), 'nn', F32)
            qr = _rope(proj[:, :Hq * SWA_DH], cos_t, sin_t)
            kr = _rope(proj[:, Hq * SWA_DH:(Hq + Hk) * SWA_DH], cos_t, sin_t)
            qp, kp = _to_heads(qr, Hq), _to_heads(kr, Hk)
            vp = _to_heads(proj[:, (Hq + Hk) * SWA_DH:].astype(BF16), Hk)
            op, lse_rep = _swa_fwd(qp, kp, vp, swa_sinks[j])
            s.update(qp=qp, kp=kp, vp=vp, op=op, lse_rep=lse_rep, Hk=Hk, mix_out=_from_heads(op).astype(BF16))
        y = _mm("mix_out", s['mix_out'], wget(i, 'wout'), 'nn', F32)
        x_mid = _post_fwd(xc, y, mix_post_g[i:i + 1], g_m)
        s.update(y_mix=y, x_mid=x_mid)
        h2 = _pre_fwd(x_mid, ffn_pre_g[i:i + 1], sc_f, sh_f)
        keys = plan[('ffn', i)]
        (g, u, a), got = _ffn_up("ffn_up_g%d" % i if keys else "ffn_up", h2, wget(i, 'wgu'), gather=[slice_of(*k) for k in keys])
        raw.update(zip(keys, got))
        y2 = _mm("ffn_down", a, wget(i, 'wd'), 'nn', F32)
        xc = _post_fwd(x_mid, y2, ffn_post_g[i:i + 1], g_f)
        s.update(h2=h2, g=g, u=u, a=a, y_ffn=y2)
        saved.append(s)

    dx, lsum = _loss(xc, loss_target.reshape(T, D))
    loss = lax.psum(0.5 * lsum[0, 0] / D, AXES)

    small = {n: [None] * W[n].shape[0] for n in SMALL}
    dmod = [None] * L
    pending = []
    stacks = {}

    def update(items, recv):
        for (name, idx, _), parts in zip(items, recv):
            stacks[name] = _adamw("adamw_" + name, parts, W[name], M[name], V[name], idx, stacked=stacks.get(name))

    for i in reversed(range(L)):
        kind, j = mixer[i % 3], i // 3
        s = saved[i]
        sh_m, sc_m, g_m, sh_f, sc_f, g_f = [mod[i, t] for t in range(6)]
        dy2, dg_f, dpost_f = _post_bwd(dx, s['y_ffn'], ffn_post_g[i:i + 1], g_f)
        dwd = _mm("ffn_dwd", s['a'], dy2, 'tn', BF16)
        dg, du = _ffn_dact(dy2, wget(i, 'wd'), s['g'], s['u'])
        n_gu = 2 * F // NDEV
        tm_gu, _, tk_gu = _matmul_tiles('tn', D, n_gu, T, 1, 2)
        dwgu = None
        for half, (name, d_act) in enumerate((("ffn_dwg", dg), ("ffn_dwu", du))):
            dwgu = _matmul(name, [(s['h2'], d_act, (0, 0), (0, 0))], 'tn', BF16, D, F, T, tm=tm_gu, tn=n_gu, tk=tk_gu,
                           slots=(NDEV, half * (NDEV // 2), dwgu))
        pending += [('ffn_w_gu', i, dwgu), ('ffn_w_down', i, _rowsplit(dwd))]
        take, total = [], 0
        while i < L - 1 and pending and (not take or total + pending[0][2].size <= HOSTED_A2A_PARAMS):
            total += pending[0][2].size
            take.append(pending.pop(0))
        dh_pairs = [(dg, wget(i, 'wgu'), (0, 0), (0, 0)), (du, wget(i, 'wgu'), (0, 0), (0, F))]
        if take:
            dh2, recv = _matmul("ffn_dh_x%d" % i, dh_pairs, 'nt', F32, T, D, F, a2a=[item[2] for item in take])
            update(take, recv)
        else:
            dh2 = _matmul("ffn_dh", dh_pairs, 'nt', F32, T, D, F)
        dx, dsh_f, dsc_f, dpre_f = _pre_bwd(dh2, s['x_mid'], ffn_pre_g[i:i + 1], sc_f, dx)
        dy, dg_m, dpost_m = _post_bwd(dx, s['y_mix'], mix_post_g[i:i + 1], g_m)
        dwout = _mm("mix_dwout", s['mix_out'], dy, 'tn', BF16)
        pending.append((kind + '_w_out', j, _rowsplit(dwout)))
        dmix = _mm("mix_dout", dy, wget(i, 'wout'), 'nt', F32)
        if kind == 'fox':
            dob, delta_rep, cql_rep = _fox_bwd_prep(dmix, s['o'], s['lse_rep'], s['cq_rep'])
            (dq, dk, dv, dck, dcq), recv = _fox_bwd("fox_bwd%d" % j, s['qkv'], dob, delta_rep, cql_rep, s['ck'],
                                                    a2a=[item[2] for item in pending])
            update(pending, recv)
            pending = []
            dcum = jnp.pad(dck.reshape(H, T).T, ((0, 0), (0, LANES - H))) + dcq
            dfg, dbf = _fox_gate_bwd(dcum, s['fg'], s['bf'])
            small['fox_b_f'][j] = dbf[0, :H]
            dproj = jnp.concatenate([dq, dk, dv, dfg.astype(BF16)], axis=1)
            dwin = _mm("fox_dwin", s['h'], dproj, 'tn', BF16)[:, :3 * D + H]
            pending.append((kind + '_w_in', j, _colsplit(dwin)))
            if i == 0:
                dh, recv = _mm("fox_dh_last", dproj, s['win_pad'], 'nt', F32, a2a=[item[2] for item in pending])
                update(pending, recv)
                pending = []
            else:
                dh = _mm("fox_dh", dproj, s['win_pad'], 'nt', F32)
        elif kind == 'sgu':
            wsmT = s['wsm'].transpose(0, 2, 1)
            dz, dws, dbs, dlg, dlb = _sgu_bwd(dmix, s['zpre'], sgu_ln_g[j:j + 1], sgu_ln_b[j:j + 1], s['wsm'], wsmT, s['bs_rep'])
            small['sgu_w_s'][j], small['sgu_b_s'][j] = dws, dbs[:, :, 0]
            small['sgu_ln_g'][j], small['sgu_ln_b'][j] = dlg[0], dlb[0]
            dwin = _mm("sgu_dwin", s['h'], dz, 'tn', BF16)
            pending.append((kind + '_w_in', j, _colsplit(dwin)))
            dh = _mm("sgu_dh", dz, wget(i, 'win'), 'nt', F32)
        else:
            Hk = s['Hk']
            dop = _to_heads(dmix, Hq)
            dqp, dsink = _swa_bwd_dq(s['qp'], s['kp'], s['vp'], dop, s['op'], s['lse_rep'], swa_sinks[j])
            dkp, dvp = _swa_bwd_dkv(s['qp'], s['kp'], s['vp'], dop, s['op'], s['lse_rep'])
            small['swa_sinks'][j] = dsink[:, 0, 0]
            dproj = jnp.concatenate([_rope(_from_heads(dqp), cos_t, -sin_t), _rope(_from_heads(dkp), cos_t, -sin_t),
                                     _from_heads(dvp).astype(BF16)], axis=1)
            dwin = _mm("swa_dwin", s['h'], dproj, 'tn', BF16)
            pending.append((kind + '_w_in', j, _colsplit(dwin)))
            dh = _mm("swa_dh", dproj, wget(i, 'win'), 'nt', F32)
        dx, dsh_m, dsc_m, dpre_m = _pre_bwd(dh, s['x_in'], mix_pre_g[i:i + 1], sc_m, dx)
        small['mix_pre_g'][i], small['mix_post_g'][i] = dpre_m[0], dpost_m[0]
        small['ffn_pre_g'][i], small['ffn_post_g'][i] = dpre_f[0], dpost_f[0]
        dmod[i] = jnp.concatenate([dsh_m, dsc_m, dg_m, dsh_f, dsc_f, dg_f], axis=1)[0]

    if pending:
        update(pending, _comm("a2a_last", [item[2] for item in pending], 'a2a'))
    grad_x = dx.reshape(1, T, D)

    small['ada_b'] = dmod
    flat = jnp.concatenate([jnp.stack(small[n]).reshape(-1) for n in SMALL])
    width = 8 * LANES
    npad = -flat.shape[0] % (8 * width)
    packed = jnp.pad(flat, (0, npad)).reshape(-1, width)
    parts = _comm("gather_small", [packed], 'gather')[0]

    def pack(d):
        f = jnp.concatenate([d[n].reshape(-1) for n in SMALL])
        return jnp.pad(f, (0, npad)).reshape(1, -1, width)

    res = _adamw("adamw_small", parts, pack(W), pack(M), pack(V), 0)
    off = 0
    for n in SMALL:
        size = W[n].size
        stacks[n] = [val.reshape(-1)[off:off + size].reshape(W[n].shape) for val in res]
        off += size

    dmod_all = parts.reshape(NDEV, -1)[:, :L * 6 * D].reshape(NDEV, L, 6 * D)
    dm = lax.dynamic_slice_in_dim(dmod_all, me * n_ada, n_ada, axis=2).transpose(1, 0, 2)
    c_rep = jnp.broadcast_to(c_all[:, :, None], (NDEV, D, LANES))
    stacks['ada_w'] = _ada_update(c_rep, dm, ada_w, m_ada_w, v_ada_w)

    return (loss, grad_x, *[stacks[n][t] for t in range(4) for n in WEIGHTS])
```

```python
import numpy as np
import jax
import jax.numpy as jnp
from jax import lax
from jax.experimental import pallas as pl
from jax.experimental.pallas import tpu as pltpu

F32 = jnp.float32
BF16 = jnp.bfloat16
NDEV = 8
AXES = ("x", "y", "c")
LANES = 128
VMEM_LIMIT_BYTES = 48 * 1024 * 1024
NEG_INF = float("-inf")
ROW_TILE = 256

EPS = 1e-6
BLOCK = 128
FOX_DH = 128
FOX_HPS = 2
FOX_FWD_BLOCK = 1024
FOX_BWD_QUERIES = 1024
FOX_BWD_KEYS = 512
HOSTED_GATHER_PARAMS = 36 * 2 ** 20
HOSTED_A2A_PARAMS = 22 * 2 ** 20
SWA_DH = 64
ROPE_DIM = 16
ROPE_THETA = 500000.0
FOX_SCALE = FOX_DH ** -0.5
SWA_SCALE = SWA_DH ** -0.5
GELU_C0 = 0.7978845608028654
GELU_C1 = 0.044715

ADAM_LR = 0.001
ADAM_B1 = 0.9
ADAM_B2 = 0.999
ADAM_EPS = 1e-08
ADAM_WD = 0.01
ADAM_STEP = 10

NN = (((1,), (0,)), ((), ()))
NT = (((1,), (1,)), ((), ()))
TN = (((0,), (0,)), ((), ()))

WEIGHTS = ['ada_w', 'ada_b', 'mix_pre_g', 'mix_post_g', 'ffn_pre_g', 'ffn_post_g', 'ffn_w_gu', 'ffn_w_down',
           'fox_w_in', 'fox_b_f', 'fox_w_out', 'sgu_w_in', 'sgu_ln_g', 'sgu_ln_b', 'sgu_w_s', 'sgu_b_s',
           'sgu_w_out', 'swa_w_in', 'swa_sinks', 'swa_w_out']
SMALL = ['ada_b', 'mix_pre_g', 'mix_post_g', 'ffn_pre_g', 'ffn_post_g', 'fox_b_f', 'sgu_ln_g', 'sgu_ln_b',
         'sgu_w_s', 'sgu_b_s', 'swa_sinks']


def _dot(a, b, dims):
    return lax.dot_general(a, b, dims, preferred_element_type=F32)


def _tile(n, pref, mult=LANES):
    t = (min(pref, n) // mult) * mult
    while t >= mult:
        if n % t == 0:
            return t
        t -= mult
    return n


def _params():
    return pltpu.CompilerParams(vmem_limit_bytes=VMEM_LIMIT_BYTES)


def _rep(a, n):
    return a if n == 1 else jnp.concatenate([a] * n, axis=-1)


def _vec_spec(d):
    return pl.BlockSpec((1, d), lambda *_: (0, 0))


def _sigmoid(z):
    return 1.0 / (1.0 + jnp.exp(-z))


def _gelu(z):
    t = jnp.tanh(GELU_C0 * (z + GELU_C1 * z * z * z))
    return 0.5 * z * (1.0 + t)


def _gelu_grad(z):
    t = jnp.tanh(GELU_C0 * (z + GELU_C1 * z * z * z))
    return 0.5 * (1.0 + t) + 0.5 * z * (1.0 - t * t) * GELU_C0 * (1.0 + 3.0 * GELU_C1 * z * z)


def _comm_out_shapes(arrs, gather):
    return [jax.ShapeDtypeStruct(((NDEV,) + a.shape) if gather else a.shape, a.dtype) for a in arrs]


def _comm_sems(n):
    return [pltpu.SemaphoreType.DMA((n,)), pltpu.SemaphoreType.DMA((n,)), pltpu.SemaphoreType.DMA((n,))]


def _me():
    x, y, c = lax.axis_index("x"), lax.axis_index("y"), lax.axis_index("c")
    return x, y, c, 4 * x + 2 * y + c


def _comm_start(ins, outs, gather, send_sems, recv_sems, local_sems):
    x, y, c, me = _me()
    for a in range(len(ins)):
        pltpu.make_async_copy(ins[a] if gather else ins[a].at[me], outs[a].at[me], local_sems.at[a]).start()
        for bits in range(1, NDEV):
            px = (1 - x) if bits & 4 else x
            py = (1 - y) if bits & 2 else y
            pc = (1 - c) if bits & 1 else c
            pltpu.make_async_remote_copy(
                src_ref=ins[a] if gather else ins[a].at[4 * px + 2 * py + pc], dst_ref=outs[a].at[me],
                send_sem=send_sems.at[a], recv_sem=recv_sems.at[a],
                device_id=(px, py, pc), device_id_type=pl.DeviceIdType.MESH).start()


def _comm_wait(ins, outs, gather, send_sems, recv_sems, local_sems):
    x, y, c, me = _me()
    for a in range(len(ins)):
        seven = outs[a].at[pl.ds(0, NDEV - 1)]
        pltpu.make_async_remote_copy(src_ref=seven, dst_ref=seven, send_sem=send_sems.at[a], recv_sem=recv_sems.at[a],
                                     device_id=(x, y, c), device_id_type=pl.DeviceIdType.MESH).wait()
        pltpu.make_async_copy(ins[a] if gather else ins[a].at[me], outs[a].at[me], local_sems.at[a]).wait()


def _comm(name, arrs, kind):
    n = len(arrs)
    gather = kind == 'gather'

    def body(*refs):
        ins, outs, sems = refs[:n], refs[n:2 * n], refs[2 * n:]
        _comm_start(ins, outs, gather, *sems)
        _comm_wait(ins, outs, gather, *sems)

    any_spec = pl.BlockSpec(memory_space=pl.ANY)
    return pl.pallas_call(
        body, name=name, out_shape=_comm_out_shapes(arrs, gather),
        in_specs=[any_spec] * n, out_specs=[any_spec] * n, scratch_shapes=_comm_sems(n),
    )(*arrs)


def _gather2_sems(n):
    return [pltpu.SemaphoreType.DMA((n,)) for _ in range(4)]


def _remote(src, dst, send_sem, recv_sem, device):
    return pltpu.make_async_remote_copy(src_ref=src, dst_ref=dst, send_sem=send_sem, recv_sem=recv_sem,
                                        device_id=device, device_id_type=pl.DeviceIdType.MESH)


def _gather2_start(ins, outs, send_sems, ici_sems, d2d_sems, local_sems):
    x, y, c, me = _me()
    for a in range(len(ins)):
        pltpu.make_async_copy(ins[a], outs[a].at[me], local_sems.at[a]).start()
        _remote(ins[a], outs[a].at[me], send_sems.at[a], d2d_sems.at[a], (x, y, 1 - c)).start()
        for px, py in ((1 - x, y), (x, 1 - y), (1 - x, 1 - y)):
            _remote(ins[a], outs[a].at[me], send_sems.at[a], ici_sems.at[a], (px, py, c)).start()


def _gather2_pass_on(ins, outs, send_sems, ici_sems, d2d_sems, local_sems):
    x, y, c, me = _me()
    for a in range(len(ins)):
        three = outs[a].at[pl.ds(0, 3)]
        _remote(three, three, send_sems.at[a], ici_sems.at[a], (x, y, c)).wait_recv()
        for px, py in ((1 - x, y), (x, 1 - y), (1 - x, 1 - y)):
            slot = outs[a].at[4 * px + 2 * py + c]
            _remote(slot, slot, send_sems.at[a], d2d_sems.at[a], (x, y, 1 - c)).start()


def _gather2_finish(ins, outs, send_sems, ici_sems, d2d_sems, local_sems):
    x, y, c, me = _me()
    for a in range(len(ins)):
        four, seven = outs[a].at[pl.ds(0, 4)], outs[a].at[pl.ds(0, NDEV - 1)]
        _remote(four, four, send_sems.at[a], d2d_sems.at[a], (x, y, c)).wait_recv()
        _remote(seven, seven, send_sems.at[a], d2d_sems.at[a], (x, y, c)).wait_send()
        pltpu.make_async_copy(ins[a], outs[a].at[me], local_sems.at[a]).wait()


def _gather2(name, arrs):
    n = len(arrs)

    def body(*refs):
        ins, outs, sems = refs[:n], refs[n:2 * n], refs[2 * n:]
        _gather2_start(ins, outs, *sems)
        _gather2_pass_on(ins, outs, *sems)
        _gather2_finish(ins, outs, *sems)

    any_spec = pl.BlockSpec(memory_space=pl.ANY)
    return pl.pallas_call(
        body, name=name, out_shape=_comm_out_shapes(arrs, True),
        in_specs=[any_spec] * n, out_specs=[any_spec] * n, scratch_shapes=_gather2_sems(n),
    )(*arrs)


MATMUL_VMEM_BUDGET = 38 * 1024 * 1024


def _matmul_tiles(mode, M, N, K, npairs, out_size):
    def uniq(vals):
        return sorted(set(vals), reverse=True)

    tms = uniq(_tile(M, p) for p in (1024, 512, 256))
    tns = uniq(_tile(N, p) for p in (1536, 1024, 512, 256))
    tks = uniq(_tile(K, p) for p in (2048, 1024)) if mode == 'tn' else [K] + uniq(_tile(K, p) for p in (2048, 1024))
    for tk in tks:
        best = None
        for tm in tms:
            for tn in tns:
                steps = K // tk
                need = (2 * npairs * (tm + tn) * tk * 2 + 2 * tm * tn * out_size + npairs * tm * tn * 4
                        + (tm * tn * 4 if steps > 1 else 0) + (tk * tm * 2 if mode == 'tn' else 0))
                if need <= MATMUL_VMEM_BUDGET and (best is None or (tm * tn, tm) > (best[0] * best[1], best[0])):
                    best = (tm, tn, tk)
        if best is not None:
            return best
    return _tile(M, 256), _tile(N, 256), _tile(K, 512)


def _matmul(name, pairs, mode, out_dtype, M, N, K, tm=None, tn=None, tk=None, a2a=(), slots=None):
    if not (tm and tn and tk):
        tm, tn, tk = _matmul_tiles(mode, M, N, K, len(pairs), jnp.dtype(out_dtype).itemsize)
    nk = K // tk
    dims = {'nn': NN, 'nt': NT, 'tn': TN}[mode]
    in_specs, ops = [], []
    for a, b, ao, bo in pairs:
        if mode == 'tn':
            assert ao[0] % tk == 0 and ao[1] % tm == 0
            sa = pl.BlockSpec((tk, tm), lambda i, j, k, r=ao[0] // tk, c=ao[1] // tm: (k + r, i + c))
        else:
            assert ao[0] % tm == 0 and ao[1] % tk == 0
            sa = pl.BlockSpec((tm, tk), lambda i, j, k, r=ao[0] // tm, c=ao[1] // tk: (i + r, k + c))
        if mode == 'nt':
            assert bo[0] % tn == 0 and bo[1] % tk == 0
            sb = pl.BlockSpec((tn, tk), lambda i, j, k, r=bo[0] // tn, c=bo[1] // tk: (j + r, k + c))
        else:
            assert bo[0] % tk == 0 and bo[1] % tn == 0
            sb = pl.BlockSpec((tk, tn), lambda i, j, k, r=bo[0] // tk, c=bo[1] // tn: (k + r, j + c))
        in_specs += [sa, sb]
        ops += [a, b]
    npairs = len(pairs)
    nc = len(a2a)
    grid = (M // tm, N // tn, nk)
    earlier = [slots[2]] if slots and slots[2] is not None else []
    first_out = 2 * npairs + nc + len(earlier)

    def body(*refs):
        cin, o_ref, cout = refs[2 * npairs:2 * npairs + nc], refs[first_out], refs[first_out + 1:first_out + nc + 1]
        scratch = refs[first_out + nc + 1:]
        sems = scratch[1:] if nk > 1 else scratch
        i, j, k = pl.program_id(0), pl.program_id(1), pl.program_id(2)

        if nc:
            @pl.when(jnp.logical_and(jnp.logical_and(i == 0, j == 0), k == 0))
            def _():
                _comm_start(cin, cout, False, *sems)

        part = _dot(refs[0][...], refs[1][...], dims)
        for p in range(1, npairs):
            part = part + _dot(refs[2 * p][...], refs[2 * p + 1][...], dims)
        if nk == 1:
            o_ref[...] = part.astype(out_dtype)
        else:
            acc = scratch[0]

            @pl.when(k == 0)
            def _():
                acc[...] = part

            @pl.when(k > 0)
            def _():
                acc[...] += part

            @pl.when(k == nk - 1)
            def _():
                o_ref[...] = acc[...].astype(out_dtype)

        if nc:
            @pl.when(jnp.logical_and(jnp.logical_and(i == grid[0] - 1, j == grid[1] - 1), k == nk - 1))
            def _():
                _comm_wait(cin, cout, False, *sems)

    any_spec = pl.BlockSpec(memory_space=pl.ANY)
    if slots:
        out_spec = pl.BlockSpec((None, tm, tn), lambda i, j, k, base=slots[1]: (j + base, i, 0))
        out_shape = jax.ShapeDtypeStruct((slots[0], M, tn), out_dtype)
    else:
        out_spec = pl.BlockSpec((tm, tn), lambda i, j, k: (i, j))
        out_shape = jax.ShapeDtypeStruct((M, N), out_dtype)
    res = pl.pallas_call(
        body, name=name, grid=grid,
        in_specs=in_specs + [any_spec] * (nc + len(earlier)),
        out_specs=[out_spec] + [any_spec] * nc,
        out_shape=[out_shape] + _comm_out_shapes(a2a, False),
        scratch_shapes=([] if nk == 1 else [pltpu.VMEM((tm, tn), F32)]) + (_comm_sems(nc) if nc else []),
        input_output_aliases={first_out - 1: 0} if earlier else {},
        compiler_params=_params(),
    )(*ops, *a2a, *earlier)
    return (res[0], res[1:]) if nc else res[0]


def _mm(name, a, b, mode, out_dtype, a2a=()):
    if mode == 'nn':
        (M, K), N = a.shape, b.shape[1]
    elif mode == 'nt':
        (M, K), N = a.shape, b.shape[0]
    else:
        (K, M), N = a.shape, b.shape[1]
    return _matmul(name, [(a, b, (0, 0), (0, 0))], mode, out_dtype, M, N, K, a2a=a2a)


def _ffn_up(name, h, wgu, gather=()):
    T, D = h.shape
    F = wgu.shape[1] // 2
    tm, tn = _tile(T, 1024), _tile(F, 512)
    grid = (T // tm, F // tn)
    nc = len(gather)

    def body(h_ref, wg_ref, wu_ref, *rest):
        cin, (g_ref, u_ref, a_ref), cout, sems = rest[:nc], rest[nc:nc + 3], rest[nc + 3:2 * nc + 3], rest[2 * nc + 3:]
        i, j = pl.program_id(0), pl.program_id(1)

        if nc:
            @pl.when(jnp.logical_and(i == 0, j == 0))
            def _():
                _gather2_start(cin, cout, *sems)

        hv = h_ref[...]
        g = _dot(hv, wg_ref[...], NN)
        u = _dot(hv, wu_ref[...], NN)
        g_ref[...] = g
        u_ref[...] = u
        a_ref[...] = (g * _sigmoid(g) * u).astype(BF16)

        if nc:
            @pl.when(jnp.logical_and(i == grid[0] - 1, j == grid[1] - 1))
            def _():
                _gather2_pass_on(cin, cout, *sems)
                _gather2_finish(cin, cout, *sems)

    any_spec = pl.BlockSpec(memory_space=pl.ANY)
    out = pl.BlockSpec((tm, tn), lambda i, j: (i, j))
    res = pl.pallas_call(
        body, name=name, grid=grid,
        in_specs=[pl.BlockSpec((tm, D), lambda i, j: (i, 0)),
                  pl.BlockSpec((D, tn), lambda i, j: (0, j)),
                  pl.BlockSpec((D, tn), lambda i, j, o=F // tn: (0, j + o))] + [any_spec] * nc,
        out_specs=[out, out, out] + [any_spec] * nc,
        out_shape=[jax.ShapeDtypeStruct((T, F), F32), jax.ShapeDtypeStruct((T, F), F32),
                   jax.ShapeDtypeStruct((T, F), BF16)] + _comm_out_shapes(gather, True),
        scratch_shapes=_gather2_sems(nc) if nc else [],
        compiler_params=_params(),
    )(h, wgu, wgu, *gather)
    return res[:3], res[3:]


def _ffn_dact(dy, wd, g, u):
    T, D = dy.shape
    F = wd.shape[0]
    tm, tn = _tile(T, 1024), _tile(F, 512)

    nsplit = 2 if tn % (2 * LANES) == 0 else 1

    def body(dy_ref, wd_ref, g_ref, u_ref, dg_ref, du_ref):
        dyv = dy_ref[...]
        for c in range(nsplit):
            cols = slice(c * (tn // nsplit), (c + 1) * (tn // nsplit))
            da = _dot(dyv, wd_ref[cols, :], NT)
            g = g_ref[:, cols]
            sg = _sigmoid(g)
            dg_ref[:, cols] = (da * u_ref[:, cols] * (sg * (1.0 + g * (1.0 - sg)))).astype(BF16)
            du_ref[:, cols] = (da * (g * sg)).astype(BF16)

    blk = pl.BlockSpec((tm, tn), lambda i, j: (i, j))
    return pl.pallas_call(
        body, name="ffn_dact", grid=(T // tm, F // tn),
        in_specs=[pl.BlockSpec((tm, D), lambda i, j: (i, 0)), pl.BlockSpec((tn, D), lambda i, j: (j, 0)), blk, blk],
        out_specs=[blk, blk],
        out_shape=[jax.ShapeDtypeStruct((T, F), BF16)] * 2,
        compiler_params=_params(),
    )(dy, wd, g, u)


def _rstd(v):
    return lax.rsqrt(jnp.mean(v * v, axis=-1, keepdims=True) + EPS)


def _pre_fwd(x, g, sc, sh):
    T, D = x.shape
    tr = _tile(T, ROW_TILE, 8)

    def body(x_ref, g_ref, sc_ref, sh_ref, h_ref):
        xv = x_ref[...]
        r = xv * _rstd(xv) * g_ref[...]
        h_ref[...] = (r * (1.0 + sc_ref[...]) + sh_ref[...]).astype(BF16)

    row = pl.BlockSpec((tr, D), lambda i: (i, 0))
    return pl.pallas_call(
        body, name="pre_fwd", grid=(T // tr,),
        in_specs=[row, _vec_spec(D), _vec_spec(D), _vec_spec(D)], out_specs=row,
        out_shape=jax.ShapeDtypeStruct((T, D), BF16), compiler_params=_params(),
    )(x, g, sc, sh)


def _post_fwd(x, y, g, gate):
    T, D = x.shape
    tr = _tile(T, ROW_TILE, 8)

    def body(x_ref, y_ref, g_ref, gate_ref, o_ref):
        yv = y_ref[...]
        o_ref[...] = x_ref[...] + gate_ref[...] * (yv * _rstd(yv) * g_ref[...])

    row = pl.BlockSpec((tr, D), lambda i: (i, 0))
    return pl.pallas_call(
        body, name="post_fwd", grid=(T // tr,),
        in_specs=[row, row, _vec_spec(D), _vec_spec(D)], out_specs=row,
        out_shape=jax.ShapeDtypeStruct((T, D), F32), compiler_params=_params(),
    )(x, y, g, gate)


def _post_pre_fwd(x, y, g_post, gate, g_pre, sc, sh):
    T, D = x.shape
    tr = _tile(T, ROW_TILE, 8)

    def body(x_ref, y_ref, gp_ref, gate_ref, g_ref, sc_ref, sh_ref, o_ref, h_ref):
        yv = y_ref[...]
        xo = x_ref[...] + gate_ref[...] * (yv * _rstd(yv) * gp_ref[...])
        o_ref[...] = xo
        r = xo * _rstd(xo) * g_ref[...]
        h_ref[...] = (r * (1.0 + sc_ref[...]) + sh_ref[...]).astype(BF16)

    row = pl.BlockSpec((tr, D), lambda i: (i, 0))
    vec = _vec_spec(D)
    return pl.pallas_call(
        body, name="post_pre_fwd", grid=(T // tr,),
        in_specs=[row, row, vec, vec, vec, vec, vec], out_specs=[row, row],
        out_shape=[jax.ShapeDtypeStruct((T, D), F32), jax.ShapeDtypeStruct((T, D), BF16)], compiler_params=_params(),
    )(x, y, g_post, gate, g_pre, sc, sh)


def _post_bwd(dx, y, g, gate):
    T, D = dx.shape
    tr = _tile(T, ROW_TILE, 8)

    def body(dx_ref, y_ref, g_ref, gate_ref, dy_ref, dgate_ref, dg_ref):
        @pl.when(pl.program_id(0) == 0)
        def _():
            dgate_ref[...] = jnp.zeros_like(dgate_ref)
            dg_ref[...] = jnp.zeros_like(dg_ref)

        yv, dxv = y_ref[...], dx_ref[...]
        rstd = _rstd(yv)
        yh = yv * rstd
        dgate_ref[...] += jnp.sum(dxv * (yh * g_ref[...]), axis=0, keepdims=True)
        dn = dxv * gate_ref[...]
        dg_ref[...] += jnp.sum(dn * yh, axis=0, keepdims=True)
        dyh = dn * g_ref[...]
        dy_ref[...] = (rstd * (dyh - yh * jnp.mean(dyh * yh, axis=-1, keepdims=True))).astype(BF16)

    row = pl.BlockSpec((tr, D), lambda i: (i, 0))
    vec = jax.ShapeDtypeStruct((1, D), F32)
    return pl.pallas_call(
        body, name="post_bwd", grid=(T // tr,),
        in_specs=[row, row, _vec_spec(D), _vec_spec(D)], out_specs=[row, _vec_spec(D), _vec_spec(D)],
        out_shape=[jax.ShapeDtypeStruct((T, D), BF16), vec, vec], compiler_params=_params(),
    )(dx, y, g, gate)


def _pre_bwd(dh, x, g, sc, dx_res):
    T, D = x.shape
    tr = _tile(T, ROW_TILE, 8)

    def body(dh_ref, x_ref, g_ref, sc_ref, dxr_ref, dx_ref, dsh_ref, dsc_ref, dg_ref):
        @pl.when(pl.program_id(0) == 0)
        def _():
            dsh_ref[...] = jnp.zeros_like(dsh_ref)
            dsc_ref[...] = jnp.zeros_like(dsc_ref)
            dg_ref[...] = jnp.zeros_like(dg_ref)

        xv, dhv = x_ref[...], dh_ref[...]
        rstd = _rstd(xv)
        xh = xv * rstd
        dsh_ref[...] += jnp.sum(dhv, axis=0, keepdims=True)
        dsc_ref[...] += jnp.sum(dhv * (xh * g_ref[...]), axis=0, keepdims=True)
        dr = dhv * (1.0 + sc_ref[...])
        dg_ref[...] += jnp.sum(dr * xh, axis=0, keepdims=True)
        dxh = dr * g_ref[...]
        dx_ref[...] = dxr_ref[...] + rstd * (dxh - xh * jnp.mean(dxh * xh, axis=-1, keepdims=True))

    row = pl.BlockSpec((tr, D), lambda i: (i, 0))
    vec = jax.ShapeDtypeStruct((1, D), F32)
    return pl.pallas_call(
        body, name="pre_bwd", grid=(T // tr,),
        in_specs=[row, row, _vec_spec(D), _vec_spec(D), row],
        out_specs=[row, _vec_spec(D), _vec_spec(D), _vec_spec(D)],
        out_shape=[jax.ShapeDtypeStruct((T, D), F32), vec, vec, vec], compiler_params=_params(),
    )(dh, x, g, sc, dx_res)


def _loss(x, target):
    T, D = x.shape
    tr = _tile(T, ROW_TILE, 8)

    def body(x_ref, t_ref, dx_ref, l_ref):
        @pl.when(pl.program_id(0) == 0)
        def _():
            l_ref[...] = jnp.zeros_like(l_ref)

        e = x_ref[...] - t_ref[...]
        dx_ref[...] = e / D
        rows = jnp.sum(e * e, axis=-1, keepdims=True)
        l_ref[...] += jnp.broadcast_to(jnp.sum(rows, axis=0, keepdims=True), (1, LANES))

    row = pl.BlockSpec((tr, D), lambda i: (i, 0))
    return pl.pallas_call(
        body, name="loss", grid=(T // tr,),
        in_specs=[row, row], out_specs=[row, _vec_spec(LANES)],
        out_shape=[jax.ShapeDtypeStruct((T, D), F32), jax.ShapeDtypeStruct((1, LANES), F32)],
        compiler_params=_params(),
    )(x, target)


def _split3(x):
    hi = x.astype(BF16)
    r = x - hi.astype(F32)
    mid = r.astype(BF16)
    lo = (r - mid.astype(F32)).astype(BF16)
    return hi, mid, lo


def _tri_sum(tri, x):
    hi, mid, lo = _split3(x)
    return _dot(tri, hi, NN) + _dot(tri, mid, NN) + _dot(tri, lo, NN)


def _fox_gate_fwd(fg, bf, H):
    T = fg.shape[0]
    tb = _tile(T, 512)

    def body(fg_ref, bf_ref, cum_ref, rep_ref, carry):
        @pl.when(pl.program_id(0) == 0)
        def _():
            carry[...] = jnp.zeros_like(carry)

        z = fg_ref[...] + bf_ref[...]
        logf = jnp.minimum(z, 0.0) - jnp.log(1.0 + jnp.exp(-jnp.abs(z)))
        row = lax.broadcasted_iota(jnp.int32, (tb, tb), 0)
        col = lax.broadcasted_iota(jnp.int32, (tb, tb), 1)
        cum = _tri_sum((row >= col).astype(BF16), logf) + carry[...]
        cum_ref[...] = cum
        carry[...] = cum_ref[pl.ds(tb - 1, 1), :]
        lane = lax.broadcasted_iota(jnp.int32, (tb, LANES), 1)
        for h in range(H):
            colv = jnp.sum(jnp.where(lane == h, cum, 0.0), axis=-1, keepdims=True)
            rep_ref[h] = jnp.broadcast_to(colv, (tb, LANES))

    return pl.pallas_call(
        body, name="fox_gate_fwd", grid=(T // tb,),
        in_specs=[pl.BlockSpec((tb, LANES), lambda i: (i, 0)), _vec_spec(LANES)],
        out_specs=[pl.BlockSpec((tb, LANES), lambda i: (i, 0)), pl.BlockSpec((H, tb, LANES), lambda i: (0, i, 0))],
        out_shape=[jax.ShapeDtypeStruct((T, LANES), F32), jax.ShapeDtypeStruct((H, T, LANES), F32)],
        scratch_shapes=[pltpu.VMEM((1, LANES), F32)], compiler_params=_params(),
    )(fg, bf)


def _fox_gate_bwd(dcum, fg, bf):
    T = fg.shape[0]
    tb = _tile(T, 512)
    nb = T // tb

    def body(dc_ref, fg_ref, bf_ref, dfg_ref, dbf_ref, carry):
        @pl.when(pl.program_id(0) == 0)
        def _():
            carry[...] = jnp.zeros_like(carry)
            dbf_ref[...] = jnp.zeros_like(dbf_ref)

        row = lax.broadcasted_iota(jnp.int32, (tb, tb), 0)
        col = lax.broadcasted_iota(jnp.int32, (tb, tb), 1)
        dc = dc_ref[...]
        dlogf = _tri_sum((row <= col).astype(BF16), dc) + carry[...]
        z = fg_ref[...] + bf_ref[...]
        dfg = dlogf * _sigmoid(-z)
        dfg_ref[...] = dfg
        dbf_ref[...] += jnp.sum(dfg, axis=0, keepdims=True)
        carry[...] += jnp.sum(dc, axis=0, keepdims=True)

    rev = pl.BlockSpec((tb, LANES), lambda i: (nb - 1 - i, 0))
    return pl.pallas_call(
        body, name="fox_gate_bwd", grid=(nb,),
        in_specs=[rev, rev, _vec_spec(LANES)], out_specs=[rev, _vec_spec(LANES)],
        out_shape=[jax.ShapeDtypeStruct((T, LANES), F32), jax.ShapeDtypeStruct((1, LANES), F32)],
        scratch_shapes=[pltpu.VMEM((1, LANES), F32)], compiler_params=_params(),
    )(dcum, fg, bf)


def _fox_blocks(T, pref=512):
    tb = pref if T >= 2 * pref else BLOCK
    return tb, T // tb


def _fox_fwd(name, qkv, cq_rep, ck, gather=()):
    T = qkv.shape[0]
    D = qkv.shape[1] // 3
    H = D // FOX_DH
    hps = FOX_HPS
    ng, wl = H // hps, hps * FOX_DH
    tb, nb = _fox_blocks(T, FOX_FWD_BLOCK)
    pairs = [(i, j) for i in range(nb) for j in range(i + 1)]
    qi = np.array([p[0] for p in pairs], np.int32)
    kj = np.array([p[1] for p in pairs], np.int32)
    npairs = len(pairs)
    nrep = tb // LANES
    nc = len(gather)

    def body(qi_ref, kj_ref, q_ref, k_ref, v_ref, cq_ref, ck_ref, *rest):
        cin, (o_ref, obf_ref, lse_ref), cout = rest[:nc], rest[nc:nc + 3], rest[nc + 3:2 * nc + 3]
        m_sc, l_sc, acc_sc = rest[2 * nc + 3:2 * nc + 6]
        sems = rest[2 * nc + 6:]
        g, p = pl.program_id(0), pl.program_id(1)
        i, j = qi_ref[p], kj_ref[p]

        if nc:
            @pl.when(jnp.logical_and(g == 0, p == 0))
            def _():
                _gather2_start(cin, cout, *sems)

        @pl.when(j == 0)
        def _():
            m_sc[...] = jnp.full_like(m_sc, NEG_INF)
            l_sc[...] = jnp.zeros_like(l_sc)
            acc_sc[...] = jnp.zeros_like(acc_sc)

        row = lax.broadcasted_iota(jnp.int32, (tb, tb), 0)
        col = lax.broadcasted_iota(jnp.int32, (tb, tb), 1)
        visible = jnp.logical_or(j < i, row >= col)
        for hh in range(hps):
            cols = slice(hh * FOX_DH, (hh + 1) * FOX_DH)
            s = _dot(q_ref[:, cols], k_ref[:, cols], NT) * FOX_SCALE
            s = jnp.where(visible, s + _rep(cq_ref[hh], nrep) - ck_ref[hh], NEG_INF)
            m_prev = m_sc[hh]
            m_new = jnp.maximum(m_prev, jnp.max(s, axis=-1, keepdims=True))
            alpha = jnp.exp(m_prev - m_new)
            pm = jnp.exp(s - _rep(m_new, nrep))
            l_sc[hh] = alpha * l_sc[hh] + jnp.sum(pm, axis=-1, keepdims=True)
            acc_sc[:, cols] = alpha * acc_sc[:, cols] + _dot(pm.astype(BF16), v_ref[:, cols], NN)
            m_sc[hh] = m_new

        @pl.when(j == i)
        def _():
            for hh in range(hps):
                cols = slice(hh * FOX_DH, (hh + 1) * FOX_DH)
                o = acc_sc[:, cols] / l_sc[hh]
                o_ref[:, cols] = o
                obf_ref[:, cols] = o.astype(BF16)
                lse_ref[hh] = m_sc[hh] + jnp.log(l_sc[hh])

        if nc:
            @pl.when(jnp.logical_and(g == ng - 1, p == npairs - 1))
            def _():
                _gather2_pass_on(cin, cout, *sems)
                _gather2_finish(cin, cout, *sems)

    any_spec = pl.BlockSpec(memory_space=pl.ANY)
    qblk = pl.BlockSpec((tb, wl), lambda g, p, qi, kj: (qi[p], g))
    qrep = pl.BlockSpec((hps, tb, LANES), lambda g, p, qi, kj: (g, qi[p], 0))
    grid_spec = pltpu.PrefetchScalarGridSpec(
        num_scalar_prefetch=2, grid=(ng, npairs),
        in_specs=[qblk,
                  pl.BlockSpec((tb, wl), lambda g, p, qi, kj: (kj[p], ng + g)),
                  pl.BlockSpec((tb, wl), lambda g, p, qi, kj: (kj[p], 2 * ng + g)),
                  qrep,
                  pl.BlockSpec((hps, 1, tb), lambda g, p, qi, kj: (g, 0, kj[p]))] + [any_spec] * nc,
        out_specs=[qblk, qblk, qrep] + [any_spec] * nc,
        scratch_shapes=[pltpu.VMEM((hps, tb, LANES), F32), pltpu.VMEM((hps, tb, LANES), F32),
                        pltpu.VMEM((tb, wl), F32)] + (_gather2_sems(nc) if nc else []))
    res = pl.pallas_call(
        body, name=name, grid_spec=grid_spec,
        out_shape=[jax.ShapeDtypeStruct((T, D), F32), jax.ShapeDtypeStruct((T, D), BF16),
                   jax.ShapeDtypeStruct((H, T, LANES), F32)] + _comm_out_shapes(gather, True),
        compiler_params=_params(),
    )(qi, kj, qkv, qkv, qkv, cq_rep, ck, *gather)
    return res[:3], res[3:]


def _fox_bwd_prep(do, o, lse_rep, cq_rep):
    T, D = do.shape
    H = D // FOX_DH
    tr = _tile(T, ROW_TILE, 8)

    def body(do_ref, o_ref, lse_ref, cq_ref, dob_ref, delta_ref, cql_ref):
        dov = do_ref[...]
        dob_ref[...] = dov.astype(BF16)
        prod = dov * o_ref[...]
        for h in range(H):
            d = jnp.sum(prod[:, h * FOX_DH:(h + 1) * FOX_DH], axis=-1, keepdims=True)
            delta_ref[h] = jnp.broadcast_to(d, (tr, LANES))
        cql_ref[...] = cq_ref[...] - lse_ref[...]

    row = pl.BlockSpec((tr, D), lambda i: (i, 0))
    rep = pl.BlockSpec((H, tr, LANES), lambda i: (0, i, 0))
    return pl.pallas_call(
        body, name="fox_bwd_prep", grid=(T // tr,),
        in_specs=[row, row, rep, rep], out_specs=[row, rep, rep],
        out_shape=[jax.ShapeDtypeStruct((T, D), BF16), jax.ShapeDtypeStruct((H, T, LANES), F32),
                   jax.ShapeDtypeStruct((H, T, LANES), F32)],
        compiler_params=_params(),
    )(do, o, lse_rep, cq_rep)


def _fox_bwd(name, qkv, dob, delta_rep, cql_rep, ck, a2a=()):
    T = qkv.shape[0]
    D = qkv.shape[1] // 3
    H = D // FOX_DH
    hps = FOX_HPS
    ng, wl = H // hps, hps * FOX_DH
    tk, nbk = _fox_blocks(T, FOX_BWD_KEYS)
    tq = max(_fox_blocks(T, FOX_BWD_QUERIES)[0], tk)
    nbq, ratio = T // tq, tq // tk
    pairs = [(i, j) for j in range(nbk) for i in range(j // ratio, nbq)]
    qi = np.array([p[0] for p in pairs], np.int32)
    kj = np.array([p[1] for p in pairs], np.int32)
    npairs = len(pairs)
    nrep = tk // LANES
    nc = len(a2a)

    def body(qi_ref, kj_ref, q_ref, k_ref, v_ref, do_ref, delta_ref, cql_ref, ck_ref, *rest):
        cin, (dq_ref, dk_ref, dv_ref, dck_ref, dcq_ref), cout = rest[:nc], rest[nc:nc + 5], rest[nc + 5:2 * nc + 5]
        dq_acc, dk_acc, dv_acc, dc_acc = rest[2 * nc + 5:2 * nc + 9]
        sems = rest[2 * nc + 9:]
        g, p = pl.program_id(0), pl.program_id(1)
        i, j = qi_ref[p], kj_ref[p]

        @pl.when(jnp.logical_and(g == 0, p == 0))
        def _():
            dcq_ref[...] = jnp.zeros_like(dcq_ref)
            if nc:
                _comm_start(cin, cout, False, *sems)

        @pl.when(p == 0)
        def _():
            dq_acc[...] = jnp.zeros_like(dq_acc)

        @pl.when(i == j // ratio)
        def _():
            dk_acc[...] = jnp.zeros_like(dk_acc)
            dv_acc[...] = jnp.zeros_like(dv_acc)
            dc_acc[...] = jnp.zeros_like(dc_acc)

        row = lax.broadcasted_iota(jnp.int32, (tq, tk), 0)
        col = lax.broadcasted_iota(jnp.int32, (tq, tk), 1)
        visible = jnp.logical_or((j + 1) * tk <= i * tq, row + i * tq >= col + j * tk)
        lane = lax.broadcasted_iota(jnp.int32, (tq, LANES), 1)
        rows = pl.ds(pl.multiple_of(i * tq, tq), tq)
        dcq = jnp.zeros((tq, LANES), F32)
        for hh in range(hps):
            cols = slice(hh * FOX_DH, (hh + 1) * FOX_DH)
            q, k, v, dov = q_ref[:, cols], k_ref[:, cols], v_ref[:, cols], do_ref[:, cols]
            s = _dot(q, k, NT) * FOX_SCALE + _rep(cql_ref[hh], nrep) - ck_ref[hh]
            pm = jnp.exp(jnp.where(visible, s, NEG_INF))
            dv_acc[:, cols] += _dot(pm.astype(BF16), dov, TN)
            ds = pm * (_dot(dov, v, NT) - _rep(delta_ref[hh], nrep))
            dsb = (ds * FOX_SCALE).astype(BF16)
            dk_acc[:, cols] += _dot(dsb, q, TN)
            dq_acc[rows, cols] += _dot(dsb, k, NN)
            dc_acc[hh] -= jnp.sum(ds, axis=0, keepdims=True)
            dcq = dcq + jnp.where(lane == g * hps + hh, jnp.sum(ds, axis=-1, keepdims=True), 0.0)
        dcq_ref[rows, :] += dcq

        @pl.when(i == nbq - 1)
        def _():
            dk_ref[...] = dk_acc[...].astype(BF16)
            dv_ref[...] = dv_acc[...].astype(BF16)
            dck_ref[...] = dc_acc[...]

        @pl.when(p == npairs - 1)
        def _():
            dq_ref[...] = dq_acc[...].astype(BF16)

        if nc:
            @pl.when(jnp.logical_and(g == ng - 1, p == npairs - 1))
            def _():
                _comm_wait(cin, cout, False, *sems)

    any_spec = pl.BlockSpec(memory_space=pl.ANY)
    qblk = pl.BlockSpec((tq, wl), lambda g, p, qi, kj: (qi[p], g))
    qrep = pl.BlockSpec((hps, tq, LANES), lambda g, p, qi, kj: (g, qi[p], 0))
    kblk = pl.BlockSpec((tk, wl), lambda g, p, qi, kj: (kj[p], g))
    krow = pl.BlockSpec((hps, 1, tk), lambda g, p, qi, kj: (g, 0, kj[p]))
    grid_spec = pltpu.PrefetchScalarGridSpec(
        num_scalar_prefetch=2, grid=(ng, npairs),
        in_specs=[qblk,
                  pl.BlockSpec((tk, wl), lambda g, p, qi, kj: (kj[p], ng + g)),
                  pl.BlockSpec((tk, wl), lambda g, p, qi, kj: (kj[p], 2 * ng + g)),
                  qblk, qrep, qrep, krow] + [any_spec] * nc,
        out_specs=[pl.BlockSpec((T, wl), lambda g, p, qi, kj: (0, g)), kblk, kblk, krow,
                   pl.BlockSpec((T, LANES), lambda g, p, qi, kj: (0, 0))] + [any_spec] * nc,
        scratch_shapes=[pltpu.VMEM((T, wl), F32), pltpu.VMEM((tk, wl), F32), pltpu.VMEM((tk, wl), F32),
                        pltpu.VMEM((hps, 1, tk), F32)] + (_comm_sems(nc) if nc else []))
    act = jax.ShapeDtypeStruct((T, D), BF16)
    res = pl.pallas_call(
        body, name=name, grid_spec=grid_spec,
        out_shape=[act, act, act, jax.ShapeDtypeStruct((H, 1, T), F32),
                   jax.ShapeDtypeStruct((T, LANES), F32)] + _comm_out_shapes(a2a, False),
        compiler_params=_params(),
    )(qi, kj, qkv, qkv, qkv, dob, delta_rep, cql_rep, ck, *a2a)
    return res[:5], res[5:]


def _sgu_rows(T):
    return 2 * BLOCK if T % (2 * BLOCK) == 0 else BLOCK


def _sgu_norm(zv, g_ref, b_ref):
    vv = _gelu(zv)
    mu = jnp.mean(vv, axis=-1, keepdims=True)
    cen = vv - mu
    rstd = lax.rsqrt(jnp.mean(cen * cen, axis=-1, keepdims=True) + EPS)
    vh = cen * rstd
    return vh, rstd, vh * g_ref[...] + b_ref[...]


def _sgu_fwd(zpre, ln_g, ln_b, wsm, bs_rep):
    T = zpre.shape[0]
    W = zpre.shape[1] // 2
    G = W // BLOCK
    tr = _sgu_rows(T)

    def body(z_ref, g_ref, b_ref, ws_ref, bs_ref, o_ref):
        u = _gelu(z_ref[:, :W])
        _, _, vln = _sgu_norm(z_ref[:, W:], g_ref, b_ref)
        for c in range(tr // BLOCK):
            rows = slice(c * BLOCK, (c + 1) * BLOCK)
            for gi in range(G):
                cols = slice(gi * BLOCK, (gi + 1) * BLOCK)
                f = _dot(ws_ref[gi], vln[rows, cols].astype(BF16), NN) + bs_ref[gi]
                o_ref[rows, cols] = (u[rows, cols] * f).astype(BF16)

    full3 = pl.BlockSpec((G, BLOCK, BLOCK), lambda i: (0, 0, 0))
    return pl.pallas_call(
        body, name="sgu_fwd", grid=(T // tr,),
        in_specs=[pl.BlockSpec((tr, 2 * W), lambda i: (i, 0)), _vec_spec(W), _vec_spec(W), full3, full3],
        out_specs=pl.BlockSpec((tr, W), lambda i: (i, 0)),
        out_shape=jax.ShapeDtypeStruct((T, W), BF16), compiler_params=_params(),
    )(zpre, ln_g, ln_b, wsm, bs_rep)


def _sgu_bwd(dgt, zpre, ln_g, ln_b, wsm, wsmT, bs_rep):
    T = zpre.shape[0]
    W = zpre.shape[1] // 2
    G = W // BLOCK
    tr = BLOCK

    def body(dgt_ref, z_ref, g_ref, b_ref, ws_ref, wst_ref, bs_ref,
             dz_ref, dws_ref, dbs_ref, dlg_ref, dlb_ref, du_sc, dvln_sc):
        @pl.when(pl.program_id(0) == 0)
        def _():
            dws_ref[...] = jnp.zeros_like(dws_ref)
            dbs_ref[...] = jnp.zeros_like(dbs_ref)
            dlg_ref[...] = jnp.zeros_like(dlg_ref)
            dlb_ref[...] = jnp.zeros_like(dlb_ref)

        zu = z_ref[:, :W]
        zv = z_ref[:, W:]
        u = _gelu(zu)
        vh, rstd, vln = _sgu_norm(zv, g_ref, b_ref)
        dgtv = dgt_ref[...]
        trow = lax.broadcasted_iota(jnp.int32, (BLOCK, BLOCK), 0)
        tcol = lax.broadcasted_iota(jnp.int32, (BLOCK, BLOCK), 1)
        causal = trow >= tcol
        for c in range(tr // BLOCK):
            rows = slice(c * BLOCK, (c + 1) * BLOCK)
            for gi in range(G):
                cols = slice(gi * BLOCK, (gi + 1) * BLOCK)
                vb = vln[rows, cols].astype(BF16)
                f = _dot(ws_ref[gi], vb, NN) + bs_ref[gi]
                d = dgtv[rows, cols]
                du_sc[rows, cols] = d * f
                df = d * u[rows, cols]
                dfb = df.astype(BF16)
                dvln_sc[rows, cols] = _dot(wst_ref[gi], dfb, NN)
                dws_ref[gi] += jnp.where(causal, _dot(dfb, vb, NT), 0.0)
                dbs_ref[gi] += jnp.broadcast_to(jnp.sum(df, axis=-1, keepdims=True), (BLOCK, BLOCK))
        dvln = dvln_sc[...]
        dlg_ref[...] += jnp.sum(dvln * vh, axis=0, keepdims=True)
        dlb_ref[...] += jnp.sum(dvln, axis=0, keepdims=True)
        dvh = dvln * g_ref[...]
        dvv = rstd * (dvh - jnp.mean(dvh, axis=-1, keepdims=True)
                      - vh * jnp.mean(dvh * vh, axis=-1, keepdims=True))
        dz_ref[:, :W] = (du_sc[...] * _gelu_grad(zu)).astype(BF16)
        dz_ref[:, W:] = (dvv * _gelu_grad(zv)).astype(BF16)

    full3 = pl.BlockSpec((G, BLOCK, BLOCK), lambda i: (0, 0, 0))
    vec = jax.ShapeDtypeStruct((1, W), F32)
    acc3 = jax.ShapeDtypeStruct((G, BLOCK, BLOCK), F32)
    return pl.pallas_call(
        body, name="sgu_bwd", grid=(T // tr,),
        in_specs=[pl.BlockSpec((tr, W), lambda i: (i, 0)), pl.BlockSpec((tr, 2 * W), lambda i: (i, 0)),
                  _vec_spec(W), _vec_spec(W), full3, full3, full3],
        out_specs=[pl.BlockSpec((tr, 2 * W), lambda i: (i, 0)), full3, full3, _vec_spec(W), _vec_spec(W)],
        out_shape=[jax.ShapeDtypeStruct((T, 2 * W), BF16), acc3, acc3, vec, vec],
        scratch_shapes=[pltpu.VMEM((tr, W), F32), pltpu.VMEM((tr, W), F32)],
        compiler_params=_params(),
    )(dgt, zpre, ln_g, ln_b, wsm, wsmT, bs_rep)


def _rope(x, cos_t, sin_t):
    T, N = x.shape
    tr = _tile(T, ROW_TILE, 8)
    nrep = N // LANES
    half = ROPE_DIM // 2

    def body(x_ref, c_ref, s_ref, o_ref):
        xv = x_ref[...]
        lane = jnp.bitwise_and(lax.broadcasted_iota(jnp.int32, (tr, N), 1), SWA_DH - 1)
        partner = jnp.where(lane < half, -pltpu.roll(xv, N - half, 1), pltpu.roll(xv, half, 1))
        o_ref[...] = (xv * _rep(c_ref[...], nrep) + partner * _rep(s_ref[...], nrep)).astype(BF16)

    tab = pl.BlockSpec((tr, LANES), lambda i: (i, 0))
    row = pl.BlockSpec((tr, N), lambda i: (i, 0))
    return pl.pallas_call(
        body, name="rope", grid=(T // tr,), in_specs=[row, tab, tab], out_specs=row,
        out_shape=jax.ShapeDtypeStruct((T, N), BF16), compiler_params=_params(),
    )(x, cos_t, sin_t)


def _swa_tiles(T):
    sb = 4 if T >= 2048 else 2
    return sb, BLOCK * sb, T // (BLOCK * sb)


def _band_mask():
    row = lax.broadcasted_iota(jnp.int32, (BLOCK, 2 * BLOCK), 0)
    col = lax.broadcasted_iota(jnp.int32, (BLOCK, 2 * BLOCK), 1)
    return jnp.logical_and(col > row, col <= row + BLOCK), col


def _swa_specs(T, G):
    sb, tq, nq = _swa_tiles(T)
    q = pl.BlockSpec((G, tq, LANES), lambda h, i: (h, i, 0))
    kc = pl.BlockSpec((None, tq, LANES), lambda h, i: (h, i, 0))
    kp = pl.BlockSpec((None, BLOCK, LANES), lambda h, i: (h, jnp.maximum(i * sb - 1, 0), 0))
    return q, kc, kp


def _swa_band(b, i, kc_ref, kp_ref, vc_ref, vp_ref):
    rows = slice(b * BLOCK, (b + 1) * BLOCK)
    prev = slice((b - 1) * BLOCK, b * BLOCK)
    kprev = kp_ref[...] if b == 0 else kc_ref[prev, :]
    vprev = vp_ref[...] if b == 0 else vc_ref[prev, :]
    K = jnp.concatenate([kprev, kc_ref[rows, :]], axis=0)
    V = jnp.concatenate([vprev, vc_ref[rows, :]], axis=0)
    band, col = _band_mask()
    if b == 0:
        band = jnp.logical_and(band, jnp.logical_or(col >= BLOCK, i > 0))
    return rows, K, V, band


def _stack_heads(ref, rows, G):
    return jnp.concatenate([ref[g, rows, :] for g in range(G)], axis=0)


def _swa_fwd(qp, kp, vp, sinks):
    Hq, T, _ = qp.shape
    Hk = kp.shape[0]
    G = Hq // Hk
    sb, tq, nq = _swa_tiles(T)

    def body(sink_ref, q_ref, kc_ref, kp_ref, vc_ref, vp_ref, o_ref, lse_ref):
        h, i = pl.program_id(0), pl.program_id(1)
        head_of_row = lax.broadcasted_iota(jnp.int32, (G * BLOCK, 1), 0) // BLOCK
        sink = jnp.zeros((G * BLOCK, 1), F32)
        for g in range(G):
            sink = jnp.where(head_of_row == g, sink_ref[h * G + g], sink)
        for b in range(sb):
            rows, K, V, band = _swa_band(b, i, kc_ref, kp_ref, vc_ref, vp_ref)
            band = jnp.concatenate([band] * G, axis=0)
            s = jnp.where(band, _dot(_stack_heads(q_ref, rows, G), K, NT) * SWA_SCALE, NEG_INF)
            m = jnp.maximum(jnp.max(s, axis=-1, keepdims=True), sink)
            pm = jnp.exp(s - m)
            den = jnp.sum(pm, axis=-1, keepdims=True) + jnp.exp(sink - m)
            o = _dot((pm / den).astype(BF16), V, NN)
            lse = jnp.broadcast_to(m + jnp.log(den), (G * BLOCK, LANES))
            for g in range(G):
                o_ref[g, rows, :] = o[g * BLOCK:(g + 1) * BLOCK]
                lse_ref[g, rows, :] = lse[g * BLOCK:(g + 1) * BLOCK]

    q, kc, kpv = _swa_specs(T, G)
    out = jax.ShapeDtypeStruct((Hq, T, LANES), F32)
    return pl.pallas_call(
        body, name="swa_fwd", grid=(Hk, nq),
        in_specs=[pl.BlockSpec(memory_space=pltpu.SMEM), q, kc, kpv, kc, kpv], out_specs=[q, q],
        out_shape=[out, out], compiler_params=_params(),
    )(sinks, qp, kp, kp, vp, vp)


def _swa_bwd_dq(qp, kp, vp, dop, op, lse_rep, sinks):
    Hq, T, _ = qp.shape
    Hk = kp.shape[0]
    G = Hq // Hk
    sb, tq, nq = _swa_tiles(T)

    def body(sink_ref, q_ref, kc_ref, kp_ref, vc_ref, vp_ref, do_ref, o_ref, lse_ref, dq_ref, dsink_ref):
        h, i = pl.program_id(0), pl.program_id(1)

        @pl.when(i == 0)
        def _():
            dsink_ref[...] = jnp.zeros_like(dsink_ref)

        for b in range(sb):
            rows, K, V, band = _swa_band(b, i, kc_ref, kp_ref, vc_ref, vp_ref)
            band = jnp.concatenate([band] * G, axis=0)
            dov = _stack_heads(do_ref, rows, G)
            delta = jnp.sum(dov * _stack_heads(o_ref, rows, G), axis=-1, keepdims=True)
            lse = _stack_heads(lse_ref, rows, G)
            s = jnp.where(band, _dot(_stack_heads(q_ref, rows, G), K, NT) * SWA_SCALE, NEG_INF)
            pm = jnp.exp(s - _rep(lse, 2))
            ds = pm * (_dot(dov.astype(BF16), V, NT) - delta)
            dq = _dot((ds * SWA_SCALE).astype(BF16), K, NN)
            for g in range(G):
                head = slice(g * BLOCK, (g + 1) * BLOCK)
                dq_ref[g, rows, :] = dq[head]
                part = jnp.sum(jnp.exp(sink_ref[h * G + g] - lse[head]) * delta[head], axis=0, keepdims=True)
                dsink_ref[g] -= jnp.broadcast_to(part, (8, LANES))

    q, kc, kpv = _swa_specs(T, G)
    return pl.pallas_call(
        body, name="swa_bwd_dq", grid=(Hk, nq),
        in_specs=[pl.BlockSpec(memory_space=pltpu.SMEM), q, kc, kpv, kc, kpv, q, q, q],
        out_specs=[q, pl.BlockSpec((G, 8, LANES), lambda h, i: (h, 0, 0))],
        out_shape=[jax.ShapeDtypeStruct((Hq, T, LANES), F32), jax.ShapeDtypeStruct((Hq, 8, LANES), F32)],
        compiler_params=_params(),
    )(sinks, qp, kp, kp, vp, vp, dop, op, lse_rep)


def _swa_bwd_dkv(qp, kp, vp, dop, op, lse_rep):
    Hq, T, _ = qp.shape
    Hk = kp.shape[0]
    G = Hq // Hk
    sb, tq, nq = _swa_tiles(T)
    nblk = T // BLOCK

    def body(k_ref, v_ref, q_ref, qn_ref, do_ref, don_ref, o_ref, on_ref, lse_ref, lsen_ref, dk_ref, dv_ref):
        i = pl.program_id(1)
        trow = lax.broadcasted_iota(jnp.int32, (2 * BLOCK, BLOCK), 0)
        scol = lax.broadcasted_iota(jnp.int32, (2 * BLOCK, BLOCK), 1)
        band0 = jnp.logical_and(trow >= scol, trow < scol + BLOCK)
        for b in range(sb):
            rows = slice(b * BLOCK, (b + 1) * BLOCK)
            nxt = slice((b + 1) * BLOCK, (b + 2) * BLOCK)
            last = b == sb - 1
            band = band0
            if last:
                band = jnp.logical_and(band0, jnp.logical_or(trow < BLOCK, i < nq - 1))
            kb, vb = k_ref[rows, :], v_ref[rows, :]

            def bands(cur, nx):
                return jnp.concatenate([piece for g in range(G)
                                        for piece in (cur[g, rows, :], nx[g] if last else cur[g, nxt, :])], axis=0)

            Q, dov, lse = bands(q_ref, qn_ref), bands(do_ref, don_ref), bands(lse_ref, lsen_ref)
            delta = jnp.sum(dov * bands(o_ref, on_ref), axis=-1, keepdims=True)
            s = jnp.where(jnp.concatenate([band] * G, axis=0), _dot(Q, kb, NT) * SWA_SCALE, NEG_INF)
            pm = jnp.exp(s - lse)
            dob = dov.astype(BF16)
            dv_ref[rows, :] = _dot(pm.astype(BF16), dob, TN)
            ds = pm * (_dot(dob, vb, NT) - delta)
            dk_ref[rows, :] = _dot((ds * SWA_SCALE).astype(BF16), Q, TN)

    kspec = pl.BlockSpec((None, tq, LANES), lambda h, i: (h, i, 0))
    cur = pl.BlockSpec((G, tq, LANES), lambda h, i: (h, i, 0))
    nxt = pl.BlockSpec((G, BLOCK, LANES), lambda h, i: (h, jnp.minimum((i + 1) * sb, nblk - 1), 0))
    out = jax.ShapeDtypeStruct((Hk, T, LANES), F32)
    return pl.pallas_call(
        body, name="swa_bwd_dkv", grid=(Hk, nq),
        in_specs=[kspec, kspec, cur, nxt, cur, nxt, cur, nxt, cur, nxt], out_specs=[kspec, kspec],
        out_shape=[out, out], compiler_params=_params(),
    )(kp, vp, qp, qp, dop, dop, op, op, lse_rep, lse_rep)


def _to_heads(a, nh):
    T = a.shape[0]
    a = a.reshape(T, nh, SWA_DH).transpose(1, 0, 2)
    return jnp.pad(a, ((0, 0), (0, 0), (0, LANES - SWA_DH)))


def _from_heads(a):
    nh, T, _ = a.shape
    return a[:, :, :SWA_DH].transpose(1, 0, 2).reshape(T, nh * SWA_DH)


def _adam(g, w, m, v):
    m2 = ADAM_B1 * m + (1.0 - ADAM_B1) * g
    v2 = ADAM_B2 * v + (1.0 - ADAM_B2) * (g * g)
    m_hat = m2 / (1.0 - ADAM_B1 ** ADAM_STEP)
    v_hat = v2 / (1.0 - ADAM_B2 ** ADAM_STEP)
    delta = -ADAM_LR * (m_hat / (jnp.sqrt(v_hat) + ADAM_EPS) + ADAM_WD * w)
    return delta, m2, v2


def _ada_fwd(c_all, w, b):
    L, D, n = w.shape
    tn = _tile(n, 768)

    def body(c_ref, w_ref, b_ref, o_ref):
        cv = c_ref[...]
        ca = (cv * _sigmoid(cv)).astype(BF16)
        o_ref[...] = _dot(ca, w_ref[...].astype(BF16), NN) + b_ref[...]

    return pl.pallas_call(
        body, name="ada_fwd", grid=(L, n // tn),
        in_specs=[pl.BlockSpec((NDEV, D), lambda l, j: (0, 0)), pl.BlockSpec((None, D, tn), lambda l, j: (l, 0, j)),
                  pl.BlockSpec((None, 1, tn), lambda l, j: (l, 0, j))],
        out_specs=pl.BlockSpec((None, NDEV, tn), lambda l, j: (l, 0, j)),
        out_shape=jax.ShapeDtypeStruct((L, NDEV, n), F32), compiler_params=_params(),
    )(c_all, w, b)


def _ada_update(c_rep, dm, w, m, v):
    L, D, n = w.shape
    tr = _tile(D, 256, 8)
    nrep = n // LANES

    def body(c_ref, dm_ref, w_ref, m_ref, v_ref, g_ref, d_ref, m2_ref, v2_ref):
        g = jnp.zeros((tr, n), F32)
        for b in range(NDEV):
            cv = c_ref[b]
            g = g + _rep(cv * _sigmoid(cv), nrep) * dm_ref[pl.ds(b, 1), :]
        g_ref[...] = g
        d_ref[...], m2_ref[...], v2_ref[...] = _adam(g, w_ref[...], m_ref[...], v_ref[...])

    blk = pl.BlockSpec((None, tr, n), lambda l, i: (l, i, 0))
    out = jax.ShapeDtypeStruct((L, D, n), F32)
    return pl.pallas_call(
        body, name="ada_update", grid=(L, D // tr),
        in_specs=[pl.BlockSpec((NDEV, tr, LANES), lambda l, i: (0, i, 0)),
                  pl.BlockSpec((None, NDEV, n), lambda l, i: (l, 0, 0)), blk, blk, blk],
        out_specs=[blk, blk, blk, blk], out_shape=[out, out, out, out], compiler_params=_params(),
    )(c_rep, dm, w, m, v)


def _adamw(name, parts, w, m, v, layer, stacked=None):
    P, R, C = parts.shape
    Lw = w.shape[0]
    cpad = -(-C // LANES) * LANES
    per_row = cpad * (P * parts.dtype.itemsize + 7 * 4) * 2
    tr = _tile(R, max(8, (24 * 1024 * 1024 // per_row) // 8 * 8), 8)
    if stacked is None:
        stacked = [lax.empty((Lw, R, C), F32) for _ in range(4)]

    def body(p_ref, w_ref, m_ref, v_ref, *rest):
        g_ref, d_ref, m2_ref, v2_ref = rest[4:]
        g = p_ref[0].astype(F32)
        for s in range(1, P):
            g = g + p_ref[s].astype(F32)
        g_ref[...] = g
        d_ref[...], m2_ref[...], v2_ref[...] = _adam(g, w_ref[...], m_ref[...], v_ref[...])

    stk = pl.BlockSpec((None, tr, C), lambda i: (layer, i, 0))
    any_spec = pl.BlockSpec(memory_space=pl.ANY)
    out = jax.ShapeDtypeStruct((Lw, R, C), F32)
    return pl.pallas_call(
        body, name=name, grid=(R // tr,),
        in_specs=[pl.BlockSpec((P, tr, C), lambda i: (0, i, 0)), stk, stk, stk] + [any_spec] * 4,
        out_specs=[stk, stk, stk, stk], out_shape=[out, out, out, out],
        input_output_aliases={4: 0, 5: 1, 6: 2, 7: 3}, compiler_params=_params(),
    )(parts, w, m, v, *stacked)


def _colcat(a):
    s, k, n = a.shape
    return a.transpose(1, 0, 2).reshape(k, s * n)


def _colsplit(a):
    k, n8 = a.shape
    return a.reshape(k, NDEV, n8 // NDEV).transpose(1, 0, 2)


def _rowsplit(a):
    r, c = a.shape
    return a.reshape(NDEV, r // NDEV, c)


def kernel(x, c, positions, ada_w, ada_b, mix_pre_g, mix_post_g, ffn_pre_g, ffn_post_g, ffn_w_gu, ffn_w_down, fox_w_in, fox_b_f, fox_w_out, sgu_w_in, sgu_ln_g, sgu_ln_b, sgu_w_s, sgu_b_s, sgu_w_out, swa_w_in, swa_sinks, swa_w_out, loss_target, m_ada_w, m_ada_b, m_mix_pre_g, m_mix_post_g, m_ffn_pre_g, m_ffn_post_g, m_ffn_w_gu, m_ffn_w_down, m_fox_w_in, m_fox_b_f, m_fox_w_out, m_sgu_w_in, m_sgu_ln_g, m_sgu_ln_b, m_sgu_w_s, m_sgu_b_s, m_sgu_w_out, m_swa_w_in, m_swa_sinks, m_swa_w_out, v_ada_w, v_ada_b, v_mix_pre_g, v_mix_post_g, v_ffn_pre_g, v_ffn_post_g, v_ffn_w_gu, v_ffn_w_down, v_fox_w_in, v_fox_b_f, v_fox_w_out, v_sgu_w_in, v_sgu_ln_g, v_sgu_ln_b, v_sgu_w_s, v_sgu_b_s, v_sgu_w_out, v_swa_w_in, v_swa_sinks, v_swa_w_out):
    env = locals()
    W = {n: env[n] for n in WEIGHTS}
    M = {n: env["m_" + n] for n in WEIGHTS}
    V = {n: env["v_" + n] for n in WEIGHTS}

    me = 4 * lax.axis_index("x") + 2 * lax.axis_index("y") + lax.axis_index("c")
    _, T, D = x.shape
    L = ada_w.shape[0]
    n_ada = ada_w.shape[2]
    F = ffn_w_gu.shape[2] * NDEV // 2
    H = D // FOX_DH
    Hq = D // SWA_DH
    x0 = x.reshape(T, D)
    mixer = {0: 'fox', 1: 'sgu', 2: 'swa'}

    c_all = _comm("gather_c", [c], 'gather')[0].reshape(NDEV, D)
    ada_b_mine = lax.dynamic_slice_in_dim(ada_b, me * n_ada, n_ada, axis=1).reshape(L, 1, n_ada)
    mod_cols = _ada_fwd(c_all, ada_w, ada_b_mine)
    mod = _comm("a2a_mod", [mod_cols.transpose(1, 0, 2)], 'a2a')[0]
    mod = mod.transpose(1, 0, 2).reshape(L, 6, 1, D)

    inv = ROPE_THETA ** (-jnp.arange(0, ROPE_DIM, 2, dtype=F32) / ROPE_DIM)
    ang = positions[0].astype(F32)[:, None] * inv
    pad1 = jnp.ones((T, SWA_DH - ROPE_DIM), F32)
    cos64 = jnp.concatenate([jnp.cos(ang), jnp.cos(ang), pad1], axis=1)
    sin64 = jnp.concatenate([jnp.sin(ang), jnp.sin(ang), 0.0 * pad1], axis=1)
    cos_t = jnp.concatenate([cos64, cos64], axis=1)
    sin_t = jnp.concatenate([sin64, sin64], axis=1)

    fox_layers = [i for i in range(L) if mixer[i % 3] == 'fox']
    assert fox_layers and fox_layers[0] == 0

    def slice_of(i, role):
        kind, j = mixer[i % 3], i // 3
        src = {'wgu': (ffn_w_gu, i), 'wd': (ffn_w_down, i), 'win': (W[kind + '_w_in'], j), 'wout': (W[kind + '_w_out'], j)}[role]
        return src[0][src[1]].astype(BF16)

    def nparams(key):
        return int(np.prod(slice_of(*key).shape)) * NDEV

    def gather_plan():
        plan = {}
        for f in fox_layers:
            later = [i for i in fox_layers if i > f]
            stop = later[0] if later else L
            keys = [(f, 'wout'), (f, 'wgu'), (f, 'wd')]
            for i in range(f + 1, stop):
                keys += [(i, 'wgu'), (i, 'wd'), (i, 'win'), (i, 'wout')]
            if later:
                keys += [(stop, 'win')]
            near = [k for k in keys if k[0] <= f + 1]
            far = [k for k in keys if k[0] > f + 1]
            plan[('fox', f)] = near
            for i in range(f, stop):
                take, total = [], 0
                while far and (not take or total + nparams(far[0]) <= HOSTED_GATHER_PARAMS):
                    total += nparams(far[0])
                    take.append(far.pop(0))
                plan[('ffn', i)] = take
            assert not far
        return plan

    plan = gather_plan()
    raw, full = {}, {}
    first_keys = [(0, 'win')]
    raw.update(zip(first_keys, _gather2("gather_first", [slice_of(*k) for k in first_keys])))

    def wget(i, role):
        if (i, role) not in full:
            got = raw[(i, role)]
            full[(i, role)] = _colcat(got) if role in ('wgu', 'win') else got.reshape(-1, D)
        return full[(i, role)]

    saved = []
    xc = x0
    h = _pre_fwd(xc, mix_pre_g[0:1], mod[0, 1], mod[0, 0])
    for i in range(L):
        kind, j = mixer[i % 3], i // 3
        s = dict(x_in=xc)
        sh_m, sc_m, g_m, sh_f, sc_f, g_f = [mod[i, t] for t in range(6)]
        s['h'] = h
        if kind == 'fox':
            wqkv = wget(i, 'win')[:, :3 * D]
            wf = jnp.pad(wget(i, 'win')[:, 3 * D:], ((0, 0), (0, LANES - H)))
            s['win_pad'] = jnp.concatenate([wqkv, wf], axis=1)
            bf = jnp.pad(fox_b_f[j:j + 1], ((0, 0), (0, LANES - H)))
            qkv = _mm("fox_qkv", h, wqkv, 'nn', BF16)
            fg = _mm("fox_fg", h, wf, 'nn', F32)
            cum, cq_rep = _fox_gate_fwd(fg, bf, H)
            ck = cum[:, :H].T.reshape(H, 1, T)
            keys = plan[('fox', i)]
            (o, obf, lse_rep), got = _fox_fwd("fox_fwd%d" % j, qkv, cq_rep, ck, gather=[slice_of(*k) for k in keys])
            raw.update(zip(keys, got))
            s.update(qkv=qkv, fg=fg, bf=bf, cq_rep=cq_rep, ck=ck, o=o, lse_rep=lse_rep, mix_out=obf)
        elif kind == 'sgu':
            G = D // BLOCK
            causal = jnp.tril(jnp.ones((BLOCK, BLOCK), bool))
            wsm = jnp.where(causal[None], sgu_w_s[j], 0.0).astype(BF16)
            bs_rep = jnp.broadcast_to(sgu_b_s[j][:, :, None], (G, BLOCK, BLOCK))
            zpre = _mm("sgu_in", h, wget(i, 'win'), 'nn', F32)
            gated = _sgu_fwd(zpre, sgu_ln_g[j:j + 1], sgu_ln_b[j:j + 1], wsm, bs_rep)
            s.update(zpre=zpre, wsm=wsm, bs_rep=bs_rep, mix_out=gated)
        else:
            Hk = (wget(i, 'win').shape[1] // SWA_DH - Hq) // 2
            proj = _mm("swa_in", h, wget(i, 'win'), 'nn', F32)
            qr = _rope(proj[:, :Hq * SWA_DH], cos_t, sin_t)
            kr = _rope(proj[:, Hq * SWA_DH:(Hq + Hk) * SWA_DH], cos_t, sin_t)
            qp, kp = _to_heads(qr, Hq), _to_heads(kr, Hk)
            vp = _to_heads(proj[:, (Hq + Hk) * SWA_DH:].astype(BF16), Hk)
            op, lse_rep = _swa_fwd(qp, kp, vp, swa_sinks[j])
            s.update(qp=qp, kp=kp, vp=vp, op=op, lse_rep=lse_rep, Hk=Hk, mix_out=_from_heads(op).astype(BF16))
        y = _mm("mix_out", s['mix_out'], wget(i, 'wout'), 'nn', F32)
        x_mid, h2 = _post_pre_fwd(xc, y, mix_post_g[i:i + 1], g_m, ffn_pre_g[i:i + 1], sc_f, sh_f)
        s.update(y_mix=y, x_mid=x_mid)
        keys = plan[('ffn', i)]
        (g, u, a), got = _ffn_up("ffn_up_g%d" % i if keys else "ffn_up", h2, wget(i, 'wgu'), gather=[slice_of(*k) for k in keys])
        raw.update(zip(keys, got))
        y2 = _mm("ffn_down", a, wget(i, 'wd'), 'nn', F32)
        if i + 1 < L:
            xc, h = _post_pre_fwd(x_mid, y2, ffn_post_g[i:i + 1], g_f, mix_pre_g[i + 1:i + 2], mod[i + 1, 1], mod[i + 1, 0])
        else:
            xc = _post_fwd(x_mid, y2, ffn_post_g[i:i + 1], g_f)
        s.update(h2=h2, g=g, u=u, a=a, y_ffn=y2)
        saved.append(s)

    dx, lsum = _loss(xc, loss_target.reshape(T, D))
    loss = lax.psum(0.5 * lsum[0, 0] / D, AXES)

    small = {n: [None] * W[n].shape[0] for n in SMALL}
    dmod = [None] * L
    pending = []
    stacks = {}

    def update(items, recv):
        for (name, idx, _), parts in zip(items, recv):
            stacks[name] = _adamw("adamw_" + name, parts, W[name], M[name], V[name], idx, stacked=stacks.get(name))

    for i in reversed(range(L)):
        kind, j = mixer[i % 3], i // 3
        s = saved[i]
        sh_m, sc_m, g_m, sh_f, sc_f, g_f = [mod[i, t] for t in range(6)]
        dy2, dg_f, dpost_f = _post_bwd(dx, s['y_ffn'], ffn_post_g[i:i + 1], g_f)
        dwd = _mm("ffn_dwd", s['a'], dy2, 'tn', BF16)
        dg, du = _ffn_dact(dy2, wget(i, 'wd'), s['g'], s['u'])
        n_gu = 2 * F // NDEV
        tm_gu, _, tk_gu = _matmul_tiles('tn', D, n_gu, T, 1, 2)
        dwgu = None
        for half, (name, d_act) in enumerate((("ffn_dwg", dg), ("ffn_dwu", du))):
            dwgu = _matmul(name, [(s['h2'], d_act, (0, 0), (0, 0))], 'tn', BF16, D, F, T, tm=tm_gu, tn=n_gu, tk=tk_gu,
                           slots=(NDEV, half * (NDEV // 2), dwgu))
        pending += [('ffn_w_gu', i, dwgu), ('ffn_w_down', i, _rowsplit(dwd))]
        take, total = [], 0
        while i < L - 1 and pending and (not take or total + pending[0][2].size <= HOSTED_A2A_PARAMS):
            total += pending[0][2].size
            take.append(pending.pop(0))
        dh_pairs = [(dg, wget(i, 'wgu'), (0, 0), (0, 0)), (du, wget(i, 'wgu'), (0, 0), (0, F))]
        if take:
            dh2, recv = _matmul("ffn_dh_x%d" % i, dh_pairs, 'nt', F32, T, D, F, a2a=[item[2] for item in take])
            update(take, recv)
        else:
            dh2 = _matmul("ffn_dh", dh_pairs, 'nt', F32, T, D, F)
        dx, dsh_f, dsc_f, dpre_f = _pre_bwd(dh2, s['x_mid'], ffn_pre_g[i:i + 1], sc_f, dx)
        dy, dg_m, dpost_m = _post_bwd(dx, s['y_mix'], mix_post_g[i:i + 1], g_m)
        dwout = _mm("mix_dwout", s['mix_out'], dy, 'tn', BF16)
        pending.append((kind + '_w_out', j, _rowsplit(dwout)))
        dmix = _mm("mix_dout", dy, wget(i, 'wout'), 'nt', F32)
        if kind == 'fox':
            dob, delta_rep, cql_rep = _fox_bwd_prep(dmix, s['o'], s['lse_rep'], s['cq_rep'])
            (dq, dk, dv, dck, dcq), recv = _fox_bwd("fox_bwd%d" % j, s['qkv'], dob, delta_rep, cql_rep, s['ck'],
                                                    a2a=[item[2] for item in pending])
            update(pending, recv)
            pending = []
            dcum = jnp.pad(dck.reshape(H, T).T, ((0, 0), (0, LANES - H))) + dcq
            dfg, dbf = _fox_gate_bwd(dcum, s['fg'], s['bf'])
            small['fox_b_f'][j] = dbf[0, :H]
            dproj = jnp.concatenate([dq, dk, dv, dfg.astype(BF16)], axis=1)
            dwin = _mm("fox_dwin", s['h'], dproj, 'tn', BF16)[:, :3 * D + H]
            pending.append((kind + '_w_in', j, _colsplit(dwin)))
            if i == 0:
                dh, recv = _mm("fox_dh_last", dproj, s['win_pad'], 'nt', F32, a2a=[item[2] for item in pending])
                update(pending, recv)
                pending = []
            else:
                dh = _mm("fox_dh", dproj, s['win_pad'], 'nt', F32)
        elif kind == 'sgu':
            wsmT = s['wsm'].transpose(0, 2, 1)
            dz, dws, dbs, dlg, dlb = _sgu_bwd(dmix, s['zpre'], sgu_ln_g[j:j + 1], sgu_ln_b[j:j + 1], s['wsm'], wsmT, s['bs_rep'])
            small['sgu_w_s'][j], small['sgu_b_s'][j] = dws, dbs[:, :, 0]
            small['sgu_ln_g'][j], small['sgu_ln_b'][j] = dlg[0], dlb[0]
            dwin = _mm("sgu_dwin", s['h'], dz, 'tn', BF16)
            pending.append((kind + '_w_in', j, _colsplit(dwin)))
            dh = _mm("sgu_dh", dz, wget(i, 'win'), 'nt', F32)
        else:
            Hk = s['Hk']
            dop = _to_heads(dmix, Hq)
            dqp, dsink = _swa_bwd_dq(s['qp'], s['kp'], s['vp'], dop, s['op'], s['lse_rep'], swa_sinks[j])
            dkp, dvp = _swa_bwd_dkv(s['qp'], s['kp'], s['vp'], dop, s['op'], s['lse_rep'])
            small['swa_sinks'][j] = dsink[:, 0, 0]
            dproj = jnp.concatenate([_rope(_from_heads(dqp), cos_t, -sin_t), _rope(_from_heads(dkp), cos_t, -sin_t),
                                     _from_heads(dvp).astype(BF16)], axis=1)
            dwin = _mm("swa_dwin", s['h'], dproj, 'tn', BF16)
            pending.append((kind + '_w_in', j, _colsplit(dwin)))
            dh = _mm("swa_dh", dproj, wget(i, 'win'), 'nt', F32)
        dx, dsh_m, dsc_m, dpre_m = _pre_bwd(dh, s['x_in'], mix_pre_g[i:i + 1], sc_m, dx)
        small['mix_pre_g'][i], small['mix_post_g'][i] = dpre_m[0], dpost_m[0]
        small['ffn_pre_g'][i], small['ffn_post_g'][i] = dpre_f[0], dpost_f[0]
        dmod[i] = jnp.concatenate([dsh_m, dsc_m, dg_m, dsh_f, dsc_f, dg_f], axis=1)[0]

    if pending:
        update(pending, _comm("a2a_last", [item[2] for item in pending], 'a2a'))
    grad_x = dx.reshape(1, T, D)

    small['ada_b'] = dmod
    flat = jnp.concatenate([jnp.stack(small[n]).reshape(-1) for n in SMALL])
    width = 8 * LANES
    npad = -flat.shape[0] % (8 * width)
    packed = jnp.pad(flat, (0, npad)).reshape(-1, width)
    parts = _comm("gather_small", [packed], 'gather')[0]

    def pack(d):
        f = jnp.concatenate([d[n].reshape(-1) for n in SMALL])
        return jnp.pad(f, (0, npad)).reshape(1, -1, width)

    res = _adamw("adamw_small", parts, pack(W), pack(M), pack(V), 0)
    off = 0
    for n in SMALL:
        size = W[n].size
        stacks[n] = [val.reshape(-1)[off:off + size].reshape(W[n].shape) for val in res]
        off += size

    dmod_all = parts.reshape(NDEV, -1)[:, :L * 6 * D].reshape(NDEV, L, 6 * D)
    dm = lax.dynamic_slice_in_dim(dmod_all, me * n_ada, n_ada, axis=2).transpose(1, 0, 2)
    c_rep = jnp.broadcast_to(c_all[:, :, None], (NDEV, D, LANES))
    stacks['ada_w'] = _ada_update(c_rep, dm, ada_w, m_ada_w, v_ada_w)

    return (loss, grad_x, *[stacks[n][t] for t in range(4) for n in WEIGHTS])
```
